```python
import jax, jax.numpy as jnp
from jax import lax
import numpy as np

D_MODEL = 1024
BATCH = 32
SEQ = 256
DEPTH = 2
DEC_BATCH = 2
DEC_SEQ = 1024
PAST_LEN = 512

GRID_W = 64
N_HEADS = 8
N_KV_HEADS = 2
HEAD_DIM = 64
ATTN_WIDTH = N_HEADS * HEAD_DIM
KV_WIDTH = N_KV_HEADS * HEAD_DIM
CONV_WIDTH = D_MODEL // 2
CONV_K = 3
IN_WIDTH = ATTN_WIDTH + 2 * KV_WIDTH + 3 * CONV_WIDTH
MIX_WIDTH = ATTN_WIDTH + CONV_WIDTH
D_FF = 2816
POOL_WINDOWS = (2, 4, 8, 16)
POOL_GROUP = D_MODEL // len(POOL_WINDOWS)
N_EVEN = (DEPTH + 1) // 2
N_ODD = DEPTH // 2
N_MOD = 9
Q_BLOCK = 128
ROPE_THETA = 10000.0
EPS = 1e-6

kernel_name = "hybrid_prefix_diffusion_step"


def rms_norm(x, g):
    xf = x.astype(jnp.float32)
    y = xf * lax.rsqrt(jnp.mean(xf * xf, axis=-1, keepdims=True) + EPS)
    return (y * g.astype(jnp.float32)).astype(x.dtype)


def modulate(x, g, shift, scale):
    return rms_norm(x, g) * (1 + scale[:, None, :]) + shift[:, None, :]


def swiglu(h, w1, w2):
    gate, up = jnp.split(h @ w1, 2, axis=-1)
    return (jax.nn.silu(gate) * up) @ w2


def rope_half(x, ang):
    cos = jnp.cos(ang)[None, :, None, :].astype(x.dtype)
    sin = jnp.sin(ang)[None, :, None, :].astype(x.dtype)
    x1, x2 = jnp.split(x, 2, axis=-1)
    return jnp.concatenate([x1 * cos - x2 * sin, x2 * cos + x1 * sin], axis=-1)


def axial_rope(x):
    rows = x.shape[1] // GRID_W
    t = jnp.arange(rows * GRID_W)
    row = (t // GRID_W).astype(jnp.float32)
    col = (t % GRID_W).astype(jnp.float32)
    half = HEAD_DIM // 2
    inv = ROPE_THETA ** (-jnp.arange(0, half, 2, dtype=jnp.float32) / half)
    xr, xc = jnp.split(x, 2, axis=-1)
    return jnp.concatenate([rope_half(xr, row[:, None] * inv[None, :]),
                            rope_half(xc, col[:, None] * inv[None, :])], axis=-1)


def attention(q, k, v):
    B, S = q.shape[:2]
    nb = S // Q_BLOCK
    G = N_HEADS // N_KV_HEADS
    qb = q.reshape(B, nb, Q_BLOCK, N_KV_HEADS, G, HEAD_DIM).transpose(1, 0, 2, 3, 4, 5)
    scale = HEAD_DIM ** -0.5

    def block(qi):
        s = jnp.einsum('bqhgd,bkhd->bhgqk', qi, k).astype(jnp.float32) * scale
        p = jax.nn.softmax(s, axis=-1).astype(v.dtype)
        return jnp.einsum('bhgqk,bkhd->bqhgd', p, v)

    o = lax.map(block, qb)
    return o.transpose(1, 0, 2, 3, 4, 5).reshape(B, S, ATTN_WIDTH)


def short_conv(x, w):
    S = x.shape[1]
    xp = jnp.pad(x, ((0, 0), (1, 1), (0, 0)))
    return xp[:, 0:S] * w[0] + xp[:, 1:S + 1] * w[1] + xp[:, 2:S + 2] * w[2]


def conv_attn_mixer(h, w_in, w_out, q_g, k_g, conv_w, ctx_kv):
    B, S, _ = h.shape
    splits = np.cumsum([ATTN_WIDTH, KV_WIDTH, KV_WIDTH, CONV_WIDTH, CONV_WIDTH]).tolist()
    q, k, v, bg, cg, xc = jnp.split(h @ w_in, splits, axis=-1)
    q = rms_norm(q.reshape(B, S, N_HEADS, HEAD_DIM), q_g)
    k = rms_norm(k.reshape(B, S, N_KV_HEADS, HEAD_DIM), k_g)
    v = v.reshape(B, S, N_KV_HEADS, HEAD_DIM)
    if ctx_kv is None:
        attn = attention(q, k, v)
        new_kv = (k, v)
    else:
        ck, cv = ctx_kv
        attn = attention(axial_rope(q),
                         jnp.concatenate([ck, axial_rope(k)], axis=1),
                         jnp.concatenate([cv, v], axis=1))
        new_kv = None
    conv = bg * short_conv(cg * xc, conv_w)
    return jnp.concatenate([attn, conv], axis=-1) @ w_out, new_kv


def pool_mixer(h, pool_w, pool_scale):
    B, S, D = h.shape
    hf = h.astype(jnp.float32)
    cs = jnp.concatenate([jnp.zeros((B, 1, D), jnp.float32), jnp.cumsum(hf, axis=1)], axis=1)
    t = jnp.arange(S)
    outs = []
    for gi, w in enumerate(POOL_WINDOWS):
        left = w // 2
        right = w - 1 - left
        lo = jnp.maximum(t - left, 0)
        hi = jnp.minimum(t + right + 1, S)
        sl = slice(gi * POOL_GROUP, (gi + 1) * POOL_GROUP)
        csg = cs[:, :, sl]
        mean = (csg[:, hi] - csg[:, lo]) / (hi - lo).astype(jnp.float32)[None, :, None]
        diff = (mean - hf[:, :, sl]).astype(h.dtype)
        outs.append(diff @ pool_w[gi])
    return jnp.concatenate(outs, axis=-1) * pool_scale


def run_trunk(x, cvec, cache_k, cache_v, ada_w, ada_b, norm_g, ffn_w1, ffn_w2,
              mix_w_in, mix_w_out, q_norm, k_norm, conv_w, pool_w, pool_scale, final_g):
    is_ctx = cache_k is None
    ks, vs = [], []
    for l in range(DEPTH):
        mod = jax.nn.silu(cvec) @ ada_w[l] + ada_b[l]
        sh1, sc1, g1, sh2, sc2, g2, sh3, sc3, g3 = jnp.split(mod, N_MOD, axis=-1)
        x = x + 0.5 * g1[:, None, :] * swiglu(modulate(x, norm_g[l, 0], sh1, sc1),
                                              ffn_w1[l, 0], ffn_w2[l, 0])
        h = modulate(x, norm_g[l, 1], sh2, sc2)
        if l % 2 == 0:
            e = l // 2
            ctx_kv = None if is_ctx else (cache_k[:, e], cache_v[:, e])
            out, kv = conv_attn_mixer(h, mix_w_in[e], mix_w_out[e], q_norm[e], k_norm[e],
                                      conv_w[e], ctx_kv)
            if is_ctx:
                ks.append(kv[0])
                vs.append(kv[1])
        else:
            o = l // 2
            out = pool_mixer(h, pool_w[o], pool_scale[o])
        x = x + g2[:, None, :] * out
        x = x + 0.5 * g3[:, None, :] * swiglu(modulate(x, norm_g[l, 2], sh3, sc3),
                                              ffn_w1[l, 1], ffn_w2[l, 1])
    return rms_norm(x, final_g), ks, vs


def setup_inputs(seed: int = 0) -> dict:
    key = jax.random.key(seed)
    ks = jax.random.split(key, 20)
    f32 = jnp.float32
    n = lambda k, s, sc: jax.random.normal(k, s, f32) * sc
    return {
        "x_prompt": n(ks[0], (BATCH, SEQ, D_MODEL), 1.0),
        "x_sample": n(ks[1], (DEC_BATCH, DEC_SEQ, D_MODEL), 1.0),
        "c": n(ks[2], (DEC_BATCH, D_MODEL), 1.0),
        "cache_k": n(ks[3], (DEC_BATCH, N_EVEN, PAST_LEN, N_KV_HEADS, HEAD_DIM), 1.0),
        "cache_v": n(ks[4], (DEC_BATCH, N_EVEN, PAST_LEN, N_KV_HEADS, HEAD_DIM), 1.0),
        "c_ctx": n(ks[5], (D_MODEL,), 1.0),
        "ada_w": n(ks[6], (DEPTH, D_MODEL, N_MOD * D_MODEL), 0.5 * D_MODEL ** -0.5),
        "ada_b": n(ks[7], (DEPTH, N_MOD * D_MODEL), 0.02),
        "norm_g": 1.0 + n(ks[8], (DEPTH, 3, D_MODEL), 0.1),
        "ffn_w1": n(ks[9], (DEPTH, 2, D_MODEL, 2 * D_FF), D_MODEL ** -0.5),
        "ffn_w2": n(ks[10], (DEPTH, 2, D_FF, D_MODEL), D_FF ** -0.5),
        "mix_w_in": n(ks[11], (N_EVEN, D_MODEL, IN_WIDTH), D_MODEL ** -0.5),
        "mix_w_out": n(ks[12], (N_EVEN, MIX_WIDTH, D_MODEL), MIX_WIDTH ** -0.5),
        "q_norm": 1.0 + n(ks[13], (N_EVEN, HEAD_DIM), 0.1),
        "k_norm": 1.0 + n(ks[14], (N_EVEN, HEAD_DIM), 0.1),
        "conv_w": n(ks[15], (N_EVEN, CONV_K, CONV_WIDTH), CONV_K ** -0.5),
        "pool_w": n(ks[16], (N_ODD, len(POOL_WINDOWS), POOL_GROUP, POOL_GROUP), POOL_GROUP ** -0.5),
        "pool_scale": 1.0 + n(ks[17], (N_ODD, D_MODEL), 0.1),
        "final_g": 1.0 + n(ks[18], (D_MODEL,), 0.1),
    }


def reference(x_prompt, x_sample, c, cache_k, cache_v, c_ctx, ada_w, ada_b, norm_g, ffn_w1, ffn_w2,
              mix_w_in, mix_w_out, q_norm, k_norm, conv_w, pool_w, pool_scale, final_g):
    weights = (ada_w, ada_b, norm_g, ffn_w1, ffn_w2, mix_w_in, mix_w_out, q_norm, k_norm,
               conv_w, pool_w, pool_scale, final_g)
    y_prompt, ks, vs = run_trunk(x_prompt, c_ctx[None, :], None, None, *weights)
    new_cache_k = jnp.stack(ks, axis=1)
    new_cache_v = jnp.stack(vs, axis=1)
    y_sample, _, _ = run_trunk(x_sample, c, cache_k, cache_v, *weights)
    return (y_prompt, y_sample, new_cache_k, new_cache_v)
```

```python
import functools

import numpy as np
import jax
import jax.numpy as jnp
from jax import lax
from jax.experimental import pallas as pl
from jax.experimental.pallas import tpu as pltpu

F32 = jnp.float32
BF16 = jnp.bfloat16

D_MODEL = 1024
BATCH = 32
SEQ = 256
DEC_BATCH = 2
DEC_SEQ = 1024
PAST_LEN = 512
GRID_W = 64
N_HEADS = 8
N_KV_HEADS = 2
HEAD_DIM = 64
HEADS_PER_KV = N_HEADS // N_KV_HEADS
ATTN_WIDTH = N_HEADS * HEAD_DIM
KV_WIDTH = N_KV_HEADS * HEAD_DIM
CONV_WIDTH = D_MODEL // 2
IN_WIDTH = ATTN_WIDTH + 2 * KV_WIDTH + 3 * CONV_WIDTH
D_FF = 2816
POOL_WINDOWS = (2, 4, 8, 16)
POOL_GROUP = D_MODEL // len(POOL_WINDOWS)
N_MOD = 9
ROPE_THETA = 10000.0
EPS = 1e-6

N_CTX = BATCH * SEQ
N_SMP = DEC_BATCH * DEC_SEQ
N_ALL = N_CTX + N_SMP
ROW_TILE = 1024
SUB_TILE = 256
FF_CHUNK = 256
N_CTX_TILES = N_CTX // ROW_TILE
N_TILES = N_ALL // ROW_TILE
MOD_ROWS = 8
MOD_COLS = 3 * D_MODEL
VMEM_LIMIT = 56 * 1024 * 1024


def _mod_row(i):
    return jnp.where(i < N_CTX_TILES, 0, 1 + (i - N_CTX_TILES) * ROW_TILE // DEC_SEQ)


def _rms(x, g):
    return x * lax.rsqrt(jnp.mean(x * x, axis=-1, keepdims=True) + EPS) * g


def _modulate(x, g, mod_ref, r, j):
    shift = mod_ref[j, pl.ds(r, 1), :]
    scale = mod_ref[j + 1, pl.ds(r, 1), :]
    return _rms(x, g) * (1 + scale) + shift


def _ffn_rows(x, g, mod_ref, r, j, w1_ref, w2_ref):
    h = _modulate(x, g, mod_ref, r, j).astype(BF16)
    acc = None
    for c in range(D_FF // FF_CHUNK):
        lo = c * FF_CHUNK
        gate = jnp.dot(h, w1_ref[:, lo:lo + FF_CHUNK], preferred_element_type=F32)
        up = jnp.dot(h, w1_ref[:, D_FF + lo:D_FF + lo + FF_CHUNK], preferred_element_type=F32)
        act = (gate / (1 + jnp.exp(-gate)) * up).astype(BF16)
        y = jnp.dot(act, w2_ref[lo:lo + FF_CHUNK, :], preferred_element_type=F32)
        acc = y if acc is None else acc + y
    return x + (0.5 * mod_ref[j + 2, pl.ds(r, 1), :]) * acc


def _mod_kernel(c_ref, w_ref, b_ref, o_ref):
    c = c_ref[...]
    s = (c / (1 + jnp.exp(-c))).astype(BF16)
    y = jnp.dot(s, w_ref[...].astype(BF16), preferred_element_type=F32)
    for t in range(MOD_COLS // D_MODEL):
        o_ref[t] = y[:, t * D_MODEL:(t + 1) * D_MODEL] + b_ref[t]


def _mod_call(cvec, ada_w, ada_b):
    depth = ada_w.shape[0]
    per = MOD_COLS // D_MODEL
    return pl.pallas_call(
        _mod_kernel,
        grid=(depth, N_MOD // per),
        in_specs=[
            pl.BlockSpec((MOD_ROWS, D_MODEL), lambda l, j: (0, 0)),
            pl.BlockSpec((None, D_MODEL, MOD_COLS), lambda l, j: (l, 0, j)),
            pl.BlockSpec((None, per, 1, D_MODEL), lambda l, j: (l, j, 0, 0)),
        ],
        out_specs=pl.BlockSpec((None, per, MOD_ROWS, D_MODEL), lambda l, j: (l, j, 0, 0)),
        out_shape=jax.ShapeDtypeStruct((depth, N_MOD, MOD_ROWS, D_MODEL), F32),
        compiler_params=pltpu.CompilerParams(
            dimension_semantics=("arbitrary", "arbitrary"), vmem_limit_bytes=VMEM_LIMIT),
        name="adaln_mod",
    )(cvec, ada_w, ada_b.reshape(depth, N_MOD, 1, D_MODEL))


def _rows_spec(width):
    return pl.BlockSpec((ROW_TILE, width), lambda i: (i, 0))


def _mod_spec(layer):
    return pl.BlockSpec((None, N_MOD, MOD_ROWS, D_MODEL), lambda i: (layer, 0, 0, 0))


def _gain_spec(layer, j):
    return pl.BlockSpec((None, None, 1, D_MODEL), lambda i: (layer, j, 0, 0))


def _resident_spec(shape):
    zeros = (0,) * len(shape)
    return pl.BlockSpec(shape, lambda i: zeros, pipeline_mode=pl.Buffered(1))


def _params():
    return pltpu.CompilerParams(dimension_semantics=("arbitrary",), vmem_limit_bytes=VMEM_LIMIT)


def _sub_tiles(body):
    def step(s, carry):
        body(pl.ds(pl.multiple_of(s * SUB_TILE, SUB_TILE), SUB_TILE))
        return carry
    lax.fori_loop(0, ROW_TILE // SUB_TILE, step, 0)


def _ffn_in_kernel(xp_ref, xs_ref, mod_ref, g_ref, w1_ref, w2_ref, o_ref):
    i = pl.program_id(0)
    r = _mod_row(i)
    is_ctx = i < N_CTX_TILES

    def body(rows):
        x = jnp.where(is_ctx, xp_ref[rows, :], xs_ref[rows, :])
        o_ref[rows, :] = _ffn_rows(x, g_ref[...], mod_ref, r, 0, w1_ref, w2_ref)

    _sub_tiles(body)


def _ffn_in_call(xp, xs, mod, norm_g, w1, w2):
    return pl.pallas_call(
        _ffn_in_kernel,
        grid=(N_TILES,),
        in_specs=[
            pl.BlockSpec((ROW_TILE, D_MODEL), lambda i: (jnp.minimum(i, N_CTX_TILES - 1), 0)),
            pl.BlockSpec((ROW_TILE, D_MODEL), lambda i: (jnp.maximum(i - N_CTX_TILES, 0), 0)),
            _mod_spec(0), _gain_spec(0, 0),
            _resident_spec((D_MODEL, 2 * D_FF)), _resident_spec((D_FF, D_MODEL)),
        ],
        out_specs=_rows_spec(D_MODEL),
        out_shape=jax.ShapeDtypeStruct((N_ALL, D_MODEL), F32),
        compiler_params=_params(),
        name="ffn_in",
    )(xp, xs, mod, norm_g, w1, w2)


def _ffn_kernel(x_ref, mod_ref, g_ref, w1_ref, w2_ref, o_ref, *, j):
    r = _mod_row(pl.program_id(0))

    def body(rows):
        o_ref[rows, :] = _ffn_rows(x_ref[rows, :], g_ref[...], mod_ref, r, j, w1_ref, w2_ref)

    _sub_tiles(body)


def _ffn_call(x, mod, norm_g, w1, w2, layer, which):
    return pl.pallas_call(
        functools.partial(_ffn_kernel, j=3 * which),
        grid=(N_TILES,),
        in_specs=[
            _rows_spec(D_MODEL), _mod_spec(layer), _gain_spec(layer, which),
            _resident_spec((D_MODEL, 2 * D_FF)), _resident_spec((D_FF, D_MODEL)),
        ],
        out_specs=_rows_spec(D_MODEL),
        out_shape=jax.ShapeDtypeStruct((N_ALL, D_MODEL), F32),
        compiler_params=_params(),
        name="ffn",
    )(x, mod, norm_g, w1, w2)


def _head_mean_sq(x, seg):
    sq = x * x
    hi = sq.astype(BF16)
    lo = (sq - hi.astype(F32)).astype(BF16)
    return (jnp.dot(hi, seg, preferred_element_type=F32)
            + jnp.dot(lo, seg, preferred_element_type=F32))


def _rope(x, cos, sin_signed):
    width = x.shape[-1]
    lane = lax.broadcasted_iota(jnp.int32, (1, width), 1)
    partner = jnp.where((lane & 31) < 16,
                        pltpu.roll(x, width - 16, 1), pltpu.roll(x, 16, 1))
    return x * cos + partner * sin_signed


def _inproj_kernel(x_ref, mod_ref, g_ref, w_ref, qg_ref, kg_ref, cw_ref, seg_ref, cos_ref,
                   sin_ref, q_ref, k_ref, v_ref, conv_ref):
    i = pl.program_id(0)
    r = _mod_row(i)
    is_ctx = i < N_CTX_TILES
    h = _modulate(x_ref[...], g_ref[...], mod_ref, r, 3).astype(BF16)

    def proj(lo, width):
        return jnp.dot(h, w_ref[:, lo:lo + width], preferred_element_type=F32)

    q = proj(0, ATTN_WIDTH)
    k = proj(ATTN_WIDTH, KV_WIDTH)
    v_ref[...] = proj(ATTN_WIDTH + KV_WIDTH, KV_WIDTH)
    q = q * lax.rsqrt(_head_mean_sq(q, seg_ref[...]) + EPS) * (qg_ref[...] * HEAD_DIM ** -0.5)
    k = k * lax.rsqrt(_head_mean_sq(k, seg_ref[:KV_WIDTH, :KV_WIDTH]) + EPS) * kg_ref[...]

    @pl.when(is_ctx)
    def _():
        q_ref[...] = q.astype(BF16)
        k_ref[...] = k

    @pl.when(jnp.logical_not(is_ctx))
    def _():
        q_ref[...] = _rope(q, cos_ref[...], sin_ref[...]).astype(BF16)
        k_ref[...] = _rope(k, cos_ref[:, :KV_WIDTH], sin_ref[:, :KV_WIDTH])

    base = ATTN_WIDTH + 2 * KV_WIDTH
    z = proj(base + CONV_WIDTH, CONV_WIDTH) * proj(base + 2 * CONV_WIDTH, CONV_WIDTH)
    pos = lax.broadcasted_iota(jnp.int32, (ROW_TILE, 1), 0) & jnp.where(is_ctx, SEQ - 1, DEC_SEQ - 1)
    last = jnp.where(is_ctx, SEQ - 1, DEC_SEQ - 1)
    z_prev = jnp.where(pos == 0, 0.0, pltpu.roll(z, 1, 0))
    z_next = jnp.where(pos == last, 0.0, pltpu.roll(z, ROW_TILE - 1, 0))
    conv = z_prev * cw_ref[0:1, :] + z * cw_ref[1:2, :] + z_next * cw_ref[2:3, :]
    conv_ref[...] = (proj(base, CONV_WIDTH) * conv).astype(BF16)


def _inproj_call(x, mod, norm_g, w_in, qg, kg, conv_w, seg, cos, sin):
    return pl.pallas_call(
        _inproj_kernel,
        grid=(N_TILES,),
        in_specs=[
            _rows_spec(D_MODEL), _mod_spec(0), _gain_spec(0, 1),
            _resident_spec((D_MODEL, IN_WIDTH)),
            _resident_spec((1, ATTN_WIDTH)), _resident_spec((1, KV_WIDTH)),
            _resident_spec((3, CONV_WIDTH)),
            _resident_spec((ATTN_WIDTH, ATTN_WIDTH)),
            _resident_spec((DEC_SEQ, ATTN_WIDTH)), _resident_spec((DEC_SEQ, ATTN_WIDTH)),
        ],
        out_specs=[_rows_spec(ATTN_WIDTH), _rows_spec(KV_WIDTH), _rows_spec(KV_WIDTH),
                   _rows_spec(CONV_WIDTH)],
        out_shape=[jax.ShapeDtypeStruct((N_ALL, ATTN_WIDTH), BF16),
                   jax.ShapeDtypeStruct((N_ALL, KV_WIDTH), F32),
                   jax.ShapeDtypeStruct((N_ALL, KV_WIDTH), F32),
                   jax.ShapeDtypeStruct((N_ALL, CONV_WIDTH), BF16)],
        compiler_params=_params(),
        name="mixer_in",
    )(x, mod, norm_g, w_in, qg, kg, conv_w, seg, cos, sin)


def _stack_heads(q_ref, rows, kvh):
    first = kvh * HEADS_PER_KV
    return jnp.concatenate(
        [q_ref[rows, (first + g) * HEAD_DIM:(first + g + 1) * HEAD_DIM] for g in range(HEADS_PER_KV)],
        axis=0)


def _scores(q, k):
    return lax.dot_general(q, k, (((1,), (1,)), ((), ())), preferred_element_type=F32)


def _unstack_heads(o, n):
    return [o[g * n:(g + 1) * n] for g in range(HEADS_PER_KV)]


def _attn_ctx_kernel(q_ref, k_ref, v_ref, o_ref):
    for b in range(ROW_TILE // SEQ):
        rows = slice(b * SEQ, (b + 1) * SEQ)
        outs = []
        for kvh in range(N_KV_HEADS):
            lanes = slice(kvh * HEAD_DIM, (kvh + 1) * HEAD_DIM)
            s = _scores(_stack_heads(q_ref, rows, kvh), k_ref[rows, lanes].astype(BF16))
            e = jnp.exp(s - jnp.max(s, axis=-1, keepdims=True))
            p = (e / jnp.sum(e, axis=-1, keepdims=True)).astype(BF16)
            o = jnp.dot(p, v_ref[rows, lanes].astype(BF16), preferred_element_type=F32)
            outs += _unstack_heads(o, SEQ)
        o_ref[rows, :] = jnp.concatenate(outs, axis=1).astype(BF16)


def _attn_ctx_call(q, k, v):
    return pl.pallas_call(
        _attn_ctx_kernel,
        grid=(N_CTX_TILES,),
        in_specs=[_rows_spec(ATTN_WIDTH), _rows_spec(KV_WIDTH), _rows_spec(KV_WIDTH)],
        out_specs=_rows_spec(ATTN_WIDTH),
        out_shape=jax.ShapeDtypeStruct((N_ALL, ATTN_WIDTH), BF16),
        compiler_params=_params(),
        name="attn_ctx",
    )(q, k, v)


Q_ROWS = 256


def _attn_smp_kernel(prev_ref, q_ref, kn_ref, vn_ref, ck_ref, cv_ref, o_ref):
    del prev_ref
    rows = slice(0, Q_ROWS)
    outs = []
    for kvh in range(N_KV_HEADS):
        lanes = slice(kvh * HEAD_DIM, (kvh + 1) * HEAD_DIM)
        q = _stack_heads(q_ref, rows, kvh)
        s_ctx = _scores(q, ck_ref[:, lanes].astype(BF16))
        s_new = _scores(q, kn_ref[:, lanes].astype(BF16))
        m = jnp.maximum(jnp.max(s_ctx, axis=-1, keepdims=True), jnp.max(s_new, axis=-1, keepdims=True))
        e_ctx = jnp.exp(s_ctx - m)
        e_new = jnp.exp(s_new - m)
        den = jnp.sum(e_ctx, axis=-1, keepdims=True) + jnp.sum(e_new, axis=-1, keepdims=True)
        o = (jnp.dot((e_ctx / den).astype(BF16), cv_ref[:, lanes].astype(BF16), preferred_element_type=F32)
             + jnp.dot((e_new / den).astype(BF16), vn_ref[:, lanes].astype(BF16), preferred_element_type=F32))
        outs += _unstack_heads(o, Q_ROWS)
    o_ref[...] = jnp.concatenate(outs, axis=1).astype(BF16)


def _attn_smp_call(attn, q, k, v, cache_k, cache_v):
    steps = DEC_SEQ // Q_ROWS
    first_q = N_CTX // Q_ROWS
    first_kv = N_CTX // DEC_SEQ
    q_spec = pl.BlockSpec((Q_ROWS, ATTN_WIDTH), lambda b, t: (first_q + b * steps + t, 0))
    kv_spec = pl.BlockSpec((DEC_SEQ, KV_WIDTH), lambda b, t: (first_kv + b, 0))
    cache_spec = pl.BlockSpec((None, PAST_LEN, KV_WIDTH), lambda b, t: (b, 0, 0))
    return pl.pallas_call(
        _attn_smp_kernel,
        grid=(DEC_BATCH, steps),
        in_specs=[pl.BlockSpec(memory_space=pl.ANY), q_spec, kv_spec, kv_spec, cache_spec, cache_spec],
        out_specs=q_spec,
        out_shape=jax.ShapeDtypeStruct((N_ALL, ATTN_WIDTH), BF16),
        input_output_aliases={0: 0},
        compiler_params=pltpu.CompilerParams(
            dimension_semantics=("arbitrary", "arbitrary"), vmem_limit_bytes=VMEM_LIMIT),
        name="attn_smp",
    )(attn, q, k, v, cache_k, cache_v)


def _mixer_out_kernel(x_ref, a_ref, c_ref, mod_ref, g_ref, wo_ref, w1_ref, w2_ref, o_ref):
    r = _mod_row(pl.program_id(0))

    def body(rows):
        mix = (jnp.dot(a_ref[rows, :], wo_ref[:ATTN_WIDTH, :], preferred_element_type=F32)
               + jnp.dot(c_ref[rows, :], wo_ref[ATTN_WIDTH:, :], preferred_element_type=F32))
        x = x_ref[rows, :] + mod_ref[5, pl.ds(r, 1), :] * mix
        o_ref[rows, :] = _ffn_rows(x, g_ref[...], mod_ref, r, 6, w1_ref, w2_ref)

    _sub_tiles(body)


def _mixer_out_call(x, attn, conv, mod, norm_g, w_out, w1, w2):
    return pl.pallas_call(
        _mixer_out_kernel,
        grid=(N_TILES,),
        in_specs=[
            _rows_spec(D_MODEL), _rows_spec(ATTN_WIDTH), _rows_spec(CONV_WIDTH),
            _mod_spec(0), _gain_spec(0, 2),
            _resident_spec((ATTN_WIDTH + CONV_WIDTH, D_MODEL)),
            _resident_spec((D_MODEL, 2 * D_FF)), _resident_spec((D_FF, D_MODEL)),
        ],
        out_specs=_rows_spec(D_MODEL),
        out_shape=jax.ShapeDtypeStruct((N_ALL, D_MODEL), F32),
        compiler_params=_params(),
        name="mixer_out_ffn",
    )(x, attn, conv, mod, norm_g, w_out, w1, w2)


def _shift_rows(a, n, pos, seq_len):
    if n > 0:
        return jnp.where(pos < n, 0.0, pltpu.roll(a, n, 0))
    return jnp.where(pos >= seq_len + n, 0.0, pltpu.roll(a, a.shape[0] + n, 0))


def _pool_kernel(x_ref, mod_ref, gm_ref, gf_ref, gn_ref, pw_ref, ps_ref, w1_ref, w2_ref, o_ref,
                 x2_ref):
    i = pl.program_id(0)
    r = _mod_row(i)
    seq_len = jnp.where(i < N_CTX_TILES, SEQ, DEC_SEQ)
    pos = lax.broadcasted_iota(jnp.int32, (ROW_TILE, 1), 0) & (seq_len - 1)
    gate = mod_ref[5, pl.ds(r, 1), :]
    x = x_ref[...]
    h_all = _modulate(x, gm_ref[...], mod_ref, r, 3)

    for gi, w in enumerate(POOL_WINDOWS):
        lanes = slice(gi * POOL_GROUP, (gi + 1) * POOL_GROUP)
        h = h_all[:, lanes]
        back, fwd, n = h, h, 1
        while n < w // 2:
            back = back + _shift_rows(back, n, pos, seq_len)
            fwd = fwd + _shift_rows(fwd, -n, pos, seq_len)
            n *= 2
        total = _shift_rows(back, 1, pos, seq_len) + fwd
        left = w // 2
        right = w - 1 - left
        count = jnp.minimum(pos + right + 1, seq_len) - jnp.maximum(pos - left, 0)
        diff = (total / count.astype(F32) - h).astype(BF16)
        out = jnp.dot(diff, pw_ref[gi], preferred_element_type=F32) * ps_ref[:, lanes]
        x2_ref[:, lanes] = x[:, lanes] + gate[:, lanes] * out

    def body(rows):
        y = _ffn_rows(x2_ref[rows, :], gf_ref[...], mod_ref, r, 6, w1_ref, w2_ref)
        o_ref[rows, :] = _rms(y, gn_ref[...])

    _sub_tiles(body)


def _pool_call(x, mod, norm_g, final_g, pool_w, pool_scale, w1, w2):
    return pl.pallas_call(
        _pool_kernel,
        grid=(N_TILES,),
        in_specs=[
            _rows_spec(D_MODEL), _mod_spec(1), _gain_spec(1, 1), _gain_spec(1, 2),
            _resident_spec((1, D_MODEL)),
            _resident_spec((len(POOL_WINDOWS), POOL_GROUP, POOL_GROUP)),
            _resident_spec((1, D_MODEL)),
            _resident_spec((D_MODEL, 2 * D_FF)), _resident_spec((D_FF, D_MODEL)),
        ],
        out_specs=_rows_spec(D_MODEL),
        out_shape=jax.ShapeDtypeStruct((N_ALL, D_MODEL), F32),
        scratch_shapes=[pltpu.VMEM((ROW_TILE, D_MODEL), F32)],
        compiler_params=_params(),
        name="pool_ffn_norm",
    )(x, mod, norm_g, norm_g, final_g, pool_w, pool_scale, w1, w2)


def _rope_tables():
    t = np.arange(DEC_SEQ)
    half = HEAD_DIM // 2
    inv = ROPE_THETA ** (-np.arange(0, half, 2, dtype=np.float64) / half)
    ang_row = (t // GRID_W)[:, None] * inv[None, :]
    ang_col = (t % GRID_W)[:, None] * inv[None, :]
    cos = np.concatenate([np.cos(ang_row), np.cos(ang_row), np.cos(ang_col), np.cos(ang_col)], axis=1)
    sin = np.concatenate([-np.sin(ang_row), np.sin(ang_row), -np.sin(ang_col), np.sin(ang_col)], axis=1)
    return (jnp.asarray(np.tile(cos, (1, N_HEADS)), F32), jnp.asarray(np.tile(sin, (1, N_HEADS)), F32))


def _head_segments():
    head = np.arange(ATTN_WIDTH) // HEAD_DIM
    return jnp.asarray((head[:, None] == head[None, :]) / HEAD_DIM, BF16)


def kernel(x_prompt, x_sample, c, cache_k, cache_v, c_ctx, ada_w, ada_b, norm_g, ffn_w1, ffn_w2,
           mix_w_in, mix_w_out, q_norm, k_norm, conv_w, pool_w, pool_scale, final_g):
    cvec = jnp.concatenate(
        [c_ctx[None, :], c, jnp.zeros((MOD_ROWS - 1 - DEC_BATCH, D_MODEL), F32)], axis=0)
    mod = _mod_call(cvec, ada_w, ada_b)

    w1 = ffn_w1.astype(BF16)
    w2 = ffn_w2.astype(BF16)
    w_in = mix_w_in[0].astype(BF16)
    w_out = mix_w_out[0].astype(BF16)
    pw = pool_w[0].astype(BF16)
    gains = norm_g.reshape(norm_g.shape[0], 3, 1, D_MODEL)
    cos, sin = _rope_tables()

    x = _ffn_in_call(x_prompt.reshape(N_CTX, D_MODEL), x_sample.reshape(N_SMP, D_MODEL),
                     mod, gains, w1[0, 0], w2[0, 0])
    q, k, v, conv = _inproj_call(
        x, mod, gains, w_in, jnp.tile(q_norm[0], N_HEADS)[None, :],
        jnp.tile(k_norm[0], N_KV_HEADS)[None, :], conv_w[0], _head_segments(), cos, sin)
    attn = _attn_ctx_call(q, k, v)
    attn = _attn_smp_call(attn, q, k, v,
                          cache_k[:, 0].reshape(DEC_BATCH, PAST_LEN, KV_WIDTH),
                          cache_v[:, 0].reshape(DEC_BATCH, PAST_LEN, KV_WIDTH))
    x = _mixer_out_call(x, attn, conv, mod, gains, w_out, w1[0, 1], w2[0, 1])
    x = _ffn_call(x, mod, gains, w1[1, 0], w2[1, 0], 1, 0)
    y = _pool_call(x, mod, gains, final_g[None, :], pw, pool_scale[0][None, :], w1[1, 1], w2[1, 1])

    y_prompt = y[:N_CTX].reshape(BATCH, SEQ, D_MODEL)
    y_sample = y[N_CTX:].reshape(DEC_BATCH, DEC_SEQ, D_MODEL)
    new_k = k[:N_CTX].reshape(BATCH, 1, SEQ, N_KV_HEADS, HEAD_DIM)
    new_v = v[:N_CTX].reshape(BATCH, 1, SEQ, N_KV_HEADS, HEAD_DIM)
    return (y_prompt, y_sample, new_k, new_v)
```

```python
import functools

import numpy as np
import jax
import jax.numpy as jnp
from jax import lax
from jax.experimental import pallas as pl
from jax.experimental.pallas import tpu as pltpu

F32 = jnp.float32
BF16 = jnp.bfloat16

D_MODEL = 1024
BATCH = 32
SEQ = 256
DEC_BATCH = 2
DEC_SEQ = 1024
PAST_LEN = 512
GRID_W = 64
N_HEADS = 8
N_KV_HEADS = 2
HEAD_DIM = 64
HEADS_PER_KV = N_HEADS // N_KV_HEADS
ATTN_WIDTH = N_HEADS * HEAD_DIM
KV_WIDTH = N_KV_HEADS * HEAD_DIM
CONV_WIDTH = D_MODEL // 2
MIX_WIDTH = ATTN_WIDTH + CONV_WIDTH
IN_WIDTH = ATTN_WIDTH + 2 * KV_WIDTH + 3 * CONV_WIDTH
D_FF = 2816
POOL_WINDOWS = (2, 4, 8, 16)
POOL_GROUP = D_MODEL // len(POOL_WINDOWS)
N_MOD = 9
ROPE_THETA = 10000.0
EPS = 1e-6

N_CTX = BATCH * SEQ
N_SMP = DEC_BATCH * DEC_SEQ
N_ALL = N_CTX + N_SMP
ROW_TILE = 1024
SUB_TILE = 512
FF_CHUNK = 256
Q_ROWS = 256
N_CTX_TILES = N_CTX // ROW_TILE
N_TILES = N_ALL // ROW_TILE
SEQ_PER_TILE = ROW_TILE // SEQ
MOD_ROWS = 8
MOD_COLS = 3 * D_MODEL
VMEM_LIMIT = 56 * 1024 * 1024


def _mod_row(i):
    return jnp.where(i < N_CTX_TILES, 0, 1 + (i - N_CTX_TILES) * ROW_TILE // DEC_SEQ)


def _ctx_tile(i):
    return jnp.minimum(i, N_CTX_TILES - 1)


def _smp_tile(i):
    return jnp.maximum(i - N_CTX_TILES, 0)


def _rms(x, g):
    return x * lax.rsqrt(jnp.mean(x * x, axis=-1, keepdims=True) + EPS) * g


def _modulate(x, g, mod_ref, r, j):
    shift = mod_ref[j, pl.ds(r, 1), :]
    scale = mod_ref[j + 1, pl.ds(r, 1), :]
    return _rms(x, g) * (1 + scale) + shift


def _ffn_rows(x, g, mod_ref, r, j, w1_ref, w2_ref):
    h = _modulate(x, g, mod_ref, r, j).astype(BF16)
    acc = None
    for c in range(D_FF // FF_CHUNK):
        lo = c * FF_CHUNK
        gate = jnp.dot(h, w1_ref[:, lo:lo + FF_CHUNK], preferred_element_type=F32)
        up = jnp.dot(h, w1_ref[:, D_FF + lo:D_FF + lo + FF_CHUNK], preferred_element_type=F32)
        act = (gate / (1 + jnp.exp(-gate)) * up).astype(BF16)
        y = jnp.dot(act, w2_ref[lo:lo + FF_CHUNK, :], preferred_element_type=F32)
        acc = y if acc is None else acc + y
    return x + (0.5 * mod_ref[j + 2, pl.ds(r, 1), :]) * acc


def _mod_kernel(c_ref, w_ref, b_ref, o_ref):
    c = c_ref[...]
    s = (c / (1 + jnp.exp(-c))).astype(BF16)
    y = jnp.dot(s, w_ref[...].astype(BF16), preferred_element_type=F32)
    for t in range(MOD_COLS // D_MODEL):
        o_ref[t] = y[:, t * D_MODEL:(t + 1) * D_MODEL] + b_ref[t]


def _mod_call(cvec, ada_w, ada_b):
    depth = ada_w.shape[0]
    per = MOD_COLS // D_MODEL
    return pl.pallas_call(
        _mod_kernel,
        grid=(depth, N_MOD // per),
        in_specs=[
            pl.BlockSpec((MOD_ROWS, D_MODEL), lambda l, j: (0, 0)),
            pl.BlockSpec((None, D_MODEL, MOD_COLS), lambda l, j: (l, 0, j)),
            pl.BlockSpec((None, per, 1, D_MODEL), lambda l, j: (l, j, 0, 0)),
        ],
        out_specs=pl.BlockSpec((None, per, MOD_ROWS, D_MODEL), lambda l, j: (l, j, 0, 0)),
        out_shape=jax.ShapeDtypeStruct((depth, N_MOD, MOD_ROWS, D_MODEL), F32),
        compiler_params=pltpu.CompilerParams(
            dimension_semantics=("arbitrary", "arbitrary"), vmem_limit_bytes=VMEM_LIMIT),
        name="adaln_mod",
    )(cvec, ada_w, ada_b.reshape(depth, N_MOD, 1, D_MODEL))


def _rows_spec(width):
    return pl.BlockSpec((ROW_TILE, width), lambda i: (i, 0))


def _ctx_rows_spec(width):
    return pl.BlockSpec((ROW_TILE, width), lambda i: (_ctx_tile(i), 0))


def _smp_rows_spec(width):
    return pl.BlockSpec((ROW_TILE, width), lambda i: (_smp_tile(i), 0))


def _mod_spec(layer):
    return pl.BlockSpec((None, N_MOD, MOD_ROWS, D_MODEL), lambda i: (layer, 0, 0, 0))


def _gain_spec(layer, j):
    return pl.BlockSpec((None, None, 1, D_MODEL), lambda i: (layer, j, 0, 0))


def _resident_spec(shape, *lead):
    index = tuple(lead) + (0,) * len(shape)
    return pl.BlockSpec((None,) * len(lead) + tuple(shape), lambda i: index,
                        pipeline_mode=pl.Buffered(1))


def _ffn_weight_specs(layer, which):
    return [_resident_spec((D_MODEL, 2 * D_FF), layer, which),
            _resident_spec((D_FF, D_MODEL), layer, which)]


def _params():
    return pltpu.CompilerParams(dimension_semantics=("arbitrary",), vmem_limit_bytes=VMEM_LIMIT)


def _sub_tiles(body):
    def step(s, carry):
        body(pl.ds(pl.multiple_of(s * SUB_TILE, SUB_TILE), SUB_TILE))
        return carry
    lax.fori_loop(0, ROW_TILE // SUB_TILE, step, 0)


def _ffn_in_kernel(xp_ref, xs_ref, mod_ref, g_ref, w1_ref, w2_ref, o_ref):
    i = pl.program_id(0)
    r = _mod_row(i)
    is_ctx = i < N_CTX_TILES

    def body(rows):
        x = jnp.where(is_ctx, xp_ref[rows, :], xs_ref[rows, :])
        o_ref[rows, :] = _ffn_rows(x, g_ref[...], mod_ref, r, 0, w1_ref, w2_ref)

    _sub_tiles(body)


def _ffn_in_call(xp, xs, mod, norm_g, w1, w2):
    return pl.pallas_call(
        _ffn_in_kernel,
        grid=(N_TILES,),
        in_specs=[_ctx_rows_spec(D_MODEL), _smp_rows_spec(D_MODEL), _mod_spec(0), _gain_spec(0, 0)]
        + _ffn_weight_specs(0, 0),
        out_specs=_rows_spec(D_MODEL),
        out_shape=jax.ShapeDtypeStruct((N_ALL, D_MODEL), F32),
        compiler_params=_params(),
        name="ffn_in",
    )(xp, xs, mod, norm_g, w1, w2)


def _ffn_kernel(x_ref, mod_ref, g_ref, w1_ref, w2_ref, o_ref, *, j):
    r = _mod_row(pl.program_id(0))

    def body(rows):
        o_ref[rows, :] = _ffn_rows(x_ref[rows, :], g_ref[...], mod_ref, r, j, w1_ref, w2_ref)

    _sub_tiles(body)


def _ffn_call(x, mod, norm_g, w1, w2, layer, which):
    return pl.pallas_call(
        functools.partial(_ffn_kernel, j=6 * which),
        grid=(N_TILES,),
        in_specs=[_rows_spec(D_MODEL), _mod_spec(layer), _gain_spec(layer, 2 * which)]
        + _ffn_weight_specs(layer, which),
        out_specs=_rows_spec(D_MODEL),
        out_shape=jax.ShapeDtypeStruct((N_ALL, D_MODEL), F32),
        compiler_params=_params(),
        name="ffn",
    )(x, mod, norm_g, w1, w2)


def _head_mean_sq(x, seg):
    sq = x * x
    hi = sq.astype(BF16)
    lo = (sq - hi.astype(F32)).astype(BF16)
    return (jnp.dot(hi, seg, preferred_element_type=F32)
            + jnp.dot(lo, seg, preferred_element_type=F32))


def _rope(x, cos, sin_signed):
    width = x.shape[-1]
    lane = lax.broadcasted_iota(jnp.int32, (1, width), 1)
    partner = jnp.where((lane & 31) < 16,
                        pltpu.roll(x, width - 16, 1), pltpu.roll(x, 16, 1))
    return x * cos + partner * sin_signed


def _stack_heads(q_ref, rows, kvh):
    first = kvh * HEADS_PER_KV
    return jnp.concatenate(
        [q_ref[rows, (first + g) * HEAD_DIM:(first + g + 1) * HEAD_DIM] for g in range(HEADS_PER_KV)],
        axis=0)


def _scores(q, k):
    return lax.dot_general(q, k, (((1,), (1,)), ((), ())), preferred_element_type=F32)


def _unstack_heads(o, n):
    return [o[g * n:(g + 1) * n] for g in range(HEADS_PER_KV)]


def _attend_ctx(q_s, k_s, v_s, mix_s):
    for b in range(SEQ_PER_TILE):
        rows = slice(b * SEQ, (b + 1) * SEQ)
        outs = []
        for kvh in range(N_KV_HEADS):
            lanes = slice(kvh * HEAD_DIM, (kvh + 1) * HEAD_DIM)
            s = _scores(_stack_heads(q_s, rows, kvh), k_s[rows, lanes])
            e = jnp.exp(s - jnp.max(s, axis=-1, keepdims=True))
            p = (e / jnp.sum(e, axis=-1, keepdims=True)).astype(BF16)
            outs += _unstack_heads(jnp.dot(p, v_s[rows, lanes], preferred_element_type=F32), SEQ)
        mix_s[rows, :ATTN_WIDTH] = jnp.concatenate(outs, axis=1).astype(BF16)


def _attend_smp(q_s, k_s, v_s, ck_ref, cv_ref, mix_s):
    for t in range(ROW_TILE // Q_ROWS):
        rows = slice(t * Q_ROWS, (t + 1) * Q_ROWS)
        outs = []
        for kvh in range(N_KV_HEADS):
            lanes = slice(kvh * HEAD_DIM, (kvh + 1) * HEAD_DIM)
            q = _stack_heads(q_s, rows, kvh)
            s_ctx = _scores(q, ck_ref[:, lanes].astype(BF16))
            s_new = _scores(q, k_s[:, lanes])
            m = jnp.maximum(jnp.max(s_ctx, axis=-1, keepdims=True),
                            jnp.max(s_new, axis=-1, keepdims=True))
            e_ctx = jnp.exp(s_ctx - m)
            e_new = jnp.exp(s_new - m)
            den = jnp.sum(e_ctx, axis=-1, keepdims=True) + jnp.sum(e_new, axis=-1, keepdims=True)
            o = (jnp.dot((e_ctx / den).astype(BF16), cv_ref[:, lanes].astype(BF16),
                         preferred_element_type=F32)
                 + jnp.dot((e_new / den).astype(BF16), v_s[:, lanes], preferred_element_type=F32))
            outs += _unstack_heads(o, Q_ROWS)
        mix_s[rows, :ATTN_WIDTH] = jnp.concatenate(outs, axis=1).astype(BF16)


def _mixer_kernel(x_ref, mod_ref, g_ref, w_ref, qg_ref, kg_ref, cw_ref, seg_ref, cos_ref, sin_ref,
                  ck_ref, cv_ref, wo_ref, o_ref, kt_ref, vt_ref, q_s, k_s, v_s, mix_s):
    i = pl.program_id(0)
    r = _mod_row(i)
    is_ctx = i < N_CTX_TILES
    is_smp = jnp.logical_not(is_ctx)
    h = _modulate(x_ref[...], g_ref[...], mod_ref, r, 3).astype(BF16)

    def proj(lo, width):
        return jnp.dot(h, w_ref[:, lo:lo + width], preferred_element_type=F32)

    q = proj(0, ATTN_WIDTH)
    k = proj(ATTN_WIDTH, KV_WIDTH)
    v = proj(ATTN_WIDTH + KV_WIDTH, KV_WIDTH)
    v_s[...] = v.astype(BF16)
    q = q * lax.rsqrt(_head_mean_sq(q, seg_ref[...]) + EPS) * (qg_ref[...] * HEAD_DIM ** -0.5)
    k = k * lax.rsqrt(_head_mean_sq(k, seg_ref[:KV_WIDTH, :KV_WIDTH]) + EPS) * kg_ref[...]

    @pl.when(is_ctx)
    def _():
        q_s[...] = q.astype(BF16)
        k_s[...] = k.astype(BF16)
        for b in range(SEQ_PER_TILE):
            kt_ref[b] = k[b * SEQ:(b + 1) * SEQ, :].T
            vt_ref[b] = v[b * SEQ:(b + 1) * SEQ, :].T

    @pl.when(is_smp)
    def _():
        q_s[...] = _rope(q, cos_ref[...], sin_ref[...]).astype(BF16)
        k_s[...] = _rope(k, cos_ref[:, :KV_WIDTH], sin_ref[:, :KV_WIDTH]).astype(BF16)

    base = ATTN_WIDTH + 2 * KV_WIDTH
    z = proj(base + CONV_WIDTH, CONV_WIDTH) * proj(base + 2 * CONV_WIDTH, CONV_WIDTH)
    last = jnp.where(is_ctx, SEQ - 1, DEC_SEQ - 1)
    pos = lax.broadcasted_iota(jnp.int32, (ROW_TILE, 1), 0) & last
    z_prev = jnp.where(pos == 0, 0.0, pltpu.roll(z, 1, 0))
    z_next = jnp.where(pos == last, 0.0, pltpu.roll(z, ROW_TILE - 1, 0))
    conv = z_prev * cw_ref[0:1, :] + z * cw_ref[1:2, :] + z_next * cw_ref[2:3, :]
    mix_s[:, ATTN_WIDTH:] = (proj(base, CONV_WIDTH) * conv).astype(BF16)

    @pl.when(is_ctx)
    def _():
        _attend_ctx(q_s, k_s, v_s, mix_s)

    @pl.when(is_smp)
    def _():
        _attend_smp(q_s, k_s, v_s, ck_ref, cv_ref, mix_s)

    out = jnp.dot(mix_s[...], wo_ref[...], preferred_element_type=F32)
    o_ref[...] = x_ref[...] + mod_ref[5, pl.ds(r, 1), :] * out


def _mixer_call(x, mod, norm_g, w_in, qg, kg, conv_w, seg, cos, sin, cache_k, cache_v, w_out):
    cache_spec = pl.BlockSpec((None, PAST_LEN, KV_WIDTH), lambda i: (_smp_tile(i), 0, 0))
    kvt_spec = pl.BlockSpec((SEQ_PER_TILE, KV_WIDTH, SEQ), lambda i: (_ctx_tile(i), 0, 0))
    kvt_shape = jax.ShapeDtypeStruct((BATCH, KV_WIDTH, SEQ), F32)
    return pl.pallas_call(
        _mixer_kernel,
        grid=(N_TILES,),
        in_specs=[
            _rows_spec(D_MODEL), _mod_spec(0), _gain_spec(0, 1),
            _resident_spec((D_MODEL, IN_WIDTH), 0),
            _resident_spec((1, ATTN_WIDTH)), _resident_spec((1, KV_WIDTH)),
            _resident_spec((3, CONV_WIDTH), 0),
            _resident_spec((ATTN_WIDTH, ATTN_WIDTH)),
            _resident_spec((DEC_SEQ, ATTN_WIDTH)), _resident_spec((DEC_SEQ, ATTN_WIDTH)),
            cache_spec, cache_spec,
            _resident_spec((MIX_WIDTH, D_MODEL), 0),
        ],
        out_specs=[_rows_spec(D_MODEL), kvt_spec, kvt_spec],
        out_shape=[jax.ShapeDtypeStruct((N_ALL, D_MODEL), F32), kvt_shape, kvt_shape],
        scratch_shapes=[pltpu.VMEM((ROW_TILE, ATTN_WIDTH), BF16),
                        pltpu.VMEM((ROW_TILE, KV_WIDTH), BF16),
                        pltpu.VMEM((ROW_TILE, KV_WIDTH), BF16),
                        pltpu.VMEM((ROW_TILE, MIX_WIDTH), BF16)],
        compiler_params=_params(),
        name="mixer",
    )(x, mod, norm_g, w_in, qg, kg, conv_w, seg, cos, sin, cache_k, cache_v, w_out)


def _shift_rows(a, n, pos, seq_len):
    if n > 0:
        return jnp.where(pos < n, 0.0, pltpu.roll(a, n, 0))
    return jnp.where(pos >= seq_len + n, 0.0, pltpu.roll(a, a.shape[0] + n, 0))


def _pool_kernel(x_ref, mod_ref, gm_ref, gf_ref, gn_ref, pw_ref, ps_ref, w1_ref, w2_ref,
                 yp_ref, ys_ref, x2_ref):
    i = pl.program_id(0)
    r = _mod_row(i)
    is_ctx = i < N_CTX_TILES
    seq_len = jnp.where(is_ctx, SEQ, DEC_SEQ)
    pos = lax.broadcasted_iota(jnp.int32, (ROW_TILE, 1), 0) & (seq_len - 1)
    gate = mod_ref[5, pl.ds(r, 1), :]
    x = x_ref[...]
    h_all = _modulate(x, gm_ref[...], mod_ref, r, 3)

    for gi, w in enumerate(POOL_WINDOWS):
        lanes = slice(gi * POOL_GROUP, (gi + 1) * POOL_GROUP)
        h = h_all[:, lanes]
        back, fwd, n = h, h, 1
        while n < w // 2:
            back = back + _shift_rows(back, n, pos, seq_len)
            fwd = fwd + _shift_rows(fwd, -n, pos, seq_len)
            n *= 2
        total = _shift_rows(back, 1, pos, seq_len) + fwd
        left = w // 2
        right = w - 1 - left
        count = jnp.minimum(pos + right + 1, seq_len) - jnp.maximum(pos - left, 0)
        diff = (total / count.astype(F32) - h).astype(BF16)
        out = jnp.dot(diff, pw_ref[gi], preferred_element_type=F32) * ps_ref[:, lanes]
        x2_ref[:, lanes] = x[:, lanes] + gate[:, lanes] * out

    def body(rows):
        y = _ffn_rows(x2_ref[rows, :], gf_ref[...], mod_ref, r, 6, w1_ref, w2_ref)
        x2_ref[rows, :] = _rms(y, gn_ref[...])

    _sub_tiles(body)

    @pl.when(is_ctx)
    def _():
        yp_ref[...] = x2_ref[...]

    @pl.when(jnp.logical_not(is_ctx))
    def _():
        ys_ref[...] = x2_ref[...]


def _pool_call(x, mod, norm_g, final_g, pool_w, pool_scale, w1, w2):
    return pl.pallas_call(
        _pool_kernel,
        grid=(N_TILES,),
        in_specs=[
            _rows_spec(D_MODEL), _mod_spec(1), _gain_spec(1, 1), _gain_spec(1, 2),
            _resident_spec((1, D_MODEL)),
            _resident_spec((len(POOL_WINDOWS), POOL_GROUP, POOL_GROUP), 0),
            _resident_spec((1, D_MODEL), 0),
        ] + _ffn_weight_specs(1, 1),
        out_specs=[_ctx_rows_spec(D_MODEL), _smp_rows_spec(D_MODEL)],
        out_shape=[jax.ShapeDtypeStruct((N_CTX, D_MODEL), F32),
                   jax.ShapeDtypeStruct((N_SMP, D_MODEL), F32)],
        scratch_shapes=[pltpu.VMEM((ROW_TILE, D_MODEL), F32)],
        compiler_params=_params(),
        name="pool_ffn_norm",
    )(x, mod, norm_g, norm_g, final_g, pool_w, pool_scale, w1, w2)


def _rope_tables():
    t = np.arange(DEC_SEQ)
    half = HEAD_DIM // 2
    inv = ROPE_THETA ** (-np.arange(0, half, 2, dtype=np.float64) / half)
    ang_row = (t // GRID_W)[:, None] * inv[None, :]
    ang_col = (t % GRID_W)[:, None] * inv[None, :]
    cos = np.concatenate([np.cos(ang_row), np.cos(ang_row), np.cos(ang_col), np.cos(ang_col)], axis=1)
    sin = np.concatenate([-np.sin(ang_row), np.sin(ang_row), -np.sin(ang_col), np.sin(ang_col)], axis=1)
    return (jnp.asarray(np.tile(cos, (1, N_HEADS)), F32), jnp.asarray(np.tile(sin, (1, N_HEADS)), F32))


def _head_segments():
    head = np.arange(ATTN_WIDTH) // HEAD_DIM
    return jnp.asarray((head[:, None] == head[None, :]) / HEAD_DIM, BF16)


def _cache_layout(t):
    return jnp.transpose(t.reshape(BATCH, 1, N_KV_HEADS, HEAD_DIM, SEQ), (0, 1, 4, 2, 3))


def kernel(x_prompt, x_sample, c, cache_k, cache_v, c_ctx, ada_w, ada_b, norm_g, ffn_w1, ffn_w2,
           mix_w_in, mix_w_out, q_norm, k_norm, conv_w, pool_w, pool_scale, final_g):
    cvec = jnp.concatenate(
        [c_ctx[None, :], c, jnp.zeros((MOD_ROWS - 1 - DEC_BATCH, D_MODEL), F32)], axis=0)
    mod = _mod_call(cvec, ada_w, ada_b)

    w1 = ffn_w1.astype(BF16)
    w2 = ffn_w2.astype(BF16)
    w_in = mix_w_in.astype(BF16)
    w_out = mix_w_out.astype(BF16)
    pw = pool_w.astype(BF16)
    gains = norm_g.reshape(norm_g.shape[0], 3, 1, D_MODEL)
    cos, sin = _rope_tables()

    x = _ffn_in_call(x_prompt.reshape(N_CTX, D_MODEL), x_sample.reshape(N_SMP, D_MODEL),
                     mod, gains, w1, w2)
    x, kt, vt = _mixer_call(
        x, mod, gains, w_in, jnp.tile(q_norm[0], N_HEADS)[None, :],
        jnp.tile(k_norm[0], N_KV_HEADS)[None, :], conv_w, _head_segments(), cos, sin,
        cache_k[:, 0].reshape(DEC_BATCH, PAST_LEN, KV_WIDTH),
        cache_v[:, 0].reshape(DEC_BATCH, PAST_LEN, KV_WIDTH), w_out)
    x = _ffn_call(x, mod, gains, w1, w2, 0, 1)
    x = _ffn_call(x, mod, gains, w1, w2, 1, 0)
    yp, ys = _pool_call(x, mod, gains, final_g[None, :], pw,
                        pool_scale.reshape(pool_scale.shape[0], 1, D_MODEL), w1, w2)

    return (yp.reshape(BATCH, SEQ, D_MODEL), ys.reshape(DEC_BATCH, DEC_SEQ, D_MODEL),
            _cache_layout(kt), _cache_layout(vt))
```

```python
import functools

import numpy as np
import jax
import jax.numpy as jnp
from jax import lax
from jax.experimental import pallas as pl
from jax.experimental.pallas import tpu as pltpu

F32 = jnp.float32
BF16 = jnp.bfloat16

D_MODEL = 1024
BATCH = 32
SEQ = 256
DEC_BATCH = 2
DEC_SEQ = 1024
PAST_LEN = 512
GRID_W = 64
N_HEADS = 8
N_KV_HEADS = 2
HEAD_DIM = 64
HEADS_PER_KV = N_HEADS // N_KV_HEADS
ATTN_WIDTH = N_HEADS * HEAD_DIM
KV_WIDTH = N_KV_HEADS * HEAD_DIM
CONV_WIDTH = D_MODEL // 2
MIX_WIDTH = ATTN_WIDTH + CONV_WIDTH
IN_WIDTH = ATTN_WIDTH + 2 * KV_WIDTH + 3 * CONV_WIDTH
D_FF = 2816
POOL_WINDOWS = (2, 4, 8, 16)
POOL_GROUP = D_MODEL // len(POOL_WINDOWS)
N_MOD = 9
ROPE_THETA = 10000.0
EPS = 1e-6

N_CTX = BATCH * SEQ
N_SMP = DEC_BATCH * DEC_SEQ
N_ALL = N_CTX + N_SMP
MIX_TILE = 1024
FFN_TILE = 512
FF_CHUNK = 256
Q_ROWS = 256
SEQ_PER_TILE = MIX_TILE // SEQ
MOD_ROWS = 8
MOD_WIDTH = N_MOD * D_MODEL
MOD_STEPS = 4
FFN_CAST_BLOCKS = 16
VMEM_LIMIT = 56 * 1024 * 1024


def _n_ctx_tiles(tile):
    return N_CTX // tile


def _mod_row(i, tile):
    first = _n_ctx_tiles(tile)
    return jnp.where(i < first, 0, 1 + (i - first) * tile // DEC_SEQ)


def _rms(x, g):
    return x * lax.rsqrt(jnp.mean(x * x, axis=-1, keepdims=True) + EPS) * g


def _mod_vec(mod_ref, r, j):
    return mod_ref[pl.ds(r, 1), j * D_MODEL:(j + 1) * D_MODEL]


def _modulate(x, g, mod_ref, r, j):
    return _rms(x, g) * (1 + _mod_vec(mod_ref, r, j + 1)) + _mod_vec(mod_ref, r, j)


def _ffn_rows(x, g, mod_ref, r, j, w1_ref, w2_ref):
    h = _modulate(x, g, mod_ref, r, j).astype(BF16)
    acc = None
    for c in range(D_FF // FF_CHUNK):
        lo = c * FF_CHUNK
        gate = jnp.dot(h, w1_ref[:, lo:lo + FF_CHUNK], preferred_element_type=F32)
        up = jnp.dot(h, w1_ref[:, D_FF + lo:D_FF + lo + FF_CHUNK], preferred_element_type=F32)
        act = (gate / (1 + jnp.exp(-gate)) * up).astype(BF16)
        y = jnp.dot(act, w2_ref[lo:lo + FF_CHUNK, :], preferred_element_type=F32)
        acc = y if acc is None else acc + y
    return x + (0.5 * _mod_vec(mod_ref, r, j + 2)) * acc


class _Cast:
    def __init__(self, src, lead, n_blocks):
        self.src, self.lead, self.n_blocks = src, tuple(lead), n_blocks
        self.rows, self.cols = src.shape[len(lead):]
        self.block_rows = self.rows // n_blocks

    def in_spec(self, step):
        lead, last = self.lead, self.n_blocks - 1
        return pl.BlockSpec((None,) * len(lead) + (self.block_rows, self.cols),
                            lambda *g: lead + (jnp.minimum(step(*g), last), 0))

    def out_spec(self, step):
        last = self.n_blocks - 1
        return pl.BlockSpec((self.block_rows, self.cols), lambda *g: (jnp.minimum(step(*g), last), 0))

    def out_shape(self):
        return jax.ShapeDtypeStruct((self.rows, self.cols), BF16)


def _cast_blocks(step, n_blocks, srcs, dsts):
    @pl.when(step < n_blocks)
    def _():
        for s, d in zip(srcs, dsts):
            d[...] = s[...].astype(BF16)


def _mod_kernel(c_ref, w_ref, b_ref, *refs, n_cast):
    srcs, o_ref, dsts = refs[:n_cast], refs[n_cast], refs[n_cast + 1:]
    c = c_ref[...]
    s = (c / (1 + jnp.exp(-c))).astype(BF16)
    o_ref[...] = jnp.dot(s, w_ref[...].astype(BF16), preferred_element_type=F32) + b_ref[...]
    _cast_blocks(pl.program_id(0) * MOD_STEPS + pl.program_id(1), MOD_STEPS * 2, srcs, dsts)


def _mod_call(cvec, ada_w, ada_b, casts):
    depth = ada_w.shape[0]
    cols = MOD_WIDTH // MOD_STEPS
    step = lambda l, j: l * MOD_STEPS + j
    out = pl.pallas_call(
        functools.partial(_mod_kernel, n_cast=len(casts)),
        grid=(depth, MOD_STEPS),
        in_specs=[
            pl.BlockSpec((MOD_ROWS, D_MODEL), lambda l, j: (0, 0)),
            pl.BlockSpec((None, D_MODEL, cols), lambda l, j: (l, 0, j)),
            pl.BlockSpec((None, 1, cols), lambda l, j: (l, 0, j)),
        ] + [c.in_spec(step) for c in casts],
        out_specs=[pl.BlockSpec((None, MOD_ROWS, cols), lambda l, j: (l, 0, j))]
        + [c.out_spec(step) for c in casts],
        out_shape=[jax.ShapeDtypeStruct((depth, MOD_ROWS, MOD_WIDTH), F32)]
        + [c.out_shape() for c in casts],
        compiler_params=pltpu.CompilerParams(
            dimension_semantics=("arbitrary", "arbitrary"), vmem_limit_bytes=VMEM_LIMIT),
        name="adaln_mod",
    )(cvec, ada_w, ada_b.reshape(depth, 1, MOD_WIDTH), *[c.src for c in casts])
    return out[0], out[1:]


def _rows_spec(tile, width):
    return pl.BlockSpec((tile, width), lambda i: (i, 0))


def _ctx_rows_spec(tile, width):
    last = _n_ctx_tiles(tile) - 1
    return pl.BlockSpec((tile, width), lambda i: (jnp.minimum(i, last), 0))


def _smp_rows_spec(tile, width):
    first = _n_ctx_tiles(tile)
    return pl.BlockSpec((tile, width), lambda i: (jnp.maximum(i - first, 0), 0))


def _mod_spec(layer):
    return pl.BlockSpec((None, MOD_ROWS, MOD_WIDTH), lambda i: (layer, 0, 0))


def _gain_spec(layer, j):
    return pl.BlockSpec((None, None, 1, D_MODEL), lambda i: (layer, j, 0, 0))


def _resident_spec(shape, *lead):
    index = tuple(lead) + (0,) * len(shape)
    return pl.BlockSpec((None,) * len(lead) + tuple(shape), lambda i: index,
                        pipeline_mode=pl.Buffered(1))


def _ffn_weight_specs():
    return [_resident_spec((D_MODEL, 2 * D_FF)), _resident_spec((D_FF, D_MODEL))]


def _params():
    return pltpu.CompilerParams(dimension_semantics=("arbitrary",), vmem_limit_bytes=VMEM_LIMIT)


def _ffn_kernel(*refs, j, n_in, n_cast):
    x_refs = refs[:n_in]
    mod_ref, g_ref, w1_ref, w2_ref = refs[n_in:n_in + 4]
    srcs = refs[n_in + 4:n_in + 4 + n_cast]
    o_ref = refs[n_in + 4 + n_cast]
    dsts = refs[n_in + 5 + n_cast:]
    i = pl.program_id(0)
    if n_in == 2:
        x = jnp.where(i < _n_ctx_tiles(FFN_TILE), x_refs[0][...], x_refs[1][...])
    else:
        x = x_refs[0][...]
    o_ref[...] = _ffn_rows(x, g_ref[...], mod_ref, _mod_row(i, FFN_TILE), j, w1_ref, w2_ref)
    _cast_blocks(i, FFN_CAST_BLOCKS, srcs, dsts)


def _ffn_call(xs, mod, norm_g, w1, w2, layer, which, casts=()):
    step = lambda i: i
    if len(xs) == 2:
        x_specs = [_ctx_rows_spec(FFN_TILE, D_MODEL), _smp_rows_spec(FFN_TILE, D_MODEL)]
    else:
        x_specs = [_rows_spec(FFN_TILE, D_MODEL)]
    out = pl.pallas_call(
        functools.partial(_ffn_kernel, j=6 * which, n_in=len(xs), n_cast=len(casts)),
        grid=(N_ALL // FFN_TILE,),
        in_specs=x_specs + [_mod_spec(layer), _gain_spec(layer, 2 * which)] + _ffn_weight_specs()
        + [c.in_spec(step) for c in casts],
        out_specs=[_rows_spec(FFN_TILE, D_MODEL)] + [c.out_spec(step) for c in casts],
        out_shape=[jax.ShapeDtypeStruct((N_ALL, D_MODEL), F32)] + [c.out_shape() for c in casts],
        compiler_params=_params(),
        name="ffn",
    )(*xs, mod, norm_g, w1, w2, *[c.src for c in casts])
    return out[0], out[1:]


def _head_mean_sq(x, seg):
    sq = x * x
    hi = sq.astype(BF16)
    lo = (sq - hi.astype(F32)).astype(BF16)
    return (jnp.dot(hi, seg, preferred_element_type=F32)
            + jnp.dot(lo, seg, preferred_element_type=F32))


def _rope(x, cos, sin_signed):
    width = x.shape[-1]
    lane = lax.broadcasted_iota(jnp.int32, (1, width), 1)
    partner = jnp.where((lane & 31) < 16,
                        pltpu.roll(x, width - 16, 1), pltpu.roll(x, 16, 1))
    return x * cos + partner * sin_signed


def _stack_heads(q_ref, rows, kvh):
    first = kvh * HEADS_PER_KV
    return jnp.concatenate(
        [q_ref[rows, (first + g) * HEAD_DIM:(first + g + 1) * HEAD_DIM] for g in range(HEADS_PER_KV)],
        axis=0)


def _scores(q, k):
    return lax.dot_general(q, k, (((1,), (1,)), ((), ())), preferred_element_type=F32)


def _unstack_heads(o, n):
    return [o[g * n:(g + 1) * n] for g in range(HEADS_PER_KV)]


def _attend_ctx(q_s, k_s, v_s, mix_s):
    for b in range(SEQ_PER_TILE):
        rows = slice(b * SEQ, (b + 1) * SEQ)
        outs = []
        for kvh in range(N_KV_HEADS):
            lanes = slice(kvh * HEAD_DIM, (kvh + 1) * HEAD_DIM)
            s = _scores(_stack_heads(q_s, rows, kvh), k_s[rows, lanes])
            e = jnp.exp(s - jnp.max(s, axis=-1, keepdims=True))
            p = (e / jnp.sum(e, axis=-1, keepdims=True)).astype(BF16)
            outs += _unstack_heads(jnp.dot(p, v_s[rows, lanes], preferred_element_type=F32), SEQ)
        mix_s[rows, :ATTN_WIDTH] = jnp.concatenate(outs, axis=1).astype(BF16)


def _attend_smp(q_s, k_s, v_s, ck_ref, cv_ref, mix_s):
    for t in range(MIX_TILE // Q_ROWS):
        rows = slice(t * Q_ROWS, (t + 1) * Q_ROWS)
        outs = []
        for kvh in range(N_KV_HEADS):
            lanes = slice(kvh * HEAD_DIM, (kvh + 1) * HEAD_DIM)
            q = _stack_heads(q_s, rows, kvh)
            s_ctx = _scores(q, ck_ref[:, lanes].astype(BF16))
            s_new = _scores(q, k_s[:, lanes])
            m = jnp.maximum(jnp.max(s_ctx, axis=-1, keepdims=True),
                            jnp.max(s_new, axis=-1, keepdims=True))
            e_ctx = jnp.exp(s_ctx - m)
            e_new = jnp.exp(s_new - m)
            den = jnp.sum(e_ctx, axis=-1, keepdims=True) + jnp.sum(e_new, axis=-1, keepdims=True)
            o = (jnp.dot((e_ctx / den).astype(BF16), cv_ref[:, lanes].astype(BF16),
                         preferred_element_type=F32)
                 + jnp.dot((e_new / den).astype(BF16), v_s[:, lanes], preferred_element_type=F32))
            outs += _unstack_heads(o, Q_ROWS)
        mix_s[rows, :ATTN_WIDTH] = jnp.concatenate(outs, axis=1).astype(BF16)


def _mixer_kernel(x_ref, mod_ref, g_ref, w_ref, qg_ref, kg_ref, cw_ref, seg_ref, cos_ref, sin_ref,
                  ck_ref, cv_ref, wo_ref, o_ref, kt_ref, vt_ref, q_s, k_s, v_s, mix_s):
    i = pl.program_id(0)
    r = _mod_row(i, MIX_TILE)
    is_ctx = i < _n_ctx_tiles(MIX_TILE)
    is_smp = jnp.logical_not(is_ctx)
    h = _modulate(x_ref[...], g_ref[...], mod_ref, r, 3).astype(BF16)

    def proj(lo, width):
        return jnp.dot(h, w_ref[:, lo:lo + width], preferred_element_type=F32)

    q = proj(0, ATTN_WIDTH)
    k = proj(ATTN_WIDTH, KV_WIDTH)
    v = proj(ATTN_WIDTH + KV_WIDTH, KV_WIDTH)
    v_s[...] = v.astype(BF16)
    q = q * lax.rsqrt(_head_mean_sq(q, seg_ref[...]) + EPS) * (qg_ref[...] * HEAD_DIM ** -0.5)
    k = k * lax.rsqrt(_head_mean_sq(k, seg_ref[:KV_WIDTH, :KV_WIDTH]) + EPS) * kg_ref[...]

    @pl.when(is_ctx)
    def _():
        q_s[...] = q.astype(BF16)
        k_s[...] = k.astype(BF16)
        for b in range(SEQ_PER_TILE):
            kt_ref[b] = k[b * SEQ:(b + 1) * SEQ, :].T
            vt_ref[b] = v[b * SEQ:(b + 1) * SEQ, :].T

    @pl.when(is_smp)
    def _():
        q_s[...] = _rope(q, cos_ref[...], sin_ref[...]).astype(BF16)
        k_s[...] = _rope(k, cos_ref[:, :KV_WIDTH], sin_ref[:, :KV_WIDTH]).astype(BF16)

    base = ATTN_WIDTH + 2 * KV_WIDTH
    z = proj(base + CONV_WIDTH, CONV_WIDTH) * proj(base + 2 * CONV_WIDTH, CONV_WIDTH)
    last = jnp.where(is_ctx, SEQ - 1, DEC_SEQ - 1)
    pos = lax.broadcasted_iota(jnp.int32, (MIX_TILE, 1), 0) & last
    z_prev = jnp.where(pos == 0, 0.0, pltpu.roll(z, 1, 0))
    z_next = jnp.where(pos == last, 0.0, pltpu.roll(z, MIX_TILE - 1, 0))
    conv = z_prev * cw_ref[0:1, :] + z * cw_ref[1:2, :] + z_next * cw_ref[2:3, :]
    mix_s[:, ATTN_WIDTH:] = (proj(base, CONV_WIDTH) * conv).astype(BF16)

    @pl.when(is_ctx)
    def _():
        _attend_ctx(q_s, k_s, v_s, mix_s)

    @pl.when(is_smp)
    def _():
        _attend_smp(q_s, k_s, v_s, ck_ref, cv_ref, mix_s)

    out = jnp.dot(mix_s[...], wo_ref[...], preferred_element_type=F32)
    o_ref[...] = x_ref[...] + _mod_vec(mod_ref, r, 5) * out


def _mixer_call(x, mod, norm_g, w_in, qg, kg, conv_w, seg, cos, sin, cache_k, cache_v, w_out):
    first_smp = _n_ctx_tiles(MIX_TILE)
    cache_spec = pl.BlockSpec((None, PAST_LEN, KV_WIDTH), lambda i: (jnp.maximum(i - first_smp, 0), 0, 0))
    kvt_spec = pl.BlockSpec((SEQ_PER_TILE, KV_WIDTH, SEQ), lambda i: (jnp.minimum(i, first_smp - 1), 0, 0))
    kvt_shape = jax.ShapeDtypeStruct((BATCH, KV_WIDTH, SEQ), F32)
    return pl.pallas_call(
        _mixer_kernel,
        grid=(N_ALL // MIX_TILE,),
        in_specs=[
            _rows_spec(MIX_TILE, D_MODEL), _mod_spec(0), _gain_spec(0, 1),
            _resident_spec((D_MODEL, IN_WIDTH)),
            _resident_spec((1, ATTN_WIDTH)), _resident_spec((1, KV_WIDTH)),
            _resident_spec((3, CONV_WIDTH), 0),
            _resident_spec((ATTN_WIDTH, ATTN_WIDTH)),
            _resident_spec((DEC_SEQ, ATTN_WIDTH)), _resident_spec((DEC_SEQ, ATTN_WIDTH)),
            cache_spec, cache_spec,
            _resident_spec((MIX_WIDTH, D_MODEL)),
        ],
        out_specs=[_rows_spec(MIX_TILE, D_MODEL), kvt_spec, kvt_spec],
        out_shape=[jax.ShapeDtypeStruct((N_ALL, D_MODEL), F32), kvt_shape, kvt_shape],
        scratch_shapes=[pltpu.VMEM((MIX_TILE, ATTN_WIDTH), BF16),
                        pltpu.VMEM((MIX_TILE, KV_WIDTH), BF16),
                        pltpu.VMEM((MIX_TILE, KV_WIDTH), BF16),
                        pltpu.VMEM((MIX_TILE, MIX_WIDTH), BF16)],
        compiler_params=_params(),
        name="mixer",
    )(x, mod, norm_g, w_in, qg, kg, conv_w, seg, cos, sin, cache_k, cache_v, w_out)


def _shift_rows(a, n, pos, seq_len):
    if n > 0:
        return jnp.where(pos < n, 0.0, pltpu.roll(a, n, 0))
    return jnp.where(pos >= seq_len + n, 0.0, pltpu.roll(a, a.shape[0] + n, 0))


def _pool_kernel(x_ref, mod_ref, gm_ref, gf_ref, gn_ref, pw_ref, ps_ref, w1_ref, w2_ref,
                 yp_ref, ys_ref, x2_ref):
    i = pl.program_id(0)
    r = _mod_row(i, MIX_TILE)
    is_ctx = i < _n_ctx_tiles(MIX_TILE)
    seq_len = jnp.where(is_ctx, SEQ, DEC_SEQ)
    pos = lax.broadcasted_iota(jnp.int32, (MIX_TILE, 1), 0) & (seq_len - 1)
    gate = _mod_vec(mod_ref, r, 5)
    x = x_ref[...]
    h_all = _modulate(x, gm_ref[...], mod_ref, r, 3)

    for gi, w in enumerate(POOL_WINDOWS):
        lanes = slice(gi * POOL_GROUP, (gi + 1) * POOL_GROUP)
        h = h_all[:, lanes]
        back, fwd, n = h, h, 1
        while n < w // 2:
            back = back + _shift_rows(back, n, pos, seq_len)
            fwd = fwd + _shift_rows(fwd, -n, pos, seq_len)
            n *= 2
        total = _shift_rows(back, 1, pos, seq_len) + fwd
        left = w // 2
        right = w - 1 - left
        count = jnp.minimum(pos + right + 1, seq_len) - jnp.maximum(pos - left, 0)
        diff = (total / count.astype(F32) - h).astype(BF16)
        out = jnp.dot(diff, pw_ref[lanes, :], preferred_element_type=F32) * ps_ref[:, lanes]
        x2_ref[:, lanes] = x[:, lanes] + gate[:, lanes] * out

    def ffn_step(s, carry):
        rows = pl.ds(pl.multiple_of(s * FFN_TILE, FFN_TILE), FFN_TILE)
        y = _ffn_rows(x2_ref[rows, :], gf_ref[...], mod_ref, r, 6, w1_ref, w2_ref)
        x2_ref[rows, :] = _rms(y, gn_ref[...])
        return carry

    lax.fori_loop(0, MIX_TILE // FFN_TILE, ffn_step, 0)

    @pl.when(is_ctx)
    def _():
        yp_ref[...] = x2_ref[...]

    @pl.when(jnp.logical_not(is_ctx))
    def _():
        ys_ref[...] = x2_ref[...]


def _pool_call(x, mod, norm_g, final_g, pool_w, pool_scale, w1, w2):
    return pl.pallas_call(
        _pool_kernel,
        grid=(N_ALL // MIX_TILE,),
        in_specs=[
            _rows_spec(MIX_TILE, D_MODEL), _mod_spec(1), _gain_spec(1, 1), _gain_spec(1, 2),
            _resident_spec((1, D_MODEL)),
            _resident_spec((D_MODEL, POOL_GROUP)),
            _resident_spec((1, D_MODEL), 0),
        ] + _ffn_weight_specs(),
        out_specs=[_ctx_rows_spec(MIX_TILE, D_MODEL), _smp_rows_spec(MIX_TILE, D_MODEL)],
        out_shape=[jax.ShapeDtypeStruct((N_CTX, D_MODEL), F32),
                   jax.ShapeDtypeStruct((N_SMP, D_MODEL), F32)],
        scratch_shapes=[pltpu.VMEM((MIX_TILE, D_MODEL), F32)],
        compiler_params=_params(),
        name="pool_ffn_norm",
    )(x, mod, norm_g, norm_g, final_g, pool_w, pool_scale, w1, w2)


def _rope_tables():
    t = np.arange(DEC_SEQ)
    half = HEAD_DIM // 2
    inv = ROPE_THETA ** (-np.arange(0, half, 2, dtype=np.float64) / half)
    ang_row = (t // GRID_W)[:, None] * inv[None, :]
    ang_col = (t % GRID_W)[:, None] * inv[None, :]
    cos = np.concatenate([np.cos(ang_row), np.cos(ang_row), np.cos(ang_col), np.cos(ang_col)], axis=1)
    sin = np.concatenate([-np.sin(ang_row), np.sin(ang_row), -np.sin(ang_col), np.sin(ang_col)], axis=1)
    return (jnp.asarray(np.tile(cos, (1, N_HEADS)), F32), jnp.asarray(np.tile(sin, (1, N_HEADS)), F32))


def _head_segments():
    head = np.arange(ATTN_WIDTH) // HEAD_DIM
    return jnp.asarray((head[:, None] == head[None, :]) / HEAD_DIM, BF16)


def _cache_layout(t):
    return jnp.transpose(t.reshape(BATCH, 1, N_KV_HEADS, HEAD_DIM, SEQ), (0, 1, 4, 2, 3))


def kernel(x_prompt, x_sample, c, cache_k, cache_v, c_ctx, ada_w, ada_b, norm_g, ffn_w1, ffn_w2,
           mix_w_in, mix_w_out, q_norm, k_norm, conv_w, pool_w, pool_scale, final_g):
    cvec = jnp.concatenate(
        [c_ctx[None, :], c, jnp.zeros((MOD_ROWS - 1 - DEC_BATCH, D_MODEL), F32)], axis=0)
    gains = norm_g.reshape(norm_g.shape[0], 3, 1, D_MODEL)
    cos, sin = _rope_tables()

    def ffn_casts(layer, which, n_blocks):
        return [_Cast(ffn_w1, (layer, which), n_blocks), _Cast(ffn_w2, (layer, which), n_blocks)]

    mod, (w1_00, w2_00) = _mod_call(cvec, ada_w, ada_b, ffn_casts(0, 0, 2 * MOD_STEPS))
    x, (w1_01, w2_01, w1_10, w2_10, w_in, w_out) = _ffn_call(
        [x_prompt.reshape(N_CTX, D_MODEL), x_sample.reshape(N_SMP, D_MODEL)],
        mod, gains, w1_00, w2_00, 0, 0,
        ffn_casts(0, 1, FFN_CAST_BLOCKS) + ffn_casts(1, 0, FFN_CAST_BLOCKS)
        + [_Cast(mix_w_in, (0,), FFN_CAST_BLOCKS), _Cast(mix_w_out, (0,), FFN_CAST_BLOCKS)])
    x, kt, vt = _mixer_call(
        x, mod, gains, w_in, jnp.tile(q_norm[0], N_HEADS)[None, :],
        jnp.tile(k_norm[0], N_KV_HEADS)[None, :], conv_w, _head_segments(), cos, sin,
        cache_k[:, 0].reshape(DEC_BATCH, PAST_LEN, KV_WIDTH),
        cache_v[:, 0].reshape(DEC_BATCH, PAST_LEN, KV_WIDTH), w_out)
    x, (w1_11, w2_11, pw) = _ffn_call(
        [x], mod, gains, w1_01, w2_01, 0, 1,
        ffn_casts(1, 1, FFN_CAST_BLOCKS)
        + [_Cast(pool_w.reshape(pool_w.shape[0], D_MODEL, POOL_GROUP), (0,), FFN_CAST_BLOCKS)])
    x, _ = _ffn_call([x], mod, gains, w1_10, w2_10, 1, 0)
    yp, ys = _pool_call(x, mod, gains, final_g[None, :], pw,
                        pool_scale.reshape(pool_scale.shape[0], 1, D_MODEL), w1_11, w2_11)

    return (yp.reshape(BATCH, SEQ, D_MODEL), ys.reshape(DEC_BATCH, DEC_SEQ, D_MODEL),
            _cache_layout(kt), _cache_layout(vt))
```

```python
import functools

import numpy as np
import jax
import jax.numpy as jnp
from jax import lax
from jax.experimental import pallas as pl
from jax.experimental.pallas import tpu as pltpu

F32 = jnp.float32
BF16 = jnp.bfloat16

D_MODEL = 1024
BATCH = 32
SEQ = 256
DEC_BATCH = 2
DEC_SEQ = 1024
PAST_LEN = 512
GRID_W = 64
N_HEADS = 8
N_KV_HEADS = 2
HEAD_DIM = 64
HEADS_PER_KV = N_HEADS // N_KV_HEADS
ATTN_WIDTH = N_HEADS * HEAD_DIM
KV_WIDTH = N_KV_HEADS * HEAD_DIM
CONV_WIDTH = D_MODEL // 2
MIX_WIDTH = ATTN_WIDTH + CONV_WIDTH
IN_WIDTH = ATTN_WIDTH + 2 * KV_WIDTH + 3 * CONV_WIDTH
D_FF = 2816
POOL_WINDOWS = (2, 4, 8, 16)
POOL_GROUP = D_MODEL // len(POOL_WINDOWS)
N_MOD = 9
ROPE_THETA = 10000.0
EPS = 1e-6

N_CTX = BATCH * SEQ
N_SMP = DEC_BATCH * DEC_SEQ
N_ALL = N_CTX + N_SMP
MIX_TILE = 1024
FFN_TILE = 512
FF_CHUNK = 256
Q_ROWS = 256
SEQ_PER_TILE = MIX_TILE // SEQ
MOD_ROWS = 8
MOD_WIDTH = N_MOD * D_MODEL
MOD_STEPS = 4
FFN_CAST_BLOCKS = 16
VMEM_LIMIT = 56 * 1024 * 1024


def _n_ctx_tiles(tile):
    return N_CTX // tile


def _mod_row(i, tile):
    first = _n_ctx_tiles(tile)
    return jnp.where(i < first, 0, 1 + (i - first) * tile // DEC_SEQ)


def _rms(x, g):
    return x * lax.rsqrt(jnp.mean(x * x, axis=-1, keepdims=True) + EPS) * g


def _mod_vec(mod_ref, r, j):
    return mod_ref[pl.ds(r, 1), j * D_MODEL:(j + 1) * D_MODEL]


def _modulate(x, g, mod_ref, r, j):
    return _rms(x, g) * (1 + _mod_vec(mod_ref, r, j + 1)) + _mod_vec(mod_ref, r, j)


def _ffn_rows(x, g, mod_ref, r, j, w1_ref, w2_ref):
    h = _modulate(x, g, mod_ref, r, j).astype(BF16)
    acc = None
    for c in range(D_FF // FF_CHUNK):
        lo = c * FF_CHUNK
        gate = jnp.dot(h, w1_ref[:, lo:lo + FF_CHUNK], preferred_element_type=F32)
        up = jnp.dot(h, w1_ref[:, D_FF + lo:D_FF + lo + FF_CHUNK], preferred_element_type=F32)
        act = (gate / (1 + jnp.exp(-gate)) * up).astype(BF16)
        y = jnp.dot(act, w2_ref[lo:lo + FF_CHUNK, :], preferred_element_type=F32)
        acc = y if acc is None else acc + y
    return x + (0.5 * _mod_vec(mod_ref, r, j + 2)) * acc


class _Cast:
    def __init__(self, src, lead, n_blocks):
        self.src, self.lead, self.n_blocks = src, tuple(lead), n_blocks
        self.rows, self.cols = src.shape[len(lead):]
        self.block_rows = self.rows // n_blocks

    def in_spec(self, step):
        lead, last = self.lead, self.n_blocks - 1
        return pl.BlockSpec((None,) * len(lead) + (self.block_rows, self.cols),
                            lambda *g: lead + (jnp.minimum(step(*g), last), 0))

    def out_spec(self, step):
        last = self.n_blocks - 1
        return pl.BlockSpec((self.block_rows, self.cols), lambda *g: (jnp.minimum(step(*g), last), 0))

    def out_shape(self):
        return jax.ShapeDtypeStruct((self.rows, self.cols), BF16)


def _cast_blocks(step, n_blocks, srcs, dsts):
    @pl.when(step < n_blocks)
    def _():
        for s, d in zip(srcs, dsts):
            d[...] = s[...].astype(BF16)


def _mod_kernel(c_ref, w_ref, b_ref, *refs, n_cast):
    srcs, o_ref, dsts = refs[:n_cast], refs[n_cast], refs[n_cast + 1:]
    c = c_ref[...]
    s = (c / (1 + jnp.exp(-c))).astype(BF16)
    o_ref[...] = jnp.dot(s, w_ref[...].astype(BF16), preferred_element_type=F32) + b_ref[...]
    _cast_blocks(pl.program_id(0) * MOD_STEPS + pl.program_id(1), MOD_STEPS * 2, srcs, dsts)


def _mod_call(cvec, ada_w, ada_b, casts):
    depth = ada_w.shape[0]
    cols = MOD_WIDTH // MOD_STEPS
    step = lambda l, j: l * MOD_STEPS + j
    out = pl.pallas_call(
        functools.partial(_mod_kernel, n_cast=len(casts)),
        grid=(depth, MOD_STEPS),
        in_specs=[
            pl.BlockSpec((MOD_ROWS, D_MODEL), lambda l, j: (0, 0)),
            pl.BlockSpec((None, D_MODEL, cols), lambda l, j: (l, 0, j)),
            pl.BlockSpec((None, 1, cols), lambda l, j: (l, 0, j)),
        ] + [c.in_spec(step) for c in casts],
        out_specs=[pl.BlockSpec((None, MOD_ROWS, cols), lambda l, j: (l, 0, j))]
        + [c.out_spec(step) for c in casts],
        out_shape=[jax.ShapeDtypeStruct((depth, MOD_ROWS, MOD_WIDTH), F32)]
        + [c.out_shape() for c in casts],
        compiler_params=pltpu.CompilerParams(
            dimension_semantics=("arbitrary", "arbitrary"), vmem_limit_bytes=VMEM_LIMIT),
        name="adaln_mod",
    )(cvec, ada_w, ada_b.reshape(depth, 1, MOD_WIDTH), *[c.src for c in casts])
    return out[0], out[1:]


def _rows_spec(tile, width):
    return pl.BlockSpec((tile, width), lambda i: (i, 0))


def _ctx_rows_spec(tile, width):
    last = _n_ctx_tiles(tile) - 1
    return pl.BlockSpec((tile, width), lambda i: (jnp.minimum(i, last), 0))


def _smp_rows_spec(tile, width):
    first = _n_ctx_tiles(tile)
    return pl.BlockSpec((tile, width), lambda i: (jnp.maximum(i - first, 0), 0))


def _mod_spec(layer):
    return pl.BlockSpec((None, MOD_ROWS, MOD_WIDTH), lambda i: (layer, 0, 0))


def _gain_spec(layer, j):
    return pl.BlockSpec((None, None, 1, D_MODEL), lambda i: (layer, j, 0, 0))


def _resident_spec(shape, *lead):
    index = tuple(lead) + (0,) * len(shape)
    return pl.BlockSpec((None,) * len(lead) + tuple(shape), lambda i: index,
                        pipeline_mode=pl.Buffered(1))


def _ffn_weight_specs():
    return [_resident_spec((D_MODEL, 2 * D_FF)), _resident_spec((D_FF, D_MODEL))]


def _params():
    return pltpu.CompilerParams(dimension_semantics=("arbitrary",), vmem_limit_bytes=VMEM_LIMIT)


def _ffn_kernel(*refs, j, n_in, n_cast):
    x_refs = refs[:n_in]
    mod_ref, g_ref, w1_ref, w2_ref = refs[n_in:n_in + 4]
    srcs = refs[n_in + 4:n_in + 4 + n_cast]
    o_ref = refs[n_in + 4 + n_cast]
    dsts = refs[n_in + 5 + n_cast:]
    i = pl.program_id(0)
    if n_in == 2:
        x = jnp.where(i < _n_ctx_tiles(FFN_TILE), x_refs[0][...], x_refs[1][...])
    else:
        x = x_refs[0][...]
    o_ref[...] = _ffn_rows(x, g_ref[...], mod_ref, _mod_row(i, FFN_TILE), j, w1_ref, w2_ref)
    _cast_blocks(i, FFN_CAST_BLOCKS, srcs, dsts)


def _ffn_call(xs, mod, norm_g, w1, w2, layer, which, casts=()):
    step = lambda i: i
    if len(xs) == 2:
        x_specs = [_ctx_rows_spec(FFN_TILE, D_MODEL), _smp_rows_spec(FFN_TILE, D_MODEL)]
    else:
        x_specs = [_rows_spec(FFN_TILE, D_MODEL)]
    out = pl.pallas_call(
        functools.partial(_ffn_kernel, j=6 * which, n_in=len(xs), n_cast=len(casts)),
        grid=(N_ALL // FFN_TILE,),
        in_specs=x_specs + [_mod_spec(layer), _gain_spec(layer, 2 * which)] + _ffn_weight_specs()
        + [c.in_spec(step) for c in casts],
        out_specs=[_rows_spec(FFN_TILE, D_MODEL)] + [c.out_spec(step) for c in casts],
        out_shape=[jax.ShapeDtypeStruct((N_ALL, D_MODEL), F32)] + [c.out_shape() for c in casts],
        compiler_params=_params(),
        name="ffn",
    )(*xs, mod, norm_g, w1, w2, *[c.src for c in casts])
    return out[0], out[1:]


def _head_mean_sq(x, seg):
    sq = x * x
    hi = sq.astype(BF16)
    lo = (sq - hi.astype(F32)).astype(BF16)
    return (jnp.dot(hi, seg, preferred_element_type=F32)
            + jnp.dot(lo, seg, preferred_element_type=F32))


def _rope(x, cos, sin_signed):
    width = x.shape[-1]
    lane = lax.broadcasted_iota(jnp.int32, (1, width), 1)
    partner = jnp.where((lane & 31) < 16,
                        pltpu.roll(x, width - 16, 1), pltpu.roll(x, 16, 1))
    return x * cos + partner * sin_signed


def _stack_heads(q_ref, rows, kvh):
    first = kvh * HEADS_PER_KV
    return jnp.concatenate(
        [q_ref[rows, (first + g) * HEAD_DIM:(first + g + 1) * HEAD_DIM] for g in range(HEADS_PER_KV)],
        axis=0)


def _scores(q, k):
    return lax.dot_general(q, k, (((1,), (1,)), ((), ())), preferred_element_type=F32)


def _unstack_heads(o, n):
    return [o[g * n:(g + 1) * n] for g in range(HEADS_PER_KV)]


def _attend_ctx(q_s, k_s, v_s, mix_s):
    for b in range(SEQ_PER_TILE):
        rows = slice(b * SEQ, (b + 1) * SEQ)
        outs = []
        for kvh in range(N_KV_HEADS):
            lanes = slice(kvh * HEAD_DIM, (kvh + 1) * HEAD_DIM)
            s = _scores(_stack_heads(q_s, rows, kvh), k_s[rows, lanes])
            e = jnp.exp(s - jnp.max(s, axis=-1, keepdims=True))
            p = (e / jnp.sum(e, axis=-1, keepdims=True)).astype(BF16)
            outs += _unstack_heads(jnp.dot(p, v_s[rows, lanes], preferred_element_type=F32), SEQ)
        mix_s[rows, :ATTN_WIDTH] = jnp.concatenate(outs, axis=1).astype(BF16)


def _attend_smp(q_s, k_s, v_s, ck_ref, cv_ref, mix_s):
    for t in range(MIX_TILE // Q_ROWS):
        rows = slice(t * Q_ROWS, (t + 1) * Q_ROWS)
        outs = []
        for kvh in range(N_KV_HEADS):
            lanes = slice(kvh * HEAD_DIM, (kvh + 1) * HEAD_DIM)
            q = _stack_heads(q_s, rows, kvh)
            s_ctx = _scores(q, ck_ref[:, lanes].astype(BF16))
            s_new = _scores(q, k_s[:, lanes])
            m = jnp.maximum(jnp.max(s_ctx, axis=-1, keepdims=True),
                            jnp.max(s_new, axis=-1, keepdims=True))
            e_ctx = jnp.exp(s_ctx - m)
            e_new = jnp.exp(s_new - m)
            den = jnp.sum(e_ctx, axis=-1, keepdims=True) + jnp.sum(e_new, axis=-1, keepdims=True)
            o = (jnp.dot((e_ctx / den).astype(BF16), cv_ref[:, lanes].astype(BF16),
                         preferred_element_type=F32)
                 + jnp.dot((e_new / den).astype(BF16), v_s[:, lanes], preferred_element_type=F32))
            outs += _unstack_heads(o, Q_ROWS)
        mix_s[rows, :ATTN_WIDTH] = jnp.concatenate(outs, axis=1).astype(BF16)


def _mixer_tile(ctx, r, x_ref, mod_ref, g_ref, w_ref, qg_ref, kg_ref, cw_ref, seg_ref, cos_ref, sin_ref,
                ck_ref, cv_ref, wo_ref, o_ref, kt_ref, vt_ref, q_s, k_s, v_s, mix_s):
    seq_len = SEQ if ctx else DEC_SEQ
    x = x_ref[...]
    h = _modulate(x, g_ref[...], mod_ref, r, 3).astype(BF16)

    def proj(lo, width):
        return jnp.dot(h, w_ref[:, lo:lo + width], preferred_element_type=F32)

    q = proj(0, ATTN_WIDTH)
    k = proj(ATTN_WIDTH, KV_WIDTH)
    v = proj(ATTN_WIDTH + KV_WIDTH, KV_WIDTH)
    v_s[...] = v.astype(BF16)
    q = q * lax.rsqrt(_head_mean_sq(q, seg_ref[...]) + EPS) * (qg_ref[...] * HEAD_DIM ** -0.5)
    k = k * lax.rsqrt(_head_mean_sq(k, seg_ref[:KV_WIDTH, :KV_WIDTH]) + EPS) * kg_ref[...]
    if ctx:
        q_s[...] = q.astype(BF16)
        k_s[...] = k.astype(BF16)
        for b in range(SEQ_PER_TILE):
            kt_ref[b] = k[b * SEQ:(b + 1) * SEQ, :].T
            vt_ref[b] = v[b * SEQ:(b + 1) * SEQ, :].T
    else:
        q_s[...] = _rope(q, cos_ref[...], sin_ref[...]).astype(BF16)
        k_s[...] = _rope(k, cos_ref[:, :KV_WIDTH], sin_ref[:, :KV_WIDTH]).astype(BF16)

    base = ATTN_WIDTH + 2 * KV_WIDTH
    z = proj(base + CONV_WIDTH, CONV_WIDTH) * proj(base + 2 * CONV_WIDTH, CONV_WIDTH)
    pos = lax.broadcasted_iota(jnp.int32, (MIX_TILE, 1), 0) & (seq_len - 1)
    z_prev = jnp.where(pos == 0, 0.0, pltpu.roll(z, 1, 0))
    z_next = jnp.where(pos == seq_len - 1, 0.0, pltpu.roll(z, MIX_TILE - 1, 0))
    conv = z_prev * cw_ref[0:1, :] + z * cw_ref[1:2, :] + z_next * cw_ref[2:3, :]
    mix_s[:, ATTN_WIDTH:] = (proj(base, CONV_WIDTH) * conv).astype(BF16)

    if ctx:
        _attend_ctx(q_s, k_s, v_s, mix_s)
    else:
        _attend_smp(q_s, k_s, v_s, ck_ref, cv_ref, mix_s)

    out = (jnp.dot(mix_s[:, :ATTN_WIDTH], wo_ref[:ATTN_WIDTH, :], preferred_element_type=F32)
           + jnp.dot(mix_s[:, ATTN_WIDTH:], wo_ref[ATTN_WIDTH:, :], preferred_element_type=F32))
    o_ref[...] = x + _mod_vec(mod_ref, r, 5) * out


def _mixer_kernel(*refs):
    i = pl.program_id(0)
    r = _mod_row(i, MIX_TILE)
    is_ctx = i < _n_ctx_tiles(MIX_TILE)

    @pl.when(is_ctx)
    def _():
        _mixer_tile(True, r, *refs)

    @pl.when(jnp.logical_not(is_ctx))
    def _():
        _mixer_tile(False, r, *refs)


def _mixer_call(x, mod, norm_g, w_in, qg, kg, conv_w, seg, cos, sin, cache_k, cache_v, w_out):
    first_smp = _n_ctx_tiles(MIX_TILE)
    cache_spec = pl.BlockSpec((None, PAST_LEN, KV_WIDTH), lambda i: (jnp.maximum(i - first_smp, 0), 0, 0))
    kvt_spec = pl.BlockSpec((SEQ_PER_TILE, KV_WIDTH, SEQ), lambda i: (jnp.minimum(i, first_smp - 1), 0, 0))
    kvt_shape = jax.ShapeDtypeStruct((BATCH, KV_WIDTH, SEQ), F32)
    return pl.pallas_call(
        _mixer_kernel,
        grid=(N_ALL // MIX_TILE,),
        in_specs=[
            _rows_spec(MIX_TILE, D_MODEL), _mod_spec(0), _gain_spec(0, 1),
            _resident_spec((D_MODEL, IN_WIDTH)),
            _resident_spec((1, ATTN_WIDTH)), _resident_spec((1, KV_WIDTH)),
            _resident_spec((3, CONV_WIDTH), 0),
            _resident_spec((ATTN_WIDTH, ATTN_WIDTH)),
            _resident_spec((DEC_SEQ, ATTN_WIDTH)), _resident_spec((DEC_SEQ, ATTN_WIDTH)),
            cache_spec, cache_spec,
            _resident_spec((MIX_WIDTH, D_MODEL)),
        ],
        out_specs=[_rows_spec(MIX_TILE, D_MODEL), kvt_spec, kvt_spec],
        out_shape=[jax.ShapeDtypeStruct((N_ALL, D_MODEL), F32), kvt_shape, kvt_shape],
        scratch_shapes=[pltpu.VMEM((MIX_TILE, ATTN_WIDTH), BF16),
                        pltpu.VMEM((MIX_TILE, KV_WIDTH), BF16),
                        pltpu.VMEM((MIX_TILE, KV_WIDTH), BF16),
                        pltpu.VMEM((MIX_TILE, MIX_WIDTH), BF16)],
        compiler_params=_params(),
        name="mixer",
    )(x, mod, norm_g, w_in, qg, kg, conv_w, seg, cos, sin, cache_k, cache_v, w_out)


def _shift_rows(a, n, pos, seq_len):
    if n > 0:
        return jnp.where(pos < n, 0.0, pltpu.roll(a, n, 0))
    return jnp.where(pos >= seq_len + n, 0.0, pltpu.roll(a, a.shape[0] + n, 0))


def _pool_kernel(x_ref, mod_ref, gm_ref, gf_ref, gn_ref, pw_ref, ps_ref, w1_ref, w2_ref,
                 yp_ref, ys_ref, x2_ref):
    i = pl.program_id(0)
    r = _mod_row(i, MIX_TILE)
    is_ctx = i < _n_ctx_tiles(MIX_TILE)
    seq_len = jnp.where(is_ctx, SEQ, DEC_SEQ)
    pos = lax.broadcasted_iota(jnp.int32, (MIX_TILE, 1), 0) & (seq_len - 1)
    gate = _mod_vec(mod_ref, r, 5)
    x = x_ref[...]
    h_all = _modulate(x, gm_ref[...], mod_ref, r, 3)

    for gi, w in enumerate(POOL_WINDOWS):
        lanes = slice(gi * POOL_GROUP, (gi + 1) * POOL_GROUP)
        h = h_all[:, lanes]
        back, fwd, n = h, h, 1
        while n < w // 2:
            back = back + _shift_rows(back, n, pos, seq_len)
            fwd = fwd + _shift_rows(fwd, -n, pos, seq_len)
            n *= 2
        total = _shift_rows(back, 1, pos, seq_len) + fwd
        left = w // 2
        right = w - 1 - left
        count = jnp.minimum(pos + right + 1, seq_len) - jnp.maximum(pos - left, 0)
        diff = (total / count.astype(F32) - h).astype(BF16)
        out = jnp.dot(diff, pw_ref[lanes, :], preferred_element_type=F32) * ps_ref[:, lanes]
        x2_ref[:, lanes] = x[:, lanes] + gate[:, lanes] * out

    def ffn_step(s, carry):
        rows = pl.ds(pl.multiple_of(s * FFN_TILE, FFN_TILE), FFN_TILE)
        y = _ffn_rows(x2_ref[rows, :], gf_ref[...], mod_ref, r, 6, w1_ref, w2_ref)
        x2_ref[rows, :] = _rms(y, gn_ref[...])
        return carry

    lax.fori_loop(0, MIX_TILE // FFN_TILE, ffn_step, 0)

    @pl.when(is_ctx)
    def _():
        yp_ref[...] = x2_ref[...]

    @pl.when(jnp.logical_not(is_ctx))
    def _():
        ys_ref[...] = x2_ref[...]


def _pool_call(x, mod, norm_g, final_g, pool_w, pool_scale, w1, w2):
    return pl.pallas_call(
        _pool_kernel,
        grid=(N_ALL // MIX_TILE,),
        in_specs=[
            _rows_spec(MIX_TILE, D_MODEL), _mod_spec(1), _gain_spec(1, 1), _gain_spec(1, 2),
            _resident_spec((1, D_MODEL)),
            _resident_spec((D_MODEL, POOL_GROUP)),
            _resident_spec((1, D_MODEL), 0),
        ] + _ffn_weight_specs(),
        out_specs=[_ctx_rows_spec(MIX_TILE, D_MODEL), _smp_rows_spec(MIX_TILE, D_MODEL)],
        out_shape=[jax.ShapeDtypeStruct((N_CTX, D_MODEL), F32),
                   jax.ShapeDtypeStruct((N_SMP, D_MODEL), F32)],
        scratch_shapes=[pltpu.VMEM((MIX_TILE, D_MODEL), F32)],
        compiler_params=_params(),
        name="pool_ffn_norm",
    )(x, mod, norm_g, norm_g, final_g, pool_w, pool_scale, w1, w2)


def _rope_tables():
    t = np.arange(DEC_SEQ)
    half = HEAD_DIM // 2
    inv = ROPE_THETA ** (-np.arange(0, half, 2, dtype=np.float64) / half)
    ang_row = (t // GRID_W)[:, None] * inv[None, :]
    ang_col = (t % GRID_W)[:, None] * inv[None, :]
    cos = np.concatenate([np.cos(ang_row), np.cos(ang_row), np.cos(ang_col), np.cos(ang_col)], axis=1)
    sin = np.concatenate([-np.sin(ang_row), np.sin(ang_row), -np.sin(ang_col), np.sin(ang_col)], axis=1)
    return (jnp.asarray(np.tile(cos, (1, N_HEADS)), F32), jnp.asarray(np.tile(sin, (1, N_HEADS)), F32))


def _head_segments():
    head = np.arange(ATTN_WIDTH) // HEAD_DIM
    return jnp.asarray((head[:, None] == head[None, :]) / HEAD_DIM, BF16)


def _cache_layout(t):
    return jnp.transpose(t.reshape(BATCH, 1, N_KV_HEADS, HEAD_DIM, SEQ), (0, 1, 4, 2, 3))


def kernel(x_prompt, x_sample, c, cache_k, cache_v, c_ctx, ada_w, ada_b, norm_g, ffn_w1, ffn_w2,
           mix_w_in, mix_w_out, q_norm, k_norm, conv_w, pool_w, pool_scale, final_g):
    cvec = jnp.concatenate(
        [c_ctx[None, :], c, jnp.zeros((MOD_ROWS - 1 - DEC_BATCH, D_MODEL), F32)], axis=0)
    gains = norm_g.reshape(norm_g.shape[0], 3, 1, D_MODEL)
    cos, sin = _rope_tables()

    def ffn_casts(layer, which, n_blocks):
        return [_Cast(ffn_w1, (layer, which), n_blocks), _Cast(ffn_w2, (layer, which), n_blocks)]

    mod, (w1_00, w2_00) = _mod_call(cvec, ada_w, ada_b, ffn_casts(0, 0, 2 * MOD_STEPS))
    x, (w1_01, w2_01, w1_10, w2_10, w_in, w_out) = _ffn_call(
        [x_prompt.reshape(N_CTX, D_MODEL), x_sample.reshape(N_SMP, D_MODEL)],
        mod, gains, w1_00, w2_00, 0, 0,
        ffn_casts(0, 1, FFN_CAST_BLOCKS) + ffn_casts(1, 0, FFN_CAST_BLOCKS)
        + [_Cast(mix_w_in, (0,), FFN_CAST_BLOCKS), _Cast(mix_w_out, (0,), FFN_CAST_BLOCKS)])
    x, kt, vt = _mixer_call(
        x, mod, gains, w_in, jnp.tile(q_norm[0], N_HEADS)[None, :],
        jnp.tile(k_norm[0], N_KV_HEADS)[None, :], conv_w, _head_segments(), cos, sin,
        cache_k[:, 0].reshape(DEC_BATCH, PAST_LEN, KV_WIDTH),
        cache_v[:, 0].reshape(DEC_BATCH, PAST_LEN, KV_WIDTH), w_out)
    x, (w1_11, w2_11, pw) = _ffn_call(
        [x], mod, gains, w1_01, w2_01, 0, 1,
        ffn_casts(1, 1, FFN_CAST_BLOCKS)
        + [_Cast(pool_w.reshape(pool_w.shape[0], D_MODEL, POOL_GROUP), (0,), FFN_CAST_BLOCKS)])
    x, _ = _ffn_call([x], mod, gains, w1_10, w2_10, 1, 0)
    yp, ys = _pool_call(x, mod, gains, final_g[None, :], pw,
                        pool_scale.reshape(pool_scale.shape[0], 1, D_MODEL), w1_11, w2_11)

    return (yp.reshape(BATCH, SEQ, D_MODEL), ys.reshape(DEC_BATCH, DEC_SEQ, D_MODEL),
            _cache_layout(kt), _cache_layout(vt))
```

```python
import functools

import numpy as np
import jax
import jax.numpy as jnp
from jax import lax
from jax.experimental import pallas as pl
from jax.experimental.pallas import tpu as pltpu

F32 = jnp.float32
BF16 = jnp.bfloat16

D_MODEL = 1024
BATCH = 32
SEQ = 256
DEC_BATCH = 2
DEC_SEQ = 1024
PAST_LEN = 512
GRID_W = 64
N_HEADS = 8
N_KV_HEADS = 2
HEAD_DIM = 64
HEADS_PER_KV = N_HEADS // N_KV_HEADS
ATTN_WIDTH = N_HEADS * HEAD_DIM
KV_WIDTH = N_KV_HEADS * HEAD_DIM
CONV_WIDTH = D_MODEL // 2
MIX_WIDTH = ATTN_WIDTH + CONV_WIDTH
IN_WIDTH = ATTN_WIDTH + 2 * KV_WIDTH + 3 * CONV_WIDTH
D_FF = 2816
POOL_WINDOWS = (2, 4, 8, 16)
POOL_GROUP = D_MODEL // len(POOL_WINDOWS)
N_MOD = 9
ROPE_THETA = 10000.0
EPS = 1e-6

N_CTX = BATCH * SEQ
N_SMP = DEC_BATCH * DEC_SEQ
N_ALL = N_CTX + N_SMP
MIX_TILE = 1024
FFN_TILE = 512
FF_CHUNK = 256
Q_ROWS = 256
SEQ_PER_TILE = MIX_TILE // SEQ
MOD_ROWS = 8
MOD_WIDTH = N_MOD * D_MODEL
MOD_STEPS = 4
FFN_CAST_BLOCKS = 16
VMEM_LIMIT = 56 * 1024 * 1024


def _n_ctx_tiles(tile):
    return N_CTX // tile


def _mod_row(i, tile):
    first = _n_ctx_tiles(tile)
    return jnp.where(i < first, 0, 1 + (i - first) * tile // DEC_SEQ)


def _rms(x, g):
    return x * lax.rsqrt(jnp.mean(x * x, axis=-1, keepdims=True) + EPS) * g


def _mod_vec(mod_ref, r, j):
    return mod_ref[pl.ds(r, 1), j * D_MODEL:(j + 1) * D_MODEL]


def _modulate(x, g, mod_ref, r, j):
    return _rms(x, g) * (1 + _mod_vec(mod_ref, r, j + 1)) + _mod_vec(mod_ref, r, j)


def _ffn_rows(x, g, mod_ref, r, j, w1_ref, w2_ref):
    h = _modulate(x, g, mod_ref, r, j).astype(BF16)
    acc = None
    for c in range(D_FF // FF_CHUNK):
        lo = c * FF_CHUNK
        gate = jnp.dot(h, w1_ref[:, lo:lo + FF_CHUNK], preferred_element_type=F32)
        up = jnp.dot(h, w1_ref[:, D_FF + lo:D_FF + lo + FF_CHUNK], preferred_element_type=F32)
        act = (gate / (1 + jnp.exp(-gate)) * up).astype(BF16)
        y = jnp.dot(act, w2_ref[lo:lo + FF_CHUNK, :], preferred_element_type=F32)
        acc = y if acc is None else acc + y
    return x + (0.5 * _mod_vec(mod_ref, r, j + 2)) * acc


class _Cast:
    def __init__(self, src, lead, n_blocks):
        self.src, self.lead, self.n_blocks = src, tuple(lead), n_blocks
        self.rows, self.cols = src.shape[len(lead):]
        self.block_rows = self.rows // n_blocks

    def in_spec(self, step):
        lead, last = self.lead, self.n_blocks - 1
        return pl.BlockSpec((None,) * len(lead) + (self.block_rows, self.cols),
                            lambda *g: lead + (jnp.minimum(step(*g), last), 0))

    def out_spec(self, step):
        last = self.n_blocks - 1
        return pl.BlockSpec((self.block_rows, self.cols), lambda *g: (jnp.minimum(step(*g), last), 0))

    def out_shape(self):
        return jax.ShapeDtypeStruct((self.rows, self.cols), BF16)


def _cast_blocks(step, n_blocks, srcs, dsts):
    @pl.when(step < n_blocks)
    def _():
        for s, d in zip(srcs, dsts):
            d[...] = s[...].astype(BF16)


def _mod_kernel(c_ref, w_ref, b_ref, *refs, n_cast):
    srcs, o_ref, dsts = refs[:n_cast], refs[n_cast], refs[n_cast + 1:]
    c = c_ref[...]
    s = (c / (1 + jnp.exp(-c))).astype(BF16)
    o_ref[...] = jnp.dot(s, w_ref[...].astype(BF16), preferred_element_type=F32) + b_ref[...]
    _cast_blocks(pl.program_id(0) * MOD_STEPS + pl.program_id(1), MOD_STEPS * 2, srcs, dsts)


def _mod_call(cvec, ada_w, ada_b, casts):
    depth = ada_w.shape[0]
    cols = MOD_WIDTH // MOD_STEPS
    step = lambda l, j: l * MOD_STEPS + j
    out = pl.pallas_call(
        functools.partial(_mod_kernel, n_cast=len(casts)),
        grid=(depth, MOD_STEPS),
        in_specs=[
            pl.BlockSpec((MOD_ROWS, D_MODEL), lambda l, j: (0, 0)),
            pl.BlockSpec((None, D_MODEL, cols), lambda l, j: (l, 0, j)),
            pl.BlockSpec((None, 1, cols), lambda l, j: (l, 0, j)),
        ] + [c.in_spec(step) for c in casts],
        out_specs=[pl.BlockSpec((None, MOD_ROWS, cols), lambda l, j: (l, 0, j))]
        + [c.out_spec(step) for c in casts],
        out_shape=[jax.ShapeDtypeStruct((depth, MOD_ROWS, MOD_WIDTH), F32)]
        + [c.out_shape() for c in casts],
        compiler_params=pltpu.CompilerParams(
            dimension_semantics=("arbitrary", "arbitrary"), vmem_limit_bytes=VMEM_LIMIT),
        name="adaln_mod",
    )(cvec, ada_w, ada_b.reshape(depth, 1, MOD_WIDTH), *[c.src for c in casts])
    return out[0], out[1:]


def _rows_spec(tile, width):
    return pl.BlockSpec((tile, width), lambda i: (i, 0))


def _ctx_rows_spec(tile, width):
    last = _n_ctx_tiles(tile) - 1
    return pl.BlockSpec((tile, width), lambda i: (jnp.minimum(i, last), 0))


def _smp_rows_spec(tile, width):
    first = _n_ctx_tiles(tile)
    return pl.BlockSpec((tile, width), lambda i: (jnp.maximum(i - first, 0), 0))


def _mod_spec(layer):
    return pl.BlockSpec((None, MOD_ROWS, MOD_WIDTH), lambda i: (layer, 0, 0))


def _gain_spec(layer, j):
    return pl.BlockSpec((None, None, 1, D_MODEL), lambda i: (layer, j, 0, 0))


def _resident_spec(shape, *lead):
    index = tuple(lead) + (0,) * len(shape)
    return pl.BlockSpec((None,) * len(lead) + tuple(shape), lambda i: index,
                        pipeline_mode=pl.Buffered(1))


def _ffn_weight_specs():
    return [_resident_spec((D_MODEL, 2 * D_FF)), _resident_spec((D_FF, D_MODEL))]


def _params():
    return pltpu.CompilerParams(dimension_semantics=("arbitrary",), vmem_limit_bytes=VMEM_LIMIT)


def _ffn_kernel(*refs, j, n_in, n_cast):
    x_refs = refs[:n_in]
    mod_ref, g_ref, w1_ref, w2_ref = refs[n_in:n_in + 4]
    srcs = refs[n_in + 4:n_in + 4 + n_cast]
    o_ref = refs[n_in + 4 + n_cast]
    dsts = refs[n_in + 5 + n_cast:]
    i = pl.program_id(0)
    if n_in == 2:
        x = jnp.where(i < _n_ctx_tiles(FFN_TILE), x_refs[0][...], x_refs[1][...])
    else:
        x = x_refs[0][...]
    o_ref[...] = _ffn_rows(x, g_ref[...], mod_ref, _mod_row(i, FFN_TILE), j, w1_ref, w2_ref)
    _cast_blocks(i, FFN_CAST_BLOCKS, srcs, dsts)


def _ffn_call(xs, mod, norm_g, w1, w2, layer, which, casts=()):
    step = lambda i: i
    if len(xs) == 2:
        x_specs = [_ctx_rows_spec(FFN_TILE, D_MODEL), _smp_rows_spec(FFN_TILE, D_MODEL)]
    else:
        x_specs = [_rows_spec(FFN_TILE, D_MODEL)]
    out = pl.pallas_call(
        functools.partial(_ffn_kernel, j=6 * which, n_in=len(xs), n_cast=len(casts)),
        grid=(N_ALL // FFN_TILE,),
        in_specs=x_specs + [_mod_spec(layer), _gain_spec(layer, 2 * which)] + _ffn_weight_specs()
        + [c.in_spec(step) for c in casts],
        out_specs=[_rows_spec(FFN_TILE, D_MODEL)] + [c.out_spec(step) for c in casts],
        out_shape=[jax.ShapeDtypeStruct((N_ALL, D_MODEL), F32)] + [c.out_shape() for c in casts],
        compiler_params=_params(),
        name="ffn",
    )(*xs, mod, norm_g, w1, w2, *[c.src for c in casts])
    return out[0], out[1:]


def _head_mean_sq(x, seg):
    sq = x * x
    hi = sq.astype(BF16)
    lo = (sq - hi.astype(F32)).astype(BF16)
    return (jnp.dot(hi, seg, preferred_element_type=F32)
            + jnp.dot(lo, seg, preferred_element_type=F32))


def _rope(x, cos, sin_signed):
    width = x.shape[-1]
    lane = lax.broadcasted_iota(jnp.int32, (1, width), 1)
    partner = jnp.where((lane & 31) < 16,
                        pltpu.roll(x, width - 16, 1), pltpu.roll(x, 16, 1))
    return x * cos + partner * sin_signed


def _stack_heads(q_ref, rows, kvh):
    first = kvh * HEADS_PER_KV
    return jnp.concatenate(
        [q_ref[rows, (first + g) * HEAD_DIM:(first + g + 1) * HEAD_DIM] for g in range(HEADS_PER_KV)],
        axis=0)


ONES_ROWS = 16


def _attend_group(q, k, v1t):
    st = lax.dot_general(k, q, (((1,), (1,)), ((), ())), preferred_element_type=F32)
    e = jnp.exp(st - jnp.max(st, axis=0, keepdims=True)).astype(BF16)
    ot = jnp.dot(v1t, e, preferred_element_type=F32)
    return ot[:HEAD_DIM] / ot[HEAD_DIM:HEAD_DIM + 1]


def _attend_rows(q_s, rows, n, kv, mix_s):
    heads = []
    for kvh in range(N_KV_HEADS):
        k, v1t = kv(kvh)
        ot = _attend_group(_stack_heads(q_s, rows, kvh), k, v1t)
        heads += [ot[:, g * n:(g + 1) * n] for g in range(HEADS_PER_KV)]
    mix_s[rows, :ATTN_WIDTH] = jnp.concatenate(heads, axis=0).T.astype(BF16)


def _with_ones(vt):
    return jnp.concatenate([vt, jnp.ones((ONES_ROWS, vt.shape[1]), BF16)], axis=0)


def _mixer_tile(ctx, r, x_ref, mod_ref, g_ref, w_ref, qg_ref, kg_ref, cw_ref, seg_ref, cos_ref, sin_ref,
                ck_ref, cvt_ref, wo_ref, o_ref, kt_ref, vt_ref, q_s, k_s, vt_s, mix_s):
    seq_len = SEQ if ctx else DEC_SEQ
    x = x_ref[...]
    h = _modulate(x, g_ref[...], mod_ref, r, 3).astype(BF16)

    def proj(lo, width):
        return jnp.dot(h, w_ref[:, lo:lo + width], preferred_element_type=F32)

    q = proj(0, ATTN_WIDTH)
    k = proj(ATTN_WIDTH, KV_WIDTH)
    v = proj(ATTN_WIDTH + KV_WIDTH, KV_WIDTH)
    q = q * lax.rsqrt(_head_mean_sq(q, seg_ref[...]) + EPS) * (qg_ref[...] * HEAD_DIM ** -0.5)
    k = k * lax.rsqrt(_head_mean_sq(k, seg_ref[:KV_WIDTH, :KV_WIDTH]) + EPS) * kg_ref[...]
    if ctx:
        q_s[...] = q.astype(BF16)
        k_s[...] = k.astype(BF16)
        for b in range(SEQ_PER_TILE):
            vt = v[b * SEQ:(b + 1) * SEQ, :].T
            kt_ref[b] = k[b * SEQ:(b + 1) * SEQ, :].T
            vt_ref[b] = vt
            vt_s[:, b * SEQ:(b + 1) * SEQ] = vt.astype(BF16)
    else:
        q_s[...] = _rope(q, cos_ref[...], sin_ref[...]).astype(BF16)
        k_s[...] = _rope(k, cos_ref[:, :KV_WIDTH], sin_ref[:, :KV_WIDTH]).astype(BF16)
        vt_s[...] = v.T.astype(BF16)

    base = ATTN_WIDTH + 2 * KV_WIDTH
    z = proj(base + CONV_WIDTH, CONV_WIDTH) * proj(base + 2 * CONV_WIDTH, CONV_WIDTH)
    pos = lax.broadcasted_iota(jnp.int32, (MIX_TILE, 1), 0) & (seq_len - 1)
    z_prev = jnp.where(pos == 0, 0.0, pltpu.roll(z, 1, 0))
    z_next = jnp.where(pos == seq_len - 1, 0.0, pltpu.roll(z, MIX_TILE - 1, 0))
    conv = z_prev * cw_ref[0:1, :] + z * cw_ref[1:2, :] + z_next * cw_ref[2:3, :]
    mix_s[:, ATTN_WIDTH:] = (proj(base, CONV_WIDTH) * conv).astype(BF16)

    def head(kvh):
        return slice(kvh * HEAD_DIM, (kvh + 1) * HEAD_DIM)

    if ctx:
        for b in range(SEQ_PER_TILE):
            rows = slice(b * SEQ, (b + 1) * SEQ)
            _attend_rows(q_s, rows, SEQ,
                         lambda kvh: (k_s[rows, head(kvh)], _with_ones(vt_s[head(kvh), rows])), mix_s)
    else:
        kv = [(jnp.concatenate([ck_ref[:, head(kvh)].astype(BF16), k_s[:, head(kvh)]], axis=0),
               _with_ones(jnp.concatenate([cvt_ref[head(kvh), :].astype(BF16), vt_s[head(kvh), :]],
                                          axis=1)))
              for kvh in range(N_KV_HEADS)]
        for t in range(MIX_TILE // Q_ROWS):
            _attend_rows(q_s, slice(t * Q_ROWS, (t + 1) * Q_ROWS), Q_ROWS, lambda kvh: kv[kvh], mix_s)

    out = (jnp.dot(mix_s[:, :ATTN_WIDTH], wo_ref[:ATTN_WIDTH, :], preferred_element_type=F32)
           + jnp.dot(mix_s[:, ATTN_WIDTH:], wo_ref[ATTN_WIDTH:, :], preferred_element_type=F32))
    o_ref[...] = x + _mod_vec(mod_ref, r, 5) * out


def _mixer_kernel(*refs):
    i = pl.program_id(0)
    r = _mod_row(i, MIX_TILE)
    is_ctx = i < _n_ctx_tiles(MIX_TILE)

    @pl.when(is_ctx)
    def _():
        _mixer_tile(True, r, *refs)

    @pl.when(jnp.logical_not(is_ctx))
    def _():
        _mixer_tile(False, r, *refs)


def _mixer_call(x, mod, norm_g, w_in, qg, kg, conv_w, seg, cos, sin, cache_k, cache_vt, w_out):
    first_smp = _n_ctx_tiles(MIX_TILE)
    smp_batch = lambda i: (jnp.maximum(i - first_smp, 0), 0, 0)
    kvt_spec = pl.BlockSpec((SEQ_PER_TILE, KV_WIDTH, SEQ), lambda i: (jnp.minimum(i, first_smp - 1), 0, 0))
    kvt_shape = jax.ShapeDtypeStruct((BATCH, KV_WIDTH, SEQ), F32)
    return pl.pallas_call(
        _mixer_kernel,
        grid=(N_ALL // MIX_TILE,),
        in_specs=[
            _rows_spec(MIX_TILE, D_MODEL), _mod_spec(0), _gain_spec(0, 1),
            _resident_spec((D_MODEL, IN_WIDTH)),
            _resident_spec((1, ATTN_WIDTH)), _resident_spec((1, KV_WIDTH)),
            _resident_spec((3, CONV_WIDTH), 0),
            _resident_spec((ATTN_WIDTH, ATTN_WIDTH)),
            _resident_spec((DEC_SEQ, ATTN_WIDTH)), _resident_spec((DEC_SEQ, ATTN_WIDTH)),
            pl.BlockSpec((None, PAST_LEN, KV_WIDTH), smp_batch),
            pl.BlockSpec((None, KV_WIDTH, PAST_LEN), smp_batch),
            _resident_spec((MIX_WIDTH, D_MODEL)),
        ],
        out_specs=[_rows_spec(MIX_TILE, D_MODEL), kvt_spec, kvt_spec],
        out_shape=[jax.ShapeDtypeStruct((N_ALL, D_MODEL), F32), kvt_shape, kvt_shape],
        scratch_shapes=[pltpu.VMEM((MIX_TILE, ATTN_WIDTH), BF16),
                        pltpu.VMEM((MIX_TILE, KV_WIDTH), BF16),
                        pltpu.VMEM((KV_WIDTH, MIX_TILE), BF16),
                        pltpu.VMEM((MIX_TILE, MIX_WIDTH), BF16)],
        compiler_params=_params(),
        name="mixer",
    )(x, mod, norm_g, w_in, qg, kg, conv_w, seg, cos, sin, cache_k, cache_vt, w_out)


def _shift_rows(a, n, pos, seq_len):
    if n > 0:
        return jnp.where(pos < n, 0.0, pltpu.roll(a, n, 0))
    return jnp.where(pos >= seq_len + n, 0.0, pltpu.roll(a, a.shape[0] + n, 0))


def _pool_kernel(x_ref, mod_ref, gm_ref, gf_ref, gn_ref, pw_ref, ps_ref, w1_ref, w2_ref,
                 yp_ref, ys_ref, x2_ref):
    i = pl.program_id(0)
    r = _mod_row(i, MIX_TILE)
    is_ctx = i < _n_ctx_tiles(MIX_TILE)
    seq_len = jnp.where(is_ctx, SEQ, DEC_SEQ)
    pos = lax.broadcasted_iota(jnp.int32, (MIX_TILE, 1), 0) & (seq_len - 1)
    gate = _mod_vec(mod_ref, r, 5)
    x = x_ref[...]
    h_all = _modulate(x, gm_ref[...], mod_ref, r, 3)

    for gi, w in enumerate(POOL_WINDOWS):
        lanes = slice(gi * POOL_GROUP, (gi + 1) * POOL_GROUP)
        h = h_all[:, lanes]
        back, fwd, n = h, h, 1
        while n < w // 2:
            back = back + _shift_rows(back, n, pos, seq_len)
            fwd = fwd + _shift_rows(fwd, -n, pos, seq_len)
            n *= 2
        total = _shift_rows(back, 1, pos, seq_len) + fwd
        left = w // 2
        right = w - 1 - left
        count = jnp.minimum(pos + right + 1, seq_len) - jnp.maximum(pos - left, 0)
        diff = (total / count.astype(F32) - h).astype(BF16)
        out = jnp.dot(diff, pw_ref[lanes, :], preferred_element_type=F32) * ps_ref[:, lanes]
        x2_ref[:, lanes] = x[:, lanes] + gate[:, lanes] * out

    def ffn_step(s, carry):
        rows = pl.ds(pl.multiple_of(s * FFN_TILE, FFN_TILE), FFN_TILE)
        y = _ffn_rows(x2_ref[rows, :], gf_ref[...], mod_ref, r, 6, w1_ref, w2_ref)
        x2_ref[rows, :] = _rms(y, gn_ref[...])
        return carry

    lax.fori_loop(0, MIX_TILE // FFN_TILE, ffn_step, 0)

    @pl.when(is_ctx)
    def _():
        yp_ref[...] = x2_ref[...]

    @pl.when(jnp.logical_not(is_ctx))
    def _():
        ys_ref[...] = x2_ref[...]


def _pool_call(x, mod, norm_g, final_g, pool_w, pool_scale, w1, w2):
    return pl.pallas_call(
        _pool_kernel,
        grid=(N_ALL // MIX_TILE,),
        in_specs=[
            _rows_spec(MIX_TILE, D_MODEL), _mod_spec(1), _gain_spec(1, 1), _gain_spec(1, 2),
            _resident_spec((1, D_MODEL)),
            _resident_spec((D_MODEL, POOL_GROUP)),
            _resident_spec((1, D_MODEL), 0),
        ] + _ffn_weight_specs(),
        out_specs=[_ctx_rows_spec(MIX_TILE, D_MODEL), _smp_rows_spec(MIX_TILE, D_MODEL)],
        out_shape=[jax.ShapeDtypeStruct((N_CTX, D_MODEL), F32),
                   jax.ShapeDtypeStruct((N_SMP, D_MODEL), F32)],
        scratch_shapes=[pltpu.VMEM((MIX_TILE, D_MODEL), F32)],
        compiler_params=_params(),
        name="pool_ffn_norm",
    )(x, mod, norm_g, norm_g, final_g, pool_w, pool_scale, w1, w2)


def _rope_tables():
    t = np.arange(DEC_SEQ)
    half = HEAD_DIM // 2
    inv = ROPE_THETA ** (-np.arange(0, half, 2, dtype=np.float64) / half)
    ang_row = (t // GRID_W)[:, None] * inv[None, :]
    ang_col = (t % GRID_W)[:, None] * inv[None, :]
    cos = np.concatenate([np.cos(ang_row), np.cos(ang_row), np.cos(ang_col), np.cos(ang_col)], axis=1)
    sin = np.concatenate([-np.sin(ang_row), np.sin(ang_row), -np.sin(ang_col), np.sin(ang_col)], axis=1)
    return (jnp.asarray(np.tile(cos, (1, N_HEADS)), F32), jnp.asarray(np.tile(sin, (1, N_HEADS)), F32))


def _head_segments():
    head = np.arange(ATTN_WIDTH) // HEAD_DIM
    return jnp.asarray((head[:, None] == head[None, :]) / HEAD_DIM, BF16)


def _cache_layout(t):
    return jnp.transpose(t.reshape(BATCH, 1, N_KV_HEADS, HEAD_DIM, SEQ), (0, 1, 4, 2, 3))


def kernel(x_prompt, x_sample, c, cache_k, cache_v, c_ctx, ada_w, ada_b, norm_g, ffn_w1, ffn_w2,
           mix_w_in, mix_w_out, q_norm, k_norm, conv_w, pool_w, pool_scale, final_g):
    cvec = jnp.concatenate(
        [c_ctx[None, :], c, jnp.zeros((MOD_ROWS - 1 - DEC_BATCH, D_MODEL), F32)], axis=0)
    gains = norm_g.reshape(norm_g.shape[0], 3, 1, D_MODEL)
    cos, sin = _rope_tables()

    def ffn_casts(layer, which, n_blocks):
        return [_Cast(ffn_w1, (layer, which), n_blocks), _Cast(ffn_w2, (layer, which), n_blocks)]

    mod, (w1_00, w2_00) = _mod_call(cvec, ada_w, ada_b, ffn_casts(0, 0, 2 * MOD_STEPS))
    x, (w1_01, w2_01, w1_10, w2_10, w_in, w_out) = _ffn_call(
        [x_prompt.reshape(N_CTX, D_MODEL), x_sample.reshape(N_SMP, D_MODEL)],
        mod, gains, w1_00, w2_00, 0, 0,
        ffn_casts(0, 1, FFN_CAST_BLOCKS) + ffn_casts(1, 0, FFN_CAST_BLOCKS)
        + [_Cast(mix_w_in, (0,), FFN_CAST_BLOCKS), _Cast(mix_w_out, (0,), FFN_CAST_BLOCKS)])
    x, kt, vt = _mixer_call(
        x, mod, gains, w_in, jnp.tile(q_norm[0], N_HEADS)[None, :],
        jnp.tile(k_norm[0], N_KV_HEADS)[None, :], conv_w, _head_segments(), cos, sin,
        cache_k[:, 0].reshape(DEC_BATCH, PAST_LEN, KV_WIDTH),
        jnp.transpose(cache_v[:, 0], (0, 2, 3, 1)).reshape(DEC_BATCH, KV_WIDTH, PAST_LEN), w_out)
    x, (w1_11, w2_11, pw) = _ffn_call(
        [x], mod, gains, w1_01, w2_01, 0, 1,
        ffn_casts(1, 1, FFN_CAST_BLOCKS)
        + [_Cast(pool_w.reshape(pool_w.shape[0], D_MODEL, POOL_GROUP), (0,), FFN_CAST_BLOCKS)])
    x, _ = _ffn_call([x], mod, gains, w1_10, w2_10, 1, 0)
    yp, ys = _pool_call(x, mod, gains, final_g[None, :], pw,
                        pool_scale.reshape(pool_scale.shape[0], 1, D_MODEL), w1_11, w2_11)

    return (yp.reshape(BATCH, SEQ, D_MODEL), ys.reshape(DEC_BATCH, DEC_SEQ, D_MODEL),
            _cache_layout(kt), _cache_layout(vt))
```

```python
import functools

import numpy as np
import jax
import jax.numpy as jnp
from jax import lax
from jax.experimental import pallas as pl
from jax.experimental.pallas import tpu as pltpu

F32 = jnp.float32
BF16 = jnp.bfloat16

D_MODEL = 1024
BATCH = 32
SEQ = 256
DEC_BATCH = 2
DEC_SEQ = 1024
PAST_LEN = 512
GRID_W = 64
N_HEADS = 8
N_KV_HEADS = 2
HEAD_DIM = 64
HEADS_PER_KV = N_HEADS // N_KV_HEADS
ATTN_WIDTH = N_HEADS * HEAD_DIM
KV_WIDTH = N_KV_HEADS * HEAD_DIM
CONV_WIDTH = D_MODEL // 2
MIX_WIDTH = ATTN_WIDTH + CONV_WIDTH
IN_WIDTH = ATTN_WIDTH + 2 * KV_WIDTH + 3 * CONV_WIDTH
D_FF = 2816
POOL_WINDOWS = (2, 4, 8, 16)
POOL_GROUP = D_MODEL // len(POOL_WINDOWS)
N_MOD = 9
ROPE_THETA = 10000.0
EPS = 1e-6

N_CTX = BATCH * SEQ
N_SMP = DEC_BATCH * DEC_SEQ
N_ALL = N_CTX + N_SMP
MIX_TILE = 1024
FFN_TILE = 512
FF_CHUNK = 256
Q_ROWS = 256
SEQ_PER_TILE = MIX_TILE // SEQ
MOD_ROWS = 8
MOD_WIDTH = N_MOD * D_MODEL
MOD_STEPS = 4
MOD_SIDE_BLOCKS = 18
FFN_CAST_BLOCKS = 16
VMEM_LIMIT = 56 * 1024 * 1024


def _n_ctx_tiles(tile):
    return N_CTX // tile


def _mod_row(i, tile):
    first = _n_ctx_tiles(tile)
    return jnp.where(i < first, 0, 1 + (i - first) * tile // DEC_SEQ)


def _rms(x, g):
    return x * lax.rsqrt(jnp.mean(x * x, axis=-1, keepdims=True) + EPS) * g


def _mod_vec(mod_ref, r, j):
    return mod_ref[pl.ds(r, 1), j * D_MODEL:(j + 1) * D_MODEL]


def _modulate(x, g, mod_ref, r, j):
    return _rms(x, g) * (1 + _mod_vec(mod_ref, r, j + 1)) + _mod_vec(mod_ref, r, j)


def _ffn_rows(x, g, mod_ref, r, j, w1_ref, w2_ref):
    h = _modulate(x, g, mod_ref, r, j).astype(BF16)
    acc = None
    for c in range(D_FF // FF_CHUNK):
        lo = c * FF_CHUNK
        gate = jnp.dot(h, w1_ref[:, lo:lo + FF_CHUNK], preferred_element_type=F32)
        up = jnp.dot(h, w1_ref[:, D_FF + lo:D_FF + lo + FF_CHUNK], preferred_element_type=F32)
        act = (gate / (1 + jnp.exp(-gate)) * up).astype(BF16)
        y = jnp.dot(act, w2_ref[lo:lo + FF_CHUNK, :], preferred_element_type=F32)
        acc = y if acc is None else acc + y
    return x + (0.5 * _mod_vec(mod_ref, r, j + 2)) * acc


class _Cast:
    def __init__(self, src, lead, n_blocks):
        self.src, self.lead, self.n_blocks = src, tuple(lead), n_blocks
        self.rows, self.cols = src.shape[len(lead):]
        self.block_rows = self.rows // n_blocks

    def in_spec(self, step):
        lead, last = self.lead, self.n_blocks - 1
        return pl.BlockSpec((None,) * len(lead) + (self.block_rows, self.cols),
                            lambda *g: lead + (jnp.minimum(step(*g), last), 0))

    def out_spec(self, step):
        last = self.n_blocks - 1
        return pl.BlockSpec((self.block_rows, self.cols), lambda *g: (jnp.minimum(step(*g), last), 0))

    def out_shape(self):
        return jax.ShapeDtypeStruct((self.rows, self.cols), BF16)


def _cast_blocks(step, n_blocks, srcs, dsts):
    @pl.when(step < n_blocks)
    def _():
        for s, d in zip(srcs, dsts):
            d[...] = s[...].astype(BF16)


def _mod_block(c_ref, w_ref, b_ref):
    c = c_ref[...]
    s = (c / (1 + jnp.exp(-c))).astype(BF16)
    return jnp.dot(s, w_ref[...].astype(BF16), preferred_element_type=F32) + b_ref[...]


class _ModJob:
    def __init__(self, cvec, ada_w, ada_b, layer, n_blocks):
        self.args = (cvec, ada_w, ada_b.reshape(ada_b.shape[0], 1, MOD_WIDTH))
        self.layer, self.n_blocks, self.cols = layer, n_blocks, MOD_WIDTH // n_blocks

    def in_specs(self):
        layer, last, cols = self.layer, self.n_blocks - 1, self.cols
        block = lambda i: (layer, 0, jnp.minimum(i, last))
        return [pl.BlockSpec((MOD_ROWS, D_MODEL), lambda i: (0, 0)),
                pl.BlockSpec((None, D_MODEL, cols), block),
                pl.BlockSpec((None, 1, cols), block)]

    def out_spec(self):
        last = self.n_blocks - 1
        return pl.BlockSpec((MOD_ROWS, self.cols), lambda i: (0, jnp.minimum(i, last)))

    def out_shape(self):
        return jax.ShapeDtypeStruct((MOD_ROWS, MOD_WIDTH), F32)


def _mod_kernel(c_ref, w_ref, b_ref, *refs, n_cast):
    srcs, o_ref, dsts = refs[:n_cast], refs[n_cast], refs[n_cast + 1:]
    o_ref[...] = _mod_block(c_ref, w_ref, b_ref)
    _cast_blocks(pl.program_id(0), MOD_STEPS, srcs, dsts)


def _mod_call(job, casts):
    step = lambda i: i
    out = pl.pallas_call(
        functools.partial(_mod_kernel, n_cast=len(casts)),
        grid=(job.n_blocks,),
        in_specs=job.in_specs() + [c.in_spec(step) for c in casts],
        out_specs=[job.out_spec()] + [c.out_spec(step) for c in casts],
        out_shape=[job.out_shape()] + [c.out_shape() for c in casts],
        compiler_params=_params(),
        name="adaln_mod",
    )(*job.args, *[c.src for c in casts])
    return out[0], out[1:]


def _rows_spec(tile, width):
    return pl.BlockSpec((tile, width), lambda i: (i, 0))


def _ctx_rows_spec(tile, width):
    last = _n_ctx_tiles(tile) - 1
    return pl.BlockSpec((tile, width), lambda i: (jnp.minimum(i, last), 0))


def _smp_rows_spec(tile, width):
    first = _n_ctx_tiles(tile)
    return pl.BlockSpec((tile, width), lambda i: (jnp.maximum(i - first, 0), 0))


def _mod_spec():
    return pl.BlockSpec((MOD_ROWS, MOD_WIDTH), lambda i: (0, 0))


def _gain_spec(layer, j):
    return pl.BlockSpec((None, None, 1, D_MODEL), lambda i: (layer, j, 0, 0))


def _resident_spec(shape, *lead):
    index = tuple(lead) + (0,) * len(shape)
    return pl.BlockSpec((None,) * len(lead) + tuple(shape), lambda i: index,
                        pipeline_mode=pl.Buffered(1))


def _ffn_weight_specs():
    return [_resident_spec((D_MODEL, 2 * D_FF)), _resident_spec((D_FF, D_MODEL))]


def _params():
    return pltpu.CompilerParams(dimension_semantics=("arbitrary",), vmem_limit_bytes=VMEM_LIMIT)


def _ffn_kernel(*refs, j, n_in, n_cast, mod_blocks):
    n_side = n_cast + (3 if mod_blocks else 0)
    x_refs = refs[:n_in]
    mod_ref, g_ref, w1_ref, w2_ref = refs[n_in:n_in + 4]
    side_in = refs[n_in + 4:n_in + 4 + n_side]
    o_ref = refs[n_in + 4 + n_side]
    side_out = refs[n_in + 5 + n_side:]
    i = pl.program_id(0)
    if n_in == 2:
        x = jnp.where(i < _n_ctx_tiles(FFN_TILE), x_refs[0][...], x_refs[1][...])
    else:
        x = x_refs[0][...]
    o_ref[...] = _ffn_rows(x, g_ref[...], mod_ref, _mod_row(i, FFN_TILE), j, w1_ref, w2_ref)
    _cast_blocks(i, FFN_CAST_BLOCKS, side_in[:n_cast], side_out[:n_cast])
    if mod_blocks:
        @pl.when(i < mod_blocks)
        def _():
            side_out[n_cast][...] = _mod_block(*side_in[n_cast:])


def _ffn_call(xs, mod, norm_g, w1, w2, layer, which, casts=(), mod_job=None):
    step = lambda i: i
    if len(xs) == 2:
        x_specs = [_ctx_rows_spec(FFN_TILE, D_MODEL), _smp_rows_spec(FFN_TILE, D_MODEL)]
    else:
        x_specs = [_rows_spec(FFN_TILE, D_MODEL)]
    jobs = [mod_job] if mod_job else []
    out = pl.pallas_call(
        functools.partial(_ffn_kernel, j=6 * which, n_in=len(xs), n_cast=len(casts),
                          mod_blocks=mod_job.n_blocks if mod_job else 0),
        grid=(N_ALL // FFN_TILE,),
        in_specs=x_specs + [_mod_spec(), _gain_spec(layer, 2 * which)] + _ffn_weight_specs()
        + [c.in_spec(step) for c in casts] + [s for m in jobs for s in m.in_specs()],
        out_specs=[_rows_spec(FFN_TILE, D_MODEL)] + [c.out_spec(step) for c in casts]
        + [m.out_spec() for m in jobs],
        out_shape=[jax.ShapeDtypeStruct((N_ALL, D_MODEL), F32)] + [c.out_shape() for c in casts]
        + [m.out_shape() for m in jobs],
        compiler_params=_params(),
        name="ffn",
    )(*xs, mod, norm_g, w1, w2, *[c.src for c in casts], *[a for m in jobs for a in m.args])
    return out[0], out[1:]


def _head_mean_sq(x, seg):
    sq = x * x
    hi = sq.astype(BF16)
    lo = (sq - hi.astype(F32)).astype(BF16)
    return (jnp.dot(hi, seg, preferred_element_type=F32)
            + jnp.dot(lo, seg, preferred_element_type=F32))


def _rope(x, cos, sin_signed):
    width = x.shape[-1]
    lane = lax.broadcasted_iota(jnp.int32, (1, width), 1)
    partner = jnp.where((lane & 31) < 16,
                        pltpu.roll(x, width - 16, 1), pltpu.roll(x, 16, 1))
    return x * cos + partner * sin_signed


def _stack_heads(q_ref, rows, kvh):
    first = kvh * HEADS_PER_KV
    return jnp.concatenate(
        [q_ref[rows, (first + g) * HEAD_DIM:(first + g + 1) * HEAD_DIM] for g in range(HEADS_PER_KV)],
        axis=0)


ONES_ROWS = 16


def _attend_group(q, k, v1t):
    st = lax.dot_general(k, q, (((1,), (1,)), ((), ())), preferred_element_type=F32)
    e = jnp.exp(st - jnp.max(st, axis=0, keepdims=True)).astype(BF16)
    ot = jnp.dot(v1t, e, preferred_element_type=F32)
    return ot[:HEAD_DIM] / ot[HEAD_DIM:HEAD_DIM + 1]


def _attend_rows(q_s, rows, n, kv, mix_s):
    heads = []
    for kvh in range(N_KV_HEADS):
        k, v1t = kv(kvh)
        ot = _attend_group(_stack_heads(q_s, rows, kvh), k, v1t)
        heads += [ot[:, g * n:(g + 1) * n] for g in range(HEADS_PER_KV)]
    mix_s[rows, :ATTN_WIDTH] = jnp.concatenate(heads, axis=0).T.astype(BF16)


def _with_ones(vt):
    return jnp.concatenate([vt, jnp.ones((ONES_ROWS, vt.shape[1]), BF16)], axis=0)


def _mixer_tile(ctx, r, x_ref, mod_ref, g_ref, w_ref, qg_ref, kg_ref, cw_ref, seg_ref, cos_ref, sin_ref,
                ck_ref, cvt_ref, wo_ref, o_ref, kt_ref, vt_ref, q_s, k_s, vt_s, mix_s):
    seq_len = SEQ if ctx else DEC_SEQ
    x = x_ref[...]
    h = _modulate(x, g_ref[...], mod_ref, r, 3).astype(BF16)

    def proj(lo, width):
        return jnp.dot(h, w_ref[:, lo:lo + width], preferred_element_type=F32)

    q = proj(0, ATTN_WIDTH)
    k = proj(ATTN_WIDTH, KV_WIDTH)
    v = proj(ATTN_WIDTH + KV_WIDTH, KV_WIDTH)
    q = q * lax.rsqrt(_head_mean_sq(q, seg_ref[...]) + EPS) * (qg_ref[...] * HEAD_DIM ** -0.5)
    k = k * lax.rsqrt(_head_mean_sq(k, seg_ref[:KV_WIDTH, :KV_WIDTH]) + EPS) * kg_ref[...]
    if ctx:
        q_s[...] = q.astype(BF16)
        k_s[...] = k.astype(BF16)
        for b in range(SEQ_PER_TILE):
            vt = v[b * SEQ:(b + 1) * SEQ, :].T
            kt_ref[b] = k[b * SEQ:(b + 1) * SEQ, :].T
            vt_ref[b] = vt
            vt_s[:, b * SEQ:(b + 1) * SEQ] = vt.astype(BF16)
    else:
        q_s[...] = _rope(q, cos_ref[...], sin_ref[...]).astype(BF16)
        k_s[...] = _rope(k, cos_ref[:, :KV_WIDTH], sin_ref[:, :KV_WIDTH]).astype(BF16)
        vt_s[...] = v.T.astype(BF16)

    base = ATTN_WIDTH + 2 * KV_WIDTH
    z = proj(base + CONV_WIDTH, CONV_WIDTH) * proj(base + 2 * CONV_WIDTH, CONV_WIDTH)
    pos = lax.broadcasted_iota(jnp.int32, (MIX_TILE, 1), 0) & (seq_len - 1)
    z_prev = jnp.where(pos == 0, 0.0, pltpu.roll(z, 1, 0))
    z_next = jnp.where(pos == seq_len - 1, 0.0, pltpu.roll(z, MIX_TILE - 1, 0))
    conv = z_prev * cw_ref[0:1, :] + z * cw_ref[1:2, :] + z_next * cw_ref[2:3, :]
    mix_s[:, ATTN_WIDTH:] = (proj(base, CONV_WIDTH) * conv).astype(BF16)

    def head(kvh):
        return slice(kvh * HEAD_DIM, (kvh + 1) * HEAD_DIM)

    if ctx:
        for b in range(SEQ_PER_TILE):
            rows = slice(b * SEQ, (b + 1) * SEQ)
            _attend_rows(q_s, rows, SEQ,
                         lambda kvh: (k_s[rows, head(kvh)], _with_ones(vt_s[head(kvh), rows])), mix_s)
    else:
        kv = [(jnp.concatenate([ck_ref[:, head(kvh)].astype(BF16), k_s[:, head(kvh)]], axis=0),
               _with_ones(jnp.concatenate([cvt_ref[head(kvh), :].astype(BF16), vt_s[head(kvh), :]],
                                          axis=1)))
              for kvh in range(N_KV_HEADS)]
        for t in range(MIX_TILE // Q_ROWS):
            _attend_rows(q_s, slice(t * Q_ROWS, (t + 1) * Q_ROWS), Q_ROWS, lambda kvh: kv[kvh], mix_s)

    out = (jnp.dot(mix_s[:, :ATTN_WIDTH], wo_ref[:ATTN_WIDTH, :], preferred_element_type=F32)
           + jnp.dot(mix_s[:, ATTN_WIDTH:], wo_ref[ATTN_WIDTH:, :], preferred_element_type=F32))
    o_ref[...] = x + _mod_vec(mod_ref, r, 5) * out


def _mixer_kernel(*refs):
    i = pl.program_id(0)
    r = _mod_row(i, MIX_TILE)
    is_ctx = i < _n_ctx_tiles(MIX_TILE)

    @pl.when(is_ctx)
    def _():
        _mixer_tile(True, r, *refs)

    @pl.when(jnp.logical_not(is_ctx))
    def _():
        _mixer_tile(False, r, *refs)


def _mixer_call(x, mod, norm_g, w_in, qg, kg, conv_w, seg, cos, sin, cache_k, cache_vt, w_out):
    first_smp = _n_ctx_tiles(MIX_TILE)
    smp_batch = lambda i: (jnp.maximum(i - first_smp, 0), 0, 0)
    kvt_spec = pl.BlockSpec((SEQ_PER_TILE, KV_WIDTH, SEQ), lambda i: (jnp.minimum(i, first_smp - 1), 0, 0))
    kvt_shape = jax.ShapeDtypeStruct((BATCH, KV_WIDTH, SEQ), F32)
    return pl.pallas_call(
        _mixer_kernel,
        grid=(N_ALL // MIX_TILE,),
        in_specs=[
            _rows_spec(MIX_TILE, D_MODEL), _mod_spec(), _gain_spec(0, 1),
            _resident_spec((D_MODEL, IN_WIDTH)),
            _resident_spec((1, ATTN_WIDTH)), _resident_spec((1, KV_WIDTH)),
            _resident_spec((3, CONV_WIDTH), 0),
            _resident_spec((ATTN_WIDTH, ATTN_WIDTH)),
            _resident_spec((DEC_SEQ, ATTN_WIDTH)), _resident_spec((DEC_SEQ, ATTN_WIDTH)),
            pl.BlockSpec((None, PAST_LEN, KV_WIDTH), smp_batch),
            pl.BlockSpec((None, KV_WIDTH, PAST_LEN), smp_batch),
            _resident_spec((MIX_WIDTH, D_MODEL)),
        ],
        out_specs=[_rows_spec(MIX_TILE, D_MODEL), kvt_spec, kvt_spec],
        out_shape=[jax.ShapeDtypeStruct((N_ALL, D_MODEL), F32), kvt_shape, kvt_shape],
        scratch_shapes=[pltpu.VMEM((MIX_TILE, ATTN_WIDTH), BF16),
                        pltpu.VMEM((MIX_TILE, KV_WIDTH), BF16),
                        pltpu.VMEM((KV_WIDTH, MIX_TILE), BF16),
                        pltpu.VMEM((MIX_TILE, MIX_WIDTH), BF16)],
        compiler_params=_params(),
        name="mixer",
    )(x, mod, norm_g, w_in, qg, kg, conv_w, seg, cos, sin, cache_k, cache_vt, w_out)


def _shift_rows(a, n, pos, seq_len):
    if n > 0:
        return jnp.where(pos < n, 0.0, pltpu.roll(a, n, 0))
    return jnp.where(pos >= seq_len + n, 0.0, pltpu.roll(a, a.shape[0] + n, 0))


def _pool_rows(x, pos, seq_len, r, mod_ref, gm_ref, pw_ref, ps_ref):
    gate = _mod_vec(mod_ref, r, 5)
    h_all = _modulate(x, gm_ref[...], mod_ref, r, 3)
    outs = []
    for gi, w in enumerate(POOL_WINDOWS):
        lanes = slice(gi * POOL_GROUP, (gi + 1) * POOL_GROUP)
        h = h_all[:, lanes]
        back, fwd, n = h, h, 1
        while n < w // 2:
            back = back + _shift_rows(back, n, pos, seq_len)
            fwd = fwd + _shift_rows(fwd, -n, pos, seq_len)
            n *= 2
        total = _shift_rows(back, 1, pos, seq_len) + fwd
        left = w // 2
        right = w - 1 - left
        count = jnp.minimum(pos + right + 1, seq_len) - jnp.maximum(pos - left, 0)
        diff = (total / count.astype(F32) - h).astype(BF16)
        out = jnp.dot(diff, pw_ref[lanes, :], preferred_element_type=F32) * ps_ref[:, lanes]
        outs.append(x[:, lanes] + gate[:, lanes] * out)
    return jnp.concatenate(outs, axis=1)


def _pool_tile(ctx, r, x_ref, mod_ref, gm_ref, gf_ref, gn_ref, pw_ref, ps_ref, w1_ref, w2_ref,
               y_ref, x2_ref):
    seq_len = SEQ if ctx else DEC_SEQ
    block = max(seq_len, FFN_TILE)
    pos = lax.broadcasted_iota(jnp.int32, (block, 1), 0) & (seq_len - 1)
    for b in range(MIX_TILE // block):
        rows = slice(b * block, (b + 1) * block)
        x2_ref[rows, :] = _pool_rows(x_ref[rows, :], pos, seq_len, r, mod_ref, gm_ref, pw_ref, ps_ref)
    for s in range(MIX_TILE // FFN_TILE):
        rows = slice(s * FFN_TILE, (s + 1) * FFN_TILE)
        y = _ffn_rows(x2_ref[rows, :], gf_ref[...], mod_ref, r, 6, w1_ref, w2_ref)
        y_ref[rows, :] = _rms(y, gn_ref[...])


def _pool_kernel(*refs):
    ins, (yp_ref, ys_ref, x2_ref) = refs[:-3], refs[-3:]
    i = pl.program_id(0)
    r = _mod_row(i, MIX_TILE)
    is_ctx = i < _n_ctx_tiles(MIX_TILE)

    @pl.when(is_ctx)
    def _():
        _pool_tile(True, r, *ins, yp_ref, x2_ref)

    @pl.when(jnp.logical_not(is_ctx))
    def _():
        _pool_tile(False, r, *ins, ys_ref, x2_ref)


def _pool_call(x, mod, norm_g, final_g, pool_w, pool_scale, w1, w2):
    return pl.pallas_call(
        _pool_kernel,
        grid=(N_ALL // MIX_TILE,),
        in_specs=[
            _rows_spec(MIX_TILE, D_MODEL), _mod_spec(), _gain_spec(1, 1), _gain_spec(1, 2),
            _resident_spec((1, D_MODEL)),
            _resident_spec((D_MODEL, POOL_GROUP)),
            _resident_spec((1, D_MODEL), 0),
        ] + _ffn_weight_specs(),
        out_specs=[_ctx_rows_spec(MIX_TILE, D_MODEL), _smp_rows_spec(MIX_TILE, D_MODEL)],
        out_shape=[jax.ShapeDtypeStruct((N_CTX, D_MODEL), F32),
                   jax.ShapeDtypeStruct((N_SMP, D_MODEL), F32)],
        scratch_shapes=[pltpu.VMEM((MIX_TILE, D_MODEL), F32)],
        compiler_params=_params(),
        name="pool_ffn_norm",
    )(x, mod, norm_g, norm_g, final_g, pool_w, pool_scale, w1, w2)


def _rope_tables():
    t = np.arange(DEC_SEQ)
    half = HEAD_DIM // 2
    inv = ROPE_THETA ** (-np.arange(0, half, 2, dtype=np.float64) / half)
    ang_row = (t // GRID_W)[:, None] * inv[None, :]
    ang_col = (t % GRID_W)[:, None] * inv[None, :]
    cos = np.concatenate([np.cos(ang_row), np.cos(ang_row), np.cos(ang_col), np.cos(ang_col)], axis=1)
    sin = np.concatenate([-np.sin(ang_row), np.sin(ang_row), -np.sin(ang_col), np.sin(ang_col)], axis=1)
    return (jnp.asarray(np.tile(cos, (1, N_HEADS)), F32), jnp.asarray(np.tile(sin, (1, N_HEADS)), F32))


def _head_segments():
    head = np.arange(ATTN_WIDTH) // HEAD_DIM
    return jnp.asarray((head[:, None] == head[None, :]) / HEAD_DIM, BF16)


def _cache_layout(t):
    return jnp.transpose(t.reshape(BATCH, 1, N_KV_HEADS, HEAD_DIM, SEQ), (0, 1, 4, 2, 3))


def kernel(x_prompt, x_sample, c, cache_k, cache_v, c_ctx, ada_w, ada_b, norm_g, ffn_w1, ffn_w2,
           mix_w_in, mix_w_out, q_norm, k_norm, conv_w, pool_w, pool_scale, final_g):
    cvec = jnp.concatenate(
        [c_ctx[None, :], c, jnp.zeros((MOD_ROWS - 1 - DEC_BATCH, D_MODEL), F32)], axis=0)
    gains = norm_g.reshape(norm_g.shape[0], 3, 1, D_MODEL)
    cos, sin = _rope_tables()

    def ffn_casts(layer, which, n_blocks):
        return [_Cast(ffn_w1, (layer, which), n_blocks), _Cast(ffn_w2, (layer, which), n_blocks)]

    mod0, (w1_00, w2_00) = _mod_call(_ModJob(cvec, ada_w, ada_b, 0, MOD_STEPS), ffn_casts(0, 0, MOD_STEPS))
    x, (w1_01, w2_01, w1_10, w2_10, w_in, w_out) = _ffn_call(
        [x_prompt.reshape(N_CTX, D_MODEL), x_sample.reshape(N_SMP, D_MODEL)],
        mod0, gains, w1_00, w2_00, 0, 0,
        ffn_casts(0, 1, FFN_CAST_BLOCKS) + ffn_casts(1, 0, FFN_CAST_BLOCKS)
        + [_Cast(mix_w_in, (0,), FFN_CAST_BLOCKS), _Cast(mix_w_out, (0,), FFN_CAST_BLOCKS)])
    x, kt, vt = _mixer_call(
        x, mod0, gains, w_in, jnp.tile(q_norm[0], N_HEADS)[None, :],
        jnp.tile(k_norm[0], N_KV_HEADS)[None, :], conv_w, _head_segments(), cos, sin,
        cache_k[:, 0].reshape(DEC_BATCH, PAST_LEN, KV_WIDTH),
        jnp.transpose(cache_v[:, 0], (0, 2, 3, 1)).reshape(DEC_BATCH, KV_WIDTH, PAST_LEN), w_out)
    x, (w1_11, w2_11, pw, mod1) = _ffn_call(
        [x], mod0, gains, w1_01, w2_01, 0, 1,
        ffn_casts(1, 1, FFN_CAST_BLOCKS)
        + [_Cast(pool_w.reshape(pool_w.shape[0], D_MODEL, POOL_GROUP), (0,), FFN_CAST_BLOCKS)],
        _ModJob(cvec, ada_w, ada_b, 1, MOD_SIDE_BLOCKS))
    x, _ = _ffn_call([x], mod1, gains, w1_10, w2_10, 1, 0)
    yp, ys = _pool_call(x, mod1, gains, final_g[None, :], pw,
                        pool_scale.reshape(pool_scale.shape[0], 1, D_MODEL), w1_11, w2_11)

    return (yp.reshape(BATCH, SEQ, D_MODEL), ys.reshape(DEC_BATCH, DEC_SEQ, D_MODEL),
            _cache_layout(kt), _cache_layout(vt))
```

```python
import functools

import numpy as np
import jax
import jax.numpy as jnp
from jax import lax
from jax.experimental import pallas as pl
from jax.experimental.pallas import tpu as pltpu

F32 = jnp.float32
BF16 = jnp.bfloat16

D_MODEL = 1024
BATCH = 32
SEQ = 256
DEC_BATCH = 2
DEC_SEQ = 1024
PAST_LEN = 512
GRID_W = 64
N_HEADS = 8
N_KV_HEADS = 2
HEAD_DIM = 64
HEADS_PER_KV = N_HEADS // N_KV_HEADS
ATTN_WIDTH = N_HEADS * HEAD_DIM
KV_WIDTH = N_KV_HEADS * HEAD_DIM
CONV_WIDTH = D_MODEL // 2
MIX_WIDTH = ATTN_WIDTH + CONV_WIDTH
IN_WIDTH = ATTN_WIDTH + 2 * KV_WIDTH + 3 * CONV_WIDTH
D_FF = 2816
POOL_WINDOWS = (2, 4, 8, 16)
POOL_GROUP = D_MODEL // len(POOL_WINDOWS)
N_MOD = 9
ROPE_THETA = 10000.0
EPS = 1e-6

N_CTX = BATCH * SEQ
N_SMP = DEC_BATCH * DEC_SEQ
N_ALL = N_CTX + N_SMP
MIX_TILE = 1024
FFN_TILE = 512
FF_CHUNK = 256
N_FF_CHUNKS = D_FF // FF_CHUNK
STAGE_SLOTS = 2
Q_ROWS = 256
SEQ_PER_TILE = MIX_TILE // SEQ
MOD_ROWS = 8
MOD_WIDTH = N_MOD * D_MODEL
MOD_STEPS = 4
MOD_SIDE_BLOCKS = 18
FFN_CAST_BLOCKS = 16
VMEM_LIMIT = 56 * 1024 * 1024


def _n_ctx_tiles(tile):
    return N_CTX // tile


def _mod_row(i, tile):
    first = _n_ctx_tiles(tile)
    return jnp.where(i < first, 0, 1 + (i - first) * tile // DEC_SEQ)


def _rms(x, g):
    return x * lax.rsqrt(jnp.mean(x * x, axis=-1, keepdims=True) + EPS) * g


def _mod_vec(mod_ref, r, j):
    return mod_ref[pl.ds(r, 1), j * D_MODEL:(j + 1) * D_MODEL]


def _modulate(x, g, mod_ref, r, j):
    return _rms(x, g) * (1 + _mod_vec(mod_ref, r, j + 1)) + _mod_vec(mod_ref, r, j)


def _ffn_rows(x, g, mod_ref, r, j, w1_ref, w2_ref, before_chunk=None):
    h = _modulate(x, g, mod_ref, r, j).astype(BF16)
    acc = None
    for c in range(N_FF_CHUNKS):
        lo = c * FF_CHUNK
        if before_chunk is not None:
            before_chunk(c)
        gate = jnp.dot(h, w1_ref[:, lo:lo + FF_CHUNK], preferred_element_type=F32)
        up = jnp.dot(h, w1_ref[:, D_FF + lo:D_FF + lo + FF_CHUNK], preferred_element_type=F32)
        act = (gate / (1 + jnp.exp(-gate)) * up).astype(BF16)
        y = jnp.dot(act, w2_ref[lo:lo + FF_CHUNK, :], preferred_element_type=F32)
        acc = y if acc is None else acc + y
    return x + (0.5 * _mod_vec(mod_ref, r, j + 2)) * acc


class _Cast:
    def __init__(self, src, lead, n_blocks):
        self.src, self.lead, self.n_blocks = src, tuple(lead), n_blocks
        self.rows, self.cols = src.shape[len(lead):]
        self.block_rows = self.rows // n_blocks

    def in_spec(self, step):
        lead, last = self.lead, self.n_blocks - 1
        return pl.BlockSpec((None,) * len(lead) + (self.block_rows, self.cols),
                            lambda *g: lead + (jnp.minimum(step(*g), last), 0))

    def out_spec(self, step):
        last = self.n_blocks - 1
        return pl.BlockSpec((self.block_rows, self.cols), lambda *g: (jnp.minimum(step(*g), last), 0))

    def out_shape(self):
        return jax.ShapeDtypeStruct((self.rows, self.cols), BF16)


def _cast_blocks(step, n_blocks, srcs, dsts):
    @pl.when(step < n_blocks)
    def _():
        for s, d in zip(srcs, dsts):
            d[...] = s[...].astype(BF16)


class _WeightStream:
    def __init__(self, w1_hbm, w2_hbm, layer, which, w1_s, w2_s, gate_st, up_st, down_st, sems):
        self.w1_hbm, self.w2_hbm, self.lead = w1_hbm, w2_hbm, (layer, which)
        self.w1_s, self.w2_s = w1_s, w2_s
        self.stages, self.sems = (gate_st, up_st, down_st), sems

    def _copies(self, c):
        slot, lo = c % STAGE_SLOTS, c * FF_CHUNK
        l, w = self.lead
        srcs = (self.w1_hbm.at[l, w, :, pl.ds(lo, FF_CHUNK)],
                self.w1_hbm.at[l, w, :, pl.ds(D_FF + lo, FF_CHUNK)],
                self.w2_hbm.at[l, w, pl.ds(lo, FF_CHUNK), :])
        return [pltpu.make_async_copy(src, st.at[slot], self.sems.at[k, slot])
                for k, (src, st) in enumerate(zip(srcs, self.stages))]

    def start(self, c):
        for cp in self._copies(c):
            cp.start()

    def prime(self):
        for c in range(STAGE_SLOTS):
            self.start(c)

    def land(self, c):
        for cp in self._copies(c):
            cp.wait()
        slot, lo = c % STAGE_SLOTS, c * FF_CHUNK
        gate_st, up_st, down_st = self.stages
        self.w1_s[:, lo:lo + FF_CHUNK] = gate_st[slot].astype(BF16)
        self.w1_s[:, D_FF + lo:D_FF + lo + FF_CHUNK] = up_st[slot].astype(BF16)
        self.w2_s[lo:lo + FF_CHUNK, :] = down_st[slot].astype(BF16)
        if c + STAGE_SLOTS < N_FF_CHUNKS:
            self.start(c + STAGE_SLOTS)


def _weight_stream_specs():
    return [pl.BlockSpec(memory_space=pl.ANY), pl.BlockSpec(memory_space=pl.ANY)]


N_STREAM_SCRATCH = 6


def _weight_stream_scratch():
    return [pltpu.VMEM((D_MODEL, 2 * D_FF), BF16), pltpu.VMEM((D_FF, D_MODEL), BF16),
            pltpu.VMEM((STAGE_SLOTS, D_MODEL, FF_CHUNK), F32),
            pltpu.VMEM((STAGE_SLOTS, D_MODEL, FF_CHUNK), F32),
            pltpu.VMEM((STAGE_SLOTS, FF_CHUNK, D_MODEL), F32),
            pltpu.SemaphoreType.DMA((3, STAGE_SLOTS))]


def _mod_block(c_ref, w_ref, b_ref):
    c = c_ref[...]
    s = (c / (1 + jnp.exp(-c))).astype(BF16)
    return jnp.dot(s, w_ref[...].astype(BF16), preferred_element_type=F32) + b_ref[...]


class _ModJob:
    def __init__(self, cvec, ada_w, ada_b, layer, n_blocks):
        self.args = (cvec, ada_w, ada_b.reshape(ada_b.shape[0], 1, MOD_WIDTH))
        self.layer, self.n_blocks, self.cols = layer, n_blocks, MOD_WIDTH // n_blocks

    def in_specs(self):
        layer, last, cols = self.layer, self.n_blocks - 1, self.cols
        block = lambda i: (layer, 0, jnp.minimum(i, last))
        return [pl.BlockSpec((MOD_ROWS, D_MODEL), lambda i: (0, 0)),
                pl.BlockSpec((None, D_MODEL, cols), block),
                pl.BlockSpec((None, 1, cols), block)]

    def out_spec(self):
        last = self.n_blocks - 1
        return pl.BlockSpec((MOD_ROWS, self.cols), lambda i: (0, jnp.minimum(i, last)))

    def out_shape(self):
        return jax.ShapeDtypeStruct((MOD_ROWS, MOD_WIDTH), F32)


def _mod_kernel(c_ref, w_ref, b_ref, *refs, n_cast):
    srcs, o_ref, dsts = refs[:n_cast], refs[n_cast], refs[n_cast + 1:]
    o_ref[...] = _mod_block(c_ref, w_ref, b_ref)
    _cast_blocks(pl.program_id(0), MOD_STEPS, srcs, dsts)


def _mod_call(job, casts):
    step = lambda i: i
    out = pl.pallas_call(
        functools.partial(_mod_kernel, n_cast=len(casts)),
        grid=(job.n_blocks,),
        in_specs=job.in_specs() + [c.in_spec(step) for c in casts],
        out_specs=[job.out_spec()] + [c.out_spec(step) for c in casts],
        out_shape=[job.out_shape()] + [c.out_shape() for c in casts],
        compiler_params=_params(),
        name="adaln_mod",
    )(*job.args, *[c.src for c in casts])
    return out[0], out[1:]


def _rows_spec(tile, width):
    return pl.BlockSpec((tile, width), lambda i: (i, 0))


def _ctx_rows_spec(tile, width):
    last = _n_ctx_tiles(tile) - 1
    return pl.BlockSpec((tile, width), lambda i: (jnp.minimum(i, last), 0))


def _smp_rows_spec(tile, width):
    first = _n_ctx_tiles(tile)
    return pl.BlockSpec((tile, width), lambda i: (jnp.maximum(i - first, 0), 0))


def _mod_spec():
    return pl.BlockSpec((MOD_ROWS, MOD_WIDTH), lambda i: (0, 0))


def _gain_spec(layer, j):
    return pl.BlockSpec((None, None, 1, D_MODEL), lambda i: (layer, j, 0, 0))


def _resident_spec(shape, *lead):
    index = tuple(lead) + (0,) * len(shape)
    return pl.BlockSpec((None,) * len(lead) + tuple(shape), lambda i: index,
                        pipeline_mode=pl.Buffered(1))


def _params():
    return pltpu.CompilerParams(dimension_semantics=("arbitrary",), vmem_limit_bytes=VMEM_LIMIT)


def _ffn_kernel(*refs, j, n_in, n_cast, mod_blocks, layer, which):
    n_side = n_cast + (3 if mod_blocks else 0)
    x_refs = refs[:n_in]
    mod_ref, g_ref, w1_hbm, w2_hbm = refs[n_in:n_in + 4]
    side_in = refs[n_in + 4:n_in + 4 + n_side]
    o_ref = refs[n_in + 4 + n_side]
    side_out = refs[n_in + 5 + n_side:len(refs) - N_STREAM_SCRATCH]
    stream = _WeightStream(w1_hbm, w2_hbm, layer, which, *refs[len(refs) - N_STREAM_SCRATCH:])
    i = pl.program_id(0)

    def run(before_chunk):
        if n_in == 2:
            x = jnp.where(i < _n_ctx_tiles(FFN_TILE), x_refs[0][...], x_refs[1][...])
        else:
            x = x_refs[0][...]
        o_ref[...] = _ffn_rows(x, g_ref[...], mod_ref, _mod_row(i, FFN_TILE), j,
                               stream.w1_s, stream.w2_s, before_chunk)

    @pl.when(i == 0)
    def _():
        stream.prime()
        run(stream.land)

    @pl.when(i > 0)
    def _():
        run(None)

    _cast_blocks(i, FFN_CAST_BLOCKS, side_in[:n_cast], side_out[:n_cast])
    if mod_blocks:
        @pl.when(i < mod_blocks)
        def _():
            side_out[n_cast][...] = _mod_block(*side_in[n_cast:])


def _ffn_call(xs, mod, norm_g, w1, w2, layer, which, casts=(), mod_job=None):
    step = lambda i: i
    if len(xs) == 2:
        x_specs = [_ctx_rows_spec(FFN_TILE, D_MODEL), _smp_rows_spec(FFN_TILE, D_MODEL)]
    else:
        x_specs = [_rows_spec(FFN_TILE, D_MODEL)]
    jobs = [mod_job] if mod_job else []
    out = pl.pallas_call(
        functools.partial(_ffn_kernel, j=6 * which, n_in=len(xs), n_cast=len(casts),
                          mod_blocks=mod_job.n_blocks if mod_job else 0, layer=layer, which=which),
        grid=(N_ALL // FFN_TILE,),
        in_specs=x_specs + [_mod_spec(), _gain_spec(layer, 2 * which)] + _weight_stream_specs()
        + [c.in_spec(step) for c in casts] + [s for m in jobs for s in m.in_specs()],
        out_specs=[_rows_spec(FFN_TILE, D_MODEL)] + [c.out_spec(step) for c in casts]
        + [m.out_spec() for m in jobs],
        out_shape=[jax.ShapeDtypeStruct((N_ALL, D_MODEL), F32)] + [c.out_shape() for c in casts]
        + [m.out_shape() for m in jobs],
        scratch_shapes=_weight_stream_scratch(),
        compiler_params=_params(),
        name="ffn",
    )(*xs, mod, norm_g, w1, w2, *[c.src for c in casts], *[a for m in jobs for a in m.args])
    return out[0], out[1:]


def _head_mean_sq(x, seg):
    sq = x * x
    hi = sq.astype(BF16)
    lo = (sq - hi.astype(F32)).astype(BF16)
    return (jnp.dot(hi, seg, preferred_element_type=F32)
            + jnp.dot(lo, seg, preferred_element_type=F32))


def _rope(x, cos, sin_signed):
    width = x.shape[-1]
    lane = lax.broadcasted_iota(jnp.int32, (1, width), 1)
    partner = jnp.where((lane & 31) < 16,
                        pltpu.roll(x, width - 16, 1), pltpu.roll(x, 16, 1))
    return x * cos + partner * sin_signed


def _stack_heads(q_ref, rows, kvh):
    first = kvh * HEADS_PER_KV
    return jnp.concatenate(
        [q_ref[rows, (first + g) * HEAD_DIM:(first + g + 1) * HEAD_DIM] for g in range(HEADS_PER_KV)],
        axis=0)


ONES_ROWS = 16


def _attend_group(q, k, v1t):
    st = lax.dot_general(k, q, (((1,), (1,)), ((), ())), preferred_element_type=F32)
    e = jnp.exp(st - jnp.max(st, axis=0, keepdims=True)).astype(BF16)
    ot = jnp.dot(v1t, e, preferred_element_type=F32)
    return ot[:HEAD_DIM] / ot[HEAD_DIM:HEAD_DIM + 1]


def _attend_rows(q_s, rows, n, kv, mix_s):
    heads = []
    for kvh in range(N_KV_HEADS):
        k, v1t = kv(kvh)
        ot = _attend_group(_stack_heads(q_s, rows, kvh), k, v1t)
        heads += [ot[:, g * n:(g + 1) * n] for g in range(HEADS_PER_KV)]
    mix_s[rows, :ATTN_WIDTH] = jnp.concatenate(heads, axis=0).T.astype(BF16)


def _with_ones(vt):
    return jnp.concatenate([vt, jnp.ones((ONES_ROWS, vt.shape[1]), BF16)], axis=0)


def _mixer_tile(ctx, r, x_ref, mod_ref, g_ref, w_ref, qg_ref, kg_ref, cw_ref, seg_ref, cos_ref, sin_ref,
                ck_ref, cvt_ref, wo_ref, o_ref, kt_ref, vt_ref, q_s, k_s, vt_s, mix_s):
    seq_len = SEQ if ctx else DEC_SEQ
    x = x_ref[...]
    h = _modulate(x, g_ref[...], mod_ref, r, 3).astype(BF16)

    def proj(lo, width):
        return jnp.dot(h, w_ref[:, lo:lo + width], preferred_element_type=F32)

    q = proj(0, ATTN_WIDTH)
    k = proj(ATTN_WIDTH, KV_WIDTH)
    v = proj(ATTN_WIDTH + KV_WIDTH, KV_WIDTH)
    q = q * lax.rsqrt(_head_mean_sq(q, seg_ref[...]) + EPS) * (qg_ref[...] * HEAD_DIM ** -0.5)
    k = k * lax.rsqrt(_head_mean_sq(k, seg_ref[:KV_WIDTH, :KV_WIDTH]) + EPS) * kg_ref[...]
    if ctx:
        q_s[...] = q.astype(BF16)
        k_s[...] = k.astype(BF16)
        for b in range(SEQ_PER_TILE):
            vt = v[b * SEQ:(b + 1) * SEQ, :].T
            kt_ref[b] = k[b * SEQ:(b + 1) * SEQ, :].T
            vt_ref[b] = vt
            vt_s[:, b * SEQ:(b + 1) * SEQ] = vt.astype(BF16)
    else:
        q_s[...] = _rope(q, cos_ref[...], sin_ref[...]).astype(BF16)
        k_s[...] = _rope(k, cos_ref[:, :KV_WIDTH], sin_ref[:, :KV_WIDTH]).astype(BF16)
        vt_s[...] = v.T.astype(BF16)

    base = ATTN_WIDTH + 2 * KV_WIDTH
    z = proj(base + CONV_WIDTH, CONV_WIDTH) * proj(base + 2 * CONV_WIDTH, CONV_WIDTH)
    pos = lax.broadcasted_iota(jnp.int32, (MIX_TILE, 1), 0) & (seq_len - 1)
    z_prev = jnp.where(pos == 0, 0.0, pltpu.roll(z, 1, 0))
    z_next = jnp.where(pos == seq_len - 1, 0.0, pltpu.roll(z, MIX_TILE - 1, 0))
    conv = z_prev * cw_ref[0:1, :] + z * cw_ref[1:2, :] + z_next * cw_ref[2:3, :]
    mix_s[:, ATTN_WIDTH:] = (proj(base, CONV_WIDTH) * conv).astype(BF16)

    def head(kvh):
        return slice(kvh * HEAD_DIM, (kvh + 1) * HEAD_DIM)

    if ctx:
        for b in range(SEQ_PER_TILE):
            rows = slice(b * SEQ, (b + 1) * SEQ)
            _attend_rows(q_s, rows, SEQ,
                         lambda kvh: (k_s[rows, head(kvh)], _with_ones(vt_s[head(kvh), rows])), mix_s)
    else:
        kv = [(jnp.concatenate([ck_ref[:, head(kvh)].astype(BF16), k_s[:, head(kvh)]], axis=0),
               _with_ones(jnp.concatenate([cvt_ref[head(kvh), :].astype(BF16), vt_s[head(kvh), :]],
                                          axis=1)))
              for kvh in range(N_KV_HEADS)]
        for t in range(MIX_TILE // Q_ROWS):
            _attend_rows(q_s, slice(t * Q_ROWS, (t + 1) * Q_ROWS), Q_ROWS, lambda kvh: kv[kvh], mix_s)

    out = (jnp.dot(mix_s[:, :ATTN_WIDTH], wo_ref[:ATTN_WIDTH, :], preferred_element_type=F32)
           + jnp.dot(mix_s[:, ATTN_WIDTH:], wo_ref[ATTN_WIDTH:, :], preferred_element_type=F32))
    o_ref[...] = x + _mod_vec(mod_ref, r, 5) * out


def _mixer_kernel(*refs):
    i = pl.program_id(0)
    r = _mod_row(i, MIX_TILE)
    is_ctx = i < _n_ctx_tiles(MIX_TILE)

    @pl.when(is_ctx)
    def _():
        _mixer_tile(True, r, *refs)

    @pl.when(jnp.logical_not(is_ctx))
    def _():
        _mixer_tile(False, r, *refs)


def _mixer_call(x, mod, norm_g, w_in, qg, kg, conv_w, seg, cos, sin, cache_k, cache_vt, w_out):
    first_smp = _n_ctx_tiles(MIX_TILE)
    smp_batch = lambda i: (jnp.maximum(i - first_smp, 0), 0, 0)
    kvt_spec = pl.BlockSpec((SEQ_PER_TILE, KV_WIDTH, SEQ), lambda i: (jnp.minimum(i, first_smp - 1), 0, 0))
    kvt_shape = jax.ShapeDtypeStruct((BATCH, KV_WIDTH, SEQ), F32)
    return pl.pallas_call(
        _mixer_kernel,
        grid=(N_ALL // MIX_TILE,),
        in_specs=[
            _rows_spec(MIX_TILE, D_MODEL), _mod_spec(), _gain_spec(0, 1),
            _resident_spec((D_MODEL, IN_WIDTH)),
            _resident_spec((1, ATTN_WIDTH)), _resident_spec((1, KV_WIDTH)),
            _resident_spec((3, CONV_WIDTH), 0),
            _resident_spec((ATTN_WIDTH, ATTN_WIDTH)),
            _resident_spec((DEC_SEQ, ATTN_WIDTH)), _resident_spec((DEC_SEQ, ATTN_WIDTH)),
            pl.BlockSpec((None, PAST_LEN, KV_WIDTH), smp_batch),
            pl.BlockSpec((None, KV_WIDTH, PAST_LEN), smp_batch),
            _resident_spec((MIX_WIDTH, D_MODEL)),
        ],
        out_specs=[_rows_spec(MIX_TILE, D_MODEL), kvt_spec, kvt_spec],
        out_shape=[jax.ShapeDtypeStruct((N_ALL, D_MODEL), F32), kvt_shape, kvt_shape],
        scratch_shapes=[pltpu.VMEM((MIX_TILE, ATTN_WIDTH), BF16),
                        pltpu.VMEM((MIX_TILE, KV_WIDTH), BF16),
                        pltpu.VMEM((KV_WIDTH, MIX_TILE), BF16),
                        pltpu.VMEM((MIX_TILE, MIX_WIDTH), BF16)],
        compiler_params=_params(),
        name="mixer",
    )(x, mod, norm_g, w_in, qg, kg, conv_w, seg, cos, sin, cache_k, cache_vt, w_out)


def _shift_rows(a, n, pos, seq_len):
    if n > 0:
        return jnp.where(pos < n, 0.0, pltpu.roll(a, n, 0))
    return jnp.where(pos >= seq_len + n, 0.0, pltpu.roll(a, a.shape[0] + n, 0))


def _pool_rows(x, pos, seq_len, r, mod_ref, gm_ref, pw_ref, ps_ref):
    gate = _mod_vec(mod_ref, r, 5)
    h_all = _modulate(x, gm_ref[...], mod_ref, r, 3)
    outs = []
    for gi, w in enumerate(POOL_WINDOWS):
        lanes = slice(gi * POOL_GROUP, (gi + 1) * POOL_GROUP)
        h = h_all[:, lanes]
        back, fwd, n = h, h, 1
        while n < w // 2:
            back = back + _shift_rows(back, n, pos, seq_len)
            fwd = fwd + _shift_rows(fwd, -n, pos, seq_len)
            n *= 2
        total = _shift_rows(back, 1, pos, seq_len) + fwd
        left = w // 2
        right = w - 1 - left
        count = jnp.minimum(pos + right + 1, seq_len) - jnp.maximum(pos - left, 0)
        diff = (total / count.astype(F32) - h).astype(BF16)
        out = jnp.dot(diff, pw_ref[lanes, :], preferred_element_type=F32) * ps_ref[:, lanes]
        outs.append(x[:, lanes] + gate[:, lanes] * out)
    return jnp.concatenate(outs, axis=1)


def _pool_tile(ctx, r, x_ref, mod_ref, gm_ref, gf_ref, gn_ref, pw_ref, ps_ref, w1_ref, w2_ref,
               y_ref, x2_ref):
    seq_len = SEQ if ctx else DEC_SEQ
    block = max(seq_len, FFN_TILE)
    pos = lax.broadcasted_iota(jnp.int32, (block, 1), 0) & (seq_len - 1)
    for b in range(MIX_TILE // block):
        rows = slice(b * block, (b + 1) * block)
        x2_ref[rows, :] = _pool_rows(x_ref[rows, :], pos, seq_len, r, mod_ref, gm_ref, pw_ref, ps_ref)

    def ffn_step(s, carry):
        rows = pl.ds(pl.multiple_of(s * FFN_TILE, FFN_TILE), FFN_TILE)
        y = _ffn_rows(x2_ref[rows, :], gf_ref[...], mod_ref, r, 6, w1_ref, w2_ref)
        y_ref[rows, :] = _rms(y, gn_ref[...])
        return carry

    lax.fori_loop(0, MIX_TILE // FFN_TILE, ffn_step, 0)


def _pool_kernel(*refs):
    ins, (yp_ref, ys_ref, x2_ref) = refs[:-3], refs[-3:]
    i = pl.program_id(0)
    r = _mod_row(i, MIX_TILE)
    is_ctx = i < _n_ctx_tiles(MIX_TILE)

    @pl.when(is_ctx)
    def _():
        _pool_tile(True, r, *ins, yp_ref, x2_ref)

    @pl.when(jnp.logical_not(is_ctx))
    def _():
        _pool_tile(False, r, *ins, ys_ref, x2_ref)


def _pool_call(x, mod, norm_g, final_g, pool_w, pool_scale, w1, w2):
    return pl.pallas_call(
        _pool_kernel,
        grid=(N_ALL // MIX_TILE,),
        in_specs=[
            _rows_spec(MIX_TILE, D_MODEL), _mod_spec(), _gain_spec(1, 1), _gain_spec(1, 2),
            _resident_spec((1, D_MODEL)),
            _resident_spec((D_MODEL, POOL_GROUP)),
            _resident_spec((1, D_MODEL), 0),
            _resident_spec((D_MODEL, 2 * D_FF)), _resident_spec((D_FF, D_MODEL)),
        ],
        out_specs=[_ctx_rows_spec(MIX_TILE, D_MODEL), _smp_rows_spec(MIX_TILE, D_MODEL)],
        out_shape=[jax.ShapeDtypeStruct((N_CTX, D_MODEL), F32),
                   jax.ShapeDtypeStruct((N_SMP, D_MODEL), F32)],
        scratch_shapes=[pltpu.VMEM((MIX_TILE, D_MODEL), F32)],
        compiler_params=_params(),
        name="pool_ffn_norm",
    )(x, mod, norm_g, norm_g, final_g, pool_w, pool_scale, w1, w2)


def _rope_tables():
    t = np.arange(DEC_SEQ)
    half = HEAD_DIM // 2
    inv = ROPE_THETA ** (-np.arange(0, half, 2, dtype=np.float64) / half)
    ang_row = (t // GRID_W)[:, None] * inv[None, :]
    ang_col = (t % GRID_W)[:, None] * inv[None, :]
    cos = np.concatenate([np.cos(ang_row), np.cos(ang_row), np.cos(ang_col), np.cos(ang_col)], axis=1)
    sin = np.concatenate([-np.sin(ang_row), np.sin(ang_row), -np.sin(ang_col), np.sin(ang_col)], axis=1)
    return (jnp.asarray(np.tile(cos, (1, N_HEADS)), F32), jnp.asarray(np.tile(sin, (1, N_HEADS)), F32))


def _head_segments():
    head = np.arange(ATTN_WIDTH) // HEAD_DIM
    return jnp.asarray((head[:, None] == head[None, :]) / HEAD_DIM, BF16)


def _cache_layout(t):
    return jnp.transpose(t.reshape(BATCH, 1, N_KV_HEADS, HEAD_DIM, SEQ), (0, 1, 4, 2, 3))


def kernel(x_prompt, x_sample, c, cache_k, cache_v, c_ctx, ada_w, ada_b, norm_g, ffn_w1, ffn_w2,
           mix_w_in, mix_w_out, q_norm, k_norm, conv_w, pool_w, pool_scale, final_g):
    cvec = jnp.concatenate(
        [c_ctx[None, :], c, jnp.zeros((MOD_ROWS - 1 - DEC_BATCH, D_MODEL), F32)], axis=0)
    gains = norm_g.reshape(norm_g.shape[0], 3, 1, D_MODEL)
    cos, sin = _rope_tables()

    mod0, _ = _mod_call(_ModJob(cvec, ada_w, ada_b, 0, MOD_STEPS), [])
    x, (w_in, w_out) = _ffn_call(
        [x_prompt.reshape(N_CTX, D_MODEL), x_sample.reshape(N_SMP, D_MODEL)],
        mod0, gains, ffn_w1, ffn_w2, 0, 0,
        [_Cast(mix_w_in, (0,), FFN_CAST_BLOCKS), _Cast(mix_w_out, (0,), FFN_CAST_BLOCKS)])
    x, kt, vt = _mixer_call(
        x, mod0, gains, w_in, jnp.tile(q_norm[0], N_HEADS)[None, :],
        jnp.tile(k_norm[0], N_KV_HEADS)[None, :], conv_w, _head_segments(), cos, sin,
        cache_k[:, 0].reshape(DEC_BATCH, PAST_LEN, KV_WIDTH),
        jnp.transpose(cache_v[:, 0], (0, 2, 3, 1)).reshape(DEC_BATCH, KV_WIDTH, PAST_LEN), w_out)
    x, (pw, mod1) = _ffn_call(
        [x], mod0, gains, ffn_w1, ffn_w2, 0, 1,
        [_Cast(pool_w.reshape(pool_w.shape[0], D_MODEL, POOL_GROUP), (0,), FFN_CAST_BLOCKS)],
        _ModJob(cvec, ada_w, ada_b, 1, MOD_SIDE_BLOCKS))
    x, (w1_last, w2_last) = _ffn_call(
        [x], mod1, gains, ffn_w1, ffn_w2, 1, 0,
        [_Cast(ffn_w1, (1, 1), FFN_CAST_BLOCKS), _Cast(ffn_w2, (1, 1), FFN_CAST_BLOCKS)])
    yp, ys = _pool_call(x, mod1, gains, final_g[None, :], pw,
                        pool_scale.reshape(pool_scale.shape[0], 1, D_MODEL), w1_last, w2_last)

    return (yp.reshape(BATCH, SEQ, D_MODEL), ys.reshape(DEC_BATCH, DEC_SEQ, D_MODEL),
            _cache_layout(kt), _cache_layout(vt))
```

```python
import functools

import numpy as np
import jax
import jax.numpy as jnp
from jax import lax
from jax.experimental import pallas as pl
from jax.experimental.pallas import tpu as pltpu

F32 = jnp.float32
BF16 = jnp.bfloat16

D_MODEL = 1024
BATCH = 32
SEQ = 256
DEC_BATCH = 2
DEC_SEQ = 1024
PAST_LEN = 512
GRID_W = 64
N_HEADS = 8
N_KV_HEADS = 2
HEAD_DIM = 64
HEADS_PER_KV = N_HEADS // N_KV_HEADS
ATTN_WIDTH = N_HEADS * HEAD_DIM
KV_WIDTH = N_KV_HEADS * HEAD_DIM
CONV_WIDTH = D_MODEL // 2
MIX_WIDTH = ATTN_WIDTH + CONV_WIDTH
IN_WIDTH = ATTN_WIDTH + 2 * KV_WIDTH + 3 * CONV_WIDTH
D_FF = 2816
POOL_WINDOWS = (2, 4, 8, 16)
POOL_GROUP = D_MODEL // len(POOL_WINDOWS)
N_MOD = 9
ROPE_THETA = 10000.0
EPS = 1e-6

N_CTX = BATCH * SEQ
N_SMP = DEC_BATCH * DEC_SEQ
N_ALL = N_CTX + N_SMP
MIX_TILE = 1024
FFN_TILE = 512
FF_CHUNK = 256
N_FF_CHUNKS = D_FF // FF_CHUNK
STAGE_SLOTS = 2
Q_ROWS = 256
SEQ_PER_TILE = MIX_TILE // SEQ
MOD_ROWS = 8
MOD_WIDTH = N_MOD * D_MODEL
MOD_STEPS = 4
MOD_SIDE_BLOCKS = 18
FFN_CAST_BLOCKS = 16
VMEM_LIMIT = 56 * 1024 * 1024


def _n_ctx_tiles(tile):
    return N_CTX // tile


def _mod_row(i, tile):
    first = _n_ctx_tiles(tile)
    return jnp.where(i < first, 0, 1 + (i - first) * tile // DEC_SEQ)


def _rms(x, g):
    return x * lax.rsqrt(jnp.mean(x * x, axis=-1, keepdims=True) + EPS) * g


def _mod_vec(mod_ref, r, j):
    return mod_ref[pl.ds(r, 1), j * D_MODEL:(j + 1) * D_MODEL]


def _modulate(x, g, mod_ref, r, j):
    return _rms(x, g) * (1 + _mod_vec(mod_ref, r, j + 1)) + _mod_vec(mod_ref, r, j)


def _ffn_hidden(x, g, mod_ref, r, j):
    return _modulate(x, g, mod_ref, r, j).astype(BF16)


def _ffn_apply(x, h, mod_ref, r, j, w1_ref, w2_ref, before_chunk=None):
    acc = None
    for c in range(N_FF_CHUNKS):
        lo = c * FF_CHUNK
        if before_chunk is not None:
            before_chunk(c)
        gate = jnp.dot(h, w1_ref[:, lo:lo + FF_CHUNK], preferred_element_type=F32)
        up = jnp.dot(h, w1_ref[:, D_FF + lo:D_FF + lo + FF_CHUNK], preferred_element_type=F32)
        act = (gate / (1 + jnp.exp(-gate)) * up).astype(BF16)
        y = jnp.dot(act, w2_ref[lo:lo + FF_CHUNK, :], preferred_element_type=F32)
        acc = y if acc is None else acc + y
    return x + (0.5 * _mod_vec(mod_ref, r, j + 2)) * acc


def _ffn_rows(x, g, mod_ref, r, j, w1_ref, w2_ref):
    return _ffn_apply(x, _ffn_hidden(x, g, mod_ref, r, j), mod_ref, r, j, w1_ref, w2_ref)


class _Cast:
    def __init__(self, src, lead, n_blocks):
        self.src, self.lead, self.n_blocks = src, tuple(lead), n_blocks
        self.rows, self.cols = src.shape[len(lead):]
        self.block_rows = self.rows // n_blocks

    def in_spec(self, step):
        lead, last = self.lead, self.n_blocks - 1
        return pl.BlockSpec((None,) * len(lead) + (self.block_rows, self.cols),
                            lambda *g: lead + (jnp.minimum(step(*g), last), 0))

    def out_spec(self, step):
        last = self.n_blocks - 1
        return pl.BlockSpec((self.block_rows, self.cols), lambda *g: (jnp.minimum(step(*g), last), 0))

    def out_shape(self):
        return jax.ShapeDtypeStruct((self.rows, self.cols), BF16)


def _cast_blocks(step, n_blocks, srcs, dsts):
    @pl.when(step < n_blocks)
    def _():
        for s, d in zip(srcs, dsts):
            d[...] = s[...].astype(BF16)


class _WeightStream:
    def __init__(self, w1_hbm, w2_hbm, layer, which, w1_s, w2_s, gate_st, up_st, down_st, sems):
        self.w1_hbm, self.w2_hbm, self.lead = w1_hbm, w2_hbm, (layer, which)
        self.w1_s, self.w2_s = w1_s, w2_s
        self.stages, self.sems = (gate_st, up_st, down_st), sems

    def _copies(self, c):
        slot, lo = c % STAGE_SLOTS, c * FF_CHUNK
        l, w = self.lead
        srcs = (self.w1_hbm.at[l, w, :, pl.ds(lo, FF_CHUNK)],
                self.w1_hbm.at[l, w, :, pl.ds(D_FF + lo, FF_CHUNK)],
                self.w2_hbm.at[l, w, pl.ds(lo, FF_CHUNK), :])
        return [pltpu.make_async_copy(src, st.at[slot], self.sems.at[k, slot])
                for k, (src, st) in enumerate(zip(srcs, self.stages))]

    def start(self, c):
        for cp in self._copies(c):
            cp.start()

    def prime(self):
        for c in range(STAGE_SLOTS):
            self.start(c)

    def land(self, c):
        for cp in self._copies(c):
            cp.wait()
        slot, lo = c % STAGE_SLOTS, c * FF_CHUNK
        gate_st, up_st, down_st = self.stages
        self.w1_s[:, lo:lo + FF_CHUNK] = gate_st[slot].astype(BF16)
        self.w1_s[:, D_FF + lo:D_FF + lo + FF_CHUNK] = up_st[slot].astype(BF16)
        self.w2_s[lo:lo + FF_CHUNK, :] = down_st[slot].astype(BF16)
        if c + STAGE_SLOTS < N_FF_CHUNKS:
            self.start(c + STAGE_SLOTS)


def _weight_stream_specs():
    return [pl.BlockSpec(memory_space=pl.ANY), pl.BlockSpec(memory_space=pl.ANY)]


N_STREAM_SCRATCH = 6


def _weight_stream_scratch():
    return [pltpu.VMEM((D_MODEL, 2 * D_FF), BF16), pltpu.VMEM((D_FF, D_MODEL), BF16),
            pltpu.VMEM((STAGE_SLOTS, D_MODEL, FF_CHUNK), F32),
            pltpu.VMEM((STAGE_SLOTS, D_MODEL, FF_CHUNK), F32),
            pltpu.VMEM((STAGE_SLOTS, FF_CHUNK, D_MODEL), F32),
            pltpu.SemaphoreType.DMA((3, STAGE_SLOTS))]


def _mod_block(c_ref, w_ref, b_ref):
    c = c_ref[...]
    s = (c / (1 + jnp.exp(-c))).astype(BF16)
    return jnp.dot(s, w_ref[...].astype(BF16), preferred_element_type=F32) + b_ref[...]


class _ModJob:
    def __init__(self, cvec, ada_w, ada_b, layer, n_blocks):
        self.args = (cvec, ada_w, ada_b.reshape(ada_b.shape[0], 1, MOD_WIDTH))
        self.layer, self.n_blocks, self.cols = layer, n_blocks, MOD_WIDTH // n_blocks

    def in_specs(self):
        layer, last, cols = self.layer, self.n_blocks - 1, self.cols
        block = lambda i: (layer, 0, jnp.minimum(i, last))
        return [pl.BlockSpec((MOD_ROWS, D_MODEL), lambda i: (0, 0)),
                pl.BlockSpec((None, D_MODEL, cols), block),
                pl.BlockSpec((None, 1, cols), block)]

    def out_spec(self):
        last = self.n_blocks - 1
        return pl.BlockSpec((MOD_ROWS, self.cols), lambda i: (0, jnp.minimum(i, last)))

    def out_shape(self):
        return jax.ShapeDtypeStruct((MOD_ROWS, MOD_WIDTH), F32)


def _mod_kernel(c_ref, w_ref, b_ref, *refs, n_cast):
    srcs, o_ref, dsts = refs[:n_cast], refs[n_cast], refs[n_cast + 1:]
    o_ref[...] = _mod_block(c_ref, w_ref, b_ref)
    _cast_blocks(pl.program_id(0), MOD_STEPS, srcs, dsts)


def _mod_call(job, casts):
    step = lambda i: i
    out = pl.pallas_call(
        functools.partial(_mod_kernel, n_cast=len(casts)),
        grid=(job.n_blocks,),
        in_specs=job.in_specs() + [c.in_spec(step) for c in casts],
        out_specs=[job.out_spec()] + [c.out_spec(step) for c in casts],
        out_shape=[job.out_shape()] + [c.out_shape() for c in casts],
        compiler_params=_params(),
        name="adaln_mod",
    )(*job.args, *[c.src for c in casts])
    return out[0], out[1:]


def _rows_spec(tile, width, shift=0):
    last = N_ALL // tile - 1
    return pl.BlockSpec((tile, width), lambda i: (jnp.minimum(i + shift, last), 0))


def _ctx_rows_spec(tile, width, shift=0):
    last = _n_ctx_tiles(tile) - 1
    return pl.BlockSpec((tile, width), lambda i: (jnp.minimum(i + shift, last), 0))


def _smp_rows_spec(tile, width, shift=0):
    first, last = _n_ctx_tiles(tile), N_SMP // tile - 1
    return pl.BlockSpec((tile, width), lambda i: (jnp.clip(i + shift - first, 0, last), 0))


def _mod_spec():
    return pl.BlockSpec((MOD_ROWS, MOD_WIDTH), lambda i: (0, 0))


def _gain_spec(layer, j):
    return pl.BlockSpec((None, None, 1, D_MODEL), lambda i: (layer, j, 0, 0))


def _resident_spec(shape, *lead):
    index = tuple(lead) + (0,) * len(shape)
    return pl.BlockSpec((None,) * len(lead) + tuple(shape), lambda i: index,
                        pipeline_mode=pl.Buffered(1))


def _params():
    return pltpu.CompilerParams(dimension_semantics=("arbitrary",), vmem_limit_bytes=VMEM_LIMIT)


def _ffn_kernel(*refs, j, n_in, ahead, n_cast, mod_blocks, layer, which):
    n_x = n_in * (2 if ahead else 1)
    n_side = n_cast + (3 if mod_blocks else 0)
    n_scratch = N_STREAM_SCRATCH + (1 if ahead else 0)
    x_refs, next_refs = refs[:n_in], refs[n_in:n_x]
    mod_ref, g_ref, w1_hbm, w2_hbm = refs[n_x:n_x + 4]
    side_in = refs[n_x + 4:n_x + 4 + n_side]
    o_ref = refs[n_x + 4 + n_side]
    side_out = refs[n_x + 5 + n_side:len(refs) - n_scratch]
    stream = _WeightStream(w1_hbm, w2_hbm, layer, which, *refs[len(refs) - N_STREAM_SCRATCH:])
    i = pl.program_id(0)

    def tile(rows_refs, t):
        if n_in == 2:
            return jnp.where(t < _n_ctx_tiles(FFN_TILE), rows_refs[0][...], rows_refs[1][...])
        return rows_refs[0][...]

    def hidden(rows_refs, t):
        return _ffn_hidden(tile(rows_refs, t), g_ref[...], mod_ref, _mod_row(t, FFN_TILE), j)

    def run(h, before_chunk):
        o_ref[...] = _ffn_apply(tile(x_refs, i), h, mod_ref, _mod_row(i, FFN_TILE), j,
                                stream.w1_s, stream.w2_s, before_chunk)

    if ahead:
        h_s = refs[len(refs) - n_scratch]
        slot = i % 2

        def run_ahead(h, before_chunk):
            run(h, before_chunk)
            h_s[1 - slot] = hidden(next_refs, jnp.minimum(i + 1, N_ALL // FFN_TILE - 1))
    else:
        run_ahead = run

    @pl.when(i == 0)
    def _():
        stream.prime()
        run_ahead(hidden(x_refs, i), stream.land)

    @pl.when(i > 0)
    def _():
        run_ahead(h_s[slot] if ahead else hidden(x_refs, i), None)

    _cast_blocks(i, FFN_CAST_BLOCKS, side_in[:n_cast], side_out[:n_cast])
    if mod_blocks:
        @pl.when(i < mod_blocks)
        def _():
            side_out[n_cast][...] = _mod_block(*side_in[n_cast:])


def _ffn_call(xs, mod, norm_g, w1, w2, layer, which, casts=(), mod_job=None):
    step = lambda i: i
    ahead = len(xs) == 1
    if len(xs) == 2:
        x_specs = [_ctx_rows_spec(FFN_TILE, D_MODEL), _smp_rows_spec(FFN_TILE, D_MODEL)]
    else:
        x_specs = [_rows_spec(FFN_TILE, D_MODEL, shift) for shift in (0, 1)]
    jobs = [mod_job] if mod_job else []
    out = pl.pallas_call(
        functools.partial(_ffn_kernel, j=6 * which, n_in=len(xs), ahead=ahead, n_cast=len(casts),
                          mod_blocks=mod_job.n_blocks if mod_job else 0, layer=layer, which=which),
        grid=(N_ALL // FFN_TILE,),
        in_specs=x_specs + [_mod_spec(), _gain_spec(layer, 2 * which)] + _weight_stream_specs()
        + [c.in_spec(step) for c in casts] + [s for m in jobs for s in m.in_specs()],
        out_specs=[_rows_spec(FFN_TILE, D_MODEL)] + [c.out_spec(step) for c in casts]
        + [m.out_spec() for m in jobs],
        out_shape=[jax.ShapeDtypeStruct((N_ALL, D_MODEL), F32)] + [c.out_shape() for c in casts]
        + [m.out_shape() for m in jobs],
        scratch_shapes=([pltpu.VMEM((2, FFN_TILE, D_MODEL), BF16)] if ahead else [])
        + _weight_stream_scratch(),
        compiler_params=_params(),
        name="ffn",
    )(*xs, *(xs if ahead else []), mod, norm_g, w1, w2, *[c.src for c in casts], *[a for m in jobs for a in m.args])
    return out[0], out[1:]


def _head_mean_sq(x, seg):
    sq = x * x
    hi = sq.astype(BF16)
    lo = (sq - hi.astype(F32)).astype(BF16)
    return (jnp.dot(hi, seg, preferred_element_type=F32)
            + jnp.dot(lo, seg, preferred_element_type=F32))


def _rope(x, cos, sin_signed):
    width = x.shape[-1]
    lane = lax.broadcasted_iota(jnp.int32, (1, width), 1)
    partner = jnp.where((lane & 31) < 16,
                        pltpu.roll(x, width - 16, 1), pltpu.roll(x, 16, 1))
    return x * cos + partner * sin_signed


def _stack_heads(q_ref, rows, kvh):
    first = kvh * HEADS_PER_KV
    return jnp.concatenate(
        [q_ref[rows, (first + g) * HEAD_DIM:(first + g + 1) * HEAD_DIM] for g in range(HEADS_PER_KV)],
        axis=0)


ONES_ROWS = 16


def _attend_group(q, k, v1t):
    st = lax.dot_general(k, q, (((1,), (1,)), ((), ())), preferred_element_type=F32)
    e = jnp.exp(st - jnp.max(st, axis=0, keepdims=True)).astype(BF16)
    ot = jnp.dot(v1t, e, preferred_element_type=F32)
    return ot[:HEAD_DIM] / ot[HEAD_DIM:HEAD_DIM + 1]


def _attend_rows(q_s, rows, n, kv, mix_s):
    heads = []
    for kvh in range(N_KV_HEADS):
        k, v1t = kv(kvh)
        ot = _attend_group(_stack_heads(q_s, rows, kvh), k, v1t)
        heads += [ot[:, g * n:(g + 1) * n] for g in range(HEADS_PER_KV)]
    mix_s[rows, :ATTN_WIDTH] = jnp.concatenate(heads, axis=0).T.astype(BF16)


def _with_ones(vt):
    return jnp.concatenate([vt, jnp.ones((ONES_ROWS, vt.shape[1]), BF16)], axis=0)


def _mixer_tile(ctx, r, x_ref, mod_ref, g_ref, w_ref, qg_ref, kg_ref, cw_ref, seg_ref, cos_ref, sin_ref,
                ck_ref, cvt_ref, wo_ref, o_ref, kt_ref, vt_ref, q_s, k_s, vt_s, mix_s):
    seq_len = SEQ if ctx else DEC_SEQ
    x = x_ref[...]
    h = _modulate(x, g_ref[...], mod_ref, r, 3).astype(BF16)

    def proj(lo, width):
        return jnp.dot(h, w_ref[:, lo:lo + width], preferred_element_type=F32)

    q = proj(0, ATTN_WIDTH)
    k = proj(ATTN_WIDTH, KV_WIDTH)
    v = proj(ATTN_WIDTH + KV_WIDTH, KV_WIDTH)
    q = q * lax.rsqrt(_head_mean_sq(q, seg_ref[...]) + EPS) * (qg_ref[...] * HEAD_DIM ** -0.5)
    k = k * lax.rsqrt(_head_mean_sq(k, seg_ref[:KV_WIDTH, :KV_WIDTH]) + EPS) * kg_ref[...]
    if ctx:
        q_s[...] = q.astype(BF16)
        k_s[...] = k.astype(BF16)
        for b in range(SEQ_PER_TILE):
            vt = v[b * SEQ:(b + 1) * SEQ, :].T
            kt_ref[b] = k[b * SEQ:(b + 1) * SEQ, :].T
            vt_ref[b] = vt
            vt_s[:, b * SEQ:(b + 1) * SEQ] = vt.astype(BF16)
    else:
        q_s[...] = _rope(q, cos_ref[...], sin_ref[...]).astype(BF16)
        k_s[...] = _rope(k, cos_ref[:, :KV_WIDTH], sin_ref[:, :KV_WIDTH]).astype(BF16)
        vt_s[...] = v.T.astype(BF16)

    base = ATTN_WIDTH + 2 * KV_WIDTH
    z = proj(base + CONV_WIDTH, CONV_WIDTH) * proj(base + 2 * CONV_WIDTH, CONV_WIDTH)
    pos = lax.broadcasted_iota(jnp.int32, (MIX_TILE, 1), 0) & (seq_len - 1)
    z_prev = jnp.where(pos == 0, 0.0, pltpu.roll(z, 1, 0))
    z_next = jnp.where(pos == seq_len - 1, 0.0, pltpu.roll(z, MIX_TILE - 1, 0))
    conv = z_prev * cw_ref[0:1, :] + z * cw_ref[1:2, :] + z_next * cw_ref[2:3, :]
    mix_s[:, ATTN_WIDTH:] = (proj(base, CONV_WIDTH) * conv).astype(BF16)

    def head(kvh):
        return slice(kvh * HEAD_DIM, (kvh + 1) * HEAD_DIM)

    if ctx:
        for b in range(SEQ_PER_TILE):
            rows = slice(b * SEQ, (b + 1) * SEQ)
            _attend_rows(q_s, rows, SEQ,
                         lambda kvh: (k_s[rows, head(kvh)], _with_ones(vt_s[head(kvh), rows])), mix_s)
    else:
        kv = [(jnp.concatenate([ck_ref[:, head(kvh)].astype(BF16), k_s[:, head(kvh)]], axis=0),
               _with_ones(jnp.concatenate([cvt_ref[head(kvh), :].astype(BF16), vt_s[head(kvh), :]],
                                          axis=1)))
              for kvh in range(N_KV_HEADS)]
        for t in range(MIX_TILE // Q_ROWS):
            _attend_rows(q_s, slice(t * Q_ROWS, (t + 1) * Q_ROWS), Q_ROWS, lambda kvh: kv[kvh], mix_s)

    out = (jnp.dot(mix_s[:, :ATTN_WIDTH], wo_ref[:ATTN_WIDTH, :], preferred_element_type=F32)
           + jnp.dot(mix_s[:, ATTN_WIDTH:], wo_ref[ATTN_WIDTH:, :], preferred_element_type=F32))
    o_ref[...] = x + _mod_vec(mod_ref, r, 5) * out


def _mixer_kernel(*refs):
    i = pl.program_id(0)
    r = _mod_row(i, MIX_TILE)
    is_ctx = i < _n_ctx_tiles(MIX_TILE)

    @pl.when(is_ctx)
    def _():
        _mixer_tile(True, r, *refs)

    @pl.when(jnp.logical_not(is_ctx))
    def _():
        _mixer_tile(False, r, *refs)


def _mixer_call(x, mod, norm_g, w_in, qg, kg, conv_w, seg, cos, sin, cache_k, cache_vt, w_out):
    first_smp = _n_ctx_tiles(MIX_TILE)
    smp_batch = lambda i: (jnp.maximum(i - first_smp, 0), 0, 0)
    kvt_spec = pl.BlockSpec((SEQ_PER_TILE, KV_WIDTH, SEQ), lambda i: (jnp.minimum(i, first_smp - 1), 0, 0))
    kvt_shape = jax.ShapeDtypeStruct((BATCH, KV_WIDTH, SEQ), F32)
    return pl.pallas_call(
        _mixer_kernel,
        grid=(N_ALL // MIX_TILE,),
        in_specs=[
            _rows_spec(MIX_TILE, D_MODEL), _mod_spec(), _gain_spec(0, 1),
            _resident_spec((D_MODEL, IN_WIDTH)),
            _resident_spec((1, ATTN_WIDTH)), _resident_spec((1, KV_WIDTH)),
            _resident_spec((3, CONV_WIDTH), 0),
            _resident_spec((ATTN_WIDTH, ATTN_WIDTH)),
            _resident_spec((DEC_SEQ, ATTN_WIDTH)), _resident_spec((DEC_SEQ, ATTN_WIDTH)),
            pl.BlockSpec((None, PAST_LEN, KV_WIDTH), smp_batch),
            pl.BlockSpec((None, KV_WIDTH, PAST_LEN), smp_batch),
            _resident_spec((MIX_WIDTH, D_MODEL)),
        ],
        out_specs=[_rows_spec(MIX_TILE, D_MODEL), kvt_spec, kvt_spec],
        out_shape=[jax.ShapeDtypeStruct((N_ALL, D_MODEL), F32), kvt_shape, kvt_shape],
        scratch_shapes=[pltpu.VMEM((MIX_TILE, ATTN_WIDTH), BF16),
                        pltpu.VMEM((MIX_TILE, KV_WIDTH), BF16),
                        pltpu.VMEM((KV_WIDTH, MIX_TILE), BF16),
                        pltpu.VMEM((MIX_TILE, MIX_WIDTH), BF16)],
        compiler_params=_params(),
        name="mixer",
    )(x, mod, norm_g, w_in, qg, kg, conv_w, seg, cos, sin, cache_k, cache_vt, w_out)


def _shift_rows(a, n, pos, seq_len):
    if n > 0:
        return jnp.where(pos < n, 0.0, pltpu.roll(a, n, 0))
    return jnp.where(pos >= seq_len + n, 0.0, pltpu.roll(a, a.shape[0] + n, 0))


def _pool_rows(x, pos, seq_len, r, mod_ref, gm_ref, pw_ref, ps_ref):
    gate = _mod_vec(mod_ref, r, 5)
    h_all = _modulate(x, gm_ref[...], mod_ref, r, 3)
    outs = []
    for gi, w in enumerate(POOL_WINDOWS):
        lanes = slice(gi * POOL_GROUP, (gi + 1) * POOL_GROUP)
        h = h_all[:, lanes]
        back, fwd, n = h, h, 1
        while n < w // 2:
            back = back + _shift_rows(back, n, pos, seq_len)
            fwd = fwd + _shift_rows(fwd, -n, pos, seq_len)
            n *= 2
        total = _shift_rows(back, 1, pos, seq_len) + fwd
        left = w // 2
        right = w - 1 - left
        count = jnp.minimum(pos + right + 1, seq_len) - jnp.maximum(pos - left, 0)
        diff = (total / count.astype(F32) - h).astype(BF16)
        out = jnp.dot(diff, pw_ref[lanes, :], preferred_element_type=F32) * ps_ref[:, lanes]
        outs.append(x[:, lanes] + gate[:, lanes] * out)
    return jnp.concatenate(outs, axis=1)


def _pool_tile(ctx, r, x_ref, mod_ref, gm_ref, gf_ref, gn_ref, pw_ref, ps_ref, w1_ref, w2_ref,
               y_ref, x2_ref):
    seq_len = SEQ if ctx else DEC_SEQ
    block = max(seq_len, FFN_TILE)
    pos = lax.broadcasted_iota(jnp.int32, (block, 1), 0) & (seq_len - 1)
    for b in range(MIX_TILE // block):
        rows = slice(b * block, (b + 1) * block)
        x2_ref[rows, :] = _pool_rows(x_ref[rows, :], pos, seq_len, r, mod_ref, gm_ref, pw_ref, ps_ref)

    def ffn_step(s, carry):
        rows = pl.ds(pl.multiple_of(s * FFN_TILE, FFN_TILE), FFN_TILE)
        y = _ffn_rows(x2_ref[rows, :], gf_ref[...], mod_ref, r, 6, w1_ref, w2_ref)
        y_ref[rows, :] = _rms(y, gn_ref[...])
        return carry

    lax.fori_loop(0, MIX_TILE // FFN_TILE, ffn_step, 0)


def _pool_kernel(*refs):
    ins, (yp_ref, ys_ref, x2_ref) = refs[:-3], refs[-3:]
    i = pl.program_id(0)
    r = _mod_row(i, MIX_TILE)
    is_ctx = i < _n_ctx_tiles(MIX_TILE)

    @pl.when(is_ctx)
    def _():
        _pool_tile(True, r, *ins, yp_ref, x2_ref)

    @pl.when(jnp.logical_not(is_ctx))
    def _():
        _pool_tile(False, r, *ins, ys_ref, x2_ref)


def _pool_call(x, mod, norm_g, final_g, pool_w, pool_scale, w1, w2):
    return pl.pallas_call(
        _pool_kernel,
        grid=(N_ALL // MIX_TILE,),
        in_specs=[
            _rows_spec(MIX_TILE, D_MODEL), _mod_spec(), _gain_spec(1, 1), _gain_spec(1, 2),
            _resident_spec((1, D_MODEL)),
            _resident_spec((D_MODEL, POOL_GROUP)),
            _resident_spec((1, D_MODEL), 0),
            _resident_spec((D_MODEL, 2 * D_FF)), _resident_spec((D_FF, D_MODEL)),
        ],
        out_specs=[_ctx_rows_spec(MIX_TILE, D_MODEL), _smp_rows_spec(MIX_TILE, D_MODEL)],
        out_shape=[jax.ShapeDtypeStruct((N_CTX, D_MODEL), F32),
                   jax.ShapeDtypeStruct((N_SMP, D_MODEL), F32)],
        scratch_shapes=[pltpu.VMEM((MIX_TILE, D_MODEL), F32)],
        compiler_params=_params(),
        name="pool_ffn_norm",
    )(x, mod, norm_g, norm_g, final_g, pool_w, pool_scale, w1, w2)


def _rope_tables():
    t = np.arange(DEC_SEQ)
    half = HEAD_DIM // 2
    inv = ROPE_THETA ** (-np.arange(0, half, 2, dtype=np.float64) / half)
    ang_row = (t // GRID_W)[:, None] * inv[None, :]
    ang_col = (t % GRID_W)[:, None] * inv[None, :]
    cos = np.concatenate([np.cos(ang_row), np.cos(ang_row), np.cos(ang_col), np.cos(ang_col)], axis=1)
    sin = np.concatenate([-np.sin(ang_row), np.sin(ang_row), -np.sin(ang_col), np.sin(ang_col)], axis=1)
    return (jnp.asarray(np.tile(cos, (1, N_HEADS)), F32), jnp.asarray(np.tile(sin, (1, N_HEADS)), F32))


def _head_segments():
    head = np.arange(ATTN_WIDTH) // HEAD_DIM
    return jnp.asarray((head[:, None] == head[None, :]) / HEAD_DIM, BF16)


def _cache_layout(t):
    return jnp.transpose(t.reshape(BATCH, 1, N_KV_HEADS, HEAD_DIM, SEQ), (0, 1, 4, 2, 3))


def kernel(x_prompt, x_sample, c, cache_k, cache_v, c_ctx, ada_w, ada_b, norm_g, ffn_w1, ffn_w2,
           mix_w_in, mix_w_out, q_norm, k_norm, conv_w, pool_w, pool_scale, final_g):
    cvec = jnp.concatenate(
        [c_ctx[None, :], c, jnp.zeros((MOD_ROWS - 1 - DEC_BATCH, D_MODEL), F32)], axis=0)
    gains = norm_g.reshape(norm_g.shape[0], 3, 1, D_MODEL)
    cos, sin = _rope_tables()

    mod0, _ = _mod_call(_ModJob(cvec, ada_w, ada_b, 0, MOD_STEPS), [])
    x, (w_in, w_out) = _ffn_call(
        [x_prompt.reshape(N_CTX, D_MODEL), x_sample.reshape(N_SMP, D_MODEL)],
        mod0, gains, ffn_w1, ffn_w2, 0, 0,
        [_Cast(mix_w_in, (0,), FFN_CAST_BLOCKS), _Cast(mix_w_out, (0,), FFN_CAST_BLOCKS)])
    x, kt, vt = _mixer_call(
        x, mod0, gains, w_in, jnp.tile(q_norm[0], N_HEADS)[None, :],
        jnp.tile(k_norm[0], N_KV_HEADS)[None, :], conv_w, _head_segments(), cos, sin,
        cache_k[:, 0].reshape(DEC_BATCH, PAST_LEN, KV_WIDTH),
        jnp.transpose(cache_v[:, 0], (0, 2, 3, 1)).reshape(DEC_BATCH, KV_WIDTH, PAST_LEN), w_out)
    x, (pw, mod1) = _ffn_call(
        [x], mod0, gains, ffn_w1, ffn_w2, 0, 1,
        [_Cast(pool_w.reshape(pool_w.shape[0], D_MODEL, POOL_GROUP), (0,), FFN_CAST_BLOCKS)],
        _ModJob(cvec, ada_w, ada_b, 1, MOD_SIDE_BLOCKS))
    x, (w1_last, w2_last) = _ffn_call(
        [x], mod1, gains, ffn_w1, ffn_w2, 1, 0,
        [_Cast(ffn_w1, (1, 1), FFN_CAST_BLOCKS), _Cast(ffn_w2, (1, 1), FFN_CAST_BLOCKS)])
    yp, ys = _pool_call(x, mod1, gains, final_g[None, :], pw,
                        pool_scale.reshape(pool_scale.shape[0], 1, D_MODEL), w1_last, w2_last)

    return (yp.reshape(BATCH, SEQ, D_MODEL), ys.reshape(DEC_BATCH, DEC_SEQ, D_MODEL),
            _cache_layout(kt), _cache_layout(vt))
```

```python
import functools

import numpy as np
import jax
import jax.numpy as jnp
from jax import lax
from jax.experimental import pallas as pl
from jax.experimental.pallas import tpu as pltpu

F32 = jnp.float32
BF16 = jnp.bfloat16

D_MODEL = 1024
BATCH = 32
SEQ = 256
DEC_BATCH = 2
DEC_SEQ = 1024
PAST_LEN = 512
GRID_W = 64
N_HEADS = 8
N_KV_HEADS = 2
HEAD_DIM = 64
HEADS_PER_KV = N_HEADS // N_KV_HEADS
ATTN_WIDTH = N_HEADS * HEAD_DIM
KV_WIDTH = N_KV_HEADS * HEAD_DIM
CONV_WIDTH = D_MODEL // 2
MIX_WIDTH = ATTN_WIDTH + CONV_WIDTH
IN_WIDTH = ATTN_WIDTH + 2 * KV_WIDTH + 3 * CONV_WIDTH
D_FF = 2816
POOL_WINDOWS = (2, 4, 8, 16)
POOL_GROUP = D_MODEL // len(POOL_WINDOWS)
N_MOD = 9
ROPE_THETA = 10000.0
EPS = 1e-6

N_CTX = BATCH * SEQ
N_SMP = DEC_BATCH * DEC_SEQ
N_ALL = N_CTX + N_SMP
MIX_TILE = 1024
FFN_TILE = 512
FF_CHUNK = 256
N_FF_CHUNKS = D_FF // FF_CHUNK
STAGE_SLOTS = 2
Q_ROWS = 256
SEQ_PER_TILE = MIX_TILE // SEQ
MOD_ROWS = 8
MOD_WIDTH = N_MOD * D_MODEL
MOD_STEPS = 4
MOD_SIDE_BLOCKS = 18
FFN_CAST_BLOCKS = 16
VMEM_LIMIT = 56 * 1024 * 1024


def _n_ctx_tiles(tile):
    return N_CTX // tile


def _mod_row(i, tile):
    first = _n_ctx_tiles(tile)
    return jnp.where(i < first, 0, 1 + (i - first) * tile // DEC_SEQ)


def _rms(x, g):
    return x * lax.rsqrt(jnp.mean(x * x, axis=-1, keepdims=True) + EPS) * g


def _mod_vec(mod_ref, r, j):
    return mod_ref[pl.ds(r, 1), j * D_MODEL:(j + 1) * D_MODEL]


def _modulate(x, g, mod_ref, r, j):
    return _rms(x, g) * (1 + _mod_vec(mod_ref, r, j + 1)) + _mod_vec(mod_ref, r, j)


def _ffn_hidden(x, g, mod_ref, r, j):
    return _modulate(x, g, mod_ref, r, j).astype(BF16)


def _ffn_apply(x, h, mod_ref, r, j, w1_ref, w2_ref, before_chunk=None):
    acc = None
    for c in range(N_FF_CHUNKS):
        lo = c * FF_CHUNK
        if before_chunk is not None:
            before_chunk(c)
        gate = jnp.dot(h, w1_ref[:, lo:lo + FF_CHUNK], preferred_element_type=F32)
        up = jnp.dot(h, w1_ref[:, D_FF + lo:D_FF + lo + FF_CHUNK], preferred_element_type=F32)
        act = (gate / (1 + jnp.exp(-gate)) * up).astype(BF16)
        y = jnp.dot(act, w2_ref[lo:lo + FF_CHUNK, :], preferred_element_type=F32)
        acc = y if acc is None else acc + y
    return x + (0.5 * _mod_vec(mod_ref, r, j + 2)) * acc


def _ffn_rows(x, g, mod_ref, r, j, w1_ref, w2_ref):
    return _ffn_apply(x, _ffn_hidden(x, g, mod_ref, r, j), mod_ref, r, j, w1_ref, w2_ref)


class _Cast:
    def __init__(self, src, lead, n_blocks):
        self.src, self.lead, self.n_blocks = src, tuple(lead), n_blocks
        self.rows, self.cols = src.shape[len(lead):]
        self.block_rows = self.rows // n_blocks

    def in_spec(self, step):
        lead, last = self.lead, self.n_blocks - 1
        return pl.BlockSpec((None,) * len(lead) + (self.block_rows, self.cols),
                            lambda *g: lead + (jnp.minimum(step(*g), last), 0))

    def out_spec(self, step):
        last = self.n_blocks - 1
        return pl.BlockSpec((self.block_rows, self.cols), lambda *g: (jnp.minimum(step(*g), last), 0))

    def out_shape(self):
        return jax.ShapeDtypeStruct((self.rows, self.cols), BF16)


def _cast_blocks(step, n_blocks, srcs, dsts):
    @pl.when(step < n_blocks)
    def _():
        for s, d in zip(srcs, dsts):
            d[...] = s[...].astype(BF16)


class _WeightStream:
    def __init__(self, w1_hbm, w2_hbm, layer, which, w1_s, w2_s, gate_st, up_st, down_st, sems):
        self.w1_hbm, self.w2_hbm, self.lead = w1_hbm, w2_hbm, (layer, which)
        self.w1_s, self.w2_s = w1_s, w2_s
        self.stages, self.sems = (gate_st, up_st, down_st), sems

    def _copies(self, c):
        slot, lo = c % STAGE_SLOTS, c * FF_CHUNK
        l, w = self.lead
        srcs = (self.w1_hbm.at[l, w, :, pl.ds(lo, FF_CHUNK)],
                self.w1_hbm.at[l, w, :, pl.ds(D_FF + lo, FF_CHUNK)],
                self.w2_hbm.at[l, w, pl.ds(lo, FF_CHUNK), :])
        return [pltpu.make_async_copy(src, st.at[slot], self.sems.at[k, slot])
                for k, (src, st) in enumerate(zip(srcs, self.stages))]

    def start(self, c):
        for cp in self._copies(c):
            cp.start()

    def prime(self):
        for c in range(STAGE_SLOTS):
            self.start(c)

    def land(self, c):
        for cp in self._copies(c):
            cp.wait()
        slot, lo = c % STAGE_SLOTS, c * FF_CHUNK
        gate_st, up_st, down_st = self.stages
        self.w1_s[:, lo:lo + FF_CHUNK] = gate_st[slot].astype(BF16)
        self.w1_s[:, D_FF + lo:D_FF + lo + FF_CHUNK] = up_st[slot].astype(BF16)
        self.w2_s[lo:lo + FF_CHUNK, :] = down_st[slot].astype(BF16)
        if c + STAGE_SLOTS < N_FF_CHUNKS:
            self.start(c + STAGE_SLOTS)


def _weight_stream_specs():
    return [pl.BlockSpec(memory_space=pl.ANY), pl.BlockSpec(memory_space=pl.ANY)]


N_STREAM_SCRATCH = 6


def _weight_stream_scratch():
    return [pltpu.VMEM((D_MODEL, 2 * D_FF), BF16), pltpu.VMEM((D_FF, D_MODEL), BF16),
            pltpu.VMEM((STAGE_SLOTS, D_MODEL, FF_CHUNK), F32),
            pltpu.VMEM((STAGE_SLOTS, D_MODEL, FF_CHUNK), F32),
            pltpu.VMEM((STAGE_SLOTS, FF_CHUNK, D_MODEL), F32),
            pltpu.SemaphoreType.DMA((3, STAGE_SLOTS))]


def _mod_block(c_ref, w_ref, b_ref):
    c = c_ref[...]
    s = (c / (1 + jnp.exp(-c))).astype(BF16)
    return jnp.dot(s, w_ref[...].astype(BF16), preferred_element_type=F32) + b_ref[...]


class _ModJob:
    def __init__(self, cvec, ada_w, ada_b, layer, n_blocks):
        self.args = (cvec, ada_w, ada_b.reshape(ada_b.shape[0], 1, MOD_WIDTH))
        self.layer, self.n_blocks, self.cols = layer, n_blocks, MOD_WIDTH // n_blocks

    def in_specs(self):
        layer, last, cols = self.layer, self.n_blocks - 1, self.cols
        block = lambda i: (layer, 0, jnp.minimum(i, last))
        return [pl.BlockSpec((MOD_ROWS, D_MODEL), lambda i: (0, 0)),
                pl.BlockSpec((None, D_MODEL, cols), block),
                pl.BlockSpec((None, 1, cols), block)]

    def out_spec(self):
        last = self.n_blocks - 1
        return pl.BlockSpec((MOD_ROWS, self.cols), lambda i: (0, jnp.minimum(i, last)))

    def out_shape(self):
        return jax.ShapeDtypeStruct((MOD_ROWS, MOD_WIDTH), F32)


def _mod_kernel(c_ref, w_ref, b_ref, *refs, n_cast):
    srcs, o_ref, dsts = refs[:n_cast], refs[n_cast], refs[n_cast + 1:]
    o_ref[...] = _mod_block(c_ref, w_ref, b_ref)
    _cast_blocks(pl.program_id(0), MOD_STEPS, srcs, dsts)


def _mod_call(job, casts):
    step = lambda i: i
    out = pl.pallas_call(
        functools.partial(_mod_kernel, n_cast=len(casts)),
        grid=(job.n_blocks,),
        in_specs=job.in_specs() + [c.in_spec(step) for c in casts],
        out_specs=[job.out_spec()] + [c.out_spec(step) for c in casts],
        out_shape=[job.out_shape()] + [c.out_shape() for c in casts],
        compiler_params=_params(),
        name="adaln_mod",
    )(*job.args, *[c.src for c in casts])
    return out[0], out[1:]


def _rows_spec(tile, width, shift=0):
    last = N_ALL // tile - 1
    return pl.BlockSpec((tile, width), lambda i: (jnp.minimum(i + shift, last), 0))


def _ctx_rows_spec(tile, width, shift=0):
    last = _n_ctx_tiles(tile) - 1
    return pl.BlockSpec((tile, width), lambda i: (jnp.minimum(i + shift, last), 0))


def _smp_rows_spec(tile, width, shift=0):
    first, last = _n_ctx_tiles(tile), N_SMP // tile - 1
    return pl.BlockSpec((tile, width), lambda i: (jnp.clip(i + shift - first, 0, last), 0))


def _mod_spec():
    return pl.BlockSpec((MOD_ROWS, MOD_WIDTH), lambda i: (0, 0))


def _gain_spec(layer, j):
    return pl.BlockSpec((None, None, 1, D_MODEL), lambda i: (layer, j, 0, 0))


def _resident_spec(shape, *lead):
    index = tuple(lead) + (0,) * len(shape)
    return pl.BlockSpec((None,) * len(lead) + tuple(shape), lambda i: index,
                        pipeline_mode=pl.Buffered(1))


def _params():
    return pltpu.CompilerParams(dimension_semantics=("arbitrary",), vmem_limit_bytes=VMEM_LIMIT)


def _ffn_kernel(*refs, j, n_in, ahead, n_cast, mod_blocks, layer, which):
    n_x = n_in * (2 if ahead else 1)
    n_side = n_cast + (3 if mod_blocks else 0)
    n_scratch = N_STREAM_SCRATCH + (2 if ahead else 0)
    x_refs, next_refs = refs[:n_in], refs[n_in:n_x]
    mod_ref, g_ref, w1_hbm, w2_hbm = refs[n_x:n_x + 4]
    side_in = refs[n_x + 4:n_x + 4 + n_side]
    o_ref = refs[n_x + 4 + n_side]
    side_out = refs[n_x + 5 + n_side:len(refs) - n_scratch]
    stream = _WeightStream(w1_hbm, w2_hbm, layer, which, *refs[len(refs) - N_STREAM_SCRATCH:])
    i = pl.program_id(0)

    def tile(rows_refs, t):
        if n_in == 2:
            return jnp.where(t < _n_ctx_tiles(FFN_TILE), rows_refs[0][...], rows_refs[1][...])
        return rows_refs[0][...]

    def hidden(rows_refs, t):
        return _ffn_hidden(tile(rows_refs, t), g_ref[...], mod_ref, _mod_row(t, FFN_TILE), j)

    def run(h, before_chunk):
        o_ref[...] = _ffn_apply(tile(x_refs, i), h, mod_ref, _mod_row(i, FFN_TILE), j,
                                stream.w1_s, stream.w2_s, before_chunk)

    if not ahead:
        @pl.when(i == 0)
        def _():
            stream.prime()
            run(hidden(x_refs, i), stream.land)

        @pl.when(i > 0)
        def _():
            run(hidden(x_refs, i), None)
    else:
        h_even, h_odd = refs[len(refs) - n_scratch:len(refs) - N_STREAM_SCRATCH]

        def run_ahead(h, before_chunk, h_next):
            run(h, before_chunk)
            h_next[...] = hidden(next_refs, jnp.minimum(i + 1, N_ALL // FFN_TILE - 1))

        @pl.when(i == 0)
        def _():
            stream.prime()
            run_ahead(hidden(x_refs, i), stream.land, h_odd)

        @pl.when(i % 2 == 1)
        def _():
            run_ahead(h_odd[...], None, h_even)

        @pl.when(jnp.logical_and(i > 0, i % 2 == 0))
        def _():
            run_ahead(h_even[...], None, h_odd)

    _cast_blocks(i, FFN_CAST_BLOCKS, side_in[:n_cast], side_out[:n_cast])
    if mod_blocks:
        @pl.when(i < mod_blocks)
        def _():
            side_out[n_cast][...] = _mod_block(*side_in[n_cast:])


def _ffn_call(xs, mod, norm_g, w1, w2, layer, which, casts=(), mod_job=None):
    step = lambda i: i
    ahead = len(xs) == 1
    if len(xs) == 2:
        x_specs = [_ctx_rows_spec(FFN_TILE, D_MODEL), _smp_rows_spec(FFN_TILE, D_MODEL)]
    else:
        x_specs = [_rows_spec(FFN_TILE, D_MODEL, shift) for shift in (0, 1)]
    jobs = [mod_job] if mod_job else []
    out = pl.pallas_call(
        functools.partial(_ffn_kernel, j=6 * which, n_in=len(xs), ahead=ahead, n_cast=len(casts),
                          mod_blocks=mod_job.n_blocks if mod_job else 0, layer=layer, which=which),
        grid=(N_ALL // FFN_TILE,),
        in_specs=x_specs + [_mod_spec(), _gain_spec(layer, 2 * which)] + _weight_stream_specs()
        + [c.in_spec(step) for c in casts] + [s for m in jobs for s in m.in_specs()],
        out_specs=[_rows_spec(FFN_TILE, D_MODEL)] + [c.out_spec(step) for c in casts]
        + [m.out_spec() for m in jobs],
        out_shape=[jax.ShapeDtypeStruct((N_ALL, D_MODEL), F32)] + [c.out_shape() for c in casts]
        + [m.out_shape() for m in jobs],
        scratch_shapes=([pltpu.VMEM((FFN_TILE, D_MODEL), BF16)] * 2 if ahead else [])
        + _weight_stream_scratch(),
        compiler_params=_params(),
        name="ffn",
    )(*xs, *(xs if ahead else []), mod, norm_g, w1, w2, *[c.src for c in casts], *[a for m in jobs for a in m.args])
    return out[0], out[1:]


def _head_mean_sq(x, seg):
    sq = x * x
    hi = sq.astype(BF16)
    lo = (sq - hi.astype(F32)).astype(BF16)
    return (jnp.dot(hi, seg, preferred_element_type=F32)
            + jnp.dot(lo, seg, preferred_element_type=F32))


def _rope(x, cos, sin_signed):
    width = x.shape[-1]
    lane = lax.broadcasted_iota(jnp.int32, (1, width), 1)
    partner = jnp.where((lane & 31) < 16,
                        pltpu.roll(x, width - 16, 1), pltpu.roll(x, 16, 1))
    return x * cos + partner * sin_signed


def _stack_heads(q_ref, rows, kvh):
    first = kvh * HEADS_PER_KV
    return jnp.concatenate(
        [q_ref[rows, (first + g) * HEAD_DIM:(first + g + 1) * HEAD_DIM] for g in range(HEADS_PER_KV)],
        axis=0)


ONES_ROWS = 16


def _attend_group(q, k, v1t):
    st = lax.dot_general(k, q, (((1,), (1,)), ((), ())), preferred_element_type=F32)
    e = jnp.exp(st - jnp.max(st, axis=0, keepdims=True)).astype(BF16)
    ot = jnp.dot(v1t, e, preferred_element_type=F32)
    return ot[:HEAD_DIM] / ot[HEAD_DIM:HEAD_DIM + 1]


def _attend_rows(q_s, rows, n, kv, mix_s):
    heads = []
    for kvh in range(N_KV_HEADS):
        k, v1t = kv(kvh)
        ot = _attend_group(_stack_heads(q_s, rows, kvh), k, v1t)
        heads += [ot[:, g * n:(g + 1) * n] for g in range(HEADS_PER_KV)]
    mix_s[rows, :ATTN_WIDTH] = jnp.concatenate(heads, axis=0).T.astype(BF16)


def _with_ones(vt):
    return jnp.concatenate([vt, jnp.ones((ONES_ROWS, vt.shape[1]), BF16)], axis=0)


def _mixer_tile(ctx, r, x_ref, mod_ref, g_ref, w_ref, qg_ref, kg_ref, cw_ref, seg_ref, cos_ref, sin_ref,
                ck_ref, cvt_ref, wo_ref, o_ref, kt_ref, vt_ref, q_s, k_s, vt_s, mix_s):
    seq_len = SEQ if ctx else DEC_SEQ
    x = x_ref[...]
    h = _modulate(x, g_ref[...], mod_ref, r, 3).astype(BF16)

    def proj(lo, width):
        return jnp.dot(h, w_ref[:, lo:lo + width], preferred_element_type=F32)

    q = proj(0, ATTN_WIDTH)
    k = proj(ATTN_WIDTH, KV_WIDTH)
    v = proj(ATTN_WIDTH + KV_WIDTH, KV_WIDTH)
    q = q * lax.rsqrt(_head_mean_sq(q, seg_ref[...]) + EPS) * (qg_ref[...] * HEAD_DIM ** -0.5)
    k = k * lax.rsqrt(_head_mean_sq(k, seg_ref[:KV_WIDTH, :KV_WIDTH]) + EPS) * kg_ref[...]
    if ctx:
        q_s[...] = q.astype(BF16)
        k_s[...] = k.astype(BF16)
        for b in range(SEQ_PER_TILE):
            vt = v[b * SEQ:(b + 1) * SEQ, :].T
            kt_ref[b] = k[b * SEQ:(b + 1) * SEQ, :].T
            vt_ref[b] = vt
            vt_s[:, b * SEQ:(b + 1) * SEQ] = vt.astype(BF16)
    else:
        q_s[...] = _rope(q, cos_ref[...], sin_ref[...]).astype(BF16)
        k_s[...] = _rope(k, cos_ref[:, :KV_WIDTH], sin_ref[:, :KV_WIDTH]).astype(BF16)
        vt_s[...] = v.T.astype(BF16)

    base = ATTN_WIDTH + 2 * KV_WIDTH
    z = proj(base + CONV_WIDTH, CONV_WIDTH) * proj(base + 2 * CONV_WIDTH, CONV_WIDTH)
    pos = lax.broadcasted_iota(jnp.int32, (MIX_TILE, 1), 0) & (seq_len - 1)
    z_prev = jnp.where(pos == 0, 0.0, pltpu.roll(z, 1, 0))
    z_next = jnp.where(pos == seq_len - 1, 0.0, pltpu.roll(z, MIX_TILE - 1, 0))
    conv = z_prev * cw_ref[0:1, :] + z * cw_ref[1:2, :] + z_next * cw_ref[2:3, :]
    mix_s[:, ATTN_WIDTH:] = (proj(base, CONV_WIDTH) * conv).astype(BF16)

    def head(kvh):
        return slice(kvh * HEAD_DIM, (kvh + 1) * HEAD_DIM)

    if ctx:
        for b in range(SEQ_PER_TILE):
            rows = slice(b * SEQ, (b + 1) * SEQ)
            _attend_rows(q_s, rows, SEQ,
                         lambda kvh: (k_s[rows, head(kvh)], _with_ones(vt_s[head(kvh), rows])), mix_s)
    else:
        kv = [(jnp.concatenate([ck_ref[:, head(kvh)].astype(BF16), k_s[:, head(kvh)]], axis=0),
               _with_ones(jnp.concatenate([cvt_ref[head(kvh), :].astype(BF16), vt_s[head(kvh), :]],
                                          axis=1)))
              for kvh in range(N_KV_HEADS)]
        for t in range(MIX_TILE // Q_ROWS):
            _attend_rows(q_s, slice(t * Q_ROWS, (t + 1) * Q_ROWS), Q_ROWS, lambda kvh: kv[kvh], mix_s)

    out = (jnp.dot(mix_s[:, :ATTN_WIDTH], wo_ref[:ATTN_WIDTH, :], preferred_element_type=F32)
           + jnp.dot(mix_s[:, ATTN_WIDTH:], wo_ref[ATTN_WIDTH:, :], preferred_element_type=F32))
    o_ref[...] = x + _mod_vec(mod_ref, r, 5) * out


def _mixer_kernel(*refs):
    i = pl.program_id(0)
    r = _mod_row(i, MIX_TILE)
    is_ctx = i < _n_ctx_tiles(MIX_TILE)

    @pl.when(is_ctx)
    def _():
        _mixer_tile(True, r, *refs)

    @pl.when(jnp.logical_not(is_ctx))
    def _():
        _mixer_tile(False, r, *refs)


def _mixer_call(x, mod, norm_g, w_in, qg, kg, conv_w, seg, cos, sin, cache_k, cache_vt, w_out):
    first_smp = _n_ctx_tiles(MIX_TILE)
    smp_batch = lambda i: (jnp.maximum(i - first_smp, 0), 0, 0)
    kvt_spec = pl.BlockSpec((SEQ_PER_TILE, KV_WIDTH, SEQ), lambda i: (jnp.minimum(i, first_smp - 1), 0, 0))
    kvt_shape = jax.ShapeDtypeStruct((BATCH, KV_WIDTH, SEQ), F32)
    return pl.pallas_call(
        _mixer_kernel,
        grid=(N_ALL // MIX_TILE,),
        in_specs=[
            _rows_spec(MIX_TILE, D_MODEL), _mod_spec(), _gain_spec(0, 1),
            _resident_spec((D_MODEL, IN_WIDTH)),
            _resident_spec((1, ATTN_WIDTH)), _resident_spec((1, KV_WIDTH)),
            _resident_spec((3, CONV_WIDTH), 0),
            _resident_spec((ATTN_WIDTH, ATTN_WIDTH)),
            _resident_spec((DEC_SEQ, ATTN_WIDTH)), _resident_spec((DEC_SEQ, ATTN_WIDTH)),
            pl.BlockSpec((None, PAST_LEN, KV_WIDTH), smp_batch),
            pl.BlockSpec((None, KV_WIDTH, PAST_LEN), smp_batch),
            _resident_spec((MIX_WIDTH, D_MODEL)),
        ],
        out_specs=[_rows_spec(MIX_TILE, D_MODEL), kvt_spec, kvt_spec],
        out_shape=[jax.ShapeDtypeStruct((N_ALL, D_MODEL), F32), kvt_shape, kvt_shape],
        scratch_shapes=[pltpu.VMEM((MIX_TILE, ATTN_WIDTH), BF16),
                        pltpu.VMEM((MIX_TILE, KV_WIDTH), BF16),
                        pltpu.VMEM((KV_WIDTH, MIX_TILE), BF16),
                        pltpu.VMEM((MIX_TILE, MIX_WIDTH), BF16)],
        compiler_params=_params(),
        name="mixer",
    )(x, mod, norm_g, w_in, qg, kg, conv_w, seg, cos, sin, cache_k, cache_vt, w_out)


def _shift_rows(a, n, pos, seq_len):
    if n > 0:
        return jnp.where(pos < n, 0.0, pltpu.roll(a, n, 0))
    return jnp.where(pos >= seq_len + n, 0.0, pltpu.roll(a, a.shape[0] + n, 0))


def _pool_rows(x, pos, seq_len, r, mod_ref, gm_ref, pw_ref, ps_ref):
    gate = _mod_vec(mod_ref, r, 5)
    h_all = _modulate(x, gm_ref[...], mod_ref, r, 3)
    outs = []
    for gi, w in enumerate(POOL_WINDOWS):
        lanes = slice(gi * POOL_GROUP, (gi + 1) * POOL_GROUP)
        h = h_all[:, lanes]
        back, fwd, n = h, h, 1
        while n < w // 2:
            back = back + _shift_rows(back, n, pos, seq_len)
            fwd = fwd + _shift_rows(fwd, -n, pos, seq_len)
            n *= 2
        total = _shift_rows(back, 1, pos, seq_len) + fwd
        left = w // 2
        right = w - 1 - left
        count = jnp.minimum(pos + right + 1, seq_len) - jnp.maximum(pos - left, 0)
        diff = (total / count.astype(F32) - h).astype(BF16)
        out = jnp.dot(diff, pw_ref[lanes, :], preferred_element_type=F32) * ps_ref[:, lanes]
        outs.append(x[:, lanes] + gate[:, lanes] * out)
    return jnp.concatenate(outs, axis=1)


def _pool_tile(ctx, r, x_ref, mod_ref, gm_ref, gf_ref, gn_ref, pw_ref, ps_ref, w1_ref, w2_ref,
               y_ref, x2_ref):
    seq_len = SEQ if ctx else DEC_SEQ
    block = max(seq_len, FFN_TILE)
    pos = lax.broadcasted_iota(jnp.int32, (block, 1), 0) & (seq_len - 1)
    for b in range(MIX_TILE // block):
        rows = slice(b * block, (b + 1) * block)
        x2_ref[rows, :] = _pool_rows(x_ref[rows, :], pos, seq_len, r, mod_ref, gm_ref, pw_ref, ps_ref)

    def ffn_step(s, carry):
        rows = pl.ds(pl.multiple_of(s * FFN_TILE, FFN_TILE), FFN_TILE)
        y = _ffn_rows(x2_ref[rows, :], gf_ref[...], mod_ref, r, 6, w1_ref, w2_ref)
        y_ref[rows, :] = _rms(y, gn_ref[...])
        return carry

    lax.fori_loop(0, MIX_TILE // FFN_TILE, ffn_step, 0)


def _pool_kernel(*refs):
    ins, (yp_ref, ys_ref, x2_ref) = refs[:-3], refs[-3:]
    i = pl.program_id(0)
    r = _mod_row(i, MIX_TILE)
    is_ctx = i < _n_ctx_tiles(MIX_TILE)

    @pl.when(is_ctx)
    def _():
        _pool_tile(True, r, *ins, yp_ref, x2_ref)

    @pl.when(jnp.logical_not(is_ctx))
    def _():
        _pool_tile(False, r, *ins, ys_ref, x2_ref)


def _pool_call(x, mod, norm_g, final_g, pool_w, pool_scale, w1, w2):
    return pl.pallas_call(
        _pool_kernel,
        grid=(N_ALL // MIX_TILE,),
        in_specs=[
            _rows_spec(MIX_TILE, D_MODEL), _mod_spec(), _gain_spec(1, 1), _gain_spec(1, 2),
            _resident_spec((1, D_MODEL)),
            _resident_spec((D_MODEL, POOL_GROUP)),
            _resident_spec((1, D_MODEL), 0),
            _resident_spec((D_MODEL, 2 * D_FF)), _resident_spec((D_FF, D_MODEL)),
        ],
        out_specs=[_ctx_rows_spec(MIX_TILE, D_MODEL), _smp_rows_spec(MIX_TILE, D_MODEL)],
        out_shape=[jax.ShapeDtypeStruct((N_CTX, D_MODEL), F32),
                   jax.ShapeDtypeStruct((N_SMP, D_MODEL), F32)],
        scratch_shapes=[pltpu.VMEM((MIX_TILE, D_MODEL), F32)],
        compiler_params=_params(),
        name="pool_ffn_norm",
    )(x, mod, norm_g, norm_g, final_g, pool_w, pool_scale, w1, w2)


def _rope_tables():
    t = np.arange(DEC_SEQ)
    half = HEAD_DIM // 2
    inv = ROPE_THETA ** (-np.arange(0, half, 2, dtype=np.float64) / half)
    ang_row = (t // GRID_W)[:, None] * inv[None, :]
    ang_col = (t % GRID_W)[:, None] * inv[None, :]
    cos = np.concatenate([np.cos(ang_row), np.cos(ang_row), np.cos(ang_col), np.cos(ang_col)], axis=1)
    sin = np.concatenate([-np.sin(ang_row), np.sin(ang_row), -np.sin(ang_col), np.sin(ang_col)], axis=1)
    return (jnp.asarray(np.tile(cos, (1, N_HEADS)), F32), jnp.asarray(np.tile(sin, (1, N_HEADS)), F32))


def _head_segments():
    head = np.arange(ATTN_WIDTH) // HEAD_DIM
    return jnp.asarray((head[:, None] == head[None, :]) / HEAD_DIM, BF16)


def _cache_layout(t):
    return jnp.transpose(t.reshape(BATCH, 1, N_KV_HEADS, HEAD_DIM, SEQ), (0, 1, 4, 2, 3))


def kernel(x_prompt, x_sample, c, cache_k, cache_v, c_ctx, ada_w, ada_b, norm_g, ffn_w1, ffn_w2,
           mix_w_in, mix_w_out, q_norm, k_norm, conv_w, pool_w, pool_scale, final_g):
    cvec = jnp.concatenate(
        [c_ctx[None, :], c, jnp.zeros((MOD_ROWS - 1 - DEC_BATCH, D_MODEL), F32)], axis=0)
    gains = norm_g.reshape(norm_g.shape[0], 3, 1, D_MODEL)
    cos, sin = _rope_tables()

    mod0, _ = _mod_call(_ModJob(cvec, ada_w, ada_b, 0, MOD_STEPS), [])
    x, (w_in, w_out) = _ffn_call(
        [x_prompt.reshape(N_CTX, D_MODEL), x_sample.reshape(N_SMP, D_MODEL)],
        mod0, gains, ffn_w1, ffn_w2, 0, 0,
        [_Cast(mix_w_in, (0,), FFN_CAST_BLOCKS), _Cast(mix_w_out, (0,), FFN_CAST_BLOCKS)])
    x, kt, vt = _mixer_call(
        x, mod0, gains, w_in, jnp.tile(q_norm[0], N_HEADS)[None, :],
        jnp.tile(k_norm[0], N_KV_HEADS)[None, :], conv_w, _head_segments(), cos, sin,
        cache_k[:, 0].reshape(DEC_BATCH, PAST_LEN, KV_WIDTH),
        jnp.transpose(cache_v[:, 0], (0, 2, 3, 1)).reshape(DEC_BATCH, KV_WIDTH, PAST_LEN), w_out)
    x, (pw, mod1) = _ffn_call(
        [x], mod0, gains, ffn_w1, ffn_w2, 0, 1,
        [_Cast(pool_w.reshape(pool_w.shape[0], D_MODEL, POOL_GROUP), (0,), FFN_CAST_BLOCKS)],
        _ModJob(cvec, ada_w, ada_b, 1, MOD_SIDE_BLOCKS))
    x, (w1_last, w2_last) = _ffn_call(
        [x], mod1, gains, ffn_w1, ffn_w2, 1, 0,
        [_Cast(ffn_w1, (1, 1), FFN_CAST_BLOCKS), _Cast(ffn_w2, (1, 1), FFN_CAST_BLOCKS)])
    yp, ys = _pool_call(x, mod1, gains, final_g[None, :], pw,
                        pool_scale.reshape(pool_scale.shape[0], 1, D_MODEL), w1_last, w2_last)

    return (yp.reshape(BATCH, SEQ, D_MODEL), ys.reshape(DEC_BATCH, DEC_SEQ, D_MODEL),
            _cache_layout(kt), _cache_layout(vt))
```

```python
import functools

import numpy as np
import jax
import jax.numpy as jnp
from jax import lax
from jax.experimental import pallas as pl
from jax.experimental.pallas import tpu as pltpu

F32 = jnp.float32
BF16 = jnp.bfloat16

D_MODEL = 1024
BATCH = 32
SEQ = 256
DEC_BATCH = 2
DEC_SEQ = 1024
PAST_LEN = 512
GRID_W = 64
N_HEADS = 8
N_KV_HEADS = 2
HEAD_DIM = 64
HEADS_PER_KV = N_HEADS // N_KV_HEADS
ATTN_WIDTH = N_HEADS * HEAD_DIM
KV_WIDTH = N_KV_HEADS * HEAD_DIM
CONV_WIDTH = D_MODEL // 2
MIX_WIDTH = ATTN_WIDTH + CONV_WIDTH
IN_WIDTH = ATTN_WIDTH + 2 * KV_WIDTH + 3 * CONV_WIDTH
D_FF = 2816
POOL_WINDOWS = (2, 4, 8, 16)
POOL_GROUP = D_MODEL // len(POOL_WINDOWS)
N_MOD = 9
ROPE_THETA = 10000.0
EPS = 1e-6

N_CTX = BATCH * SEQ
N_SMP = DEC_BATCH * DEC_SEQ
N_ALL = N_CTX + N_SMP
MIX_TILE = 1024
FFN_TILE = 512
FF_CHUNK = 256
N_FF_CHUNKS = D_FF // FF_CHUNK
STAGE_SLOTS = 2
Q_ROWS = 256
SEQ_PER_TILE = MIX_TILE // SEQ
MOD_ROWS = 8
MOD_WIDTH = N_MOD * D_MODEL
MOD_STEPS = 4
MOD_SIDE_BLOCKS = 18
FFN_CAST_BLOCKS = 16
VMEM_LIMIT = 56 * 1024 * 1024


def _n_ctx_tiles(tile):
    return N_CTX // tile


def _mod_row(i, tile):
    first = _n_ctx_tiles(tile)
    return jnp.where(i < first, 0, 1 + (i - first) * tile // DEC_SEQ)


def _rms(x, g):
    return x * lax.rsqrt(jnp.mean(x * x, axis=-1, keepdims=True) + EPS) * g


def _mod_vec(mod_ref, r, j):
    return mod_ref[pl.ds(r, 1), j * D_MODEL:(j + 1) * D_MODEL]


def _modulate(x, g, mod_ref, r, j):
    return _rms(x, g) * (1 + _mod_vec(mod_ref, r, j + 1)) + _mod_vec(mod_ref, r, j)


def _ffn_rows(x, g, mod_ref, r, j, w1_ref, w2_ref, act_ref, before_chunk=None):
    h = _modulate(x, g, mod_ref, r, j).astype(BF16)
    for c in range(N_FF_CHUNKS):
        lo = c * FF_CHUNK
        if before_chunk is not None:
            before_chunk(c)
        gate = jnp.dot(h, w1_ref[:, lo:lo + FF_CHUNK], preferred_element_type=F32)
        up = jnp.dot(h, w1_ref[:, D_FF + lo:D_FF + lo + FF_CHUNK], preferred_element_type=F32)
        act_ref[:, lo:lo + FF_CHUNK] = (gate / (1 + jnp.exp(-gate)) * up).astype(BF16)
    y = jnp.dot(act_ref[...], w2_ref[...], preferred_element_type=F32)
    return x + (0.5 * _mod_vec(mod_ref, r, j + 2)) * y


class _Cast:
    def __init__(self, src, lead, n_blocks):
        self.src, self.lead, self.n_blocks = src, tuple(lead), n_blocks
        self.rows, self.cols = src.shape[len(lead):]
        self.block_rows = self.rows // n_blocks

    def in_spec(self, step):
        lead, last = self.lead, self.n_blocks - 1
        return pl.BlockSpec((None,) * len(lead) + (self.block_rows, self.cols),
                            lambda *g: lead + (jnp.minimum(step(*g), last), 0))

    def out_spec(self, step):
        last = self.n_blocks - 1
        return pl.BlockSpec((self.block_rows, self.cols), lambda *g: (jnp.minimum(step(*g), last), 0))

    def out_shape(self):
        return jax.ShapeDtypeStruct((self.rows, self.cols), BF16)


def _cast_blocks(step, n_blocks, srcs, dsts):
    @pl.when(step < n_blocks)
    def _():
        for s, d in zip(srcs, dsts):
            d[...] = s[...].astype(BF16)


class _WeightStream:
    def __init__(self, w1_hbm, w2_hbm, layer, which, w1_s, w2_s, gate_st, up_st, down_st, sems):
        self.w1_hbm, self.w2_hbm, self.lead = w1_hbm, w2_hbm, (layer, which)
        self.w1_s, self.w2_s = w1_s, w2_s
        self.stages, self.sems = (gate_st, up_st, down_st), sems

    def _copies(self, c):
        slot, lo = c % STAGE_SLOTS, c * FF_CHUNK
        l, w = self.lead
        srcs = (self.w1_hbm.at[l, w, :, pl.ds(lo, FF_CHUNK)],
                self.w1_hbm.at[l, w, :, pl.ds(D_FF + lo, FF_CHUNK)],
                self.w2_hbm.at[l, w, pl.ds(lo, FF_CHUNK), :])
        return [pltpu.make_async_copy(src, st.at[slot], self.sems.at[k, slot])
                for k, (src, st) in enumerate(zip(srcs, self.stages))]

    def start(self, c):
        for cp in self._copies(c):
            cp.start()

    def prime(self):
        for c in range(STAGE_SLOTS):
            self.start(c)

    def land(self, c):
        for cp in self._copies(c):
            cp.wait()
        slot, lo = c % STAGE_SLOTS, c * FF_CHUNK
        gate_st, up_st, down_st = self.stages
        self.w1_s[:, lo:lo + FF_CHUNK] = gate_st[slot].astype(BF16)
        self.w1_s[:, D_FF + lo:D_FF + lo + FF_CHUNK] = up_st[slot].astype(BF16)
        self.w2_s[lo:lo + FF_CHUNK, :] = down_st[slot].astype(BF16)
        if c + STAGE_SLOTS < N_FF_CHUNKS:
            self.start(c + STAGE_SLOTS)


def _weight_stream_specs():
    return [pl.BlockSpec(memory_space=pl.ANY), pl.BlockSpec(memory_space=pl.ANY)]


N_STREAM_SCRATCH = 6


def _weight_stream_scratch():
    return [pltpu.VMEM((D_MODEL, 2 * D_FF), BF16), pltpu.VMEM((D_FF, D_MODEL), BF16),
            pltpu.VMEM((STAGE_SLOTS, D_MODEL, FF_CHUNK), F32),
            pltpu.VMEM((STAGE_SLOTS, D_MODEL, FF_CHUNK), F32),
            pltpu.VMEM((STAGE_SLOTS, FF_CHUNK, D_MODEL), F32),
            pltpu.SemaphoreType.DMA((3, STAGE_SLOTS))]


def _mod_block(c_ref, w_ref, b_ref):
    c = c_ref[...]
    s = (c / (1 + jnp.exp(-c))).astype(BF16)
    return jnp.dot(s, w_ref[...].astype(BF16), preferred_element_type=F32) + b_ref[...]


class _ModJob:
    def __init__(self, cvec, ada_w, ada_b, layer, n_blocks):
        self.args = (cvec, ada_w, ada_b.reshape(ada_b.shape[0], 1, MOD_WIDTH))
        self.layer, self.n_blocks, self.cols = layer, n_blocks, MOD_WIDTH // n_blocks

    def in_specs(self):
        layer, last, cols = self.layer, self.n_blocks - 1, self.cols
        block = lambda i: (layer, 0, jnp.minimum(i, last))
        return [pl.BlockSpec((MOD_ROWS, D_MODEL), lambda i: (0, 0)),
                pl.BlockSpec((None, D_MODEL, cols), block),
                pl.BlockSpec((None, 1, cols), block)]

    def out_spec(self):
        last = self.n_blocks - 1
        return pl.BlockSpec((MOD_ROWS, self.cols), lambda i: (0, jnp.minimum(i, last)))

    def out_shape(self):
        return jax.ShapeDtypeStruct((MOD_ROWS, MOD_WIDTH), F32)


def _mod_kernel(c_ref, w_ref, b_ref, *refs, n_cast):
    srcs, o_ref, dsts = refs[:n_cast], refs[n_cast], refs[n_cast + 1:]
    o_ref[...] = _mod_block(c_ref, w_ref, b_ref)
    _cast_blocks(pl.program_id(0), MOD_STEPS, srcs, dsts)


def _mod_call(job, casts):
    step = lambda i: i
    out = pl.pallas_call(
        functools.partial(_mod_kernel, n_cast=len(casts)),
        grid=(job.n_blocks,),
        in_specs=job.in_specs() + [c.in_spec(step) for c in casts],
        out_specs=[job.out_spec()] + [c.out_spec(step) for c in casts],
        out_shape=[job.out_shape()] + [c.out_shape() for c in casts],
        compiler_params=_params(),
        name="adaln_mod",
    )(*job.args, *[c.src for c in casts])
    return out[0], out[1:]


def _rows_spec(tile, width):
    return pl.BlockSpec((tile, width), lambda i: (i, 0))


def _ctx_rows_spec(tile, width):
    last = _n_ctx_tiles(tile) - 1
    return pl.BlockSpec((tile, width), lambda i: (jnp.minimum(i, last), 0))


def _smp_rows_spec(tile, width):
    first = _n_ctx_tiles(tile)
    return pl.BlockSpec((tile, width), lambda i: (jnp.maximum(i - first, 0), 0))


def _mod_spec():
    return pl.BlockSpec((MOD_ROWS, MOD_WIDTH), lambda i: (0, 0))


def _gain_spec(layer, j):
    return pl.BlockSpec((None, None, 1, D_MODEL), lambda i: (layer, j, 0, 0))


def _resident_spec(shape, *lead):
    index = tuple(lead) + (0,) * len(shape)
    return pl.BlockSpec((None,) * len(lead) + tuple(shape), lambda i: index,
                        pipeline_mode=pl.Buffered(1))


def _act_scratch():
    return pltpu.VMEM((FFN_TILE, D_FF), BF16)


def _params():
    return pltpu.CompilerParams(dimension_semantics=("arbitrary",), vmem_limit_bytes=VMEM_LIMIT)


def _ffn_kernel(*refs, j, n_in, n_cast, mod_blocks, layer, which):
    n_side = n_cast + (3 if mod_blocks else 0)
    x_refs = refs[:n_in]
    mod_ref, g_ref, w1_hbm, w2_hbm = refs[n_in:n_in + 4]
    side_in = refs[n_in + 4:n_in + 4 + n_side]
    o_ref = refs[n_in + 4 + n_side]
    side_out = refs[n_in + 5 + n_side:len(refs) - N_STREAM_SCRATCH - 1]
    act_s = refs[len(refs) - N_STREAM_SCRATCH - 1]
    stream = _WeightStream(w1_hbm, w2_hbm, layer, which, *refs[len(refs) - N_STREAM_SCRATCH:])
    i = pl.program_id(0)

    def run(before_chunk):
        if n_in == 2:
            x = jnp.where(i < _n_ctx_tiles(FFN_TILE), x_refs[0][...], x_refs[1][...])
        else:
            x = x_refs[0][...]
        o_ref[...] = _ffn_rows(x, g_ref[...], mod_ref, _mod_row(i, FFN_TILE), j,
                               stream.w1_s, stream.w2_s, act_s, before_chunk)

    @pl.when(i == 0)
    def _():
        stream.prime()
        run(stream.land)

    @pl.when(i > 0)
    def _():
        run(None)

    _cast_blocks(i, FFN_CAST_BLOCKS, side_in[:n_cast], side_out[:n_cast])
    if mod_blocks:
        @pl.when(i < mod_blocks)
        def _():
            side_out[n_cast][...] = _mod_block(*side_in[n_cast:])


def _ffn_call(xs, mod, norm_g, w1, w2, layer, which, casts=(), mod_job=None):
    step = lambda i: i
    if len(xs) == 2:
        x_specs = [_ctx_rows_spec(FFN_TILE, D_MODEL), _smp_rows_spec(FFN_TILE, D_MODEL)]
    else:
        x_specs = [_rows_spec(FFN_TILE, D_MODEL)]
    jobs = [mod_job] if mod_job else []
    out = pl.pallas_call(
        functools.partial(_ffn_kernel, j=6 * which, n_in=len(xs), n_cast=len(casts),
                          mod_blocks=mod_job.n_blocks if mod_job else 0, layer=layer, which=which),
        grid=(N_ALL // FFN_TILE,),
        in_specs=x_specs + [_mod_spec(), _gain_spec(layer, 2 * which)] + _weight_stream_specs()
        + [c.in_spec(step) for c in casts] + [s for m in jobs for s in m.in_specs()],
        out_specs=[_rows_spec(FFN_TILE, D_MODEL)] + [c.out_spec(step) for c in casts]
        + [m.out_spec() for m in jobs],
        out_shape=[jax.ShapeDtypeStruct((N_ALL, D_MODEL), F32)] + [c.out_shape() for c in casts]
        + [m.out_shape() for m in jobs],
        scratch_shapes=[_act_scratch()] + _weight_stream_scratch(),
        compiler_params=_params(),
        name="ffn",
    )(*xs, mod, norm_g, w1, w2, *[c.src for c in casts], *[a for m in jobs for a in m.args])
    return out[0], out[1:]


def _head_mean_sq(x, seg):
    sq = x * x
    hi = sq.astype(BF16)
    lo = (sq - hi.astype(F32)).astype(BF16)
    return (jnp.dot(hi, seg, preferred_element_type=F32)
            + jnp.dot(lo, seg, preferred_element_type=F32))


def _rope(x, cos, sin_signed):
    width = x.shape[-1]
    lane = lax.broadcasted_iota(jnp.int32, (1, width), 1)
    partner = jnp.where((lane & 31) < 16,
                        pltpu.roll(x, width - 16, 1), pltpu.roll(x, 16, 1))
    return x * cos + partner * sin_signed


def _stack_heads(q_ref, rows, kvh):
    first = kvh * HEADS_PER_KV
    return jnp.concatenate(
        [q_ref[rows, (first + g) * HEAD_DIM:(first + g + 1) * HEAD_DIM] for g in range(HEADS_PER_KV)],
        axis=0)


ONES_ROWS = 16


def _attend_group(q, k, v1t):
    st = lax.dot_general(k, q, (((1,), (1,)), ((), ())), preferred_element_type=F32)
    e = jnp.exp(st - jnp.max(st, axis=0, keepdims=True)).astype(BF16)
    ot = jnp.dot(v1t, e, preferred_element_type=F32)
    return ot[:HEAD_DIM] / ot[HEAD_DIM:HEAD_DIM + 1]


def _attend_rows(q_s, rows, n, kv, mix_s):
    heads = []
    for kvh in range(N_KV_HEADS):
        k, v1t = kv(kvh)
        ot = _attend_group(_stack_heads(q_s, rows, kvh), k, v1t)
        heads += [ot[:, g * n:(g + 1) * n] for g in range(HEADS_PER_KV)]
    mix_s[rows, :ATTN_WIDTH] = jnp.concatenate(heads, axis=0).T.astype(BF16)


def _with_ones(vt):
    return jnp.concatenate([vt, jnp.ones((ONES_ROWS, vt.shape[1]), BF16)], axis=0)


def _mixer_tile(ctx, r, x_ref, mod_ref, g_ref, w_ref, qg_ref, kg_ref, cw_ref, seg_ref, cos_ref, sin_ref,
                ck_ref, cvt_ref, wo_ref, o_ref, kt_ref, vt_ref, q_s, k_s, vt_s, mix_s):
    seq_len = SEQ if ctx else DEC_SEQ
    x = x_ref[...]
    h = _modulate(x, g_ref[...], mod_ref, r, 3).astype(BF16)

    def proj(lo, width):
        return jnp.dot(h, w_ref[:, lo:lo + width], preferred_element_type=F32)

    q = proj(0, ATTN_WIDTH)
    k = proj(ATTN_WIDTH, KV_WIDTH)
    v = proj(ATTN_WIDTH + KV_WIDTH, KV_WIDTH)
    q = q * lax.rsqrt(_head_mean_sq(q, seg_ref[...]) + EPS) * (qg_ref[...] * HEAD_DIM ** -0.5)
    k = k * lax.rsqrt(_head_mean_sq(k, seg_ref[:KV_WIDTH, :KV_WIDTH]) + EPS) * kg_ref[...]
    if ctx:
        q_s[...] = q.astype(BF16)
        k_s[...] = k.astype(BF16)
        for b in range(SEQ_PER_TILE):
            vt = v[b * SEQ:(b + 1) * SEQ, :].T
            kt_ref[b] = k[b * SEQ:(b + 1) * SEQ, :].T
            vt_ref[b] = vt
            vt_s[:, b * SEQ:(b + 1) * SEQ] = vt.astype(BF16)
    else:
        q_s[...] = _rope(q, cos_ref[...], sin_ref[...]).astype(BF16)
        k_s[...] = _rope(k, cos_ref[:, :KV_WIDTH], sin_ref[:, :KV_WIDTH]).astype(BF16)
        vt_s[...] = v.T.astype(BF16)

    base = ATTN_WIDTH + 2 * KV_WIDTH
    z = proj(base + CONV_WIDTH, CONV_WIDTH) * proj(base + 2 * CONV_WIDTH, CONV_WIDTH)
    pos = lax.broadcasted_iota(jnp.int32, (MIX_TILE, 1), 0) & (seq_len - 1)
    z_prev = jnp.where(pos == 0, 0.0, pltpu.roll(z, 1, 0))
    z_next = jnp.where(pos == seq_len - 1, 0.0, pltpu.roll(z, MIX_TILE - 1, 0))
    conv = z_prev * cw_ref[0:1, :] + z * cw_ref[1:2, :] + z_next * cw_ref[2:3, :]
    mix_s[:, ATTN_WIDTH:] = (proj(base, CONV_WIDTH) * conv).astype(BF16)

    def head(kvh):
        return slice(kvh * HEAD_DIM, (kvh + 1) * HEAD_DIM)

    if ctx:
        for b in range(SEQ_PER_TILE):
            rows = slice(b * SEQ, (b + 1) * SEQ)
            _attend_rows(q_s, rows, SEQ,
                         lambda kvh: (k_s[rows, head(kvh)], _with_ones(vt_s[head(kvh), rows])), mix_s)
    else:
        kv = [(jnp.concatenate([ck_ref[:, head(kvh)].astype(BF16), k_s[:, head(kvh)]], axis=0),
               _with_ones(jnp.concatenate([cvt_ref[head(kvh), :].astype(BF16), vt_s[head(kvh), :]],
                                          axis=1)))
              for kvh in range(N_KV_HEADS)]
        for t in range(MIX_TILE // Q_ROWS):
            _attend_rows(q_s, slice(t * Q_ROWS, (t + 1) * Q_ROWS), Q_ROWS, lambda kvh: kv[kvh], mix_s)

    out = (jnp.dot(mix_s[:, :ATTN_WIDTH], wo_ref[:ATTN_WIDTH, :], preferred_element_type=F32)
           + jnp.dot(mix_s[:, ATTN_WIDTH:], wo_ref[ATTN_WIDTH:, :], preferred_element_type=F32))
    o_ref[...] = x + _mod_vec(mod_ref, r, 5) * out


def _mixer_kernel(*refs):
    i = pl.program_id(0)
    r = _mod_row(i, MIX_TILE)
    is_ctx = i < _n_ctx_tiles(MIX_TILE)

    @pl.when(is_ctx)
    def _():
        _mixer_tile(True, r, *refs)

    @pl.when(jnp.logical_not(is_ctx))
    def _():
        _mixer_tile(False, r, *refs)


def _mixer_call(x, mod, norm_g, w_in, qg, kg, conv_w, seg, cos, sin, cache_k, cache_vt, w_out):
    first_smp = _n_ctx_tiles(MIX_TILE)
    smp_batch = lambda i: (jnp.maximum(i - first_smp, 0), 0, 0)
    kvt_spec = pl.BlockSpec((SEQ_PER_TILE, KV_WIDTH, SEQ), lambda i: (jnp.minimum(i, first_smp - 1), 0, 0))
    kvt_shape = jax.ShapeDtypeStruct((BATCH, KV_WIDTH, SEQ), F32)
    return pl.pallas_call(
        _mixer_kernel,
        grid=(N_ALL // MIX_TILE,),
        in_specs=[
            _rows_spec(MIX_TILE, D_MODEL), _mod_spec(), _gain_spec(0, 1),
            _resident_spec((D_MODEL, IN_WIDTH)),
            _resident_spec((1, ATTN_WIDTH)), _resident_spec((1, KV_WIDTH)),
            _resident_spec((3, CONV_WIDTH), 0),
            _resident_spec((ATTN_WIDTH, ATTN_WIDTH)),
            _resident_spec((DEC_SEQ, ATTN_WIDTH)), _resident_spec((DEC_SEQ, ATTN_WIDTH)),
            pl.BlockSpec((None, PAST_LEN, KV_WIDTH), smp_batch),
            pl.BlockSpec((None, KV_WIDTH, PAST_LEN), smp_batch),
            _resident_spec((MIX_WIDTH, D_MODEL)),
        ],
        out_specs=[_rows_spec(MIX_TILE, D_MODEL), kvt_spec, kvt_spec],
        out_shape=[jax.ShapeDtypeStruct((N_ALL, D_MODEL), F32), kvt_shape, kvt_shape],
        scratch_shapes=[pltpu.VMEM((MIX_TILE, ATTN_WIDTH), BF16),
                        pltpu.VMEM((MIX_TILE, KV_WIDTH), BF16),
                        pltpu.VMEM((KV_WIDTH, MIX_TILE), BF16),
                        pltpu.VMEM((MIX_TILE, MIX_WIDTH), BF16)],
        compiler_params=_params(),
        name="mixer",
    )(x, mod, norm_g, w_in, qg, kg, conv_w, seg, cos, sin, cache_k, cache_vt, w_out)


def _shift_rows(a, n, pos, seq_len):
    if n > 0:
        return jnp.where(pos < n, 0.0, pltpu.roll(a, n, 0))
    return jnp.where(pos >= seq_len + n, 0.0, pltpu.roll(a, a.shape[0] + n, 0))


def _pool_rows(x, pos, seq_len, r, mod_ref, gm_ref, pw_ref, ps_ref):
    gate = _mod_vec(mod_ref, r, 5)
    h_all = _modulate(x, gm_ref[...], mod_ref, r, 3)
    outs = []
    for gi, w in enumerate(POOL_WINDOWS):
        lanes = slice(gi * POOL_GROUP, (gi + 1) * POOL_GROUP)
        h = h_all[:, lanes]
        back, fwd, n = h, h, 1
        while n < w // 2:
            back = back + _shift_rows(back, n, pos, seq_len)
            fwd = fwd + _shift_rows(fwd, -n, pos, seq_len)
            n *= 2
        total = _shift_rows(back, 1, pos, seq_len) + fwd
        left = w // 2
        right = w - 1 - left
        count = jnp.minimum(pos + right + 1, seq_len) - jnp.maximum(pos - left, 0)
        diff = (total / count.astype(F32) - h).astype(BF16)
        out = jnp.dot(diff, pw_ref[lanes, :], preferred_element_type=F32) * ps_ref[:, lanes]
        outs.append(x[:, lanes] + gate[:, lanes] * out)
    return jnp.concatenate(outs, axis=1)


def _pool_tile(ctx, r, x_ref, mod_ref, gm_ref, gf_ref, gn_ref, pw_ref, ps_ref, w1_ref, w2_ref,
               y_ref, x2_ref, act_s):
    seq_len = SEQ if ctx else DEC_SEQ
    block = max(seq_len, FFN_TILE)
    pos = lax.broadcasted_iota(jnp.int32, (block, 1), 0) & (seq_len - 1)
    for b in range(MIX_TILE // block):
        rows = slice(b * block, (b + 1) * block)
        x2_ref[rows, :] = _pool_rows(x_ref[rows, :], pos, seq_len, r, mod_ref, gm_ref, pw_ref, ps_ref)

    def ffn_step(s, carry):
        rows = pl.ds(pl.multiple_of(s * FFN_TILE, FFN_TILE), FFN_TILE)
        y = _ffn_rows(x2_ref[rows, :], gf_ref[...], mod_ref, r, 6, w1_ref, w2_ref, act_s)
        y_ref[rows, :] = _rms(y, gn_ref[...])
        return carry

    lax.fori_loop(0, MIX_TILE // FFN_TILE, ffn_step, 0)


def _pool_kernel(*refs):
    ins, (yp_ref, ys_ref, x2_ref, act_s) = refs[:-4], refs[-4:]
    i = pl.program_id(0)
    r = _mod_row(i, MIX_TILE)
    is_ctx = i < _n_ctx_tiles(MIX_TILE)

    @pl.when(is_ctx)
    def _():
        _pool_tile(True, r, *ins, yp_ref, x2_ref, act_s)

    @pl.when(jnp.logical_not(is_ctx))
    def _():
        _pool_tile(False, r, *ins, ys_ref, x2_ref, act_s)


def _pool_call(x, mod, norm_g, final_g, pool_w, pool_scale, w1, w2):
    return pl.pallas_call(
        _pool_kernel,
        grid=(N_ALL // MIX_TILE,),
        in_specs=[
            _rows_spec(MIX_TILE, D_MODEL), _mod_spec(), _gain_spec(1, 1), _gain_spec(1, 2),
            _resident_spec((1, D_MODEL)),
            _resident_spec((D_MODEL, POOL_GROUP)),
            _resident_spec((1, D_MODEL), 0),
            _resident_spec((D_MODEL, 2 * D_FF)), _resident_spec((D_FF, D_MODEL)),
        ],
        out_specs=[_ctx_rows_spec(MIX_TILE, D_MODEL), _smp_rows_spec(MIX_TILE, D_MODEL)],
        out_shape=[jax.ShapeDtypeStruct((N_CTX, D_MODEL), F32),
                   jax.ShapeDtypeStruct((N_SMP, D_MODEL), F32)],
        scratch_shapes=[pltpu.VMEM((MIX_TILE, D_MODEL), F32), _act_scratch()],
        compiler_params=_params(),
        name="pool_ffn_norm",
    )(x, mod, norm_g, norm_g, final_g, pool_w, pool_scale, w1, w2)


def _rope_tables():
    t = np.arange(DEC_SEQ)
    half = HEAD_DIM // 2
    inv = ROPE_THETA ** (-np.arange(0, half, 2, dtype=np.float64) / half)
    ang_row = (t // GRID_W)[:, None] * inv[None, :]
    ang_col = (t % GRID_W)[:, None] * inv[None, :]
    cos = np.concatenate([np.cos(ang_row), np.cos(ang_row), np.cos(ang_col), np.cos(ang_col)], axis=1)
    sin = np.concatenate([-np.sin(ang_row), np.sin(ang_row), -np.sin(ang_col), np.sin(ang_col)], axis=1)
    return (jnp.asarray(np.tile(cos, (1, N_HEADS)), F32), jnp.asarray(np.tile(sin, (1, N_HEADS)), F32))


def _head_segments():
    head = np.arange(ATTN_WIDTH) // HEAD_DIM
    return jnp.asarray((head[:, None] == head[None, :]) / HEAD_DIM, BF16)


def _cache_layout(t):
    return jnp.transpose(t.reshape(BATCH, 1, N_KV_HEADS, HEAD_DIM, SEQ), (0, 1, 4, 2, 3))


def kernel(x_prompt, x_sample, c, cache_k, cache_v, c_ctx, ada_w, ada_b, norm_g, ffn_w1, ffn_w2,
           mix_w_in, mix_w_out, q_norm, k_norm, conv_w, pool_w, pool_scale, final_g):
    cvec = jnp.concatenate(
        [c_ctx[None, :], c, jnp.zeros((MOD_ROWS - 1 - DEC_BATCH, D_MODEL), F32)], axis=0)
    gains = norm_g.reshape(norm_g.shape[0], 3, 1, D_MODEL)
    cos, sin = _rope_tables()

    mod0, _ = _mod_call(_ModJob(cvec, ada_w, ada_b, 0, MOD_STEPS), [])
    x, (w_in, w_out) = _ffn_call(
        [x_prompt.reshape(N_CTX, D_MODEL), x_sample.reshape(N_SMP, D_MODEL)],
        mod0, gains, ffn_w1, ffn_w2, 0, 0,
        [_Cast(mix_w_in, (0,), FFN_CAST_BLOCKS), _Cast(mix_w_out, (0,), FFN_CAST_BLOCKS)])
    x, kt, vt = _mixer_call(
        x, mod0, gains, w_in, jnp.tile(q_norm[0], N_HEADS)[None, :],
        jnp.tile(k_norm[0], N_KV_HEADS)[None, :], conv_w, _head_segments(), cos, sin,
        cache_k[:, 0].reshape(DEC_BATCH, PAST_LEN, KV_WIDTH),
        jnp.transpose(cache_v[:, 0], (0, 2, 3, 1)).reshape(DEC_BATCH, KV_WIDTH, PAST_LEN), w_out)
    x, (pw, mod1) = _ffn_call(
        [x], mod0, gains, ffn_w1, ffn_w2, 0, 1,
        [_Cast(pool_w.reshape(pool_w.shape[0], D_MODEL, POOL_GROUP), (0,), FFN_CAST_BLOCKS)],
        _ModJob(cvec, ada_w, ada_b, 1, MOD_SIDE_BLOCKS))
    x, (w1_last, w2_last) = _ffn_call(
        [x], mod1, gains, ffn_w1, ffn_w2, 1, 0,
        [_Cast(ffn_w1, (1, 1), FFN_CAST_BLOCKS), _Cast(ffn_w2, (1, 1), FFN_CAST_BLOCKS)])
    yp, ys = _pool_call(x, mod1, gains, final_g[None, :], pw,
                        pool_scale.reshape(pool_scale.shape[0], 1, D_MODEL), w1_last, w2_last)

    return (yp.reshape(BATCH, SEQ, D_MODEL), ys.reshape(DEC_BATCH, DEC_SEQ, D_MODEL),
            _cache_layout(kt), _cache_layout(vt))
```

```python
import functools

import numpy as np
import jax
import jax.numpy as jnp
from jax import lax
from jax.experimental import pallas as pl
from jax.experimental.pallas import tpu as pltpu

F32 = jnp.float32
BF16 = jnp.bfloat16

D_MODEL = 1024
BATCH = 32
SEQ = 256
DEC_BATCH = 2
DEC_SEQ = 1024
PAST_LEN = 512
GRID_W = 64
N_HEADS = 8
N_KV_HEADS = 2
HEAD_DIM = 64
HEADS_PER_KV = N_HEADS // N_KV_HEADS
ATTN_WIDTH = N_HEADS * HEAD_DIM
KV_WIDTH = N_KV_HEADS * HEAD_DIM
CONV_WIDTH = D_MODEL // 2
MIX_WIDTH = ATTN_WIDTH + CONV_WIDTH
IN_WIDTH = ATTN_WIDTH + 2 * KV_WIDTH + 3 * CONV_WIDTH
D_FF = 2816
POOL_WINDOWS = (2, 4, 8, 16)
POOL_GROUP = D_MODEL // len(POOL_WINDOWS)
N_MOD = 9
ROPE_THETA = 10000.0
EPS = 1e-6

N_CTX = BATCH * SEQ
N_SMP = DEC_BATCH * DEC_SEQ
N_ALL = N_CTX + N_SMP
MIX_TILE = 1024
FFN_TILE = 512
FF_CHUNK = 256
N_FF_CHUNKS = D_FF // FF_CHUNK
STAGE_SLOTS = 2
Q_ROWS = 256
SEG_WIDTH = 256
SEQ_PER_TILE = MIX_TILE // SEQ
MOD_ROWS = 8
MOD_WIDTH = N_MOD * D_MODEL
MOD_STEPS = 4
MOD_SIDE_BLOCKS = 18
FFN_CAST_BLOCKS = 16
VMEM_LIMIT = 56 * 1024 * 1024


def _n_ctx_tiles(tile):
    return N_CTX // tile


def _mod_row(i, tile):
    first = _n_ctx_tiles(tile)
    return jnp.where(i < first, 0, 1 + (i - first) * tile // DEC_SEQ)


def _rms(x, g):
    return x * lax.rsqrt(jnp.mean(x * x, axis=-1, keepdims=True) + EPS) * g


def _mod_vec(mod_ref, r, j):
    return mod_ref[pl.ds(r, 1), j * D_MODEL:(j + 1) * D_MODEL]


def _modulate(x, g, mod_ref, r, j):
    return _rms(x, g) * (1 + _mod_vec(mod_ref, r, j + 1)) + _mod_vec(mod_ref, r, j)


def _ffn_rows(x, g, mod_ref, r, j, w1_ref, w2_ref, before_chunk=None):
    h = _modulate(x, g, mod_ref, r, j).astype(BF16)
    acc = None
    for c in range(N_FF_CHUNKS):
        lo = c * FF_CHUNK
        if before_chunk is not None:
            before_chunk(c)
        gate = jnp.dot(h, w1_ref[:, lo:lo + FF_CHUNK], preferred_element_type=F32)
        up = jnp.dot(h, w1_ref[:, D_FF + lo:D_FF + lo + FF_CHUNK], preferred_element_type=F32)
        act = (gate / (1 + jnp.exp(-gate)) * up).astype(BF16)
        y = jnp.dot(act, w2_ref[lo:lo + FF_CHUNK, :], preferred_element_type=F32)
        acc = y if acc is None else acc + y
    return x + (0.5 * _mod_vec(mod_ref, r, j + 2)) * acc


class _Cast:
    def __init__(self, src, lead, n_blocks):
        self.src, self.lead, self.n_blocks = src, tuple(lead), n_blocks
        self.rows, self.cols = src.shape[len(lead):]
        self.block_rows = self.rows // n_blocks

    def in_spec(self, step):
        lead, last = self.lead, self.n_blocks - 1
        return pl.BlockSpec((None,) * len(lead) + (self.block_rows, self.cols),
                            lambda *g: lead + (jnp.minimum(step(*g), last), 0))

    def out_spec(self, step):
        last = self.n_blocks - 1
        return pl.BlockSpec((self.block_rows, self.cols), lambda *g: (jnp.minimum(step(*g), last), 0))

    def out_shape(self):
        return jax.ShapeDtypeStruct((self.rows, self.cols), BF16)


def _cast_blocks(step, n_blocks, srcs, dsts):
    @pl.when(step < n_blocks)
    def _():
        for s, d in zip(srcs, dsts):
            d[...] = s[...].astype(BF16)


class _WeightStream:
    def __init__(self, w1_hbm, w2_hbm, layer, which, w1_s, w2_s, gate_st, up_st, down_st, sems):
        self.w1_hbm, self.w2_hbm, self.lead = w1_hbm, w2_hbm, (layer, which)
        self.w1_s, self.w2_s = w1_s, w2_s
        self.stages, self.sems = (gate_st, up_st, down_st), sems

    def _copies(self, c):
        slot, lo = c % STAGE_SLOTS, c * FF_CHUNK
        l, w = self.lead
        srcs = (self.w1_hbm.at[l, w, :, pl.ds(lo, FF_CHUNK)],
                self.w1_hbm.at[l, w, :, pl.ds(D_FF + lo, FF_CHUNK)],
                self.w2_hbm.at[l, w, pl.ds(lo, FF_CHUNK), :])
        return [pltpu.make_async_copy(src, st.at[slot], self.sems.at[k, slot])
                for k, (src, st) in enumerate(zip(srcs, self.stages))]

    def start(self, c):
        for cp in self._copies(c):
            cp.start()

    def prime(self):
        for c in range(STAGE_SLOTS):
            self.start(c)

    def land(self, c):
        for cp in self._copies(c):
            cp.wait()
        slot, lo = c % STAGE_SLOTS, c * FF_CHUNK
        gate_st, up_st, down_st = self.stages
        self.w1_s[:, lo:lo + FF_CHUNK] = gate_st[slot].astype(BF16)
        self.w1_s[:, D_FF + lo:D_FF + lo + FF_CHUNK] = up_st[slot].astype(BF16)
        self.w2_s[lo:lo + FF_CHUNK, :] = down_st[slot].astype(BF16)
        if c + STAGE_SLOTS < N_FF_CHUNKS:
            self.start(c + STAGE_SLOTS)


def _weight_stream_specs():
    return [pl.BlockSpec(memory_space=pl.ANY), pl.BlockSpec(memory_space=pl.ANY)]


N_STREAM_SCRATCH = 6


def _weight_stream_scratch():
    return [pltpu.VMEM((D_MODEL, 2 * D_FF), BF16), pltpu.VMEM((D_FF, D_MODEL), BF16),
            pltpu.VMEM((STAGE_SLOTS, D_MODEL, FF_CHUNK), F32),
            pltpu.VMEM((STAGE_SLOTS, D_MODEL, FF_CHUNK), F32),
            pltpu.VMEM((STAGE_SLOTS, FF_CHUNK, D_MODEL), F32),
            pltpu.SemaphoreType.DMA((3, STAGE_SLOTS))]


def _mod_block(c_ref, w_ref, b_ref):
    c = c_ref[...]
    s = (c / (1 + jnp.exp(-c))).astype(BF16)
    return jnp.dot(s, w_ref[...].astype(BF16), preferred_element_type=F32) + b_ref[...]


class _ModJob:
    def __init__(self, cvec, ada_w, ada_b, layer, n_blocks):
        self.args = (cvec, ada_w, ada_b.reshape(ada_b.shape[0], 1, MOD_WIDTH))
        self.layer, self.n_blocks, self.cols = layer, n_blocks, MOD_WIDTH // n_blocks

    def in_specs(self):
        layer, last, cols = self.layer, self.n_blocks - 1, self.cols
        block = lambda i: (layer, 0, jnp.minimum(i, last))
        return [pl.BlockSpec((MOD_ROWS, D_MODEL), lambda i: (0, 0)),
                pl.BlockSpec((None, D_MODEL, cols), block),
                pl.BlockSpec((None, 1, cols), block)]

    def out_spec(self):
        last = self.n_blocks - 1
        return pl.BlockSpec((MOD_ROWS, self.cols), lambda i: (0, jnp.minimum(i, last)))

    def out_shape(self):
        return jax.ShapeDtypeStruct((MOD_ROWS, MOD_WIDTH), F32)


def _mod_kernel(c_ref, w_ref, b_ref, *refs, n_cast):
    srcs, o_ref, dsts = refs[:n_cast], refs[n_cast], refs[n_cast + 1:]
    o_ref[...] = _mod_block(c_ref, w_ref, b_ref)
    _cast_blocks(pl.program_id(0), MOD_STEPS, srcs, dsts)


def _mod_call(job, casts):
    step = lambda i: i
    out = pl.pallas_call(
        functools.partial(_mod_kernel, n_cast=len(casts)),
        grid=(job.n_blocks,),
        in_specs=job.in_specs() + [c.in_spec(step) for c in casts],
        out_specs=[job.out_spec()] + [c.out_spec(step) for c in casts],
        out_shape=[job.out_shape()] + [c.out_shape() for c in casts],
        compiler_params=_params(),
        name="adaln_mod",
    )(*job.args, *[c.src for c in casts])
    return out[0], out[1:]


def _rows_spec(tile, width):
    return pl.BlockSpec((tile, width), lambda i: (i, 0))


def _ctx_rows_spec(tile, width):
    last = _n_ctx_tiles(tile) - 1
    return pl.BlockSpec((tile, width), lambda i: (jnp.minimum(i, last), 0))


def _smp_rows_spec(tile, width):
    first = _n_ctx_tiles(tile)
    return pl.BlockSpec((tile, width), lambda i: (jnp.maximum(i - first, 0), 0))


def _mod_spec():
    return pl.BlockSpec((MOD_ROWS, MOD_WIDTH), lambda i: (0, 0))


def _gain_spec(layer, j):
    return pl.BlockSpec((None, None, 1, D_MODEL), lambda i: (layer, j, 0, 0))


def _resident_spec(shape, *lead):
    index = tuple(lead) + (0,) * len(shape)
    return pl.BlockSpec((None,) * len(lead) + tuple(shape), lambda i: index,
                        pipeline_mode=pl.Buffered(1))


def _params():
    return pltpu.CompilerParams(dimension_semantics=("arbitrary",), vmem_limit_bytes=VMEM_LIMIT)


def _ffn_kernel(*refs, j, n_in, n_cast, mod_blocks, layer, which):
    n_side = n_cast + (3 if mod_blocks else 0)
    x_refs = refs[:n_in]
    mod_ref, g_ref, w1_hbm, w2_hbm = refs[n_in:n_in + 4]
    side_in = refs[n_in + 4:n_in + 4 + n_side]
    o_ref = refs[n_in + 4 + n_side]
    side_out = refs[n_in + 5 + n_side:len(refs) - N_STREAM_SCRATCH]
    stream = _WeightStream(w1_hbm, w2_hbm, layer, which, *refs[len(refs) - N_STREAM_SCRATCH:])
    i = pl.program_id(0)

    def run(before_chunk):
        if n_in == 2:
            x = jnp.where(i < _n_ctx_tiles(FFN_TILE), x_refs[0][...], x_refs[1][...])
        else:
            x = x_refs[0][...]
        o_ref[...] = _ffn_rows(x, g_ref[...], mod_ref, _mod_row(i, FFN_TILE), j,
                               stream.w1_s, stream.w2_s, before_chunk)

    @pl.when(i == 0)
    def _():
        stream.prime()
        run(stream.land)

    @pl.when(i > 0)
    def _():
        run(None)

    _cast_blocks(i, FFN_CAST_BLOCKS, side_in[:n_cast], side_out[:n_cast])
    if mod_blocks:
        @pl.when(i < mod_blocks)
        def _():
            side_out[n_cast][...] = _mod_block(*side_in[n_cast:])


def _ffn_call(xs, mod, norm_g, w1, w2, layer, which, casts=(), mod_job=None):
    step = lambda i: i
    if len(xs) == 2:
        x_specs = [_ctx_rows_spec(FFN_TILE, D_MODEL), _smp_rows_spec(FFN_TILE, D_MODEL)]
    else:
        x_specs = [_rows_spec(FFN_TILE, D_MODEL)]
    jobs = [mod_job] if mod_job else []
    out = pl.pallas_call(
        functools.partial(_ffn_kernel, j=6 * which, n_in=len(xs), n_cast=len(casts),
                          mod_blocks=mod_job.n_blocks if mod_job else 0, layer=layer, which=which),
        grid=(N_ALL // FFN_TILE,),
        in_specs=x_specs + [_mod_spec(), _gain_spec(layer, 2 * which)] + _weight_stream_specs()
        + [c.in_spec(step) for c in casts] + [s for m in jobs for s in m.in_specs()],
        out_specs=[_rows_spec(FFN_TILE, D_MODEL)] + [c.out_spec(step) for c in casts]
        + [m.out_spec() for m in jobs],
        out_shape=[jax.ShapeDtypeStruct((N_ALL, D_MODEL), F32)] + [c.out_shape() for c in casts]
        + [m.out_shape() for m in jobs],
        scratch_shapes=_weight_stream_scratch(),
        compiler_params=_params(),
        name="ffn",
    )(*xs, mod, norm_g, w1, w2, *[c.src for c in casts], *[a for m in jobs for a in m.args])
    return out[0], out[1:]


def _head_mean_sq(x, seg):
    sq = x * x
    hi = sq.astype(BF16)
    lo = (sq - hi.astype(F32)).astype(BF16)
    width = x.shape[-1]
    parts = []
    for c in range(0, width, SEG_WIDTH):
        n = min(SEG_WIDTH, width - c)
        parts.append(jnp.dot(hi[:, c:c + n], seg[:n, :n], preferred_element_type=F32)
                     + jnp.dot(lo[:, c:c + n], seg[:n, :n], preferred_element_type=F32))
    return parts[0] if len(parts) == 1 else jnp.concatenate(parts, axis=1)


def _rope(x, cos, sin_signed):
    width = x.shape[-1]
    lane = lax.broadcasted_iota(jnp.int32, (1, width), 1)
    partner = jnp.where((lane & 31) < 16,
                        pltpu.roll(x, width - 16, 1), pltpu.roll(x, 16, 1))
    return x * cos + partner * sin_signed


def _stack_heads(q_ref, rows, kvh):
    first = kvh * HEADS_PER_KV
    return jnp.concatenate(
        [q_ref[rows, (first + g) * HEAD_DIM:(first + g + 1) * HEAD_DIM] for g in range(HEADS_PER_KV)],
        axis=0)


ONES_ROWS = 16


def _attend_group(q, k, v1t):
    st = lax.dot_general(k, q, (((1,), (1,)), ((), ())), preferred_element_type=F32)
    e = jnp.exp(st - jnp.max(st, axis=0, keepdims=True)).astype(BF16)
    ot = jnp.dot(v1t, e, preferred_element_type=F32)
    return ot[:HEAD_DIM] / ot[HEAD_DIM:HEAD_DIM + 1]


def _attend_rows(q_s, rows, n, kv, mix_s):
    heads = []
    for kvh in range(N_KV_HEADS):
        k, v1t = kv(kvh)
        ot = _attend_group(_stack_heads(q_s, rows, kvh), k, v1t)
        heads += [ot[:, g * n:(g + 1) * n] for g in range(HEADS_PER_KV)]
    mix_s[rows, :ATTN_WIDTH] = jnp.concatenate(heads, axis=0).T.astype(BF16)


def _with_ones(vt):
    return jnp.concatenate([vt, jnp.ones((ONES_ROWS, vt.shape[1]), BF16)], axis=0)


def _mixer_tile(ctx, r, x_ref, mod_ref, g_ref, w_ref, qg_ref, kg_ref, cw_ref, seg_ref, cos_ref, sin_ref,
                ck_ref, cvt_ref, wo_ref, o_ref, kt_ref, vt_ref, q_s, k_s, vt_s, mix_s):
    seq_len = SEQ if ctx else DEC_SEQ
    x = x_ref[...]
    h = _modulate(x, g_ref[...], mod_ref, r, 3).astype(BF16)

    def proj(lo, width):
        return jnp.dot(h, w_ref[:, lo:lo + width], preferred_element_type=F32)

    q = proj(0, ATTN_WIDTH)
    kv = proj(ATTN_WIDTH, 2 * KV_WIDTH)
    k, v = kv[:, :KV_WIDTH], kv[:, KV_WIDTH:]
    q = q * lax.rsqrt(_head_mean_sq(q, seg_ref[...]) + EPS) * (qg_ref[...] * HEAD_DIM ** -0.5)
    k = k * lax.rsqrt(_head_mean_sq(k, seg_ref[...]) + EPS) * kg_ref[...]
    if ctx:
        q_s[...] = q.astype(BF16)
        k_s[...] = k.astype(BF16)
        for b in range(SEQ_PER_TILE):
            vt = v[b * SEQ:(b + 1) * SEQ, :].T
            kt_ref[b] = k[b * SEQ:(b + 1) * SEQ, :].T
            vt_ref[b] = vt
            vt_s[:, b * SEQ:(b + 1) * SEQ] = vt.astype(BF16)
    else:
        q_s[...] = _rope(q, cos_ref[...], sin_ref[...]).astype(BF16)
        k_s[...] = _rope(k, cos_ref[:, :KV_WIDTH], sin_ref[:, :KV_WIDTH]).astype(BF16)
        vt_s[...] = v.T.astype(BF16)

    base = ATTN_WIDTH + 2 * KV_WIDTH
    z = proj(base + CONV_WIDTH, CONV_WIDTH) * proj(base + 2 * CONV_WIDTH, CONV_WIDTH)
    pos = lax.broadcasted_iota(jnp.int32, (MIX_TILE, 1), 0) & (seq_len - 1)
    z_prev = jnp.where(pos == 0, 0.0, pltpu.roll(z, 1, 0))
    z_next = jnp.where(pos == seq_len - 1, 0.0, pltpu.roll(z, MIX_TILE - 1, 0))
    conv = z_prev * cw_ref[0:1, :] + z * cw_ref[1:2, :] + z_next * cw_ref[2:3, :]
    mix_s[:, ATTN_WIDTH:] = (proj(base, CONV_WIDTH) * conv).astype(BF16)

    def head(kvh):
        return slice(kvh * HEAD_DIM, (kvh + 1) * HEAD_DIM)

    if ctx:
        for b in range(SEQ_PER_TILE):
            rows = slice(b * SEQ, (b + 1) * SEQ)
            _attend_rows(q_s, rows, SEQ,
                         lambda kvh: (k_s[rows, head(kvh)], _with_ones(vt_s[head(kvh), rows])), mix_s)
    else:
        kv = [(jnp.concatenate([ck_ref[:, head(kvh)].astype(BF16), k_s[:, head(kvh)]], axis=0),
               _with_ones(jnp.concatenate([cvt_ref[head(kvh), :].astype(BF16), vt_s[head(kvh), :]],
                                          axis=1)))
              for kvh in range(N_KV_HEADS)]
        for t in range(MIX_TILE // Q_ROWS):
            _attend_rows(q_s, slice(t * Q_ROWS, (t + 1) * Q_ROWS), Q_ROWS, lambda kvh: kv[kvh], mix_s)

    out = (jnp.dot(mix_s[:, :ATTN_WIDTH], wo_ref[:ATTN_WIDTH, :], preferred_element_type=F32)
           + jnp.dot(mix_s[:, ATTN_WIDTH:], wo_ref[ATTN_WIDTH:, :], preferred_element_type=F32))
    o_ref[...] = x + _mod_vec(mod_ref, r, 5) * out


def _mixer_kernel(*refs):
    i = pl.program_id(0)
    r = _mod_row(i, MIX_TILE)
    is_ctx = i < _n_ctx_tiles(MIX_TILE)

    @pl.when(is_ctx)
    def _():
        _mixer_tile(True, r, *refs)

    @pl.when(jnp.logical_not(is_ctx))
    def _():
        _mixer_tile(False, r, *refs)


def _mixer_call(x, mod, norm_g, w_in, qg, kg, conv_w, seg, cos, sin, cache_k, cache_vt, w_out):
    first_smp = _n_ctx_tiles(MIX_TILE)
    smp_batch = lambda i: (jnp.maximum(i - first_smp, 0), 0, 0)
    kvt_spec = pl.BlockSpec((SEQ_PER_TILE, KV_WIDTH, SEQ), lambda i: (jnp.minimum(i, first_smp - 1), 0, 0))
    kvt_shape = jax.ShapeDtypeStruct((BATCH, KV_WIDTH, SEQ), F32)
    return pl.pallas_call(
        _mixer_kernel,
        grid=(N_ALL // MIX_TILE,),
        in_specs=[
            _rows_spec(MIX_TILE, D_MODEL), _mod_spec(), _gain_spec(0, 1),
            _resident_spec((D_MODEL, IN_WIDTH)),
            _resident_spec((1, ATTN_WIDTH)), _resident_spec((1, KV_WIDTH)),
            _resident_spec((3, CONV_WIDTH), 0),
            _resident_spec((SEG_WIDTH, SEG_WIDTH)),
            _resident_spec((DEC_SEQ, ATTN_WIDTH)), _resident_spec((DEC_SEQ, ATTN_WIDTH)),
            pl.BlockSpec((None, PAST_LEN, KV_WIDTH), smp_batch),
            pl.BlockSpec((None, KV_WIDTH, PAST_LEN), smp_batch),
            _resident_spec((MIX_WIDTH, D_MODEL)),
        ],
        out_specs=[_rows_spec(MIX_TILE, D_MODEL), kvt_spec, kvt_spec],
        out_shape=[jax.ShapeDtypeStruct((N_ALL, D_MODEL), F32), kvt_shape, kvt_shape],
        scratch_shapes=[pltpu.VMEM((MIX_TILE, ATTN_WIDTH), BF16),
                        pltpu.VMEM((MIX_TILE, KV_WIDTH), BF16),
                        pltpu.VMEM((KV_WIDTH, MIX_TILE), BF16),
                        pltpu.VMEM((MIX_TILE, MIX_WIDTH), BF16)],
        compiler_params=_params(),
        name="mixer",
    )(x, mod, norm_g, w_in, qg, kg, conv_w, seg, cos, sin, cache_k, cache_vt, w_out)


def _shift_rows(a, n, pos, seq_len):
    if n > 0:
        return jnp.where(pos < n, 0.0, pltpu.roll(a, n, 0))
    return jnp.where(pos >= seq_len + n, 0.0, pltpu.roll(a, a.shape[0] + n, 0))


def _pool_rows(x, pos, seq_len, r, mod_ref, gm_ref, pw_ref, ps_ref):
    gate = _mod_vec(mod_ref, r, 5)
    h_all = _modulate(x, gm_ref[...], mod_ref, r, 3)
    outs = []
    for gi, w in enumerate(POOL_WINDOWS):
        lanes = slice(gi * POOL_GROUP, (gi + 1) * POOL_GROUP)
        h = h_all[:, lanes]
        back, fwd, n = h, h, 1
        while n < w // 2:
            back = back + _shift_rows(back, n, pos, seq_len)
            fwd = fwd + _shift_rows(fwd, -n, pos, seq_len)
            n *= 2
        total = _shift_rows(back, 1, pos, seq_len) + fwd
        left = w // 2
        right = w - 1 - left
        count = jnp.minimum(pos + right + 1, seq_len) - jnp.maximum(pos - left, 0)
        diff = (total / count.astype(F32) - h).astype(BF16)
        out = jnp.dot(diff, pw_ref[lanes, :], preferred_element_type=F32) * ps_ref[:, lanes]
        outs.append(x[:, lanes] + gate[:, lanes] * out)
    return jnp.concatenate(outs, axis=1)


def _pool_tile(ctx, r, x_ref, mod_ref, gm_ref, gf_ref, gn_ref, pw_ref, ps_ref, w1_ref, w2_ref,
               y_ref, x2_ref):
    seq_len = SEQ if ctx else DEC_SEQ
    block = max(seq_len, FFN_TILE)
    pos = lax.broadcasted_iota(jnp.int32, (block, 1), 0) & (seq_len - 1)
    for b in range(MIX_TILE // block):
        rows = slice(b * block, (b + 1) * block)
        x2_ref[rows, :] = _pool_rows(x_ref[rows, :], pos, seq_len, r, mod_ref, gm_ref, pw_ref, ps_ref)

    def ffn_step(s, carry):
        rows = pl.ds(pl.multiple_of(s * FFN_TILE, FFN_TILE), FFN_TILE)
        y = _ffn_rows(x2_ref[rows, :], gf_ref[...], mod_ref, r, 6, w1_ref, w2_ref)
        y_ref[rows, :] = _rms(y, gn_ref[...])
        return carry

    lax.fori_loop(0, MIX_TILE // FFN_TILE, ffn_step, 0)


def _pool_kernel(*refs):
    ins, (yp_ref, ys_ref, x2_ref) = refs[:-3], refs[-3:]
    i = pl.program_id(0)
    r = _mod_row(i, MIX_TILE)
    is_ctx = i < _n_ctx_tiles(MIX_TILE)

    @pl.when(is_ctx)
    def _():
        _pool_tile(True, r, *ins, yp_ref, x2_ref)

    @pl.when(jnp.logical_not(is_ctx))
    def _():
        _pool_tile(False, r, *ins, ys_ref, x2_ref)


def _pool_call(x, mod, norm_g, final_g, pool_w, pool_scale, w1, w2):
    return pl.pallas_call(
        _pool_kernel,
        grid=(N_ALL // MIX_TILE,),
        in_specs=[
            _rows_spec(MIX_TILE, D_MODEL), _mod_spec(), _gain_spec(1, 1), _gain_spec(1, 2),
            _resident_spec((1, D_MODEL)),
            _resident_spec((D_MODEL, POOL_GROUP)),
            _resident_spec((1, D_MODEL), 0),
            _resident_spec((D_MODEL, 2 * D_FF)), _resident_spec((D_FF, D_MODEL)),
        ],
        out_specs=[_ctx_rows_spec(MIX_TILE, D_MODEL), _smp_rows_spec(MIX_TILE, D_MODEL)],
        out_shape=[jax.ShapeDtypeStruct((N_CTX, D_MODEL), F32),
                   jax.ShapeDtypeStruct((N_SMP, D_MODEL), F32)],
        scratch_shapes=[pltpu.VMEM((MIX_TILE, D_MODEL), F32)],
        compiler_params=_params(),
        name="pool_ffn_norm",
    )(x, mod, norm_g, norm_g, final_g, pool_w, pool_scale, w1, w2)


def _rope_tables():
    t = np.arange(DEC_SEQ)
    half = HEAD_DIM // 2
    inv = ROPE_THETA ** (-np.arange(0, half, 2, dtype=np.float64) / half)
    ang_row = (t // GRID_W)[:, None] * inv[None, :]
    ang_col = (t % GRID_W)[:, None] * inv[None, :]
    cos = np.concatenate([np.cos(ang_row), np.cos(ang_row), np.cos(ang_col), np.cos(ang_col)], axis=1)
    sin = np.concatenate([-np.sin(ang_row), np.sin(ang_row), -np.sin(ang_col), np.sin(ang_col)], axis=1)
    return (jnp.asarray(np.tile(cos, (1, N_HEADS)), F32), jnp.asarray(np.tile(sin, (1, N_HEADS)), F32))


def _head_segments():
    head = np.arange(SEG_WIDTH) // HEAD_DIM
    return jnp.asarray((head[:, None] == head[None, :]) / HEAD_DIM, BF16)


def _cache_layout(t):
    return jnp.transpose(t.reshape(BATCH, 1, N_KV_HEADS, HEAD_DIM, SEQ), (0, 1, 4, 2, 3))


def kernel(x_prompt, x_sample, c, cache_k, cache_v, c_ctx, ada_w, ada_b, norm_g, ffn_w1, ffn_w2,
           mix_w_in, mix_w_out, q_norm, k_norm, conv_w, pool_w, pool_scale, final_g):
    cvec = jnp.concatenate(
        [c_ctx[None, :], c, jnp.zeros((MOD_ROWS - 1 - DEC_BATCH, D_MODEL), F32)], axis=0)
    gains = norm_g.reshape(norm_g.shape[0], 3, 1, D_MODEL)
    cos, sin = _rope_tables()

    mod0, _ = _mod_call(_ModJob(cvec, ada_w, ada_b, 0, MOD_STEPS), [])
    x, (w_in, w_out) = _ffn_call(
        [x_prompt.reshape(N_CTX, D_MODEL), x_sample.reshape(N_SMP, D_MODEL)],
        mod0, gains, ffn_w1, ffn_w2, 0, 0,
        [_Cast(mix_w_in, (0,), FFN_CAST_BLOCKS), _Cast(mix_w_out, (0,), FFN_CAST_BLOCKS)])
    x, kt, vt = _mixer_call(
        x, mod0, gains, w_in, jnp.tile(q_norm[0], N_HEADS)[None, :],
        jnp.tile(k_norm[0], N_KV_HEADS)[None, :], conv_w, _head_segments(), cos, sin,
        cache_k[:, 0].reshape(DEC_BATCH, PAST_LEN, KV_WIDTH),
        jnp.transpose(cache_v[:, 0], (0, 2, 3, 1)).reshape(DEC_BATCH, KV_WIDTH, PAST_LEN), w_out)
    x, (pw, mod1) = _ffn_call(
        [x], mod0, gains, ffn_w1, ffn_w2, 0, 1,
        [_Cast(pool_w.reshape(pool_w.shape[0], D_MODEL, POOL_GROUP), (0,), FFN_CAST_BLOCKS)],
        _ModJob(cvec, ada_w, ada_b, 1, MOD_SIDE_BLOCKS))
    x, (w1_last, w2_last) = _ffn_call(
        [x], mod1, gains, ffn_w1, ffn_w2, 1, 0,
        [_Cast(ffn_w1, (1, 1), FFN_CAST_BLOCKS), _Cast(ffn_w2, (1, 1), FFN_CAST_BLOCKS)])
    yp, ys = _pool_call(x, mod1, gains, final_g[None, :], pw,
                        pool_scale.reshape(pool_scale.shape[0], 1, D_MODEL), w1_last, w2_last)

    return (yp.reshape(BATCH, SEQ, D_MODEL), ys.reshape(DEC_BATCH, DEC_SEQ, D_MODEL),
            _cache_layout(kt), _cache_layout(vt))
```

```python
import functools

import numpy as np
import jax
import jax.numpy as jnp
from jax import lax
from jax.experimental import pallas as pl
from jax.experimental.pallas import tpu as pltpu

F32 = jnp.float32
BF16 = jnp.bfloat16

D_MODEL = 1024
BATCH = 32
SEQ = 256
DEC_BATCH = 2
DEC_SEQ = 1024
PAST_LEN = 512
GRID_W = 64
N_HEADS = 8
N_KV_HEADS = 2
HEAD_DIM = 64
HEADS_PER_KV = N_HEADS // N_KV_HEADS
ATTN_WIDTH = N_HEADS * HEAD_DIM
KV_WIDTH = N_KV_HEADS * HEAD_DIM
CONV_WIDTH = D_MODEL // 2
MIX_WIDTH = ATTN_WIDTH + CONV_WIDTH
IN_WIDTH = ATTN_WIDTH + 2 * KV_WIDTH + 3 * CONV_WIDTH
D_FF = 2816
POOL_WINDOWS = (2, 4, 8, 16)
POOL_GROUP = D_MODEL // len(POOL_WINDOWS)
N_MOD = 9
ROPE_THETA = 10000.0
EPS = 1e-6

N_CTX = BATCH * SEQ
N_SMP = DEC_BATCH * DEC_SEQ
N_ALL = N_CTX + N_SMP
MIX_TILE = 1024
FFN_TILE = 512
FF_CHUNK = 256
N_FF_CHUNKS = D_FF // FF_CHUNK
STAGE_SLOTS = 2
Q_ROWS = 256
SEG_WIDTH = 256
SEQ_PER_TILE = MIX_TILE // SEQ
MOD_ROWS = 8
MOD_WIDTH = N_MOD * D_MODEL
MOD_STEPS = 4
MOD_SIDE_BLOCKS = 18
FFN_CAST_BLOCKS = 16
VMEM_LIMIT = 56 * 1024 * 1024


def _n_ctx_tiles(tile):
    return N_CTX // tile


def _mod_row(i, tile):
    first = _n_ctx_tiles(tile)
    return jnp.where(i < first, 0, 1 + (i - first) * tile // DEC_SEQ)


def _rms(x, g):
    return x * lax.rsqrt(jnp.mean(x * x, axis=-1, keepdims=True) + EPS) * g


def _mod_vec(mod_ref, r, j):
    return mod_ref[pl.ds(r, 1), j * D_MODEL:(j + 1) * D_MODEL]


def _modulate(x, g, mod_ref, r, j):
    return _rms(x, g) * (1 + _mod_vec(mod_ref, r, j + 1)) + _mod_vec(mod_ref, r, j)


def _ffn_rows(x, g, mod_ref, r, j, w1_ref, w2_ref, before_chunk=None):
    h = _modulate(x, g, mod_ref, r, j).astype(BF16)
    acc = None
    for c in range(N_FF_CHUNKS):
        lo = c * FF_CHUNK
        if before_chunk is not None:
            before_chunk(c)
        gate = jnp.dot(h, w1_ref[:, lo:lo + FF_CHUNK], preferred_element_type=F32)
        up = jnp.dot(h, w1_ref[:, D_FF + lo:D_FF + lo + FF_CHUNK], preferred_element_type=F32)
        act = (gate / (1 + jnp.exp(-gate)) * up).astype(BF16)
        y = jnp.dot(act, w2_ref[lo:lo + FF_CHUNK, :], preferred_element_type=F32)
        acc = y if acc is None else acc + y
    return x + (0.5 * _mod_vec(mod_ref, r, j + 2)) * acc


class _Cast:
    def __init__(self, src, lead, n_blocks):
        self.src, self.lead, self.n_blocks = src, tuple(lead), n_blocks
        self.rows, self.cols = src.shape[len(lead):]
        self.block_rows = self.rows // n_blocks

    def in_spec(self, step):
        lead, last = self.lead, self.n_blocks - 1
        return pl.BlockSpec((None,) * len(lead) + (self.block_rows, self.cols),
                            lambda *g: lead + (jnp.minimum(step(*g), last), 0))

    def out_spec(self, step):
        last = self.n_blocks - 1
        return pl.BlockSpec((self.block_rows, self.cols), lambda *g: (jnp.minimum(step(*g), last), 0))

    def out_shape(self):
        return jax.ShapeDtypeStruct((self.rows, self.cols), BF16)


def _cast_blocks(step, n_blocks, srcs, dsts):
    @pl.when(step < n_blocks)
    def _():
        for s, d in zip(srcs, dsts):
            d[...] = s[...].astype(BF16)


class _WeightStream:
    def __init__(self, w1_hbm, w2_hbm, layer, which, w1_s, w2_s, gate_st, up_st, down_st, sems):
        self.w1_hbm, self.w2_hbm, self.lead = w1_hbm, w2_hbm, (layer, which)
        self.w1_s, self.w2_s = w1_s, w2_s
        self.stages, self.sems = (gate_st, up_st, down_st), sems

    def _copies(self, c):
        slot, lo = c % STAGE_SLOTS, c * FF_CHUNK
        l, w = self.lead
        srcs = (self.w1_hbm.at[l, w, :, pl.ds(lo, FF_CHUNK)],
                self.w1_hbm.at[l, w, :, pl.ds(D_FF + lo, FF_CHUNK)],
                self.w2_hbm.at[l, w, pl.ds(lo, FF_CHUNK), :])
        return [pltpu.make_async_copy(src, st.at[slot], self.sems.at[k, slot])
                for k, (src, st) in enumerate(zip(srcs, self.stages))]

    def start(self, c):
        for cp in self._copies(c):
            cp.start()

    def prime(self):
        for c in range(STAGE_SLOTS):
            self.start(c)

    def land(self, c):
        for cp in self._copies(c):
            cp.wait()
        slot, lo = c % STAGE_SLOTS, c * FF_CHUNK
        gate_st, up_st, down_st = self.stages
        self.w1_s[:, lo:lo + FF_CHUNK] = gate_st[slot].astype(BF16)
        self.w1_s[:, D_FF + lo:D_FF + lo + FF_CHUNK] = up_st[slot].astype(BF16)
        self.w2_s[lo:lo + FF_CHUNK, :] = down_st[slot].astype(BF16)
        if c + STAGE_SLOTS < N_FF_CHUNKS:
            self.start(c + STAGE_SLOTS)


def _weight_stream_specs():
    return [pl.BlockSpec(memory_space=pl.ANY), pl.BlockSpec(memory_space=pl.ANY)]


N_STREAM_SCRATCH = 6


def _weight_stream_scratch():
    return [pltpu.VMEM((D_MODEL, 2 * D_FF), BF16), pltpu.VMEM((D_FF, D_MODEL), BF16),
            pltpu.VMEM((STAGE_SLOTS, D_MODEL, FF_CHUNK), F32),
            pltpu.VMEM((STAGE_SLOTS, D_MODEL, FF_CHUNK), F32),
            pltpu.VMEM((STAGE_SLOTS, FF_CHUNK, D_MODEL), F32),
            pltpu.SemaphoreType.DMA((3, STAGE_SLOTS))]


def _mod_block(c_ref, w_ref, b_ref):
    c = c_ref[...]
    s = (c / (1 + jnp.exp(-c))).astype(BF16)
    return jnp.dot(s, w_ref[...].astype(BF16), preferred_element_type=F32) + b_ref[...]


class _ModJob:
    def __init__(self, cvec, ada_w, ada_b, layer, n_blocks):
        self.args = (cvec, ada_w, ada_b.reshape(ada_b.shape[0], 1, MOD_WIDTH))
        self.layer, self.n_blocks, self.cols = layer, n_blocks, MOD_WIDTH // n_blocks

    def in_specs(self):
        layer, last, cols = self.layer, self.n_blocks - 1, self.cols
        block = lambda i: (layer, 0, jnp.minimum(i, last))
        return [pl.BlockSpec((MOD_ROWS, D_MODEL), lambda i: (0, 0)),
                pl.BlockSpec((None, D_MODEL, cols), block),
                pl.BlockSpec((None, 1, cols), block)]

    def out_spec(self):
        last = self.n_blocks - 1
        return pl.BlockSpec((MOD_ROWS, self.cols), lambda i: (0, jnp.minimum(i, last)))

    def out_shape(self):
        return jax.ShapeDtypeStruct((MOD_ROWS, MOD_WIDTH), F32)


def _mod_kernel(c_ref, w_ref, b_ref, *refs, n_cast):
    srcs, o_ref, dsts = refs[:n_cast], refs[n_cast], refs[n_cast + 1:]
    o_ref[...] = _mod_block(c_ref, w_ref, b_ref)
    _cast_blocks(pl.program_id(0), MOD_STEPS, srcs, dsts)


def _mod_call(job, casts):
    step = lambda i: i
    out = pl.pallas_call(
        functools.partial(_mod_kernel, n_cast=len(casts)),
        grid=(job.n_blocks,),
        in_specs=job.in_specs() + [c.in_spec(step) for c in casts],
        out_specs=[job.out_spec()] + [c.out_spec(step) for c in casts],
        out_shape=[job.out_shape()] + [c.out_shape() for c in casts],
        compiler_params=_params(),
        name="adaln_mod",
    )(*job.args, *[c.src for c in casts])
    return out[0], out[1:]


def _rows_spec(tile, width):
    return pl.BlockSpec((tile, width), lambda i: (i, 0))


def _ctx_rows_spec(tile, width):
    last = _n_ctx_tiles(tile) - 1
    return pl.BlockSpec((tile, width), lambda i: (jnp.minimum(i, last), 0))


def _smp_rows_spec(tile, width):
    first = _n_ctx_tiles(tile)
    return pl.BlockSpec((tile, width), lambda i: (jnp.maximum(i - first, 0), 0))


def _mod_spec():
    return pl.BlockSpec((MOD_ROWS, MOD_WIDTH), lambda i: (0, 0))


def _gain_spec(layer, j):
    return pl.BlockSpec((None, None, 1, D_MODEL), lambda i: (layer, j, 0, 0))


def _resident_spec(shape, *lead):
    index = tuple(lead) + (0,) * len(shape)
    return pl.BlockSpec((None,) * len(lead) + tuple(shape), lambda i: index,
                        pipeline_mode=pl.Buffered(1))


def _params():
    return pltpu.CompilerParams(dimension_semantics=("arbitrary",), vmem_limit_bytes=VMEM_LIMIT)


def _ffn_kernel(*refs, j, n_in, n_cast, mod_blocks, layer, which):
    n_side = n_cast + (3 if mod_blocks else 0)
    x_refs = refs[:n_in]
    mod_ref, g_ref, w1_hbm, w2_hbm = refs[n_in:n_in + 4]
    side_in = refs[n_in + 4:n_in + 4 + n_side]
    o_ref = refs[n_in + 4 + n_side]
    side_out = refs[n_in + 5 + n_side:len(refs) - N_STREAM_SCRATCH]
    stream = _WeightStream(w1_hbm, w2_hbm, layer, which, *refs[len(refs) - N_STREAM_SCRATCH:])
    i = pl.program_id(0)

    def run(before_chunk):
        if n_in == 2:
            x = jnp.where(i < _n_ctx_tiles(FFN_TILE), x_refs[0][...], x_refs[1][...])
        else:
            x = x_refs[0][...]
        o_ref[...] = _ffn_rows(x, g_ref[...], mod_ref, _mod_row(i, FFN_TILE), j,
                               stream.w1_s, stream.w2_s, before_chunk)

    @pl.when(i == 0)
    def _():
        stream.prime()
        run(stream.land)

    @pl.when(i > 0)
    def _():
        run(None)

    _cast_blocks(i, FFN_CAST_BLOCKS, side_in[:n_cast], side_out[:n_cast])
    if mod_blocks:
        @pl.when(i < mod_blocks)
        def _():
            side_out[n_cast][...] = _mod_block(*side_in[n_cast:])


def _ffn_call(xs, mod, norm_g, w1, w2, layer, which, casts=(), mod_job=None):
    step = lambda i: i
    if len(xs) == 2:
        x_specs = [_ctx_rows_spec(FFN_TILE, D_MODEL), _smp_rows_spec(FFN_TILE, D_MODEL)]
    else:
        x_specs = [_rows_spec(FFN_TILE, D_MODEL)]
    jobs = [mod_job] if mod_job else []
    out = pl.pallas_call(
        functools.partial(_ffn_kernel, j=6 * which, n_in=len(xs), n_cast=len(casts),
                          mod_blocks=mod_job.n_blocks if mod_job else 0, layer=layer, which=which),
        grid=(N_ALL // FFN_TILE,),
        in_specs=x_specs + [_mod_spec(), _gain_spec(layer, 2 * which)] + _weight_stream_specs()
        + [c.in_spec(step) for c in casts] + [s for m in jobs for s in m.in_specs()],
        out_specs=[_rows_spec(FFN_TILE, D_MODEL)] + [c.out_spec(step) for c in casts]
        + [m.out_spec() for m in jobs],
        out_shape=[jax.ShapeDtypeStruct((N_ALL, D_MODEL), F32)] + [c.out_shape() for c in casts]
        + [m.out_shape() for m in jobs],
        scratch_shapes=_weight_stream_scratch(),
        compiler_params=_params(),
        name="ffn",
    )(*xs, mod, norm_g, w1, w2, *[c.src for c in casts], *[a for m in jobs for a in m.args])
    return out[0], out[1:]


def _head_mean_sq(x, seg):
    sq = (x * x).astype(BF16)
    width = x.shape[-1]
    parts = []
    for c in range(0, width, SEG_WIDTH):
        n = min(SEG_WIDTH, width - c)
        parts.append(jnp.dot(sq[:, c:c + n], seg[:n, :n], preferred_element_type=F32))
    return parts[0] if len(parts) == 1 else jnp.concatenate(parts, axis=1)


def _rope(x, cos, sin_signed):
    width = x.shape[-1]
    lane = lax.broadcasted_iota(jnp.int32, (1, width), 1)
    partner = jnp.where((lane & 31) < 16,
                        pltpu.roll(x, width - 16, 1), pltpu.roll(x, 16, 1))
    return x * cos + partner * sin_signed


def _stack_heads(q_ref, rows, kvh):
    first = kvh * HEADS_PER_KV
    return jnp.concatenate(
        [q_ref[rows, (first + g) * HEAD_DIM:(first + g + 1) * HEAD_DIM] for g in range(HEADS_PER_KV)],
        axis=0)


ONES_ROWS = 16


def _attend_group(q, k, v1t):
    st = lax.dot_general(k, q, (((1,), (1,)), ((), ())), preferred_element_type=F32)
    e = jnp.exp(st - jnp.max(st, axis=0, keepdims=True)).astype(BF16)
    ot = jnp.dot(v1t, e, preferred_element_type=F32)
    return ot[:HEAD_DIM] / ot[HEAD_DIM:HEAD_DIM + 1]


def _attend_rows(q_s, rows, n, kv, mix_s):
    heads = []
    for kvh in range(N_KV_HEADS):
        k, v1t = kv(kvh)
        ot = _attend_group(_stack_heads(q_s, rows, kvh), k, v1t)
        heads += [ot[:, g * n:(g + 1) * n] for g in range(HEADS_PER_KV)]
    mix_s[rows, :ATTN_WIDTH] = jnp.concatenate(heads, axis=0).T.astype(BF16)


def _with_ones(vt):
    return jnp.concatenate([vt, jnp.ones((ONES_ROWS, vt.shape[1]), BF16)], axis=0)


def _mixer_tile(ctx, r, x_ref, mod_ref, g_ref, w_ref, qg_ref, kg_ref, cw_ref, seg_ref, cos_ref, sin_ref,
                ck_ref, cvt_ref, wo_ref, o_ref, kt_ref, vt_ref, q_s, k_s, vt_s, mix_s):
    seq_len = SEQ if ctx else DEC_SEQ
    x = x_ref[...]
    h = _modulate(x, g_ref[...], mod_ref, r, 3).astype(BF16)

    def proj(lo, width):
        return jnp.dot(h, w_ref[:, lo:lo + width], preferred_element_type=F32)

    q = proj(0, ATTN_WIDTH)
    kv = proj(ATTN_WIDTH, 2 * KV_WIDTH)
    k, v = kv[:, :KV_WIDTH], kv[:, KV_WIDTH:]
    q = q * lax.rsqrt(_head_mean_sq(q, seg_ref[...]) + EPS) * (qg_ref[...] * HEAD_DIM ** -0.5)
    k = k * lax.rsqrt(_head_mean_sq(k, seg_ref[...]) + EPS) * kg_ref[...]
    if ctx:
        q_s[...] = q.astype(BF16)
        k_s[...] = k.astype(BF16)
        for b in range(SEQ_PER_TILE):
            vt = v[b * SEQ:(b + 1) * SEQ, :].T
            kt_ref[b] = k[b * SEQ:(b + 1) * SEQ, :].T
            vt_ref[b] = vt
            vt_s[:, b * SEQ:(b + 1) * SEQ] = vt.astype(BF16)
    else:
        q_s[...] = _rope(q, cos_ref[...], sin_ref[...]).astype(BF16)
        k_s[...] = _rope(k, cos_ref[:, :KV_WIDTH], sin_ref[:, :KV_WIDTH]).astype(BF16)
        vt_s[...] = v.T.astype(BF16)

    base = ATTN_WIDTH + 2 * KV_WIDTH
    z = proj(base + CONV_WIDTH, CONV_WIDTH) * proj(base + 2 * CONV_WIDTH, CONV_WIDTH)
    pos = lax.broadcasted_iota(jnp.int32, (MIX_TILE, 1), 0) & (seq_len - 1)
    z_prev = jnp.where(pos == 0, 0.0, pltpu.roll(z, 1, 0))
    z_next = jnp.where(pos == seq_len - 1, 0.0, pltpu.roll(z, MIX_TILE - 1, 0))
    conv = z_prev * cw_ref[0:1, :] + z * cw_ref[1:2, :] + z_next * cw_ref[2:3, :]
    mix_s[:, ATTN_WIDTH:] = (proj(base, CONV_WIDTH) * conv).astype(BF16)

    def head(kvh):
        return slice(kvh * HEAD_DIM, (kvh + 1) * HEAD_DIM)

    if ctx:
        for b in range(SEQ_PER_TILE):
            rows = slice(b * SEQ, (b + 1) * SEQ)
            _attend_rows(q_s, rows, SEQ,
                         lambda kvh: (k_s[rows, head(kvh)], _with_ones(vt_s[head(kvh), rows])), mix_s)
    else:
        kv = [(jnp.concatenate([ck_ref[:, head(kvh)].astype(BF16), k_s[:, head(kvh)]], axis=0),
               _with_ones(jnp.concatenate([cvt_ref[head(kvh), :].astype(BF16), vt_s[head(kvh), :]],
                                          axis=1)))
              for kvh in range(N_KV_HEADS)]
        for t in range(MIX_TILE // Q_ROWS):
            _attend_rows(q_s, slice(t * Q_ROWS, (t + 1) * Q_ROWS), Q_ROWS, lambda kvh: kv[kvh], mix_s)

    out = (jnp.dot(mix_s[:, :ATTN_WIDTH], wo_ref[:ATTN_WIDTH, :], preferred_element_type=F32)
           + jnp.dot(mix_s[:, ATTN_WIDTH:], wo_ref[ATTN_WIDTH:, :], preferred_element_type=F32))
    o_ref[...] = x + _mod_vec(mod_ref, r, 5) * out


def _mixer_kernel(*refs):
    i = pl.program_id(0)
    r = _mod_row(i, MIX_TILE)
    is_ctx = i < _n_ctx_tiles(MIX_TILE)

    @pl.when(is_ctx)
    def _():
        _mixer_tile(True, r, *refs)

    @pl.when(jnp.logical_not(is_ctx))
    def _():
        _mixer_tile(False, r, *refs)


def _mixer_call(x, mod, norm_g, w_in, qg, kg, conv_w, seg, cos, sin, cache_k, cache_vt, w_out):
    first_smp = _n_ctx_tiles(MIX_TILE)
    smp_batch = lambda i: (jnp.maximum(i - first_smp, 0), 0, 0)
    kvt_spec = pl.BlockSpec((SEQ_PER_TILE, KV_WIDTH, SEQ), lambda i: (jnp.minimum(i, first_smp - 1), 0, 0))
    kvt_shape = jax.ShapeDtypeStruct((BATCH, KV_WIDTH, SEQ), F32)
    return pl.pallas_call(
        _mixer_kernel,
        grid=(N_ALL // MIX_TILE,),
        in_specs=[
            _rows_spec(MIX_TILE, D_MODEL), _mod_spec(), _gain_spec(0, 1),
            _resident_spec((D_MODEL, IN_WIDTH)),
            _resident_spec((1, ATTN_WIDTH)), _resident_spec((1, KV_WIDTH)),
            _resident_spec((3, CONV_WIDTH), 0),
            _resident_spec((SEG_WIDTH, SEG_WIDTH)),
            _resident_spec((DEC_SEQ, ATTN_WIDTH)), _resident_spec((DEC_SEQ, ATTN_WIDTH)),
            pl.BlockSpec((None, PAST_LEN, KV_WIDTH), smp_batch),
            pl.BlockSpec((None, KV_WIDTH, PAST_LEN), smp_batch),
            _resident_spec((MIX_WIDTH, D_MODEL)),
        ],
        out_specs=[_rows_spec(MIX_TILE, D_MODEL), kvt_spec, kvt_spec],
        out_shape=[jax.ShapeDtypeStruct((N_ALL, D_MODEL), F32), kvt_shape, kvt_shape],
        scratch_shapes=[pltpu.VMEM((MIX_TILE, ATTN_WIDTH), BF16),
                        pltpu.VMEM((MIX_TILE, KV_WIDTH), BF16),
                        pltpu.VMEM((KV_WIDTH, MIX_TILE), BF16),
                        pltpu.VMEM((MIX_TILE, MIX_WIDTH), BF16)],
        compiler_params=_params(),
        name="mixer",
    )(x, mod, norm_g, w_in, qg, kg, conv_w, seg, cos, sin, cache_k, cache_vt, w_out)


def _shift_rows(a, n, pos, seq_len):
    if n > 0:
        return jnp.where(pos < n, 0.0, pltpu.roll(a, n, 0))
    return jnp.where(pos >= seq_len + n, 0.0, pltpu.roll(a, a.shape[0] + n, 0))


def _pool_rows(x, pos, seq_len, r, mod_ref, gm_ref, pw_ref, ps_ref):
    gate = _mod_vec(mod_ref, r, 5)
    h_all = _modulate(x, gm_ref[...], mod_ref, r, 3)
    outs = []
    for gi, w in enumerate(POOL_WINDOWS):
        lanes = slice(gi * POOL_GROUP, (gi + 1) * POOL_GROUP)
        h = h_all[:, lanes]
        back, fwd, n = h, h, 1
        while n < w // 2:
            back = back + _shift_rows(back, n, pos, seq_len)
            fwd = fwd + _shift_rows(fwd, -n, pos, seq_len)
            n *= 2
        total = _shift_rows(back, 1, pos, seq_len) + fwd
        left = w // 2
        right = w - 1 - left
        count = jnp.minimum(pos + right + 1, seq_len) - jnp.maximum(pos - left, 0)
        diff = (total / count.astype(F32) - h).astype(BF16)
        out = jnp.dot(diff, pw_ref[lanes, :], preferred_element_type=F32) * ps_ref[:, lanes]
        outs.append(x[:, lanes] + gate[:, lanes] * out)
    return jnp.concatenate(outs, axis=1)


def _pool_tile(ctx, r, x_ref, mod_ref, gm_ref, gf_ref, gn_ref, pw_ref, ps_ref, w1_ref, w2_ref,
               y_ref, x2_ref):
    seq_len = SEQ if ctx else DEC_SEQ
    block = max(seq_len, FFN_TILE)
    pos = lax.broadcasted_iota(jnp.int32, (block, 1), 0) & (seq_len - 1)
    for b in range(MIX_TILE // block):
        rows = slice(b * block, (b + 1) * block)
        x2_ref[rows, :] = _pool_rows(x_ref[rows, :], pos, seq_len, r, mod_ref, gm_ref, pw_ref, ps_ref)

    def ffn_step(s, carry):
        rows = pl.ds(pl.multiple_of(s * FFN_TILE, FFN_TILE), FFN_TILE)
        y = _ffn_rows(x2_ref[rows, :], gf_ref[...], mod_ref, r, 6, w1_ref, w2_ref)
        y_ref[rows, :] = _rms(y, gn_ref[...])
        return carry

    lax.fori_loop(0, MIX_TILE // FFN_TILE, ffn_step, 0)


def _pool_kernel(*refs):
    ins, (yp_ref, ys_ref, x2_ref) = refs[:-3], refs[-3:]
    i = pl.program_id(0)
    r = _mod_row(i, MIX_TILE)
    is_ctx = i < _n_ctx_tiles(MIX_TILE)

    @pl.when(is_ctx)
    def _():
        _pool_tile(True, r, *ins, yp_ref, x2_ref)

    @pl.when(jnp.logical_not(is_ctx))
    def _():
        _pool_tile(False, r, *ins, ys_ref, x2_ref)


def _pool_call(x, mod, norm_g, final_g, pool_w, pool_scale, w1, w2):
    return pl.pallas_call(
        _pool_kernel,
        grid=(N_ALL // MIX_TILE,),
        in_specs=[
            _rows_spec(MIX_TILE, D_MODEL), _mod_spec(), _gain_spec(1, 1), _gain_spec(1, 2),
            _resident_spec((1, D_MODEL)),
            _resident_spec((D_MODEL, POOL_GROUP)),
            _resident_spec((1, D_MODEL), 0),
            _resident_spec((D_MODEL, 2 * D_FF)), _resident_spec((D_FF, D_MODEL)),
        ],
        out_specs=[_ctx_rows_spec(MIX_TILE, D_MODEL), _smp_rows_spec(MIX_TILE, D_MODEL)],
        out_shape=[jax.ShapeDtypeStruct((N_CTX, D_MODEL), F32),
                   jax.ShapeDtypeStruct((N_SMP, D_MODEL), F32)],
        scratch_shapes=[pltpu.VMEM((MIX_TILE, D_MODEL), F32)],
        compiler_params=_params(),
        name="pool_ffn_norm",
    )(x, mod, norm_g, norm_g, final_g, pool_w, pool_scale, w1, w2)


def _rope_tables():
    t = np.arange(DEC_SEQ)
    half = HEAD_DIM // 2
    inv = ROPE_THETA ** (-np.arange(0, half, 2, dtype=np.float64) / half)
    ang_row = (t // GRID_W)[:, None] * inv[None, :]
    ang_col = (t % GRID_W)[:, None] * inv[None, :]
    cos = np.concatenate([np.cos(ang_row), np.cos(ang_row), np.cos(ang_col), np.cos(ang_col)], axis=1)
    sin = np.concatenate([-np.sin(ang_row), np.sin(ang_row), -np.sin(ang_col), np.sin(ang_col)], axis=1)
    return (jnp.asarray(np.tile(cos, (1, N_HEADS)), F32), jnp.asarray(np.tile(sin, (1, N_HEADS)), F32))


def _head_segments():
    head = np.arange(SEG_WIDTH) // HEAD_DIM
    return jnp.asarray((head[:, None] == head[None, :]) / HEAD_DIM, BF16)


def _cache_layout(t):
    return jnp.transpose(t.reshape(BATCH, 1, N_KV_HEADS, HEAD_DIM, SEQ), (0, 1, 4, 2, 3))


def kernel(x_prompt, x_sample, c, cache_k, cache_v, c_ctx, ada_w, ada_b, norm_g, ffn_w1, ffn_w2,
           mix_w_in, mix_w_out, q_norm, k_norm, conv_w, pool_w, pool_scale, final_g):
    cvec = jnp.concatenate(
        [c_ctx[None, :], c, jnp.zeros((MOD_ROWS - 1 - DEC_BATCH, D_MODEL), F32)], axis=0)
    gains = norm_g.reshape(norm_g.shape[0], 3, 1, D_MODEL)
    cos, sin = _rope_tables()

    mod0, _ = _mod_call(_ModJob(cvec, ada_w, ada_b, 0, MOD_STEPS), [])
    x, (w_in, w_out) = _ffn_call(
        [x_prompt.reshape(N_CTX, D_MODEL), x_sample.reshape(N_SMP, D_MODEL)],
        mod0, gains, ffn_w1, ffn_w2, 0, 0,
        [_Cast(mix_w_in, (0,), FFN_CAST_BLOCKS), _Cast(mix_w_out, (0,), FFN_CAST_BLOCKS)])
    x, kt, vt = _mixer_call(
        x, mod0, gains, w_in, jnp.tile(q_norm[0], N_HEADS)[None, :],
        jnp.tile(k_norm[0], N_KV_HEADS)[None, :], conv_w, _head_segments(), cos, sin,
        cache_k[:, 0].reshape(DEC_BATCH, PAST_LEN, KV_WIDTH),
        jnp.transpose(cache_v[:, 0], (0, 2, 3, 1)).reshape(DEC_BATCH, KV_WIDTH, PAST_LEN), w_out)
    x, (pw, mod1) = _ffn_call(
        [x], mod0, gains, ffn_w1, ffn_w2, 0, 1,
        [_Cast(pool_w.reshape(pool_w.shape[0], D_MODEL, POOL_GROUP), (0,), FFN_CAST_BLOCKS)],
        _ModJob(cvec, ada_w, ada_b, 1, MOD_SIDE_BLOCKS))
    x, (w1_last, w2_last) = _ffn_call(
        [x], mod1, gains, ffn_w1, ffn_w2, 1, 0,
        [_Cast(ffn_w1, (1, 1), FFN_CAST_BLOCKS), _Cast(ffn_w2, (1, 1), FFN_CAST_BLOCKS)])
    yp, ys = _pool_call(x, mod1, gains, final_g[None, :], pw,
                        pool_scale.reshape(pool_scale.shape[0], 1, D_MODEL), w1_last, w2_last)

    return (yp.reshape(BATCH, SEQ, D_MODEL), ys.reshape(DEC_BATCH, DEC_SEQ, D_MODEL),
            _cache_layout(kt), _cache_layout(vt))
```

```python
import functools

import numpy as np
import jax
import jax.numpy as jnp
from jax import lax
from jax.experimental import pallas as pl
from jax.experimental.pallas import tpu as pltpu

F32 = jnp.float32
BF16 = jnp.bfloat16

D_MODEL = 1024
BATCH = 32
SEQ = 256
DEC_BATCH = 2
DEC_SEQ = 1024
PAST_LEN = 512
GRID_W = 64
N_HEADS = 8
N_KV_HEADS = 2
HEAD_DIM = 64
HEADS_PER_KV = N_HEADS // N_KV_HEADS
ATTN_WIDTH = N_HEADS * HEAD_DIM
KV_WIDTH = N_KV_HEADS * HEAD_DIM
CONV_WIDTH = D_MODEL // 2
MIX_WIDTH = ATTN_WIDTH + CONV_WIDTH
IN_WIDTH = ATTN_WIDTH + 2 * KV_WIDTH + 3 * CONV_WIDTH
D_FF = 2816
POOL_WINDOWS = (2, 4, 8, 16)
POOL_GROUP = D_MODEL // len(POOL_WINDOWS)
N_MOD = 9
ROPE_THETA = 10000.0
EPS = 1e-6

N_CTX = BATCH * SEQ
N_SMP = DEC_BATCH * DEC_SEQ
N_ALL = N_CTX + N_SMP
MIX_TILE = 1024
FFN_TILE = 512
FF_CHUNK = 256
N_FF_CHUNKS = D_FF // FF_CHUNK
STAGE_SLOTS = 2
Q_ROWS = 256
SEG_WIDTH = 256
POOL_BLOCK = 256
SEQ_PER_TILE = MIX_TILE // SEQ
MOD_ROWS = 8
MOD_WIDTH = N_MOD * D_MODEL
MOD_STEPS = 4
MOD_SIDE_BLOCKS = 18
FFN_CAST_BLOCKS = 16
VMEM_LIMIT = 56 * 1024 * 1024


def _n_ctx_tiles(tile):
    return N_CTX // tile


def _mod_row(i, tile):
    first = _n_ctx_tiles(tile)
    return jnp.where(i < first, 0, 1 + (i - first) * tile // DEC_SEQ)


def _rms(x, g):
    return x * lax.rsqrt(jnp.mean(x * x, axis=-1, keepdims=True) + EPS) * g


def _mod_vec(mod_ref, r, j):
    return mod_ref[pl.ds(r, 1), j * D_MODEL:(j + 1) * D_MODEL]


def _modulate(x, g, mod_ref, r, j):
    return _rms(x, g) * (1 + _mod_vec(mod_ref, r, j + 1)) + _mod_vec(mod_ref, r, j)


def _ffn_rows(x, g, mod_ref, r, j, w1_ref, w2_ref, before_chunk=None):
    h = _modulate(x, g, mod_ref, r, j).astype(BF16)
    acc = None
    for c in range(N_FF_CHUNKS):
        lo = c * FF_CHUNK
        if before_chunk is not None:
            before_chunk(c)
        gate = jnp.dot(h, w1_ref[:, lo:lo + FF_CHUNK], preferred_element_type=F32)
        up = jnp.dot(h, w1_ref[:, D_FF + lo:D_FF + lo + FF_CHUNK], preferred_element_type=F32)
        act = (gate / (1 + jnp.exp(-gate)) * up).astype(BF16)
        y = jnp.dot(act, w2_ref[lo:lo + FF_CHUNK, :], preferred_element_type=F32)
        acc = y if acc is None else acc + y
    return x + (0.5 * _mod_vec(mod_ref, r, j + 2)) * acc


class _Cast:
    def __init__(self, src, lead, n_blocks):
        self.src, self.lead, self.n_blocks = src, tuple(lead), n_blocks
        self.rows, self.cols = src.shape[len(lead):]
        self.block_rows = self.rows // n_blocks

    def in_spec(self, step):
        lead, last = self.lead, self.n_blocks - 1
        return pl.BlockSpec((None,) * len(lead) + (self.block_rows, self.cols),
                            lambda *g: lead + (jnp.minimum(step(*g), last), 0))

    def out_spec(self, step):
        last = self.n_blocks - 1
        return pl.BlockSpec((self.block_rows, self.cols), lambda *g: (jnp.minimum(step(*g), last), 0))

    def out_shape(self):
        return jax.ShapeDtypeStruct((self.rows, self.cols), BF16)


def _cast_blocks(step, n_blocks, srcs, dsts):
    @pl.when(step < n_blocks)
    def _():
        for s, d in zip(srcs, dsts):
            d[...] = s[...].astype(BF16)


class _WeightStream:
    def __init__(self, w1_hbm, w2_hbm, layer, which, w1_s, w2_s, gate_st, up_st, down_st, sems):
        self.w1_hbm, self.w2_hbm, self.lead = w1_hbm, w2_hbm, (layer, which)
        self.w1_s, self.w2_s = w1_s, w2_s
        self.stages, self.sems = (gate_st, up_st, down_st), sems

    def _copies(self, c):
        slot, lo = c % STAGE_SLOTS, c * FF_CHUNK
        l, w = self.lead
        srcs = (self.w1_hbm.at[l, w, :, pl.ds(lo, FF_CHUNK)],
                self.w1_hbm.at[l, w, :, pl.ds(D_FF + lo, FF_CHUNK)],
                self.w2_hbm.at[l, w, pl.ds(lo, FF_CHUNK), :])
        return [pltpu.make_async_copy(src, st.at[slot], self.sems.at[k, slot])
                for k, (src, st) in enumerate(zip(srcs, self.stages))]

    def start(self, c):
        for cp in self._copies(c):
            cp.start()

    def prime(self):
        for c in range(STAGE_SLOTS):
            self.start(c)

    def land(self, c):
        for cp in self._copies(c):
            cp.wait()
        slot, lo = c % STAGE_SLOTS, c * FF_CHUNK
        gate_st, up_st, down_st = self.stages
        self.w1_s[:, lo:lo + FF_CHUNK] = gate_st[slot].astype(BF16)
        self.w1_s[:, D_FF + lo:D_FF + lo + FF_CHUNK] = up_st[slot].astype(BF16)
        self.w2_s[lo:lo + FF_CHUNK, :] = down_st[slot].astype(BF16)
        if c + STAGE_SLOTS < N_FF_CHUNKS:
            self.start(c + STAGE_SLOTS)


def _weight_stream_specs():
    return [pl.BlockSpec(memory_space=pl.ANY), pl.BlockSpec(memory_space=pl.ANY)]


N_STREAM_SCRATCH = 6


def _weight_stream_scratch():
    return [pltpu.VMEM((D_MODEL, 2 * D_FF), BF16), pltpu.VMEM((D_FF, D_MODEL), BF16),
            pltpu.VMEM((STAGE_SLOTS, D_MODEL, FF_CHUNK), F32),
            pltpu.VMEM((STAGE_SLOTS, D_MODEL, FF_CHUNK), F32),
            pltpu.VMEM((STAGE_SLOTS, FF_CHUNK, D_MODEL), F32),
            pltpu.SemaphoreType.DMA((3, STAGE_SLOTS))]


def _mod_block(c_ref, w_ref, b_ref):
    c = c_ref[...]
    s = (c / (1 + jnp.exp(-c))).astype(BF16)
    return jnp.dot(s, w_ref[...].astype(BF16), preferred_element_type=F32) + b_ref[...]


class _ModJob:
    def __init__(self, cvec, ada_w, ada_b, layer, n_blocks):
        self.args = (cvec, ada_w, ada_b.reshape(ada_b.shape[0], 1, MOD_WIDTH))
        self.layer, self.n_blocks, self.cols = layer, n_blocks, MOD_WIDTH // n_blocks

    def in_specs(self):
        layer, last, cols = self.layer, self.n_blocks - 1, self.cols
        block = lambda i: (layer, 0, jnp.minimum(i, last))
        return [pl.BlockSpec((MOD_ROWS, D_MODEL), lambda i: (0, 0)),
                pl.BlockSpec((None, D_MODEL, cols), block),
                pl.BlockSpec((None, 1, cols), block)]

    def out_spec(self):
        last = self.n_blocks - 1
        return pl.BlockSpec((MOD_ROWS, self.cols), lambda i: (0, jnp.minimum(i, last)))

    def out_shape(self):
        return jax.ShapeDtypeStruct((MOD_ROWS, MOD_WIDTH), F32)


def _mod_kernel(c_ref, w_ref, b_ref, *refs, n_cast):
    srcs, o_ref, dsts = refs[:n_cast], refs[n_cast], refs[n_cast + 1:]
    o_ref[...] = _mod_block(c_ref, w_ref, b_ref)
    _cast_blocks(pl.program_id(0), MOD_STEPS, srcs, dsts)


def _mod_call(job, casts):
    step = lambda i: i
    out = pl.pallas_call(
        functools.partial(_mod_kernel, n_cast=len(casts)),
        grid=(job.n_blocks,),
        in_specs=job.in_specs() + [c.in_spec(step) for c in casts],
        out_specs=[job.out_spec()] + [c.out_spec(step) for c in casts],
        out_shape=[job.out_shape()] + [c.out_shape() for c in casts],
        compiler_params=_params(),
        name="adaln_mod",
    )(*job.args, *[c.src for c in casts])
    return out[0], out[1:]


def _rows_spec(tile, width):
    return pl.BlockSpec((tile, width), lambda i: (i, 0))


def _ctx_rows_spec(tile, width):
    last = _n_ctx_tiles(tile) - 1
    return pl.BlockSpec((tile, width), lambda i: (jnp.minimum(i, last), 0))


def _smp_rows_spec(tile, width):
    first = _n_ctx_tiles(tile)
    return pl.BlockSpec((tile, width), lambda i: (jnp.maximum(i - first, 0), 0))


def _mod_spec():
    return pl.BlockSpec((MOD_ROWS, MOD_WIDTH), lambda i: (0, 0))


def _gain_spec(layer, j):
    return pl.BlockSpec((None, None, 1, D_MODEL), lambda i: (layer, j, 0, 0))


def _resident_spec(shape, *lead):
    index = tuple(lead) + (0,) * len(shape)
    return pl.BlockSpec((None,) * len(lead) + tuple(shape), lambda i: index,
                        pipeline_mode=pl.Buffered(1))


def _params():
    return pltpu.CompilerParams(dimension_semantics=("arbitrary",), vmem_limit_bytes=VMEM_LIMIT)


def _ffn_kernel(*refs, j, n_in, n_cast, mod_blocks, layer, which):
    n_side = n_cast + (3 if mod_blocks else 0)
    x_refs = refs[:n_in]
    mod_ref, g_ref, w1_hbm, w2_hbm = refs[n_in:n_in + 4]
    side_in = refs[n_in + 4:n_in + 4 + n_side]
    o_ref = refs[n_in + 4 + n_side]
    side_out = refs[n_in + 5 + n_side:len(refs) - N_STREAM_SCRATCH]
    stream = _WeightStream(w1_hbm, w2_hbm, layer, which, *refs[len(refs) - N_STREAM_SCRATCH:])
    i = pl.program_id(0)

    def run(before_chunk):
        if n_in == 2:
            x = jnp.where(i < _n_ctx_tiles(FFN_TILE), x_refs[0][...], x_refs[1][...])
        else:
            x = x_refs[0][...]
        o_ref[...] = _ffn_rows(x, g_ref[...], mod_ref, _mod_row(i, FFN_TILE), j,
                               stream.w1_s, stream.w2_s, before_chunk)

    @pl.when(i == 0)
    def _():
        stream.prime()
        run(stream.land)

    @pl.when(i > 0)
    def _():
        run(None)

    _cast_blocks(i, FFN_CAST_BLOCKS, side_in[:n_cast], side_out[:n_cast])
    if mod_blocks:
        @pl.when(i < mod_blocks)
        def _():
            side_out[n_cast][...] = _mod_block(*side_in[n_cast:])


def _ffn_call(xs, mod, norm_g, w1, w2, layer, which, casts=(), mod_job=None):
    step = lambda i: i
    if len(xs) == 2:
        x_specs = [_ctx_rows_spec(FFN_TILE, D_MODEL), _smp_rows_spec(FFN_TILE, D_MODEL)]
    else:
        x_specs = [_rows_spec(FFN_TILE, D_MODEL)]
    jobs = [mod_job] if mod_job else []
    out = pl.pallas_call(
        functools.partial(_ffn_kernel, j=6 * which, n_in=len(xs), n_cast=len(casts),
                          mod_blocks=mod_job.n_blocks if mod_job else 0, layer=layer, which=which),
        grid=(N_ALL // FFN_TILE,),
        in_specs=x_specs + [_mod_spec(), _gain_spec(layer, 2 * which)] + _weight_stream_specs()
        + [c.in_spec(step) for c in casts] + [s for m in jobs for s in m.in_specs()],
        out_specs=[_rows_spec(FFN_TILE, D_MODEL)] + [c.out_spec(step) for c in casts]
        + [m.out_spec() for m in jobs],
        out_shape=[jax.ShapeDtypeStruct((N_ALL, D_MODEL), F32)] + [c.out_shape() for c in casts]
        + [m.out_shape() for m in jobs],
        scratch_shapes=_weight_stream_scratch(),
        compiler_params=_params(),
        name="ffn",
    )(*xs, mod, norm_g, w1, w2, *[c.src for c in casts], *[a for m in jobs for a in m.args])
    return out[0], out[1:]


def _head_mean_sq(x, seg):
    sq = x * x
    hi = sq.astype(BF16)
    lo = (sq - hi.astype(F32)).astype(BF16)
    width = x.shape[-1]
    parts = []
    for c in range(0, width, SEG_WIDTH):
        n = min(SEG_WIDTH, width - c)
        parts.append(jnp.dot(hi[:, c:c + n], seg[:n, :n], preferred_element_type=F32)
                     + jnp.dot(lo[:, c:c + n], seg[:n, :n], preferred_element_type=F32))
    return parts[0] if len(parts) == 1 else jnp.concatenate(parts, axis=1)


def _rope(x, cos, sin_signed):
    width = x.shape[-1]
    lane = lax.broadcasted_iota(jnp.int32, (1, width), 1)
    partner = jnp.where((lane & 31) < 16,
                        pltpu.roll(x, width - 16, 1), pltpu.roll(x, 16, 1))
    return x * cos + partner * sin_signed


def _stack_heads(q_ref, rows, kvh):
    first = kvh * HEADS_PER_KV
    return jnp.concatenate(
        [q_ref[rows, (first + g) * HEAD_DIM:(first + g + 1) * HEAD_DIM] for g in range(HEADS_PER_KV)],
        axis=0)


ONES_ROWS = 16


def _attend_group(q, k, v1t):
    st = lax.dot_general(k, q, (((1,), (1,)), ((), ())), preferred_element_type=F32)
    e = jnp.exp(st - jnp.max(st, axis=0, keepdims=True)).astype(BF16)
    ot = jnp.dot(v1t, e, preferred_element_type=F32)
    return ot[:HEAD_DIM] / ot[HEAD_DIM:HEAD_DIM + 1]


def _attend_rows(q_s, rows, n, kv, mix_s):
    heads = []
    for kvh in range(N_KV_HEADS):
        k, v1t = kv(kvh)
        ot = _attend_group(_stack_heads(q_s, rows, kvh), k, v1t)
        heads += [ot[:, g * n:(g + 1) * n] for g in range(HEADS_PER_KV)]
    mix_s[rows, :ATTN_WIDTH] = jnp.concatenate(heads, axis=0).T.astype(BF16)


def _with_ones(vt):
    return jnp.concatenate([vt, jnp.ones((ONES_ROWS, vt.shape[1]), BF16)], axis=0)


def _mixer_tile(ctx, r, x_ref, mod_ref, g_ref, w_ref, qg_ref, kg_ref, cw_ref, seg_ref, cos_ref, sin_ref,
                ck_ref, cvt_ref, wo_ref, o_ref, kt_ref, vt_ref, q_s, k_s, vt_s, mix_s):
    seq_len = SEQ if ctx else DEC_SEQ
    x = x_ref[...]
    h = _modulate(x, g_ref[...], mod_ref, r, 3).astype(BF16)

    def proj(lo, width):
        return jnp.dot(h, w_ref[:, lo:lo + width], preferred_element_type=F32)

    q = proj(0, ATTN_WIDTH)
    kv = proj(ATTN_WIDTH, 2 * KV_WIDTH)
    k, v = kv[:, :KV_WIDTH], kv[:, KV_WIDTH:]
    q = q * lax.rsqrt(_head_mean_sq(q, seg_ref[...]) + EPS) * (qg_ref[...] * HEAD_DIM ** -0.5)
    k = k * lax.rsqrt(_head_mean_sq(k, seg_ref[...]) + EPS) * kg_ref[...]
    if ctx:
        q_s[...] = q.astype(BF16)
        k_s[...] = k.astype(BF16)
        for b in range(SEQ_PER_TILE):
            vt = v[b * SEQ:(b + 1) * SEQ, :].T
            kt_ref[b] = k[b * SEQ:(b + 1) * SEQ, :].T
            vt_ref[b] = vt
            vt_s[:, b * SEQ:(b + 1) * SEQ] = vt.astype(BF16)
    else:
        q_s[...] = _rope(q, cos_ref[...], sin_ref[...]).astype(BF16)
        k_s[...] = _rope(k, cos_ref[:, :KV_WIDTH], sin_ref[:, :KV_WIDTH]).astype(BF16)
        vt_s[...] = v.T.astype(BF16)

    base = ATTN_WIDTH + 2 * KV_WIDTH
    z = proj(base + CONV_WIDTH, CONV_WIDTH) * proj(base + 2 * CONV_WIDTH, CONV_WIDTH)
    pos = lax.broadcasted_iota(jnp.int32, (MIX_TILE, 1), 0) & (seq_len - 1)
    z_prev = jnp.where(pos == 0, 0.0, pltpu.roll(z, 1, 0))
    z_next = jnp.where(pos == seq_len - 1, 0.0, pltpu.roll(z, MIX_TILE - 1, 0))
    conv = z_prev * cw_ref[0:1, :] + z * cw_ref[1:2, :] + z_next * cw_ref[2:3, :]
    mix_s[:, ATTN_WIDTH:] = (proj(base, CONV_WIDTH) * conv).astype(BF16)

    def head(kvh):
        return slice(kvh * HEAD_DIM, (kvh + 1) * HEAD_DIM)

    if ctx:
        for b in range(SEQ_PER_TILE):
            rows = slice(b * SEQ, (b + 1) * SEQ)
            _attend_rows(q_s, rows, SEQ,
                         lambda kvh: (k_s[rows, head(kvh)], _with_ones(vt_s[head(kvh), rows])), mix_s)
    else:
        kv = [(jnp.concatenate([ck_ref[:, head(kvh)].astype(BF16), k_s[:, head(kvh)]], axis=0),
               _with_ones(jnp.concatenate([cvt_ref[head(kvh), :].astype(BF16), vt_s[head(kvh), :]],
                                          axis=1)))
              for kvh in range(N_KV_HEADS)]
        for t in range(MIX_TILE // Q_ROWS):
            _attend_rows(q_s, slice(t * Q_ROWS, (t + 1) * Q_ROWS), Q_ROWS, lambda kvh: kv[kvh], mix_s)

    out = (jnp.dot(mix_s[:, :ATTN_WIDTH], wo_ref[:ATTN_WIDTH, :], preferred_element_type=F32)
           + jnp.dot(mix_s[:, ATTN_WIDTH:], wo_ref[ATTN_WIDTH:, :], preferred_element_type=F32))
    o_ref[...] = x + _mod_vec(mod_ref, r, 5) * out


def _mixer_kernel(*refs):
    i = pl.program_id(0)
    r = _mod_row(i, MIX_TILE)
    is_ctx = i < _n_ctx_tiles(MIX_TILE)

    @pl.when(is_ctx)
    def _():
        _mixer_tile(True, r, *refs)

    @pl.when(jnp.logical_not(is_ctx))
    def _():
        _mixer_tile(False, r, *refs)


def _mixer_call(x, mod, norm_g, w_in, qg, kg, conv_w, seg, cos, sin, cache_k, cache_vt, w_out):
    first_smp = _n_ctx_tiles(MIX_TILE)
    smp_batch = lambda i: (jnp.maximum(i - first_smp, 0), 0, 0)
    kvt_spec = pl.BlockSpec((SEQ_PER_TILE, KV_WIDTH, SEQ), lambda i: (jnp.minimum(i, first_smp - 1), 0, 0))
    kvt_shape = jax.ShapeDtypeStruct((BATCH, KV_WIDTH, SEQ), F32)
    return pl.pallas_call(
        _mixer_kernel,
        grid=(N_ALL // MIX_TILE,),
        in_specs=[
            _rows_spec(MIX_TILE, D_MODEL), _mod_spec(), _gain_spec(0, 1),
            _resident_spec((D_MODEL, IN_WIDTH)),
            _resident_spec((1, ATTN_WIDTH)), _resident_spec((1, KV_WIDTH)),
            _resident_spec((3, CONV_WIDTH), 0),
            _resident_spec((SEG_WIDTH, SEG_WIDTH)),
            _resident_spec((DEC_SEQ, ATTN_WIDTH)), _resident_spec((DEC_SEQ, ATTN_WIDTH)),
            pl.BlockSpec((None, PAST_LEN, KV_WIDTH), smp_batch),
            pl.BlockSpec((None, KV_WIDTH, PAST_LEN), smp_batch),
            _resident_spec((MIX_WIDTH, D_MODEL)),
        ],
        out_specs=[_rows_spec(MIX_TILE, D_MODEL), kvt_spec, kvt_spec],
        out_shape=[jax.ShapeDtypeStruct((N_ALL, D_MODEL), F32), kvt_shape, kvt_shape],
        scratch_shapes=[pltpu.VMEM((MIX_TILE, ATTN_WIDTH), BF16),
                        pltpu.VMEM((MIX_TILE, KV_WIDTH), BF16),
                        pltpu.VMEM((KV_WIDTH, MIX_TILE), BF16),
                        pltpu.VMEM((MIX_TILE, MIX_WIDTH), BF16)],
        compiler_params=_params(),
        name="mixer",
    )(x, mod, norm_g, w_in, qg, kg, conv_w, seg, cos, sin, cache_k, cache_vt, w_out)


def _window_sums(h, band_ref, gi, seq_len):
    hi = h.astype(BF16)
    lo = (h - hi.astype(F32)).astype(BF16)
    per_seq = seq_len // POOL_BLOCK
    sums = []
    for b in range(h.shape[0] // POOL_BLOCK):
        def part(kind, src):
            rows = slice(src * POOL_BLOCK, (src + 1) * POOL_BLOCK)
            band = band_ref[gi, kind]
            return (jnp.dot(band, hi[rows], preferred_element_type=F32)
                    + jnp.dot(band, lo[rows], preferred_element_type=F32))
        total = part(0, b)
        if (b + 1) % per_seq:
            total = total + part(1, b + 1)
        if b % per_seq:
            total = total + part(2, b - 1)
        sums.append(total)
    return jnp.concatenate(sums, axis=0)


def _pool_rows(x, pos, seq_len, r, mod_ref, gm_ref, pw_ref, ps_ref, band_ref):
    gate = _mod_vec(mod_ref, r, 5)
    h_all = _modulate(x, gm_ref[...], mod_ref, r, 3)
    outs = []
    for gi, w in enumerate(POOL_WINDOWS):
        lanes = slice(gi * POOL_GROUP, (gi + 1) * POOL_GROUP)
        h = h_all[:, lanes]
        left = w // 2
        right = w - 1 - left
        count = jnp.minimum(pos + right + 1, seq_len) - jnp.maximum(pos - left, 0)
        diff = (_window_sums(h, band_ref, gi, seq_len) / count.astype(F32) - h).astype(BF16)
        out = jnp.dot(diff, pw_ref[lanes, :], preferred_element_type=F32) * ps_ref[:, lanes]
        outs.append(x[:, lanes] + gate[:, lanes] * out)
    return jnp.concatenate(outs, axis=1)


def _pool_tile(ctx, r, x_ref, mod_ref, gm_ref, gf_ref, gn_ref, pw_ref, ps_ref, band_ref, w1_ref, w2_ref,
               y_ref, x2_ref):
    seq_len = SEQ if ctx else DEC_SEQ
    block = max(seq_len, FFN_TILE)
    pos = lax.broadcasted_iota(jnp.int32, (block, 1), 0) & (seq_len - 1)
    for b in range(MIX_TILE // block):
        rows = slice(b * block, (b + 1) * block)
        x2_ref[rows, :] = _pool_rows(x_ref[rows, :], pos, seq_len, r, mod_ref, gm_ref, pw_ref, ps_ref,
                                     band_ref)

    def ffn_step(s, carry):
        rows = pl.ds(pl.multiple_of(s * FFN_TILE, FFN_TILE), FFN_TILE)
        y = _ffn_rows(x2_ref[rows, :], gf_ref[...], mod_ref, r, 6, w1_ref, w2_ref)
        y_ref[rows, :] = _rms(y, gn_ref[...])
        return carry

    lax.fori_loop(0, MIX_TILE // FFN_TILE, ffn_step, 0)


def _pool_kernel(*refs):
    ins, (yp_ref, ys_ref, x2_ref) = refs[:-3], refs[-3:]
    i = pl.program_id(0)
    r = _mod_row(i, MIX_TILE)
    is_ctx = i < _n_ctx_tiles(MIX_TILE)

    @pl.when(is_ctx)
    def _():
        _pool_tile(True, r, *ins, yp_ref, x2_ref)

    @pl.when(jnp.logical_not(is_ctx))
    def _():
        _pool_tile(False, r, *ins, ys_ref, x2_ref)


def _pool_call(x, mod, norm_g, final_g, pool_w, pool_scale, bands, w1, w2):
    return pl.pallas_call(
        _pool_kernel,
        grid=(N_ALL // MIX_TILE,),
        in_specs=[
            _rows_spec(MIX_TILE, D_MODEL), _mod_spec(), _gain_spec(1, 1), _gain_spec(1, 2),
            _resident_spec((1, D_MODEL)),
            _resident_spec((D_MODEL, POOL_GROUP)),
            _resident_spec((1, D_MODEL), 0),
            _resident_spec(bands.shape),
            _resident_spec((D_MODEL, 2 * D_FF)), _resident_spec((D_FF, D_MODEL)),
        ],
        out_specs=[_ctx_rows_spec(MIX_TILE, D_MODEL), _smp_rows_spec(MIX_TILE, D_MODEL)],
        out_shape=[jax.ShapeDtypeStruct((N_CTX, D_MODEL), F32),
                   jax.ShapeDtypeStruct((N_SMP, D_MODEL), F32)],
        scratch_shapes=[pltpu.VMEM((MIX_TILE, D_MODEL), F32)],
        compiler_params=_params(),
        name="pool_ffn_norm",
    )(x, mod, norm_g, norm_g, final_g, pool_w, pool_scale, bands, w1, w2)


def _rope_tables():
    t = np.arange(DEC_SEQ)
    half = HEAD_DIM // 2
    inv = ROPE_THETA ** (-np.arange(0, half, 2, dtype=np.float64) / half)
    ang_row = (t // GRID_W)[:, None] * inv[None, :]
    ang_col = (t % GRID_W)[:, None] * inv[None, :]
    cos = np.concatenate([np.cos(ang_row), np.cos(ang_row), np.cos(ang_col), np.cos(ang_col)], axis=1)
    sin = np.concatenate([-np.sin(ang_row), np.sin(ang_row), -np.sin(ang_col), np.sin(ang_col)], axis=1)
    return (jnp.asarray(np.tile(cos, (1, N_HEADS)), F32), jnp.asarray(np.tile(sin, (1, N_HEADS)), F32))


def _head_segments():
    head = np.arange(SEG_WIDTH) // HEAD_DIM
    return jnp.asarray((head[:, None] == head[None, :]) / HEAD_DIM, BF16)


def _pool_bands():
    t = np.arange(POOL_BLOCK)[:, None]
    bands = []
    for w in POOL_WINDOWS:
        left = w // 2
        right = w - 1 - left
        src = [np.arange(POOL_BLOCK)[None, :] + shift for shift in (0, POOL_BLOCK, -POOL_BLOCK)]
        bands.append(np.stack([(s >= t - left) & (s <= t + right) for s in src]))
    return jnp.asarray(np.stack(bands), BF16)


def _cache_layout(t):
    return jnp.transpose(t.reshape(BATCH, 1, N_KV_HEADS, HEAD_DIM, SEQ), (0, 1, 4, 2, 3))


def kernel(x_prompt, x_sample, c, cache_k, cache_v, c_ctx, ada_w, ada_b, norm_g, ffn_w1, ffn_w2,
           mix_w_in, mix_w_out, q_norm, k_norm, conv_w, pool_w, pool_scale, final_g):
    cvec = jnp.concatenate(
        [c_ctx[None, :], c, jnp.zeros((MOD_ROWS - 1 - DEC_BATCH, D_MODEL), F32)], axis=0)
    gains = norm_g.reshape(norm_g.shape[0], 3, 1, D_MODEL)
    cos, sin = _rope_tables()

    mod0, _ = _mod_call(_ModJob(cvec, ada_w, ada_b, 0, MOD_STEPS), [])
    x, (w_in, w_out) = _ffn_call(
        [x_prompt.reshape(N_CTX, D_MODEL), x_sample.reshape(N_SMP, D_MODEL)],
        mod0, gains, ffn_w1, ffn_w2, 0, 0,
        [_Cast(mix_w_in, (0,), FFN_CAST_BLOCKS), _Cast(mix_w_out, (0,), FFN_CAST_BLOCKS)])
    x, kt, vt = _mixer_call(
        x, mod0, gains, w_in, jnp.tile(q_norm[0], N_HEADS)[None, :],
        jnp.tile(k_norm[0], N_KV_HEADS)[None, :], conv_w, _head_segments(), cos, sin,
        cache_k[:, 0].reshape(DEC_BATCH, PAST_LEN, KV_WIDTH),
        jnp.transpose(cache_v[:, 0], (0, 2, 3, 1)).reshape(DEC_BATCH, KV_WIDTH, PAST_LEN), w_out)
    x, (pw, mod1) = _ffn_call(
        [x], mod0, gains, ffn_w1, ffn_w2, 0, 1,
        [_Cast(pool_w.reshape(pool_w.shape[0], D_MODEL, POOL_GROUP), (0,), FFN_CAST_BLOCKS)],
        _ModJob(cvec, ada_w, ada_b, 1, MOD_SIDE_BLOCKS))
    x, (w1_last, w2_last) = _ffn_call(
        [x], mod1, gains, ffn_w1, ffn_w2, 1, 0,
        [_Cast(ffn_w1, (1, 1), FFN_CAST_BLOCKS), _Cast(ffn_w2, (1, 1), FFN_CAST_BLOCKS)])
    yp, ys = _pool_call(x, mod1, gains, final_g[None, :], pw,
                        pool_scale.reshape(pool_scale.shape[0], 1, D_MODEL), _pool_bands(), w1_last, w2_last)

    return (yp.reshape(BATCH, SEQ, D_MODEL), ys.reshape(DEC_BATCH, DEC_SEQ, D_MODEL),
            _cache_layout(kt), _cache_layout(vt))
```

```python
import functools

import numpy as np
import jax
import jax.numpy as jnp
from jax import lax
from jax.experimental import pallas as pl
from jax.experimental.pallas import tpu as pltpu

F32 = jnp.float32
BF16 = jnp.bfloat16

D_MODEL = 1024
BATCH = 32
SEQ = 256
DEC_BATCH = 2
DEC_SEQ = 1024
PAST_LEN = 512
GRID_W = 64
N_HEADS = 8
N_KV_HEADS = 2
HEAD_DIM = 64
HEADS_PER_KV = N_HEADS // N_KV_HEADS
ATTN_WIDTH = N_HEADS * HEAD_DIM
KV_WIDTH = N_KV_HEADS * HEAD_DIM
CONV_WIDTH = D_MODEL // 2
MIX_WIDTH = ATTN_WIDTH + CONV_WIDTH
IN_WIDTH = ATTN_WIDTH + 2 * KV_WIDTH + 3 * CONV_WIDTH
D_FF = 2816
POOL_WINDOWS = (2, 4, 8, 16)
POOL_GROUP = D_MODEL // len(POOL_WINDOWS)
N_MOD = 9
ROPE_THETA = 10000.0
EPS = 1e-6

N_CTX = BATCH * SEQ
N_SMP = DEC_BATCH * DEC_SEQ
N_ALL = N_CTX + N_SMP
MIX_TILE = 1024
FFN_TILE = 512
FF_CHUNK = 256
N_FF_CHUNKS = D_FF // FF_CHUNK
STAGE_SLOTS = 2
Q_ROWS = 128
SEG_WIDTH = 256
POOL_BLOCK = 256
SEQ_PER_TILE = MIX_TILE // SEQ
MOD_ROWS = 8
MOD_WIDTH = N_MOD * D_MODEL
MOD_STEPS = 4
MOD_SIDE_BLOCKS = 18
FFN_CAST_BLOCKS = 16
VMEM_LIMIT = 56 * 1024 * 1024


def _n_ctx_tiles(tile):
    return N_CTX // tile


def _mod_row(i, tile):
    first = _n_ctx_tiles(tile)
    return jnp.where(i < first, 0, 1 + (i - first) * tile // DEC_SEQ)


def _rms(x, g):
    return x * lax.rsqrt(jnp.mean(x * x, axis=-1, keepdims=True) + EPS) * g


def _mod_vec(mod_ref, r, j):
    return mod_ref[pl.ds(r, 1), j * D_MODEL:(j + 1) * D_MODEL]


def _modulate(x, g, mod_ref, r, j):
    return _rms(x, g) * (1 + _mod_vec(mod_ref, r, j + 1)) + _mod_vec(mod_ref, r, j)


def _ffn_rows(x, g, mod_ref, r, j, w1_ref, w2_ref, before_chunk=None):
    h = _modulate(x, g, mod_ref, r, j).astype(BF16)
    acc = None
    for c in range(N_FF_CHUNKS):
        lo = c * FF_CHUNK
        if before_chunk is not None:
            before_chunk(c)
        gate = jnp.dot(h, w1_ref[:, lo:lo + FF_CHUNK], preferred_element_type=F32)
        up = jnp.dot(h, w1_ref[:, D_FF + lo:D_FF + lo + FF_CHUNK], preferred_element_type=F32)
        act = (gate / (1 + jnp.exp(-gate)) * up).astype(BF16)
        y = jnp.dot(act, w2_ref[lo:lo + FF_CHUNK, :], preferred_element_type=F32)
        acc = y if acc is None else acc + y
    return x + (0.5 * _mod_vec(mod_ref, r, j + 2)) * acc


class _Cast:
    def __init__(self, src, lead, n_blocks):
        self.src, self.lead, self.n_blocks = src, tuple(lead), n_blocks
        self.rows, self.cols = src.shape[len(lead):]
        self.block_rows = self.rows // n_blocks

    def in_spec(self, step):
        lead, last = self.lead, self.n_blocks - 1
        return pl.BlockSpec((None,) * len(lead) + (self.block_rows, self.cols),
                            lambda *g: lead + (jnp.minimum(step(*g), last), 0))

    def out_spec(self, step):
        last = self.n_blocks - 1
        return pl.BlockSpec((self.block_rows, self.cols), lambda *g: (jnp.minimum(step(*g), last), 0))

    def out_shape(self):
        return jax.ShapeDtypeStruct((self.rows, self.cols), BF16)


def _cast_blocks(step, n_blocks, srcs, dsts):
    @pl.when(step < n_blocks)
    def _():
        for s, d in zip(srcs, dsts):
            d[...] = s[...].astype(BF16)


class _WeightStream:
    def __init__(self, w1_hbm, w2_hbm, layer, which, w1_s, w2_s, gate_st, up_st, down_st, sems):
        self.w1_hbm, self.w2_hbm, self.lead = w1_hbm, w2_hbm, (layer, which)
        self.w1_s, self.w2_s = w1_s, w2_s
        self.stages, self.sems = (gate_st, up_st, down_st), sems

    def _copies(self, c):
        slot, lo = c % STAGE_SLOTS, c * FF_CHUNK
        l, w = self.lead
        srcs = (self.w1_hbm.at[l, w, :, pl.ds(lo, FF_CHUNK)],
                self.w1_hbm.at[l, w, :, pl.ds(D_FF + lo, FF_CHUNK)],
                self.w2_hbm.at[l, w, pl.ds(lo, FF_CHUNK), :])
        return [pltpu.make_async_copy(src, st.at[slot], self.sems.at[k, slot])
                for k, (src, st) in enumerate(zip(srcs, self.stages))]

    def start(self, c):
        for cp in self._copies(c):
            cp.start()

    def prime(self):
        for c in range(STAGE_SLOTS):
            self.start(c)

    def land(self, c):
        for cp in self._copies(c):
            cp.wait()
        slot, lo = c % STAGE_SLOTS, c * FF_CHUNK
        gate_st, up_st, down_st = self.stages
        self.w1_s[:, lo:lo + FF_CHUNK] = gate_st[slot].astype(BF16)
        self.w1_s[:, D_FF + lo:D_FF + lo + FF_CHUNK] = up_st[slot].astype(BF16)
        self.w2_s[lo:lo + FF_CHUNK, :] = down_st[slot].astype(BF16)
        if c + STAGE_SLOTS < N_FF_CHUNKS:
            self.start(c + STAGE_SLOTS)


def _weight_stream_specs():
    return [pl.BlockSpec(memory_space=pl.ANY), pl.BlockSpec(memory_space=pl.ANY)]


N_STREAM_SCRATCH = 6


def _weight_stream_scratch():
    return [pltpu.VMEM((D_MODEL, 2 * D_FF), BF16), pltpu.VMEM((D_FF, D_MODEL), BF16),
            pltpu.VMEM((STAGE_SLOTS, D_MODEL, FF_CHUNK), F32),
            pltpu.VMEM((STAGE_SLOTS, D_MODEL, FF_CHUNK), F32),
            pltpu.VMEM((STAGE_SLOTS, FF_CHUNK, D_MODEL), F32),
            pltpu.SemaphoreType.DMA((3, STAGE_SLOTS))]


def _mod_block(c_ref, w_ref, b_ref):
    c = c_ref[...]
    s = (c / (1 + jnp.exp(-c))).astype(BF16)
    return jnp.dot(s, w_ref[...].astype(BF16), preferred_element_type=F32) + b_ref[...]


class _ModJob:
    def __init__(self, cvec, ada_w, ada_b, layer, n_blocks):
        self.args = (cvec, ada_w, ada_b.reshape(ada_b.shape[0], 1, MOD_WIDTH))
        self.layer, self.n_blocks, self.cols = layer, n_blocks, MOD_WIDTH // n_blocks

    def in_specs(self):
        layer, last, cols = self.layer, self.n_blocks - 1, self.cols
        block = lambda i: (layer, 0, jnp.minimum(i, last))
        return [pl.BlockSpec((MOD_ROWS, D_MODEL), lambda i: (0, 0)),
                pl.BlockSpec((None, D_MODEL, cols), block),
                pl.BlockSpec((None, 1, cols), block)]

    def out_spec(self):
        last = self.n_blocks - 1
        return pl.BlockSpec((MOD_ROWS, self.cols), lambda i: (0, jnp.minimum(i, last)))

    def out_shape(self):
        return jax.ShapeDtypeStruct((MOD_ROWS, MOD_WIDTH), F32)


def _mod_kernel(c_ref, w_ref, b_ref, *refs, n_cast):
    srcs, o_ref, dsts = refs[:n_cast], refs[n_cast], refs[n_cast + 1:]
    o_ref[...] = _mod_block(c_ref, w_ref, b_ref)
    _cast_blocks(pl.program_id(0), MOD_STEPS, srcs, dsts)


def _mod_call(job, casts):
    step = lambda i: i
    out = pl.pallas_call(
        functools.partial(_mod_kernel, n_cast=len(casts)),
        grid=(job.n_blocks,),
        in_specs=job.in_specs() + [c.in_spec(step) for c in casts],
        out_specs=[job.out_spec()] + [c.out_spec(step) for c in casts],
        out_shape=[job.out_shape()] + [c.out_shape() for c in casts],
        compiler_params=_params(),
        name="adaln_mod",
    )(*job.args, *[c.src for c in casts])
    return out[0], out[1:]


def _rows_spec(tile, width):
    return pl.BlockSpec((tile, width), lambda i: (i, 0))


def _ctx_rows_spec(tile, width):
    last = _n_ctx_tiles(tile) - 1
    return pl.BlockSpec((tile, width), lambda i: (jnp.minimum(i, last), 0))


def _smp_rows_spec(tile, width):
    first = _n_ctx_tiles(tile)
    return pl.BlockSpec((tile, width), lambda i: (jnp.maximum(i - first, 0), 0))


def _mod_spec():
    return pl.BlockSpec((MOD_ROWS, MOD_WIDTH), lambda i: (0, 0))


def _gain_spec(layer, j):
    return pl.BlockSpec((None, None, 1, D_MODEL), lambda i: (layer, j, 0, 0))


def _resident_spec(shape, *lead):
    index = tuple(lead) + (0,) * len(shape)
    return pl.BlockSpec((None,) * len(lead) + tuple(shape), lambda i: index,
                        pipeline_mode=pl.Buffered(1))


def _params():
    return pltpu.CompilerParams(dimension_semantics=("arbitrary",), vmem_limit_bytes=VMEM_LIMIT)


def _ffn_kernel(*refs, j, n_in, n_cast, mod_blocks, layer, which):
    n_side = n_cast + (3 if mod_blocks else 0)
    x_refs = refs[:n_in]
    mod_ref, g_ref, w1_hbm, w2_hbm = refs[n_in:n_in + 4]
    side_in = refs[n_in + 4:n_in + 4 + n_side]
    o_ref = refs[n_in + 4 + n_side]
    side_out = refs[n_in + 5 + n_side:len(refs) - N_STREAM_SCRATCH]
    stream = _WeightStream(w1_hbm, w2_hbm, layer, which, *refs[len(refs) - N_STREAM_SCRATCH:])
    i = pl.program_id(0)

    def run(before_chunk):
        if n_in == 2:
            x = jnp.where(i < _n_ctx_tiles(FFN_TILE), x_refs[0][...], x_refs[1][...])
        else:
            x = x_refs[0][...]
        o_ref[...] = _ffn_rows(x, g_ref[...], mod_ref, _mod_row(i, FFN_TILE), j,
                               stream.w1_s, stream.w2_s, before_chunk)

    @pl.when(i == 0)
    def _():
        stream.prime()
        run(stream.land)

    @pl.when(i > 0)
    def _():
        run(None)

    _cast_blocks(i, FFN_CAST_BLOCKS, side_in[:n_cast], side_out[:n_cast])
    if mod_blocks:
        @pl.when(i < mod_blocks)
        def _():
            side_out[n_cast][...] = _mod_block(*side_in[n_cast:])


def _ffn_call(xs, mod, norm_g, w1, w2, layer, which, casts=(), mod_job=None):
    step = lambda i: i
    if len(xs) == 2:
        x_specs = [_ctx_rows_spec(FFN_TILE, D_MODEL), _smp_rows_spec(FFN_TILE, D_MODEL)]
    else:
        x_specs = [_rows_spec(FFN_TILE, D_MODEL)]
    jobs = [mod_job] if mod_job else []
    out = pl.pallas_call(
        functools.partial(_ffn_kernel, j=6 * which, n_in=len(xs), n_cast=len(casts),
                          mod_blocks=mod_job.n_blocks if mod_job else 0, layer=layer, which=which),
        grid=(N_ALL // FFN_TILE,),
        in_specs=x_specs + [_mod_spec(), _gain_spec(layer, 2 * which)] + _weight_stream_specs()
        + [c.in_spec(step) for c in casts] + [s for m in jobs for s in m.in_specs()],
        out_specs=[_rows_spec(FFN_TILE, D_MODEL)] + [c.out_spec(step) for c in casts]
        + [m.out_spec() for m in jobs],
        out_shape=[jax.ShapeDtypeStruct((N_ALL, D_MODEL), F32)] + [c.out_shape() for c in casts]
        + [m.out_shape() for m in jobs],
        scratch_shapes=_weight_stream_scratch(),
        compiler_params=_params(),
        name="ffn",
    )(*xs, mod, norm_g, w1, w2, *[c.src for c in casts], *[a for m in jobs for a in m.args])
    return out[0], out[1:]


def _head_mean_sq(x, seg):
    sq = x * x
    hi = sq.astype(BF16)
    lo = (sq - hi.astype(F32)).astype(BF16)
    width = x.shape[-1]
    parts = []
    for c in range(0, width, SEG_WIDTH):
        n = min(SEG_WIDTH, width - c)
        parts.append(jnp.dot(hi[:, c:c + n], seg[:n, :n], preferred_element_type=F32)
                     + jnp.dot(lo[:, c:c + n], seg[:n, :n], preferred_element_type=F32))
    return parts[0] if len(parts) == 1 else jnp.concatenate(parts, axis=1)


def _rope(x, cos, sin_signed):
    width = x.shape[-1]
    lane = lax.broadcasted_iota(jnp.int32, (1, width), 1)
    partner = jnp.where((lane & 31) < 16,
                        pltpu.roll(x, width - 16, 1), pltpu.roll(x, 16, 1))
    return x * cos + partner * sin_signed


def _stack_heads(q_ref, rows, kvh):
    first = kvh * HEADS_PER_KV
    return jnp.concatenate(
        [q_ref[rows, (first + g) * HEAD_DIM:(first + g + 1) * HEAD_DIM] for g in range(HEADS_PER_KV)],
        axis=0)


ONES_ROWS = 16


def _attend_group(q, k, v1t):
    st = lax.dot_general(k, q, (((1,), (1,)), ((), ())), preferred_element_type=F32)
    e = jnp.exp(st - jnp.max(st, axis=0, keepdims=True)).astype(BF16)
    ot = jnp.dot(v1t, e, preferred_element_type=F32)
    return ot[:HEAD_DIM] / ot[HEAD_DIM:HEAD_DIM + 1]


def _attend_rows(q_s, rows, n, kv, mix_s):
    heads = []
    for kvh in range(N_KV_HEADS):
        k, v1t = kv(kvh)
        ot = _attend_group(_stack_heads(q_s, rows, kvh), k, v1t)
        heads += [ot[:, g * n:(g + 1) * n] for g in range(HEADS_PER_KV)]
    mix_s[rows, :ATTN_WIDTH] = jnp.concatenate(heads, axis=0).T.astype(BF16)


def _with_ones(vt):
    return jnp.concatenate([vt, jnp.ones((ONES_ROWS, vt.shape[1]), BF16)], axis=0)


def _mixer_tile(ctx, r, x_ref, mod_ref, g_ref, w_ref, qg_ref, kg_ref, cw_ref, seg_ref, cos_ref, sin_ref,
                ck_ref, cvt_ref, wo_ref, o_ref, kt_ref, vt_ref, q_s, k_s, vt_s, mix_s):
    seq_len = SEQ if ctx else DEC_SEQ
    x = x_ref[...]
    h = _modulate(x, g_ref[...], mod_ref, r, 3).astype(BF16)

    def proj(lo, width):
        return jnp.dot(h, w_ref[:, lo:lo + width], preferred_element_type=F32)

    q = proj(0, ATTN_WIDTH)
    kv = proj(ATTN_WIDTH, 2 * KV_WIDTH)
    k, v = kv[:, :KV_WIDTH], kv[:, KV_WIDTH:]
    q = q * lax.rsqrt(_head_mean_sq(q, seg_ref[...]) + EPS) * (qg_ref[...] * HEAD_DIM ** -0.5)
    k = k * lax.rsqrt(_head_mean_sq(k, seg_ref[...]) + EPS) * kg_ref[...]
    if ctx:
        q_s[...] = q.astype(BF16)
        k_s[...] = k.astype(BF16)
        for b in range(SEQ_PER_TILE):
            vt = v[b * SEQ:(b + 1) * SEQ, :].T
            kt_ref[b] = k[b * SEQ:(b + 1) * SEQ, :].T
            vt_ref[b] = vt
            vt_s[:, b * SEQ:(b + 1) * SEQ] = vt.astype(BF16)
    else:
        q_s[...] = _rope(q, cos_ref[...], sin_ref[...]).astype(BF16)
        k_s[...] = _rope(k, cos_ref[:, :KV_WIDTH], sin_ref[:, :KV_WIDTH]).astype(BF16)
        vt_s[...] = v.T.astype(BF16)

    base = ATTN_WIDTH + 2 * KV_WIDTH
    z = proj(base + CONV_WIDTH, CONV_WIDTH) * proj(base + 2 * CONV_WIDTH, CONV_WIDTH)
    pos = lax.broadcasted_iota(jnp.int32, (MIX_TILE, 1), 0) & (seq_len - 1)
    z_prev = jnp.where(pos == 0, 0.0, pltpu.roll(z, 1, 0))
    z_next = jnp.where(pos == seq_len - 1, 0.0, pltpu.roll(z, MIX_TILE - 1, 0))
    conv = z_prev * cw_ref[0:1, :] + z * cw_ref[1:2, :] + z_next * cw_ref[2:3, :]
    mix_s[:, ATTN_WIDTH:] = (proj(base, CONV_WIDTH) * conv).astype(BF16)

    def head(kvh):
        return slice(kvh * HEAD_DIM, (kvh + 1) * HEAD_DIM)

    if ctx:
        for b in range(SEQ_PER_TILE):
            rows = slice(b * SEQ, (b + 1) * SEQ)
            _attend_rows(q_s, rows, SEQ,
                         lambda kvh: (k_s[rows, head(kvh)], _with_ones(vt_s[head(kvh), rows])), mix_s)
    else:
        kv = [(jnp.concatenate([ck_ref[:, head(kvh)].astype(BF16), k_s[:, head(kvh)]], axis=0),
               _with_ones(jnp.concatenate([cvt_ref[head(kvh), :].astype(BF16), vt_s[head(kvh), :]],
                                          axis=1)))
              for kvh in range(N_KV_HEADS)]
        for t in range(MIX_TILE // Q_ROWS):
            _attend_rows(q_s, slice(t * Q_ROWS, (t + 1) * Q_ROWS), Q_ROWS, lambda kvh: kv[kvh], mix_s)

    out = (jnp.dot(mix_s[:, :ATTN_WIDTH], wo_ref[:ATTN_WIDTH, :], preferred_element_type=F32)
           + jnp.dot(mix_s[:, ATTN_WIDTH:], wo_ref[ATTN_WIDTH:, :], preferred_element_type=F32))
    o_ref[...] = x + _mod_vec(mod_ref, r, 5) * out


def _mixer_kernel(*refs):
    i = pl.program_id(0)
    r = _mod_row(i, MIX_TILE)
    is_ctx = i < _n_ctx_tiles(MIX_TILE)

    @pl.when(is_ctx)
    def _():
        _mixer_tile(True, r, *refs)

    @pl.when(jnp.logical_not(is_ctx))
    def _():
        _mixer_tile(False, r, *refs)


def _mixer_call(x, mod, norm_g, w_in, qg, kg, conv_w, seg, cos, sin, cache_k, cache_vt, w_out):
    first_smp = _n_ctx_tiles(MIX_TILE)
    smp_batch = lambda i: (jnp.maximum(i - first_smp, 0), 0, 0)
    kvt_spec = pl.BlockSpec((SEQ_PER_TILE, KV_WIDTH, SEQ), lambda i: (jnp.minimum(i, first_smp - 1), 0, 0))
    kvt_shape = jax.ShapeDtypeStruct((BATCH, KV_WIDTH, SEQ), F32)
    return pl.pallas_call(
        _mixer_kernel,
        grid=(N_ALL // MIX_TILE,),
        in_specs=[
            _rows_spec(MIX_TILE, D_MODEL), _mod_spec(), _gain_spec(0, 1),
            _resident_spec((D_MODEL, IN_WIDTH)),
            _resident_spec((1, ATTN_WIDTH)), _resident_spec((1, KV_WIDTH)),
            _resident_spec((3, CONV_WIDTH), 0),
            _resident_spec((SEG_WIDTH, SEG_WIDTH)),
            _resident_spec((DEC_SEQ, ATTN_WIDTH)), _resident_spec((DEC_SEQ, ATTN_WIDTH)),
            pl.BlockSpec((None, PAST_LEN, KV_WIDTH), smp_batch),
            pl.BlockSpec((None, KV_WIDTH, PAST_LEN), smp_batch),
            _resident_spec((MIX_WIDTH, D_MODEL)),
        ],
        out_specs=[_rows_spec(MIX_TILE, D_MODEL), kvt_spec, kvt_spec],
        out_shape=[jax.ShapeDtypeStruct((N_ALL, D_MODEL), F32), kvt_shape, kvt_shape],
        scratch_shapes=[pltpu.VMEM((MIX_TILE, ATTN_WIDTH), BF16),
                        pltpu.VMEM((MIX_TILE, KV_WIDTH), BF16),
                        pltpu.VMEM((KV_WIDTH, MIX_TILE), BF16),
                        pltpu.VMEM((MIX_TILE, MIX_WIDTH), BF16)],
        compiler_params=_params(),
        name="mixer",
    )(x, mod, norm_g, w_in, qg, kg, conv_w, seg, cos, sin, cache_k, cache_vt, w_out)


def _window_sums(h, band_ref, gi, seq_len):
    hi = h.astype(BF16)
    lo = (h - hi.astype(F32)).astype(BF16)
    per_seq = seq_len // POOL_BLOCK
    sums = []
    for b in range(h.shape[0] // POOL_BLOCK):
        def part(kind, src):
            rows = slice(src * POOL_BLOCK, (src + 1) * POOL_BLOCK)
            band = band_ref[gi, kind]
            return (jnp.dot(band, hi[rows], preferred_element_type=F32)
                    + jnp.dot(band, lo[rows], preferred_element_type=F32))
        total = part(0, b)
        if (b + 1) % per_seq:
            total = total + part(1, b + 1)
        if b % per_seq:
            total = total + part(2, b - 1)
        sums.append(total)
    return jnp.concatenate(sums, axis=0)


def _pool_rows(x, pos, seq_len, r, mod_ref, gm_ref, pw_ref, ps_ref, band_ref):
    gate = _mod_vec(mod_ref, r, 5)
    h_all = _modulate(x, gm_ref[...], mod_ref, r, 3)
    outs = []
    for gi, w in enumerate(POOL_WINDOWS):
        lanes = slice(gi * POOL_GROUP, (gi + 1) * POOL_GROUP)
        h = h_all[:, lanes]
        left = w // 2
        right = w - 1 - left
        count = jnp.minimum(pos + right + 1, seq_len) - jnp.maximum(pos - left, 0)
        diff = (_window_sums(h, band_ref, gi, seq_len) / count.astype(F32) - h).astype(BF16)
        out = jnp.dot(diff, pw_ref[lanes, :], preferred_element_type=F32) * ps_ref[:, lanes]
        outs.append(x[:, lanes] + gate[:, lanes] * out)
    return jnp.concatenate(outs, axis=1)


def _pool_tile(ctx, r, x_ref, mod_ref, gm_ref, gf_ref, gn_ref, pw_ref, ps_ref, band_ref, w1_ref, w2_ref,
               y_ref, x2_ref):
    seq_len = SEQ if ctx else DEC_SEQ
    block = max(seq_len, FFN_TILE)
    pos = lax.broadcasted_iota(jnp.int32, (block, 1), 0) & (seq_len - 1)
    for b in range(MIX_TILE // block):
        rows = slice(b * block, (b + 1) * block)
        x2_ref[rows, :] = _pool_rows(x_ref[rows, :], pos, seq_len, r, mod_ref, gm_ref, pw_ref, ps_ref,
                                     band_ref)

    def ffn_step(s, carry):
        rows = pl.ds(pl.multiple_of(s * FFN_TILE, FFN_TILE), FFN_TILE)
        y = _ffn_rows(x2_ref[rows, :], gf_ref[...], mod_ref, r, 6, w1_ref, w2_ref)
        y_ref[rows, :] = _rms(y, gn_ref[...])
        return carry

    lax.fori_loop(0, MIX_TILE // FFN_TILE, ffn_step, 0)


def _pool_kernel(*refs):
    ins, (yp_ref, ys_ref, x2_ref) = refs[:-3], refs[-3:]
    i = pl.program_id(0)
    r = _mod_row(i, MIX_TILE)
    is_ctx = i < _n_ctx_tiles(MIX_TILE)

    @pl.when(is_ctx)
    def _():
        _pool_tile(True, r, *ins, yp_ref, x2_ref)

    @pl.when(jnp.logical_not(is_ctx))
    def _():
        _pool_tile(False, r, *ins, ys_ref, x2_ref)


def _pool_call(x, mod, norm_g, final_g, pool_w, pool_scale, bands, w1, w2):
    return pl.pallas_call(
        _pool_kernel,
        grid=(N_ALL // MIX_TILE,),
        in_specs=[
            _rows_spec(MIX_TILE, D_MODEL), _mod_spec(), _gain_spec(1, 1), _gain_spec(1, 2),
            _resident_spec((1, D_MODEL)),
            _resident_spec((D_MODEL, POOL_GROUP)),
            _resident_spec((1, D_MODEL), 0),
            _resident_spec(bands.shape),
            _resident_spec((D_MODEL, 2 * D_FF)), _resident_spec((D_FF, D_MODEL)),
        ],
        out_specs=[_ctx_rows_spec(MIX_TILE, D_MODEL), _smp_rows_spec(MIX_TILE, D_MODEL)],
        out_shape=[jax.ShapeDtypeStruct((N_CTX, D_MODEL), F32),
                   jax.ShapeDtypeStruct((N_SMP, D_MODEL), F32)],
        scratch_shapes=[pltpu.VMEM((MIX_TILE, D_MODEL), F32)],
        compiler_params=_params(),
        name="pool_ffn_norm",
    )(x, mod, norm_g, norm_g, final_g, pool_w, pool_scale, bands, w1, w2)


def _rope_tables():
    t = np.arange(DEC_SEQ)
    half = HEAD_DIM // 2
    inv = ROPE_THETA ** (-np.arange(0, half, 2, dtype=np.float64) / half)
    ang_row = (t // GRID_W)[:, None] * inv[None, :]
    ang_col = (t % GRID_W)[:, None] * inv[None, :]
    cos = np.concatenate([np.cos(ang_row), np.cos(ang_row), np.cos(ang_col), np.cos(ang_col)], axis=1)
    sin = np.concatenate([-np.sin(ang_row), np.sin(ang_row), -np.sin(ang_col), np.sin(ang_col)], axis=1)
    return (jnp.asarray(np.tile(cos, (1, N_HEADS)), F32), jnp.asarray(np.tile(sin, (1, N_HEADS)), F32))


def _head_segments():
    head = np.arange(SEG_WIDTH) // HEAD_DIM
    return jnp.asarray((head[:, None] == head[None, :]) / HEAD_DIM, BF16)


def _pool_bands():
    t = np.arange(POOL_BLOCK)[:, None]
    bands = []
    for w in POOL_WINDOWS:
        left = w // 2
        right = w - 1 - left
        src = [np.arange(POOL_BLOCK)[None, :] + shift for shift in (0, POOL_BLOCK, -POOL_BLOCK)]
        bands.append(np.stack([(s >= t - left) & (s <= t + right) for s in src]))
    return jnp.asarray(np.stack(bands), BF16)


def _cache_layout(t):
    return jnp.transpose(t.reshape(BATCH, 1, N_KV_HEADS, HEAD_DIM, SEQ), (0, 1, 4, 2, 3))


def kernel(x_prompt, x_sample, c, cache_k, cache_v, c_ctx, ada_w, ada_b, norm_g, ffn_w1, ffn_w2,
           mix_w_in, mix_w_out, q_norm, k_norm, conv_w, pool_w, pool_scale, final_g):
    cvec = jnp.concatenate(
        [c_ctx[None, :], c, jnp.zeros((MOD_ROWS - 1 - DEC_BATCH, D_MODEL), F32)], axis=0)
    gains = norm_g.reshape(norm_g.shape[0], 3, 1, D_MODEL)
    cos, sin = _rope_tables()

    mod0, _ = _mod_call(_ModJob(cvec, ada_w, ada_b, 0, MOD_STEPS), [])
    x, (w_in, w_out) = _ffn_call(
        [x_prompt.reshape(N_CTX, D_MODEL), x_sample.reshape(N_SMP, D_MODEL)],
        mod0, gains, ffn_w1, ffn_w2, 0, 0,
        [_Cast(mix_w_in, (0,), FFN_CAST_BLOCKS), _Cast(mix_w_out, (0,), FFN_CAST_BLOCKS)])
    x, kt, vt = _mixer_call(
        x, mod0, gains, w_in, jnp.tile(q_norm[0], N_HEADS)[None, :],
        jnp.tile(k_norm[0], N_KV_HEADS)[None, :], conv_w, _head_segments(), cos, sin,
        cache_k[:, 0].reshape(DEC_BATCH, PAST_LEN, KV_WIDTH),
        jnp.transpose(cache_v[:, 0], (0, 2, 3, 1)).reshape(DEC_BATCH, KV_WIDTH, PAST_LEN), w_out)
    x, (pw, mod1) = _ffn_call(
        [x], mod0, gains, ffn_w1, ffn_w2, 0, 1,
        [_Cast(pool_w.reshape(pool_w.shape[0], D_MODEL, POOL_GROUP), (0,), FFN_CAST_BLOCKS)],
        _ModJob(cvec, ada_w, ada_b, 1, MOD_SIDE_BLOCKS))
    x, (w1_last, w2_last) = _ffn_call(
        [x], mod1, gains, ffn_w1, ffn_w2, 1, 0,
        [_Cast(ffn_w1, (1, 1), FFN_CAST_BLOCKS), _Cast(ffn_w2, (1, 1), FFN_CAST_BLOCKS)])
    yp, ys = _pool_call(x, mod1, gains, final_g[None, :], pw,
                        pool_scale.reshape(pool_scale.shape[0], 1, D_MODEL), _pool_bands(), w1_last, w2_last)

    return (yp.reshape(BATCH, SEQ, D_MODEL), ys.reshape(DEC_BATCH, DEC_SEQ, D_MODEL),
            _cache_layout(kt), _cache_layout(vt))
```

```python
import functools

import numpy as np
import jax
import jax.numpy as jnp
from jax import lax
from jax.experimental import pallas as pl
from jax.experimental.pallas import tpu as pltpu

F32 = jnp.float32
BF16 = jnp.bfloat16

D_MODEL = 1024
BATCH = 32
SEQ = 256
DEC_BATCH = 2
DEC_SEQ = 1024
PAST_LEN = 512
GRID_W = 64
N_HEADS = 8
N_KV_HEADS = 2
HEAD_DIM = 64
HEADS_PER_KV = N_HEADS // N_KV_HEADS
ATTN_WIDTH = N_HEADS * HEAD_DIM
KV_WIDTH = N_KV_HEADS * HEAD_DIM
CONV_WIDTH = D_MODEL // 2
MIX_WIDTH = ATTN_WIDTH + CONV_WIDTH
IN_WIDTH = ATTN_WIDTH + 2 * KV_WIDTH + 3 * CONV_WIDTH
D_FF = 2816
POOL_WINDOWS = (2, 4, 8, 16)
POOL_GROUP = D_MODEL // len(POOL_WINDOWS)
N_MOD = 9
ROPE_THETA = 10000.0
EPS = 1e-6

N_CTX = BATCH * SEQ
N_SMP = DEC_BATCH * DEC_SEQ
N_ALL = N_CTX + N_SMP
MIX_TILE = 1024
FFN_TILE = 512
FF_CHUNK = 256
N_FF_CHUNKS = D_FF // FF_CHUNK
STAGE_SLOTS = 2
Q_ROWS = 256
SEG_WIDTH = 256
POOL_BLOCK = 256
SEQ_PER_TILE = MIX_TILE // SEQ
MOD_ROWS = 8
MOD_WIDTH = N_MOD * D_MODEL
MOD_STEPS = 4
MOD_SIDE_BLOCKS = 18
FFN_CAST_BLOCKS = 16
VMEM_LIMIT = 56 * 1024 * 1024


def _n_ctx_tiles(tile):
    return N_CTX // tile


def _mod_row(i, tile):
    first = _n_ctx_tiles(tile)
    return jnp.where(i < first, 0, 1 + (i - first) * tile // DEC_SEQ)


def _rms(x, g):
    return x * lax.rsqrt(jnp.mean(x * x, axis=-1, keepdims=True) + EPS) * g


def _mod_vec(mod_ref, r, j):
    return mod_ref[pl.ds(r, 1), j * D_MODEL:(j + 1) * D_MODEL]


def _modulate(x, g, mod_ref, r, j):
    gain = g * (1 + _mod_vec(mod_ref, r, j + 1))
    return x * lax.rsqrt(jnp.mean(x * x, axis=-1, keepdims=True) + EPS) * gain + _mod_vec(mod_ref, r, j)


def _ffn_rows(x, g, mod_ref, r, j, w1_ref, w2_ref, before_chunk=None):
    h = _modulate(x, g, mod_ref, r, j).astype(BF16)
    acc = None
    for c in range(N_FF_CHUNKS):
        lo = c * FF_CHUNK
        if before_chunk is not None:
            before_chunk(c)
        gate = jnp.dot(h, w1_ref[:, lo:lo + FF_CHUNK], preferred_element_type=F32)
        up = jnp.dot(h, w1_ref[:, D_FF + lo:D_FF + lo + FF_CHUNK], preferred_element_type=F32)
        act = (gate / (1 + jnp.exp(-gate)) * up).astype(BF16)
        y = jnp.dot(act, w2_ref[lo:lo + FF_CHUNK, :], preferred_element_type=F32)
        acc = y if acc is None else acc + y
    return x + (0.5 * _mod_vec(mod_ref, r, j + 2)) * acc


class _Cast:
    def __init__(self, src, lead, n_blocks):
        self.src, self.lead, self.n_blocks = src, tuple(lead), n_blocks
        self.rows, self.cols = src.shape[len(lead):]
        self.block_rows = self.rows // n_blocks

    def in_spec(self, step):
        lead, last = self.lead, self.n_blocks - 1
        return pl.BlockSpec((None,) * len(lead) + (self.block_rows, self.cols),
                            lambda *g: lead + (jnp.minimum(step(*g), last), 0))

    def out_spec(self, step):
        last = self.n_blocks - 1
        return pl.BlockSpec((self.block_rows, self.cols), lambda *g: (jnp.minimum(step(*g), last), 0))

    def out_shape(self):
        return jax.ShapeDtypeStruct((self.rows, self.cols), BF16)


def _cast_blocks(step, n_blocks, srcs, dsts):
    @pl.when(step < n_blocks)
    def _():
        for s, d in zip(srcs, dsts):
            d[...] = s[...].astype(BF16)


class _WeightStream:
    def __init__(self, w1_hbm, w2_hbm, layer, which, w1_s, w2_s, gate_st, up_st, down_st, sems):
        self.w1_hbm, self.w2_hbm, self.lead = w1_hbm, w2_hbm, (layer, which)
        self.w1_s, self.w2_s = w1_s, w2_s
        self.stages, self.sems = (gate_st, up_st, down_st), sems

    def _copies(self, c):
        slot, lo = c % STAGE_SLOTS, c * FF_CHUNK
        l, w = self.lead
        srcs = (self.w1_hbm.at[l, w, :, pl.ds(lo, FF_CHUNK)],
                self.w1_hbm.at[l, w, :, pl.ds(D_FF + lo, FF_CHUNK)],
                self.w2_hbm.at[l, w, pl.ds(lo, FF_CHUNK), :])
        return [pltpu.make_async_copy(src, st.at[slot], self.sems.at[k, slot])
                for k, (src, st) in enumerate(zip(srcs, self.stages))]

    def start(self, c):
        for cp in self._copies(c):
            cp.start()

    def prime(self):
        for c in range(STAGE_SLOTS):
            self.start(c)

    def land(self, c):
        for cp in self._copies(c):
            cp.wait()
        slot, lo = c % STAGE_SLOTS, c * FF_CHUNK
        gate_st, up_st, down_st = self.stages
        self.w1_s[:, lo:lo + FF_CHUNK] = gate_st[slot].astype(BF16)
        self.w1_s[:, D_FF + lo:D_FF + lo + FF_CHUNK] = up_st[slot].astype(BF16)
        self.w2_s[lo:lo + FF_CHUNK, :] = down_st[slot].astype(BF16)
        if c + STAGE_SLOTS < N_FF_CHUNKS:
            self.start(c + STAGE_SLOTS)


def _weight_stream_specs():
    return [pl.BlockSpec(memory_space=pl.ANY), pl.BlockSpec(memory_space=pl.ANY)]


N_STREAM_SCRATCH = 6


def _weight_stream_scratch():
    return [pltpu.VMEM((D_MODEL, 2 * D_FF), BF16), pltpu.VMEM((D_FF, D_MODEL), BF16),
            pltpu.VMEM((STAGE_SLOTS, D_MODEL, FF_CHUNK), F32),
            pltpu.VMEM((STAGE_SLOTS, D_MODEL, FF_CHUNK), F32),
            pltpu.VMEM((STAGE_SLOTS, FF_CHUNK, D_MODEL), F32),
            pltpu.SemaphoreType.DMA((3, STAGE_SLOTS))]


def _mod_block(c_ref, w_ref, b_ref):
    c = c_ref[...]
    s = (c / (1 + jnp.exp(-c))).astype(BF16)
    return jnp.dot(s, w_ref[...].astype(BF16), preferred_element_type=F32) + b_ref[...]


class _ModJob:
    def __init__(self, cvec, ada_w, ada_b, layer, n_blocks):
        self.args = (cvec, ada_w, ada_b.reshape(ada_b.shape[0], 1, MOD_WIDTH))
        self.layer, self.n_blocks, self.cols = layer, n_blocks, MOD_WIDTH // n_blocks

    def in_specs(self):
        layer, last, cols = self.layer, self.n_blocks - 1, self.cols
        block = lambda i: (layer, 0, jnp.minimum(i, last))
        return [pl.BlockSpec((MOD_ROWS, D_MODEL), lambda i: (0, 0)),
                pl.BlockSpec((None, D_MODEL, cols), block),
                pl.BlockSpec((None, 1, cols), block)]

    def out_spec(self):
        last = self.n_blocks - 1
        return pl.BlockSpec((MOD_ROWS, self.cols), lambda i: (0, jnp.minimum(i, last)))

    def out_shape(self):
        return jax.ShapeDtypeStruct((MOD_ROWS, MOD_WIDTH), F32)


def _mod_kernel(c_ref, w_ref, b_ref, *refs, n_cast):
    srcs, o_ref, dsts = refs[:n_cast], refs[n_cast], refs[n_cast + 1:]
    o_ref[...] = _mod_block(c_ref, w_ref, b_ref)
    _cast_blocks(pl.program_id(0), MOD_STEPS, srcs, dsts)


def _mod_call(job, casts):
    step = lambda i: i
    out = pl.pallas_call(
        functools.partial(_mod_kernel, n_cast=len(casts)),
        grid=(job.n_blocks,),
        in_specs=job.in_specs() + [c.in_spec(step) for c in casts],
        out_specs=[job.out_spec()] + [c.out_spec(step) for c in casts],
        out_shape=[job.out_shape()] + [c.out_shape() for c in casts],
        compiler_params=_params(),
        name="adaln_mod",
    )(*job.args, *[c.src for c in casts])
    return out[0], out[1:]


def _rows_spec(tile, width):
    return pl.BlockSpec((tile, width), lambda i: (i, 0))


def _ctx_rows_spec(tile, width):
    last = _n_ctx_tiles(tile) - 1
    return pl.BlockSpec((tile, width), lambda i: (jnp.minimum(i, last), 0))


def _smp_rows_spec(tile, width):
    first = _n_ctx_tiles(tile)
    return pl.BlockSpec((tile, width), lambda i: (jnp.maximum(i - first, 0), 0))


def _mod_spec():
    return pl.BlockSpec((MOD_ROWS, MOD_WIDTH), lambda i: (0, 0))


def _gain_spec(layer, j):
    return pl.BlockSpec((None, None, 1, D_MODEL), lambda i: (layer, j, 0, 0))


def _resident_spec(shape, *lead):
    index = tuple(lead) + (0,) * len(shape)
    return pl.BlockSpec((None,) * len(lead) + tuple(shape), lambda i: index,
                        pipeline_mode=pl.Buffered(1))


def _params():
    return pltpu.CompilerParams(dimension_semantics=("arbitrary",), vmem_limit_bytes=VMEM_LIMIT)


def _ffn_kernel(*refs, j, n_in, n_cast, mod_blocks, layer, which):
    n_side = n_cast + (3 if mod_blocks else 0)
    x_refs = refs[:n_in]
    mod_ref, g_ref, w1_hbm, w2_hbm = refs[n_in:n_in + 4]
    side_in = refs[n_in + 4:n_in + 4 + n_side]
    o_ref = refs[n_in + 4 + n_side]
    side_out = refs[n_in + 5 + n_side:len(refs) - N_STREAM_SCRATCH]
    stream = _WeightStream(w1_hbm, w2_hbm, layer, which, *refs[len(refs) - N_STREAM_SCRATCH:])
    i = pl.program_id(0)

    def run(before_chunk):
        if n_in == 2:
            x = jnp.where(i < _n_ctx_tiles(FFN_TILE), x_refs[0][...], x_refs[1][...])
        else:
            x = x_refs[0][...]
        o_ref[...] = _ffn_rows(x, g_ref[...], mod_ref, _mod_row(i, FFN_TILE), j,
                               stream.w1_s, stream.w2_s, before_chunk)

    @pl.when(i == 0)
    def _():
        stream.prime()
        run(stream.land)

    @pl.when(i > 0)
    def _():
        run(None)

    _cast_blocks(i, FFN_CAST_BLOCKS, side_in[:n_cast], side_out[:n_cast])
    if mod_blocks:
        @pl.when(i < mod_blocks)
        def _():
            side_out[n_cast][...] = _mod_block(*side_in[n_cast:])


def _ffn_call(xs, mod, norm_g, w1, w2, layer, which, casts=(), mod_job=None):
    step = lambda i: i
    if len(xs) == 2:
        x_specs = [_ctx_rows_spec(FFN_TILE, D_MODEL), _smp_rows_spec(FFN_TILE, D_MODEL)]
    else:
        x_specs = [_rows_spec(FFN_TILE, D_MODEL)]
    jobs = [mod_job] if mod_job else []
    out = pl.pallas_call(
        functools.partial(_ffn_kernel, j=6 * which, n_in=len(xs), n_cast=len(casts),
                          mod_blocks=mod_job.n_blocks if mod_job else 0, layer=layer, which=which),
        grid=(N_ALL // FFN_TILE,),
        in_specs=x_specs + [_mod_spec(), _gain_spec(layer, 2 * which)] + _weight_stream_specs()
        + [c.in_spec(step) for c in casts] + [s for m in jobs for s in m.in_specs()],
        out_specs=[_rows_spec(FFN_TILE, D_MODEL)] + [c.out_spec(step) for c in casts]
        + [m.out_spec() for m in jobs],
        out_shape=[jax.ShapeDtypeStruct((N_ALL, D_MODEL), F32)] + [c.out_shape() for c in casts]
        + [m.out_shape() for m in jobs],
        scratch_shapes=_weight_stream_scratch(),
        compiler_params=_params(),
        name="ffn",
    )(*xs, mod, norm_g, w1, w2, *[c.src for c in casts], *[a for m in jobs for a in m.args])
    return out[0], out[1:]


def _head_mean_sq(x, seg):
    sq = x * x
    hi = sq.astype(BF16)
    lo = (sq - hi.astype(F32)).astype(BF16)
    width = x.shape[-1]
    parts = []
    for c in range(0, width, SEG_WIDTH):
        n = min(SEG_WIDTH, width - c)
        parts.append(jnp.dot(hi[:, c:c + n], seg[:n, :n], preferred_element_type=F32)
                     + jnp.dot(lo[:, c:c + n], seg[:n, :n], preferred_element_type=F32))
    return parts[0] if len(parts) == 1 else jnp.concatenate(parts, axis=1)


def _rope(x, cos, sin_signed):
    width = x.shape[-1]
    lane = lax.broadcasted_iota(jnp.int32, (1, width), 1)
    partner = jnp.where((lane & 31) < 16,
                        pltpu.roll(x, width - 16, 1), pltpu.roll(x, 16, 1))
    return x * cos + partner * sin_signed


def _stack_heads(q_ref, rows, kvh):
    first = kvh * HEADS_PER_KV
    return jnp.concatenate(
        [q_ref[rows, (first + g) * HEAD_DIM:(first + g + 1) * HEAD_DIM] for g in range(HEADS_PER_KV)],
        axis=0)


ONES_ROWS = 16


def _attend_group(q, k, v1t):
    st = lax.dot_general(k, q, (((1,), (1,)), ((), ())), preferred_element_type=F32)
    e = jnp.exp(st - jnp.max(st, axis=0, keepdims=True)).astype(BF16)
    ot = jnp.dot(v1t, e, preferred_element_type=F32)
    return ot[:HEAD_DIM] / ot[HEAD_DIM:HEAD_DIM + 1]


def _attend_rows(q_s, rows, n, kv, mix_s):
    heads = []
    for kvh in range(N_KV_HEADS):
        k, v1t = kv(kvh)
        ot = _attend_group(_stack_heads(q_s, rows, kvh), k, v1t)
        heads += [ot[:, g * n:(g + 1) * n] for g in range(HEADS_PER_KV)]
    mix_s[rows, :ATTN_WIDTH] = jnp.concatenate(heads, axis=0).T.astype(BF16)


def _with_ones(vt):
    return jnp.concatenate([vt, jnp.ones((ONES_ROWS, vt.shape[1]), BF16)], axis=0)


def _mixer_tile(ctx, r, x_ref, mod_ref, g_ref, w_ref, qg_ref, kg_ref, cw_ref, seg_ref, cos_ref, sin_ref,
                ck_ref, cvt_ref, wo_ref, o_ref, kt_ref, vt_ref, q_s, k_s, vt_s, mix_s):
    seq_len = SEQ if ctx else DEC_SEQ
    x = x_ref[...]
    h = _modulate(x, g_ref[...], mod_ref, r, 3).astype(BF16)

    def proj(lo, width):
        return jnp.dot(h, w_ref[:, lo:lo + width], preferred_element_type=F32)

    q = proj(0, ATTN_WIDTH)
    kv = proj(ATTN_WIDTH, 2 * KV_WIDTH)
    k, v = kv[:, :KV_WIDTH], kv[:, KV_WIDTH:]
    q = q * lax.rsqrt(_head_mean_sq(q, seg_ref[...]) + EPS) * (qg_ref[...] * HEAD_DIM ** -0.5)
    k = k * lax.rsqrt(_head_mean_sq(k, seg_ref[...]) + EPS) * kg_ref[...]
    if ctx:
        q_s[...] = q.astype(BF16)
        k_s[...] = k.astype(BF16)
        for b in range(SEQ_PER_TILE):
            vt = v[b * SEQ:(b + 1) * SEQ, :].T
            kt_ref[b] = k[b * SEQ:(b + 1) * SEQ, :].T
            vt_ref[b] = vt
            vt_s[:, b * SEQ:(b + 1) * SEQ] = vt.astype(BF16)
    else:
        q_s[...] = _rope(q, cos_ref[...], sin_ref[...]).astype(BF16)
        k_s[...] = _rope(k, cos_ref[:, :KV_WIDTH], sin_ref[:, :KV_WIDTH]).astype(BF16)
        vt_s[...] = v.T.astype(BF16)

    base = ATTN_WIDTH + 2 * KV_WIDTH
    z = proj(base + CONV_WIDTH, CONV_WIDTH) * proj(base + 2 * CONV_WIDTH, CONV_WIDTH)
    pos = lax.broadcasted_iota(jnp.int32, (MIX_TILE, 1), 0) & (seq_len - 1)
    z_prev = jnp.where(pos == 0, 0.0, pltpu.roll(z, 1, 0))
    z_next = jnp.where(pos == seq_len - 1, 0.0, pltpu.roll(z, MIX_TILE - 1, 0))
    conv = z_prev * cw_ref[0:1, :] + z * cw_ref[1:2, :] + z_next * cw_ref[2:3, :]
    mix_s[:, ATTN_WIDTH:] = (proj(base, CONV_WIDTH) * conv).astype(BF16)

    def head(kvh):
        return slice(kvh * HEAD_DIM, (kvh + 1) * HEAD_DIM)

    if ctx:
        for b in range(SEQ_PER_TILE):
            rows = slice(b * SEQ, (b + 1) * SEQ)
            _attend_rows(q_s, rows, SEQ,
                         lambda kvh: (k_s[rows, head(kvh)], _with_ones(vt_s[head(kvh), rows])), mix_s)
    else:
        kv = [(jnp.concatenate([ck_ref[:, head(kvh)].astype(BF16), k_s[:, head(kvh)]], axis=0),
               _with_ones(jnp.concatenate([cvt_ref[head(kvh), :].astype(BF16), vt_s[head(kvh), :]],
                                          axis=1)))
              for kvh in range(N_KV_HEADS)]
        for t in range(MIX_TILE // Q_ROWS):
            _attend_rows(q_s, slice(t * Q_ROWS, (t + 1) * Q_ROWS), Q_ROWS, lambda kvh: kv[kvh], mix_s)

    out = (jnp.dot(mix_s[:, :ATTN_WIDTH], wo_ref[:ATTN_WIDTH, :], preferred_element_type=F32)
           + jnp.dot(mix_s[:, ATTN_WIDTH:], wo_ref[ATTN_WIDTH:, :], preferred_element_type=F32))
    o_ref[...] = x + _mod_vec(mod_ref, r, 5) * out


def _mixer_kernel(*refs):
    i = pl.program_id(0)
    r = _mod_row(i, MIX_TILE)
    is_ctx = i < _n_ctx_tiles(MIX_TILE)

    @pl.when(is_ctx)
    def _():
        _mixer_tile(True, r, *refs)

    @pl.when(jnp.logical_not(is_ctx))
    def _():
        _mixer_tile(False, r, *refs)


def _mixer_call(x, mod, norm_g, w_in, qg, kg, conv_w, seg, cos, sin, cache_k, cache_vt, w_out):
    first_smp = _n_ctx_tiles(MIX_TILE)
    smp_batch = lambda i: (jnp.maximum(i - first_smp, 0), 0, 0)
    kvt_spec = pl.BlockSpec((SEQ_PER_TILE, KV_WIDTH, SEQ), lambda i: (jnp.minimum(i, first_smp - 1), 0, 0))
    kvt_shape = jax.ShapeDtypeStruct((BATCH, KV_WIDTH, SEQ), F32)
    return pl.pallas_call(
        _mixer_kernel,
        grid=(N_ALL // MIX_TILE,),
        in_specs=[
            _rows_spec(MIX_TILE, D_MODEL), _mod_spec(), _gain_spec(0, 1),
            _resident_spec((D_MODEL, IN_WIDTH)),
            _resident_spec((1, ATTN_WIDTH)), _resident_spec((1, KV_WIDTH)),
            _resident_spec((3, CONV_WIDTH), 0),
            _resident_spec((SEG_WIDTH, SEG_WIDTH)),
            _resident_spec((DEC_SEQ, ATTN_WIDTH)), _resident_spec((DEC_SEQ, ATTN_WIDTH)),
            pl.BlockSpec((None, PAST_LEN, KV_WIDTH), smp_batch),
            pl.BlockSpec((None, KV_WIDTH, PAST_LEN), smp_batch),
            _resident_spec((MIX_WIDTH, D_MODEL)),
        ],
        out_specs=[_rows_spec(MIX_TILE, D_MODEL), kvt_spec, kvt_spec],
        out_shape=[jax.ShapeDtypeStruct((N_ALL, D_MODEL), F32), kvt_shape, kvt_shape],
        scratch_shapes=[pltpu.VMEM((MIX_TILE, ATTN_WIDTH), BF16),
                        pltpu.VMEM((MIX_TILE, KV_WIDTH), BF16),
                        pltpu.VMEM((KV_WIDTH, MIX_TILE), BF16),
                        pltpu.VMEM((MIX_TILE, MIX_WIDTH), BF16)],
        compiler_params=_params(),
        name="mixer",
    )(x, mod, norm_g, w_in, qg, kg, conv_w, seg, cos, sin, cache_k, cache_vt, w_out)


def _window_sums(h, band_ref, gi, seq_len):
    hi = h.astype(BF16)
    lo = (h - hi.astype(F32)).astype(BF16)
    per_seq = seq_len // POOL_BLOCK
    sums = []
    for b in range(h.shape[0] // POOL_BLOCK):
        def part(kind, src):
            rows = slice(src * POOL_BLOCK, (src + 1) * POOL_BLOCK)
            band = band_ref[gi, kind]
            return (jnp.dot(band, hi[rows], preferred_element_type=F32)
                    + jnp.dot(band, lo[rows], preferred_element_type=F32))
        total = part(0, b)
        if (b + 1) % per_seq:
            total = total + part(1, b + 1)
        if b % per_seq:
            total = total + part(2, b - 1)
        sums.append(total)
    return jnp.concatenate(sums, axis=0)


def _pool_rows(x, pos, seq_len, r, mod_ref, gm_ref, pw_ref, ps_ref, band_ref):
    gate = _mod_vec(mod_ref, r, 5)
    h_all = _modulate(x, gm_ref[...], mod_ref, r, 3)
    outs = []
    for gi, w in enumerate(POOL_WINDOWS):
        lanes = slice(gi * POOL_GROUP, (gi + 1) * POOL_GROUP)
        h = h_all[:, lanes]
        left = w // 2
        right = w - 1 - left
        count = jnp.minimum(pos + right + 1, seq_len) - jnp.maximum(pos - left, 0)
        diff = (_window_sums(h, band_ref, gi, seq_len) / count.astype(F32) - h).astype(BF16)
        out = jnp.dot(diff, pw_ref[lanes, :], preferred_element_type=F32)
        outs.append(x[:, lanes] + (gate[:, lanes] * ps_ref[:, lanes]) * out)
    return jnp.concatenate(outs, axis=1)


def _pool_tile(ctx, r, x_ref, mod_ref, gm_ref, gf_ref, gn_ref, pw_ref, ps_ref, band_ref, w1_ref, w2_ref,
               y_ref, x2_ref):
    seq_len = SEQ if ctx else DEC_SEQ
    block = max(seq_len, FFN_TILE)
    pos = lax.broadcasted_iota(jnp.int32, (block, 1), 0) & (seq_len - 1)
    for b in range(MIX_TILE // block):
        rows = slice(b * block, (b + 1) * block)
        x2_ref[rows, :] = _pool_rows(x_ref[rows, :], pos, seq_len, r, mod_ref, gm_ref, pw_ref, ps_ref,
                                     band_ref)

    def ffn_step(s, carry):
        rows = pl.ds(pl.multiple_of(s * FFN_TILE, FFN_TILE), FFN_TILE)
        y = _ffn_rows(x2_ref[rows, :], gf_ref[...], mod_ref, r, 6, w1_ref, w2_ref)
        y_ref[rows, :] = _rms(y, gn_ref[...])
        return carry

    lax.fori_loop(0, MIX_TILE // FFN_TILE, ffn_step, 0)


def _pool_kernel(*refs):
    ins, (yp_ref, ys_ref, x2_ref) = refs[:-3], refs[-3:]
    i = pl.program_id(0)
    r = _mod_row(i, MIX_TILE)
    is_ctx = i < _n_ctx_tiles(MIX_TILE)

    @pl.when(is_ctx)
    def _():
        _pool_tile(True, r, *ins, yp_ref, x2_ref)

    @pl.when(jnp.logical_not(is_ctx))
    def _():
        _pool_tile(False, r, *ins, ys_ref, x2_ref)


def _pool_call(x, mod, norm_g, final_g, pool_w, pool_scale, bands, w1, w2):
    return pl.pallas_call(
        _pool_kernel,
        grid=(N_ALL // MIX_TILE,),
        in_specs=[
            _rows_spec(MIX_TILE, D_MODEL), _mod_spec(), _gain_spec(1, 1), _gain_spec(1, 2),
            _resident_spec((1, D_MODEL)),
            _resident_spec((D_MODEL, POOL_GROUP)),
            _resident_spec((1, D_MODEL), 0),
            _resident_spec(bands.shape),
            _resident_spec((D_MODEL, 2 * D_FF)), _resident_spec((D_FF, D_MODEL)),
        ],
        out_specs=[_ctx_rows_spec(MIX_TILE, D_MODEL), _smp_rows_spec(MIX_TILE, D_MODEL)],
        out_shape=[jax.ShapeDtypeStruct((N_CTX, D_MODEL), F32),
                   jax.ShapeDtypeStruct((N_SMP, D_MODEL), F32)],
        scratch_shapes=[pltpu.VMEM((MIX_TILE, D_MODEL), F32)],
        compiler_params=_params(),
        name="pool_ffn_norm",
    )(x, mod, norm_g, norm_g, final_g, pool_w, pool_scale, bands, w1, w2)


def _rope_tables():
    t = np.arange(DEC_SEQ)
    half = HEAD_DIM // 2
    inv = ROPE_THETA ** (-np.arange(0, half, 2, dtype=np.float64) / half)
    ang_row = (t // GRID_W)[:, None] * inv[None, :]
    ang_col = (t % GRID_W)[:, None] * inv[None, :]
    cos = np.concatenate([np.cos(ang_row), np.cos(ang_row), np.cos(ang_col), np.cos(ang_col)], axis=1)
    sin = np.concatenate([-np.sin(ang_row), np.sin(ang_row), -np.sin(ang_col), np.sin(ang_col)], axis=1)
    return (jnp.asarray(np.tile(cos, (1, N_HEADS)), F32), jnp.asarray(np.tile(sin, (1, N_HEADS)), F32))


def _head_segments():
    head = np.arange(SEG_WIDTH) // HEAD_DIM
    return jnp.asarray((head[:, None] == head[None, :]) / HEAD_DIM, BF16)


def _pool_bands():
    t = np.arange(POOL_BLOCK)[:, None]
    bands = []
    for w in POOL_WINDOWS:
        left = w // 2
        right = w - 1 - left
        src = [np.arange(POOL_BLOCK)[None, :] + shift for shift in (0, POOL_BLOCK, -POOL_BLOCK)]
        bands.append(np.stack([(s >= t - left) & (s <= t + right) for s in src]))
    return jnp.asarray(np.stack(bands), BF16)


def _cache_layout(t):
    return jnp.transpose(t.reshape(BATCH, 1, N_KV_HEADS, HEAD_DIM, SEQ), (0, 1, 4, 2, 3))


def kernel(x_prompt, x_sample, c, cache_k, cache_v, c_ctx, ada_w, ada_b, norm_g, ffn_w1, ffn_w2,
           mix_w_in, mix_w_out, q_norm, k_norm, conv_w, pool_w, pool_scale, final_g):
    cvec = jnp.concatenate(
        [c_ctx[None, :], c, jnp.zeros((MOD_ROWS - 1 - DEC_BATCH, D_MODEL), F32)], axis=0)
    gains = norm_g.reshape(norm_g.shape[0], 3, 1, D_MODEL)
    cos, sin = _rope_tables()

    mod0, _ = _mod_call(_ModJob(cvec, ada_w, ada_b, 0, MOD_STEPS), [])
    x, (w_in, w_out) = _ffn_call(
        [x_prompt.reshape(N_CTX, D_MODEL), x_sample.reshape(N_SMP, D_MODEL)],
        mod0, gains, ffn_w1, ffn_w2, 0, 0,
        [_Cast(mix_w_in, (0,), FFN_CAST_BLOCKS), _Cast(mix_w_out, (0,), FFN_CAST_BLOCKS)])
    x, kt, vt = _mixer_call(
        x, mod0, gains, w_in, jnp.tile(q_norm[0], N_HEADS)[None, :],
        jnp.tile(k_norm[0], N_KV_HEADS)[None, :], conv_w, _head_segments(), cos, sin,
        cache_k[:, 0].reshape(DEC_BATCH, PAST_LEN, KV_WIDTH),
        jnp.transpose(cache_v[:, 0], (0, 2, 3, 1)).reshape(DEC_BATCH, KV_WIDTH, PAST_LEN), w_out)
    x, (pw, mod1) = _ffn_call(
        [x], mod0, gains, ffn_w1, ffn_w2, 0, 1,
        [_Cast(pool_w.reshape(pool_w.shape[0], D_MODEL, POOL_GROUP), (0,), FFN_CAST_BLOCKS)],
        _ModJob(cvec, ada_w, ada_b, 1, MOD_SIDE_BLOCKS))
    x, (w1_last, w2_last) = _ffn_call(
        [x], mod1, gains, ffn_w1, ffn_w2, 1, 0,
        [_Cast(ffn_w1, (1, 1), FFN_CAST_BLOCKS), _Cast(ffn_w2, (1, 1), FFN_CAST_BLOCKS)])
    yp, ys = _pool_call(x, mod1, gains, final_g[None, :], pw,
                        pool_scale.reshape(pool_scale.shape[0], 1, D_MODEL), _pool_bands(), w1_last, w2_last)

    return (yp.reshape(BATCH, SEQ, D_MODEL), ys.reshape(DEC_BATCH, DEC_SEQ, D_MODEL),
            _cache_layout(kt), _cache_layout(vt))
```

```python
import functools

import numpy as np
import jax
import jax.numpy as jnp
from jax import lax
from jax.experimental import pallas as pl
from jax.experimental.pallas import tpu as pltpu

F32 = jnp.float32
BF16 = jnp.bfloat16

D_MODEL = 1024
BATCH = 32
SEQ = 256
DEC_BATCH = 2
DEC_SEQ = 1024
PAST_LEN = 512
GRID_W = 64
N_HEADS = 8
N_KV_HEADS = 2
HEAD_DIM = 64
HEADS_PER_KV = N_HEADS // N_KV_HEADS
ATTN_WIDTH = N_HEADS * HEAD_DIM
KV_WIDTH = N_KV_HEADS * HEAD_DIM
CONV_WIDTH = D_MODEL // 2
MIX_WIDTH = ATTN_WIDTH + CONV_WIDTH
IN_WIDTH = ATTN_WIDTH + 2 * KV_WIDTH + 3 * CONV_WIDTH
D_FF = 2816
POOL_WINDOWS = (2, 4, 8, 16)
POOL_GROUP = D_MODEL // len(POOL_WINDOWS)
N_MOD = 9
ROPE_THETA = 10000.0
EPS = 1e-6

N_CTX = BATCH * SEQ
N_SMP = DEC_BATCH * DEC_SEQ
N_ALL = N_CTX + N_SMP
MIX_TILE = 1024
FFN_TILE = 512
FF_CHUNK = 256
N_FF_CHUNKS = D_FF // FF_CHUNK
STAGE_SLOTS = 2
Q_ROWS = 256
SEG_WIDTH = 256
POOL_BLOCK = 256
SEQ_PER_TILE = MIX_TILE // SEQ
MOD_ROWS = 8
MOD_WIDTH = N_MOD * D_MODEL
MOD_STEPS = 2
MOD_SIDE_BLOCKS = 18
FFN_CAST_BLOCKS = 16
VMEM_LIMIT = 56 * 1024 * 1024


def _n_ctx_tiles(tile):
    return N_CTX // tile


def _mod_row(i, tile):
    first = _n_ctx_tiles(tile)
    return jnp.where(i < first, 0, 1 + (i - first) * tile // DEC_SEQ)


def _rms(x, g):
    return x * lax.rsqrt(jnp.mean(x * x, axis=-1, keepdims=True) + EPS) * g


def _mod_vec(mod_ref, r, j):
    return mod_ref[pl.ds(r, 1), j * D_MODEL:(j + 1) * D_MODEL]


def _modulate(x, g, mod_ref, r, j):
    gain = g * (1 + _mod_vec(mod_ref, r, j + 1))
    return x * lax.rsqrt(jnp.mean(x * x, axis=-1, keepdims=True) + EPS) * gain + _mod_vec(mod_ref, r, j)


def _ffn_rows(x, g, mod_ref, r, j, w1_ref, w2_ref, before_chunk=None):
    h = _modulate(x, g, mod_ref, r, j).astype(BF16)
    acc = None
    for c in range(N_FF_CHUNKS):
        lo = c * FF_CHUNK
        if before_chunk is not None:
            before_chunk(c)
        gate = jnp.dot(h, w1_ref[:, lo:lo + FF_CHUNK], preferred_element_type=F32)
        up = jnp.dot(h, w1_ref[:, D_FF + lo:D_FF + lo + FF_CHUNK], preferred_element_type=F32)
        act = (gate / (1 + jnp.exp(-gate)) * up).astype(BF16)
        y = jnp.dot(act, w2_ref[lo:lo + FF_CHUNK, :], preferred_element_type=F32)
        acc = y if acc is None else acc + y
    return x + (0.5 * _mod_vec(mod_ref, r, j + 2)) * acc


class _Cast:
    def __init__(self, src, lead, n_blocks):
        self.src, self.lead, self.n_blocks = src, tuple(lead), n_blocks
        self.rows, self.cols = src.shape[len(lead):]
        self.block_rows = self.rows // n_blocks

    def in_spec(self, step):
        lead, last = self.lead, self.n_blocks - 1
        return pl.BlockSpec((None,) * len(lead) + (self.block_rows, self.cols),
                            lambda *g: lead + (jnp.minimum(step(*g), last), 0))

    def out_spec(self, step):
        last = self.n_blocks - 1
        return pl.BlockSpec((self.block_rows, self.cols), lambda *g: (jnp.minimum(step(*g), last), 0))

    def out_shape(self):
        return jax.ShapeDtypeStruct((self.rows, self.cols), BF16)


def _cast_blocks(step, n_blocks, srcs, dsts):
    @pl.when(step < n_blocks)
    def _():
        for s, d in zip(srcs, dsts):
            d[...] = s[...].astype(BF16)


class _WeightStream:
    def __init__(self, w1_hbm, w2_hbm, layer, which, w1_s, w2_s, gate_st, up_st, down_st, sems):
        self.w1_hbm, self.w2_hbm, self.lead = w1_hbm, w2_hbm, (layer, which)
        self.w1_s, self.w2_s = w1_s, w2_s
        self.stages, self.sems = (gate_st, up_st, down_st), sems

    def _copies(self, c):
        slot, lo = c % STAGE_SLOTS, c * FF_CHUNK
        l, w = self.lead
        srcs = (self.w1_hbm.at[l, w, :, pl.ds(lo, FF_CHUNK)],
                self.w1_hbm.at[l, w, :, pl.ds(D_FF + lo, FF_CHUNK)],
                self.w2_hbm.at[l, w, pl.ds(lo, FF_CHUNK), :])
        return [pltpu.make_async_copy(src, st.at[slot], self.sems.at[k, slot])
                for k, (src, st) in enumerate(zip(srcs, self.stages))]

    def start(self, c):
        for cp in self._copies(c):
            cp.start()

    def prime(self):
        for c in range(STAGE_SLOTS):
            self.start(c)

    def land(self, c):
        for cp in self._copies(c):
            cp.wait()
        slot, lo = c % STAGE_SLOTS, c * FF_CHUNK
        gate_st, up_st, down_st = self.stages
        self.w1_s[:, lo:lo + FF_CHUNK] = gate_st[slot].astype(BF16)
        self.w1_s[:, D_FF + lo:D_FF + lo + FF_CHUNK] = up_st[slot].astype(BF16)
        self.w2_s[lo:lo + FF_CHUNK, :] = down_st[slot].astype(BF16)
        if c + STAGE_SLOTS < N_FF_CHUNKS:
            self.start(c + STAGE_SLOTS)


def _weight_stream_specs():
    return [pl.BlockSpec(memory_space=pl.ANY), pl.BlockSpec(memory_space=pl.ANY)]


N_STREAM_SCRATCH = 6


def _weight_stream_scratch():
    return [pltpu.VMEM((D_MODEL, 2 * D_FF), BF16), pltpu.VMEM((D_FF, D_MODEL), BF16),
            pltpu.VMEM((STAGE_SLOTS, D_MODEL, FF_CHUNK), F32),
            pltpu.VMEM((STAGE_SLOTS, D_MODEL, FF_CHUNK), F32),
            pltpu.VMEM((STAGE_SLOTS, FF_CHUNK, D_MODEL), F32),
            pltpu.SemaphoreType.DMA((3, STAGE_SLOTS))]


def _mod_block(c_ref, w_ref, b_ref):
    c = c_ref[...]
    s = (c / (1 + jnp.exp(-c))).astype(BF16)
    return jnp.dot(s, w_ref[...].astype(BF16), preferred_element_type=F32) + b_ref[...]


class _ModJob:
    def __init__(self, cvec, ada_w, ada_b, layer, n_blocks, width=MOD_WIDTH):
        self.args = (cvec, ada_w, ada_b.reshape(ada_b.shape[0], 1, MOD_WIDTH))
        self.layer, self.n_blocks, self.width, self.cols = layer, n_blocks, width, width // n_blocks

    def in_specs(self):
        layer, last, cols = self.layer, self.n_blocks - 1, self.cols
        block = lambda i: (layer, 0, jnp.minimum(i, last))
        return [pl.BlockSpec((MOD_ROWS, D_MODEL), lambda i: (0, 0)),
                pl.BlockSpec((None, D_MODEL, cols), block),
                pl.BlockSpec((None, 1, cols), block)]

    def out_spec(self):
        last = self.n_blocks - 1
        return pl.BlockSpec((MOD_ROWS, self.cols), lambda i: (0, jnp.minimum(i, last)))

    def out_shape(self):
        return jax.ShapeDtypeStruct((MOD_ROWS, self.width), F32)


def _mod_kernel(c_ref, w_ref, b_ref, *refs, n_cast):
    srcs, o_ref, dsts = refs[:n_cast], refs[n_cast], refs[n_cast + 1:]
    o_ref[...] = _mod_block(c_ref, w_ref, b_ref)
    _cast_blocks(pl.program_id(0), pl.num_programs(0), srcs, dsts)


def _mod_call(job, casts):
    step = lambda i: i
    out = pl.pallas_call(
        functools.partial(_mod_kernel, n_cast=len(casts)),
        grid=(job.n_blocks,),
        in_specs=job.in_specs() + [c.in_spec(step) for c in casts],
        out_specs=[job.out_spec()] + [c.out_spec(step) for c in casts],
        out_shape=[job.out_shape()] + [c.out_shape() for c in casts],
        compiler_params=_params(),
        name="adaln_mod",
    )(*job.args, *[c.src for c in casts])
    return out[0], out[1:]


def _rows_spec(tile, width):
    return pl.BlockSpec((tile, width), lambda i: (i, 0))


def _ctx_rows_spec(tile, width):
    last = _n_ctx_tiles(tile) - 1
    return pl.BlockSpec((tile, width), lambda i: (jnp.minimum(i, last), 0))


def _smp_rows_spec(tile, width):
    first = _n_ctx_tiles(tile)
    return pl.BlockSpec((tile, width), lambda i: (jnp.maximum(i - first, 0), 0))


def _mod_spec(mod):
    return pl.BlockSpec(mod.shape, lambda i: (0, 0))


def _gain_spec(layer, j):
    return pl.BlockSpec((None, None, 1, D_MODEL), lambda i: (layer, j, 0, 0))


def _resident_spec(shape, *lead):
    index = tuple(lead) + (0,) * len(shape)
    return pl.BlockSpec((None,) * len(lead) + tuple(shape), lambda i: index,
                        pipeline_mode=pl.Buffered(1))


def _params():
    return pltpu.CompilerParams(dimension_semantics=("arbitrary",), vmem_limit_bytes=VMEM_LIMIT)


def _ffn_kernel(*refs, j, n_in, n_cast, mod_blocks, layer, which):
    n_side = n_cast + (3 if mod_blocks else 0)
    x_refs = refs[:n_in]
    mod_ref, g_ref, w1_hbm, w2_hbm = refs[n_in:n_in + 4]
    side_in = refs[n_in + 4:n_in + 4 + n_side]
    o_ref = refs[n_in + 4 + n_side]
    side_out = refs[n_in + 5 + n_side:len(refs) - N_STREAM_SCRATCH]
    stream = _WeightStream(w1_hbm, w2_hbm, layer, which, *refs[len(refs) - N_STREAM_SCRATCH:])
    i = pl.program_id(0)

    def run(before_chunk):
        if n_in == 2:
            x = jnp.where(i < _n_ctx_tiles(FFN_TILE), x_refs[0][...], x_refs[1][...])
        else:
            x = x_refs[0][...]
        o_ref[...] = _ffn_rows(x, g_ref[...], mod_ref, _mod_row(i, FFN_TILE), j,
                               stream.w1_s, stream.w2_s, before_chunk)

    @pl.when(i == 0)
    def _():
        stream.prime()
        run(stream.land)

    @pl.when(i > 0)
    def _():
        run(None)

    _cast_blocks(i, FFN_CAST_BLOCKS, side_in[:n_cast], side_out[:n_cast])
    if mod_blocks:
        @pl.when(i < mod_blocks)
        def _():
            side_out[n_cast][...] = _mod_block(*side_in[n_cast:])


def _ffn_call(xs, mod, norm_g, w1, w2, layer, which, casts=(), mod_job=None):
    step = lambda i: i
    if len(xs) == 2:
        x_specs = [_ctx_rows_spec(FFN_TILE, D_MODEL), _smp_rows_spec(FFN_TILE, D_MODEL)]
    else:
        x_specs = [_rows_spec(FFN_TILE, D_MODEL)]
    jobs = [mod_job] if mod_job else []
    out = pl.pallas_call(
        functools.partial(_ffn_kernel, j=6 * which, n_in=len(xs), n_cast=len(casts),
                          mod_blocks=mod_job.n_blocks if mod_job else 0, layer=layer, which=which),
        grid=(N_ALL // FFN_TILE,),
        in_specs=x_specs + [_mod_spec(mod), _gain_spec(layer, 2 * which)] + _weight_stream_specs()
        + [c.in_spec(step) for c in casts] + [s for m in jobs for s in m.in_specs()],
        out_specs=[_rows_spec(FFN_TILE, D_MODEL)] + [c.out_spec(step) for c in casts]
        + [m.out_spec() for m in jobs],
        out_shape=[jax.ShapeDtypeStruct((N_ALL, D_MODEL), F32)] + [c.out_shape() for c in casts]
        + [m.out_shape() for m in jobs],
        scratch_shapes=_weight_stream_scratch(),
        compiler_params=_params(),
        name="ffn",
    )(*xs, mod, norm_g, w1, w2, *[c.src for c in casts], *[a for m in jobs for a in m.args])
    return out[0], out[1:]


def _head_mean_sq(x, seg):
    sq = x * x
    hi = sq.astype(BF16)
    lo = (sq - hi.astype(F32)).astype(BF16)
    width = x.shape[-1]
    parts = []
    for c in range(0, width, SEG_WIDTH):
        n = min(SEG_WIDTH, width - c)
        parts.append(jnp.dot(hi[:, c:c + n], seg[:n, :n], preferred_element_type=F32)
                     + jnp.dot(lo[:, c:c + n], seg[:n, :n], preferred_element_type=F32))
    return parts[0] if len(parts) == 1 else jnp.concatenate(parts, axis=1)


def _rope(x, cos, sin_signed):
    width = x.shape[-1]
    lane = lax.broadcasted_iota(jnp.int32, (1, width), 1)
    partner = jnp.where((lane & 31) < 16,
                        pltpu.roll(x, width - 16, 1), pltpu.roll(x, 16, 1))
    return x * cos + partner * sin_signed


def _stack_heads(q_ref, rows, kvh):
    first = kvh * HEADS_PER_KV
    return jnp.concatenate(
        [q_ref[rows, (first + g) * HEAD_DIM:(first + g + 1) * HEAD_DIM] for g in range(HEADS_PER_KV)],
        axis=0)


ONES_ROWS = 16


def _attend_group(q, k, v1t):
    st = lax.dot_general(k, q, (((1,), (1,)), ((), ())), preferred_element_type=F32)
    e = jnp.exp(st - jnp.max(st, axis=0, keepdims=True)).astype(BF16)
    ot = jnp.dot(v1t, e, preferred_element_type=F32)
    return ot[:HEAD_DIM] / ot[HEAD_DIM:HEAD_DIM + 1]


def _attend_rows(q_s, rows, n, kv, mix_s):
    heads = []
    for kvh in range(N_KV_HEADS):
        k, v1t = kv(kvh)
        ot = _attend_group(_stack_heads(q_s, rows, kvh), k, v1t)
        heads += [ot[:, g * n:(g + 1) * n] for g in range(HEADS_PER_KV)]
    mix_s[rows, :ATTN_WIDTH] = jnp.concatenate(heads, axis=0).T.astype(BF16)


def _with_ones(vt):
    return jnp.concatenate([vt, jnp.ones((ONES_ROWS, vt.shape[1]), BF16)], axis=0)


def _mixer_tile(ctx, r, x_ref, mod_ref, g_ref, w_ref, qg_ref, kg_ref, cw_ref, seg_ref, cos_ref, sin_ref,
                ck_ref, cvt_ref, wo_ref, o_ref, kt_ref, vt_ref, q_s, k_s, vt_s, mix_s):
    seq_len = SEQ if ctx else DEC_SEQ
    x = x_ref[...]
    h = _modulate(x, g_ref[...], mod_ref, r, 3).astype(BF16)

    def proj(lo, width):
        return jnp.dot(h, w_ref[:, lo:lo + width], preferred_element_type=F32)

    q = proj(0, ATTN_WIDTH)
    kv = proj(ATTN_WIDTH, 2 * KV_WIDTH)
    k, v = kv[:, :KV_WIDTH], kv[:, KV_WIDTH:]
    q = q * lax.rsqrt(_head_mean_sq(q, seg_ref[...]) + EPS) * (qg_ref[...] * HEAD_DIM ** -0.5)
    k = k * lax.rsqrt(_head_mean_sq(k, seg_ref[...]) + EPS) * kg_ref[...]
    if ctx:
        q_s[...] = q.astype(BF16)
        k_s[...] = k.astype(BF16)
        for b in range(SEQ_PER_TILE):
            vt = v[b * SEQ:(b + 1) * SEQ, :].T
            kt_ref[b] = k[b * SEQ:(b + 1) * SEQ, :].T
            vt_ref[b] = vt
            vt_s[:, b * SEQ:(b + 1) * SEQ] = vt.astype(BF16)
    else:
        q_s[...] = _rope(q, cos_ref[...], sin_ref[...]).astype(BF16)
        k_s[...] = _rope(k, cos_ref[:, :KV_WIDTH], sin_ref[:, :KV_WIDTH]).astype(BF16)
        vt_s[...] = v.T.astype(BF16)

    base = ATTN_WIDTH + 2 * KV_WIDTH
    z = proj(base + CONV_WIDTH, CONV_WIDTH) * proj(base + 2 * CONV_WIDTH, CONV_WIDTH)
    pos = lax.broadcasted_iota(jnp.int32, (MIX_TILE, 1), 0) & (seq_len - 1)
    z_prev = jnp.where(pos == 0, 0.0, pltpu.roll(z, 1, 0))
    z_next = jnp.where(pos == seq_len - 1, 0.0, pltpu.roll(z, MIX_TILE - 1, 0))
    conv = z_prev * cw_ref[0:1, :] + z * cw_ref[1:2, :] + z_next * cw_ref[2:3, :]
    mix_s[:, ATTN_WIDTH:] = (proj(base, CONV_WIDTH) * conv).astype(BF16)

    def head(kvh):
        return slice(kvh * HEAD_DIM, (kvh + 1) * HEAD_DIM)

    if ctx:
        for b in range(SEQ_PER_TILE):
            rows = slice(b * SEQ, (b + 1) * SEQ)
            _attend_rows(q_s, rows, SEQ,
                         lambda kvh: (k_s[rows, head(kvh)], _with_ones(vt_s[head(kvh), rows])), mix_s)
    else:
        kv = [(jnp.concatenate([ck_ref[:, head(kvh)].astype(BF16), k_s[:, head(kvh)]], axis=0),
               _with_ones(jnp.concatenate([cvt_ref[head(kvh), :].astype(BF16), vt_s[head(kvh), :]],
                                          axis=1)))
              for kvh in range(N_KV_HEADS)]
        for t in range(MIX_TILE // Q_ROWS):
            _attend_rows(q_s, slice(t * Q_ROWS, (t + 1) * Q_ROWS), Q_ROWS, lambda kvh: kv[kvh], mix_s)

    out = (jnp.dot(mix_s[:, :ATTN_WIDTH], wo_ref[:ATTN_WIDTH, :], preferred_element_type=F32)
           + jnp.dot(mix_s[:, ATTN_WIDTH:], wo_ref[ATTN_WIDTH:, :], preferred_element_type=F32))
    o_ref[...] = x + _mod_vec(mod_ref, r, 5) * out


def _mixer_kernel(*refs):
    i = pl.program_id(0)
    r = _mod_row(i, MIX_TILE)
    is_ctx = i < _n_ctx_tiles(MIX_TILE)

    @pl.when(is_ctx)
    def _():
        _mixer_tile(True, r, *refs)

    @pl.when(jnp.logical_not(is_ctx))
    def _():
        _mixer_tile(False, r, *refs)


def _mixer_call(x, mod, norm_g, w_in, qg, kg, conv_w, seg, cos, sin, cache_k, cache_vt, w_out):
    first_smp = _n_ctx_tiles(MIX_TILE)
    smp_batch = lambda i: (jnp.maximum(i - first_smp, 0), 0, 0)
    kvt_spec = pl.BlockSpec((SEQ_PER_TILE, KV_WIDTH, SEQ), lambda i: (jnp.minimum(i, first_smp - 1), 0, 0))
    kvt_shape = jax.ShapeDtypeStruct((BATCH, KV_WIDTH, SEQ), F32)
    return pl.pallas_call(
        _mixer_kernel,
        grid=(N_ALL // MIX_TILE,),
        in_specs=[
            _rows_spec(MIX_TILE, D_MODEL), _mod_spec(mod), _gain_spec(0, 1),
            _resident_spec((D_MODEL, IN_WIDTH)),
            _resident_spec((1, ATTN_WIDTH)), _resident_spec((1, KV_WIDTH)),
            _resident_spec((3, CONV_WIDTH), 0),
            _resident_spec((SEG_WIDTH, SEG_WIDTH)),
            _resident_spec((DEC_SEQ, ATTN_WIDTH)), _resident_spec((DEC_SEQ, ATTN_WIDTH)),
            pl.BlockSpec((None, PAST_LEN, KV_WIDTH), smp_batch),
            pl.BlockSpec((None, KV_WIDTH, PAST_LEN), smp_batch),
            _resident_spec((MIX_WIDTH, D_MODEL)),
        ],
        out_specs=[_rows_spec(MIX_TILE, D_MODEL), kvt_spec, kvt_spec],
        out_shape=[jax.ShapeDtypeStruct((N_ALL, D_MODEL), F32), kvt_shape, kvt_shape],
        scratch_shapes=[pltpu.VMEM((MIX_TILE, ATTN_WIDTH), BF16),
                        pltpu.VMEM((MIX_TILE, KV_WIDTH), BF16),
                        pltpu.VMEM((KV_WIDTH, MIX_TILE), BF16),
                        pltpu.VMEM((MIX_TILE, MIX_WIDTH), BF16)],
        compiler_params=_params(),
        name="mixer",
    )(x, mod, norm_g, w_in, qg, kg, conv_w, seg, cos, sin, cache_k, cache_vt, w_out)


def _window_sums(h, band_ref, gi, seq_len):
    hi = h.astype(BF16)
    lo = (h - hi.astype(F32)).astype(BF16)
    per_seq = seq_len // POOL_BLOCK
    sums = []
    for b in range(h.shape[0] // POOL_BLOCK):
        def part(kind, src):
            rows = slice(src * POOL_BLOCK, (src + 1) * POOL_BLOCK)
            band = band_ref[gi, kind]
            return (jnp.dot(band, hi[rows], preferred_element_type=F32)
                    + jnp.dot(band, lo[rows], preferred_element_type=F32))
        total = part(0, b)
        if (b + 1) % per_seq:
            total = total + part(1, b + 1)
        if b % per_seq:
            total = total + part(2, b - 1)
        sums.append(total)
    return jnp.concatenate(sums, axis=0)


def _pool_rows(x, pos, seq_len, r, mod_ref, gm_ref, pw_ref, ps_ref, band_ref):
    gate = _mod_vec(mod_ref, r, 5)
    h_all = _modulate(x, gm_ref[...], mod_ref, r, 3)
    outs = []
    for gi, w in enumerate(POOL_WINDOWS):
        lanes = slice(gi * POOL_GROUP, (gi + 1) * POOL_GROUP)
        h = h_all[:, lanes]
        left = w // 2
        right = w - 1 - left
        count = jnp.minimum(pos + right + 1, seq_len) - jnp.maximum(pos - left, 0)
        diff = (_window_sums(h, band_ref, gi, seq_len) / count.astype(F32) - h).astype(BF16)
        out = jnp.dot(diff, pw_ref[lanes, :], preferred_element_type=F32)
        outs.append(x[:, lanes] + (gate[:, lanes] * ps_ref[:, lanes]) * out)
    return jnp.concatenate(outs, axis=1)


def _pool_tile(ctx, r, x_ref, mod_ref, gm_ref, gf_ref, gn_ref, pw_ref, ps_ref, band_ref, w1_ref, w2_ref,
               y_ref, x2_ref):
    seq_len = SEQ if ctx else DEC_SEQ
    block = max(seq_len, FFN_TILE)
    pos = lax.broadcasted_iota(jnp.int32, (block, 1), 0) & (seq_len - 1)
    for b in range(MIX_TILE // block):
        rows = slice(b * block, (b + 1) * block)
        x2_ref[rows, :] = _pool_rows(x_ref[rows, :], pos, seq_len, r, mod_ref, gm_ref, pw_ref, ps_ref,
                                     band_ref)

    def ffn_step(s, carry):
        rows = pl.ds(pl.multiple_of(s * FFN_TILE, FFN_TILE), FFN_TILE)
        y = _ffn_rows(x2_ref[rows, :], gf_ref[...], mod_ref, r, 6, w1_ref, w2_ref)
        y_ref[rows, :] = _rms(y, gn_ref[...])
        return carry

    lax.fori_loop(0, MIX_TILE // FFN_TILE, ffn_step, 0)


def _pool_kernel(*refs):
    ins, (yp_ref, ys_ref, x2_ref) = refs[:-3], refs[-3:]
    i = pl.program_id(0)
    r = _mod_row(i, MIX_TILE)
    is_ctx = i < _n_ctx_tiles(MIX_TILE)

    @pl.when(is_ctx)
    def _():
        _pool_tile(True, r, *ins, yp_ref, x2_ref)

    @pl.when(jnp.logical_not(is_ctx))
    def _():
        _pool_tile(False, r, *ins, ys_ref, x2_ref)


def _pool_call(x, mod, norm_g, final_g, pool_w, pool_scale, bands, w1, w2):
    return pl.pallas_call(
        _pool_kernel,
        grid=(N_ALL // MIX_TILE,),
        in_specs=[
            _rows_spec(MIX_TILE, D_MODEL), _mod_spec(mod), _gain_spec(1, 1), _gain_spec(1, 2),
            _resident_spec((1, D_MODEL)),
            _resident_spec((D_MODEL, POOL_GROUP)),
            _resident_spec((1, D_MODEL), 0),
            _resident_spec(bands.shape),
            _resident_spec((D_MODEL, 2 * D_FF)), _resident_spec((D_FF, D_MODEL)),
        ],
        out_specs=[_ctx_rows_spec(MIX_TILE, D_MODEL), _smp_rows_spec(MIX_TILE, D_MODEL)],
        out_shape=[jax.ShapeDtypeStruct((N_CTX, D_MODEL), F32),
                   jax.ShapeDtypeStruct((N_SMP, D_MODEL), F32)],
        scratch_shapes=[pltpu.VMEM((MIX_TILE, D_MODEL), F32)],
        compiler_params=_params(),
        name="pool_ffn_norm",
    )(x, mod, norm_g, norm_g, final_g, pool_w, pool_scale, bands, w1, w2)


def _rope_tables():
    t = np.arange(DEC_SEQ)
    half = HEAD_DIM // 2
    inv = ROPE_THETA ** (-np.arange(0, half, 2, dtype=np.float64) / half)
    ang_row = (t // GRID_W)[:, None] * inv[None, :]
    ang_col = (t % GRID_W)[:, None] * inv[None, :]
    cos = np.concatenate([np.cos(ang_row), np.cos(ang_row), np.cos(ang_col), np.cos(ang_col)], axis=1)
    sin = np.concatenate([-np.sin(ang_row), np.sin(ang_row), -np.sin(ang_col), np.sin(ang_col)], axis=1)
    return (jnp.asarray(np.tile(cos, (1, N_HEADS)), F32), jnp.asarray(np.tile(sin, (1, N_HEADS)), F32))


def _head_segments():
    head = np.arange(SEG_WIDTH) // HEAD_DIM
    return jnp.asarray((head[:, None] == head[None, :]) / HEAD_DIM, BF16)


def _pool_bands():
    t = np.arange(POOL_BLOCK)[:, None]
    bands = []
    for w in POOL_WINDOWS:
        left = w // 2
        right = w - 1 - left
        src = [np.arange(POOL_BLOCK)[None, :] + shift for shift in (0, POOL_BLOCK, -POOL_BLOCK)]
        bands.append(np.stack([(s >= t - left) & (s <= t + right) for s in src]))
    return jnp.asarray(np.stack(bands), BF16)


def _cache_layout(t):
    return jnp.transpose(t.reshape(BATCH, 1, N_KV_HEADS, HEAD_DIM, SEQ), (0, 1, 4, 2, 3))


def kernel(x_prompt, x_sample, c, cache_k, cache_v, c_ctx, ada_w, ada_b, norm_g, ffn_w1, ffn_w2,
           mix_w_in, mix_w_out, q_norm, k_norm, conv_w, pool_w, pool_scale, final_g):
    cvec = jnp.concatenate(
        [c_ctx[None, :], c, jnp.zeros((MOD_ROWS - 1 - DEC_BATCH, D_MODEL), F32)], axis=0)
    gains = norm_g.reshape(norm_g.shape[0], 3, 1, D_MODEL)
    cos, sin = _rope_tables()

    mod_first, _ = _mod_call(_ModJob(cvec, ada_w, ada_b, 0, MOD_STEPS, 3 * D_MODEL), [])
    x, (w_in, w_out, mod0) = _ffn_call(
        [x_prompt.reshape(N_CTX, D_MODEL), x_sample.reshape(N_SMP, D_MODEL)],
        mod_first, gains, ffn_w1, ffn_w2, 0, 0,
        [_Cast(mix_w_in, (0,), FFN_CAST_BLOCKS), _Cast(mix_w_out, (0,), FFN_CAST_BLOCKS)],
        _ModJob(cvec, ada_w, ada_b, 0, MOD_SIDE_BLOCKS))
    x, kt, vt = _mixer_call(
        x, mod0, gains, w_in, jnp.tile(q_norm[0], N_HEADS)[None, :],
        jnp.tile(k_norm[0], N_KV_HEADS)[None, :], conv_w, _head_segments(), cos, sin,
        cache_k[:, 0].reshape(DEC_BATCH, PAST_LEN, KV_WIDTH),
        jnp.transpose(cache_v[:, 0], (0, 2, 3, 1)).reshape(DEC_BATCH, KV_WIDTH, PAST_LEN), w_out)
    x, (pw, w1_last, w2_last, mod1) = _ffn_call(
        [x], mod0, gains, ffn_w1, ffn_w2, 0, 1,
        [_Cast(pool_w.reshape(pool_w.shape[0], D_MODEL, POOL_GROUP), (0,), FFN_CAST_BLOCKS),
         _Cast(ffn_w1, (1, 1), FFN_CAST_BLOCKS), _Cast(ffn_w2, (1, 1), FFN_CAST_BLOCKS)],
        _ModJob(cvec, ada_w, ada_b, 1, MOD_SIDE_BLOCKS))
    x, _ = _ffn_call([x], mod1, gains, ffn_w1, ffn_w2, 1, 0)
    yp, ys = _pool_call(x, mod1, gains, final_g[None, :], pw,
                        pool_scale.reshape(pool_scale.shape[0], 1, D_MODEL), _pool_bands(), w1_last, w2_last)

    return (yp.reshape(BATCH, SEQ, D_MODEL), ys.reshape(DEC_BATCH, DEC_SEQ, D_MODEL),
            _cache_layout(kt), _cache_layout(vt))
```

```python
import functools

import numpy as np
import jax
import jax.numpy as jnp
from jax import lax
from jax.experimental import pallas as pl
from jax.experimental.pallas import tpu as pltpu

F32 = jnp.float32
BF16 = jnp.bfloat16

D_MODEL = 1024
BATCH = 32
SEQ = 256
DEC_BATCH = 2
DEC_SEQ = 1024
PAST_LEN = 512
GRID_W = 64
N_HEADS = 8
N_KV_HEADS = 2
HEAD_DIM = 64
HEADS_PER_KV = N_HEADS // N_KV_HEADS
ATTN_WIDTH = N_HEADS * HEAD_DIM
KV_WIDTH = N_KV_HEADS * HEAD_DIM
CONV_WIDTH = D_MODEL // 2
MIX_WIDTH = ATTN_WIDTH + CONV_WIDTH
IN_WIDTH = ATTN_WIDTH + 2 * KV_WIDTH + 3 * CONV_WIDTH
D_FF = 2816
POOL_WINDOWS = (2, 4, 8, 16)
POOL_GROUP = D_MODEL // len(POOL_WINDOWS)
N_MOD = 9
ROPE_THETA = 10000.0
EPS = 1e-6

N_CTX = BATCH * SEQ
N_SMP = DEC_BATCH * DEC_SEQ
N_ALL = N_CTX + N_SMP
MIX_TILE = 1024
FFN_TILE = 512
FF_CHUNK = 256
N_FF_CHUNKS = D_FF // FF_CHUNK
CHUNKS_PER_MATMUL = 2
STAGE_SLOTS = 2
Q_ROWS = 256
SEG_WIDTH = 256
POOL_BLOCK = 256
SEQ_PER_TILE = MIX_TILE // SEQ
MOD_ROWS = 8
MOD_WIDTH = N_MOD * D_MODEL
MOD_STEPS = 2
MOD_SIDE_BLOCKS = 18
FFN_CAST_BLOCKS = 16
VMEM_LIMIT = 56 * 1024 * 1024


def _n_ctx_tiles(tile):
    return N_CTX // tile


def _mod_row(i, tile):
    first = _n_ctx_tiles(tile)
    return jnp.where(i < first, 0, 1 + (i - first) * tile // DEC_SEQ)


def _rms(x, g):
    return x * lax.rsqrt(jnp.mean(x * x, axis=-1, keepdims=True) + EPS) * g


def _mod_vec(mod_ref, r, j):
    return mod_ref[pl.ds(r, 1), j * D_MODEL:(j + 1) * D_MODEL]


def _modulate(x, g, mod_ref, r, j):
    gain = g * (1 + _mod_vec(mod_ref, r, j + 1))
    return x * lax.rsqrt(jnp.mean(x * x, axis=-1, keepdims=True) + EPS) * gain + _mod_vec(mod_ref, r, j)


def _ffn_rows(x, g, mod_ref, r, j, w1_ref, w2_ref, before_chunk=None):
    h = _modulate(x, g, mod_ref, r, j).astype(BF16)
    acc = None
    for first in range(0, N_FF_CHUNKS, CHUNKS_PER_MATMUL):
        n = min(CHUNKS_PER_MATMUL, N_FF_CHUNKS - first)
        lo, width = first * FF_CHUNK, n * FF_CHUNK
        if before_chunk is not None:
            for c in range(first, first + n):
                before_chunk(c)
        gate = jnp.dot(h, w1_ref[:, lo:lo + width], preferred_element_type=F32)
        up = jnp.dot(h, w1_ref[:, D_FF + lo:D_FF + lo + width], preferred_element_type=F32)
        act = (gate / (1 + jnp.exp(-gate)) * up).astype(BF16)
        y = jnp.dot(act, w2_ref[lo:lo + width, :], preferred_element_type=F32)
        acc = y if acc is None else acc + y
    return x + (0.5 * _mod_vec(mod_ref, r, j + 2)) * acc


class _Cast:
    def __init__(self, src, lead, n_blocks):
        self.src, self.lead, self.n_blocks = src, tuple(lead), n_blocks
        self.rows, self.cols = src.shape[len(lead):]
        self.block_rows = self.rows // n_blocks

    def in_spec(self, step):
        lead, last = self.lead, self.n_blocks - 1
        return pl.BlockSpec((None,) * len(lead) + (self.block_rows, self.cols),
                            lambda *g: lead + (jnp.minimum(step(*g), last), 0))

    def out_spec(self, step):
        last = self.n_blocks - 1
        return pl.BlockSpec((self.block_rows, self.cols), lambda *g: (jnp.minimum(step(*g), last), 0))

    def out_shape(self):
        return jax.ShapeDtypeStruct((self.rows, self.cols), BF16)


def _cast_blocks(step, n_blocks, srcs, dsts):
    @pl.when(step < n_blocks)
    def _():
        for s, d in zip(srcs, dsts):
            d[...] = s[...].astype(BF16)


class _WeightStream:
    def __init__(self, w1_hbm, w2_hbm, layer, which, w1_s, w2_s, gate_st, up_st, down_st, sems):
        self.w1_hbm, self.w2_hbm, self.lead = w1_hbm, w2_hbm, (layer, which)
        self.w1_s, self.w2_s = w1_s, w2_s
        self.stages, self.sems = (gate_st, up_st, down_st), sems

    def _copies(self, c):
        slot, lo = c % STAGE_SLOTS, c * FF_CHUNK
        l, w = self.lead
        srcs = (self.w1_hbm.at[l, w, :, pl.ds(lo, FF_CHUNK)],
                self.w1_hbm.at[l, w, :, pl.ds(D_FF + lo, FF_CHUNK)],
                self.w2_hbm.at[l, w, pl.ds(lo, FF_CHUNK), :])
        return [pltpu.make_async_copy(src, st.at[slot], self.sems.at[k, slot])
                for k, (src, st) in enumerate(zip(srcs, self.stages))]

    def start(self, c):
        for cp in self._copies(c):
            cp.start()

    def prime(self):
        for c in range(STAGE_SLOTS):
            self.start(c)

    def land(self, c):
        for cp in self._copies(c):
            cp.wait()
        slot, lo = c % STAGE_SLOTS, c * FF_CHUNK
        gate_st, up_st, down_st = self.stages
        self.w1_s[:, lo:lo + FF_CHUNK] = gate_st[slot].astype(BF16)
        self.w1_s[:, D_FF + lo:D_FF + lo + FF_CHUNK] = up_st[slot].astype(BF16)
        self.w2_s[lo:lo + FF_CHUNK, :] = down_st[slot].astype(BF16)
        if c + STAGE_SLOTS < N_FF_CHUNKS:
            self.start(c + STAGE_SLOTS)


def _weight_stream_specs():
    return [pl.BlockSpec(memory_space=pl.ANY), pl.BlockSpec(memory_space=pl.ANY)]


N_STREAM_SCRATCH = 6


def _weight_stream_scratch():
    return [pltpu.VMEM((D_MODEL, 2 * D_FF), BF16), pltpu.VMEM((D_FF, D_MODEL), BF16),
            pltpu.VMEM((STAGE_SLOTS, D_MODEL, FF_CHUNK), F32),
            pltpu.VMEM((STAGE_SLOTS, D_MODEL, FF_CHUNK), F32),
            pltpu.VMEM((STAGE_SLOTS, FF_CHUNK, D_MODEL), F32),
            pltpu.SemaphoreType.DMA((3, STAGE_SLOTS))]


def _mod_block(c_ref, w_ref, b_ref):
    c = c_ref[...]
    s = (c / (1 + jnp.exp(-c))).astype(BF16)
    return jnp.dot(s, w_ref[...].astype(BF16), preferred_element_type=F32) + b_ref[...]


class _ModJob:
    def __init__(self, cvec, ada_w, ada_b, layer, n_blocks, width=MOD_WIDTH):
        self.args = (cvec, ada_w, ada_b.reshape(ada_b.shape[0], 1, MOD_WIDTH))
        self.layer, self.n_blocks, self.width, self.cols = layer, n_blocks, width, width // n_blocks

    def in_specs(self):
        layer, last, cols = self.layer, self.n_blocks - 1, self.cols
        block = lambda i: (layer, 0, jnp.minimum(i, last))
        return [pl.BlockSpec((MOD_ROWS, D_MODEL), lambda i: (0, 0)),
                pl.BlockSpec((None, D_MODEL, cols), block),
                pl.BlockSpec((None, 1, cols), block)]

    def out_spec(self):
        last = self.n_blocks - 1
        return pl.BlockSpec((MOD_ROWS, self.cols), lambda i: (0, jnp.minimum(i, last)))

    def out_shape(self):
        return jax.ShapeDtypeStruct((MOD_ROWS, self.width), F32)


def _mod_kernel(c_ref, w_ref, b_ref, *refs, n_cast):
    srcs, o_ref, dsts = refs[:n_cast], refs[n_cast], refs[n_cast + 1:]
    o_ref[...] = _mod_block(c_ref, w_ref, b_ref)
    _cast_blocks(pl.program_id(0), pl.num_programs(0), srcs, dsts)


def _mod_call(job, casts):
    step = lambda i: i
    out = pl.pallas_call(
        functools.partial(_mod_kernel, n_cast=len(casts)),
        grid=(job.n_blocks,),
        in_specs=job.in_specs() + [c.in_spec(step) for c in casts],
        out_specs=[job.out_spec()] + [c.out_spec(step) for c in casts],
        out_shape=[job.out_shape()] + [c.out_shape() for c in casts],
        compiler_params=_params(),
        name="adaln_mod",
    )(*job.args, *[c.src for c in casts])
    return out[0], out[1:]


def _rows_spec(tile, width):
    return pl.BlockSpec((tile, width), lambda i: (i, 0))


def _ctx_rows_spec(tile, width):
    last = _n_ctx_tiles(tile) - 1
    return pl.BlockSpec((tile, width), lambda i: (jnp.minimum(i, last), 0))


def _smp_rows_spec(tile, width):
    first = _n_ctx_tiles(tile)
    return pl.BlockSpec((tile, width), lambda i: (jnp.maximum(i - first, 0), 0))


def _mod_spec(mod):
    return pl.BlockSpec(mod.shape, lambda i: (0, 0))


def _gain_spec(layer, j):
    return pl.BlockSpec((None, None, 1, D_MODEL), lambda i: (layer, j, 0, 0))


def _resident_spec(shape, *lead):
    index = tuple(lead) + (0,) * len(shape)
    return pl.BlockSpec((None,) * len(lead) + tuple(shape), lambda i: index,
                        pipeline_mode=pl.Buffered(1))


def _params():
    return pltpu.CompilerParams(dimension_semantics=("arbitrary",), vmem_limit_bytes=VMEM_LIMIT)


def _ffn_kernel(*refs, j, n_in, n_cast, mod_blocks, layer, which):
    n_side = n_cast + (3 if mod_blocks else 0)
    x_refs = refs[:n_in]
    mod_ref, g_ref, w1_hbm, w2_hbm = refs[n_in:n_in + 4]
    side_in = refs[n_in + 4:n_in + 4 + n_side]
    o_ref = refs[n_in + 4 + n_side]
    side_out = refs[n_in + 5 + n_side:len(refs) - N_STREAM_SCRATCH]
    stream = _WeightStream(w1_hbm, w2_hbm, layer, which, *refs[len(refs) - N_STREAM_SCRATCH:])
    i = pl.program_id(0)

    def run(before_chunk):
        if n_in == 2:
            x = jnp.where(i < _n_ctx_tiles(FFN_TILE), x_refs[0][...], x_refs[1][...])
        else:
            x = x_refs[0][...]
        o_ref[...] = _ffn_rows(x, g_ref[...], mod_ref, _mod_row(i, FFN_TILE), j,
                               stream.w1_s, stream.w2_s, before_chunk)

    @pl.when(i == 0)
    def _():
        stream.prime()
        run(stream.land)

    @pl.when(i > 0)
    def _():
        run(None)

    _cast_blocks(i, FFN_CAST_BLOCKS, side_in[:n_cast], side_out[:n_cast])
    if mod_blocks:
        @pl.when(i < mod_blocks)
        def _():
            side_out[n_cast][...] = _mod_block(*side_in[n_cast:])


def _ffn_call(xs, mod, norm_g, w1, w2, layer, which, casts=(), mod_job=None):
    step = lambda i: i
    if len(xs) == 2:
        x_specs = [_ctx_rows_spec(FFN_TILE, D_MODEL), _smp_rows_spec(FFN_TILE, D_MODEL)]
    else:
        x_specs = [_rows_spec(FFN_TILE, D_MODEL)]
    jobs = [mod_job] if mod_job else []
    out = pl.pallas_call(
        functools.partial(_ffn_kernel, j=6 * which, n_in=len(xs), n_cast=len(casts),
                          mod_blocks=mod_job.n_blocks if mod_job else 0, layer=layer, which=which),
        grid=(N_ALL // FFN_TILE,),
        in_specs=x_specs + [_mod_spec(mod), _gain_spec(layer, 2 * which)] + _weight_stream_specs()
        + [c.in_spec(step) for c in casts] + [s for m in jobs for s in m.in_specs()],
        out_specs=[_rows_spec(FFN_TILE, D_MODEL)] + [c.out_spec(step) for c in casts]
        + [m.out_spec() for m in jobs],
        out_shape=[jax.ShapeDtypeStruct((N_ALL, D_MODEL), F32)] + [c.out_shape() for c in casts]
        + [m.out_shape() for m in jobs],
        scratch_shapes=_weight_stream_scratch(),
        compiler_params=_params(),
        name="ffn",
    )(*xs, mod, norm_g, w1, w2, *[c.src for c in casts], *[a for m in jobs for a in m.args])
    return out[0], out[1:]


def _head_mean_sq(x, seg):
    sq = x * x
    hi = sq.astype(BF16)
    lo = (sq - hi.astype(F32)).astype(BF16)
    width = x.shape[-1]
    parts = []
    for c in range(0, width, SEG_WIDTH):
        n = min(SEG_WIDTH, width - c)
        parts.append(jnp.dot(hi[:, c:c + n], seg[:n, :n], preferred_element_type=F32)
                     + jnp.dot(lo[:, c:c + n], seg[:n, :n], preferred_element_type=F32))
    return parts[0] if len(parts) == 1 else jnp.concatenate(parts, axis=1)


def _rope(x, cos, sin_signed):
    width = x.shape[-1]
    lane = lax.broadcasted_iota(jnp.int32, (1, width), 1)
    partner = jnp.where((lane & 31) < 16,
                        pltpu.roll(x, width - 16, 1), pltpu.roll(x, 16, 1))
    return x * cos + partner * sin_signed


def _stack_heads(q_ref, rows, kvh):
    first = kvh * HEADS_PER_KV
    return jnp.concatenate(
        [q_ref[rows, (first + g) * HEAD_DIM:(first + g + 1) * HEAD_DIM] for g in range(HEADS_PER_KV)],
        axis=0)


ONES_ROWS = 16


def _attend_group(q, k, v1t):
    st = lax.dot_general(k, q, (((1,), (1,)), ((), ())), preferred_element_type=F32)
    e = jnp.exp(st - jnp.max(st, axis=0, keepdims=True)).astype(BF16)
    ot = jnp.dot(v1t, e, preferred_element_type=F32)
    return ot[:HEAD_DIM] / ot[HEAD_DIM:HEAD_DIM + 1]


def _attend_rows(q_s, rows, n, kv, mix_s):
    heads = []
    for kvh in range(N_KV_HEADS):
        k, v1t = kv(kvh)
        ot = _attend_group(_stack_heads(q_s, rows, kvh), k, v1t)
        heads += [ot[:, g * n:(g + 1) * n] for g in range(HEADS_PER_KV)]
    mix_s[rows, :ATTN_WIDTH] = jnp.concatenate(heads, axis=0).T.astype(BF16)


def _with_ones(vt):
    return jnp.concatenate([vt, jnp.ones((ONES_ROWS, vt.shape[1]), BF16)], axis=0)


def _mixer_tile(ctx, r, x_ref, mod_ref, g_ref, w_ref, qg_ref, kg_ref, cw_ref, seg_ref, cos_ref, sin_ref,
                ck_ref, cvt_ref, wo_ref, o_ref, kt_ref, vt_ref, q_s, k_s, vt_s, mix_s):
    seq_len = SEQ if ctx else DEC_SEQ
    x = x_ref[...]
    h = _modulate(x, g_ref[...], mod_ref, r, 3).astype(BF16)

    def proj(lo, width):
        return jnp.dot(h, w_ref[:, lo:lo + width], preferred_element_type=F32)

    q = proj(0, ATTN_WIDTH)
    kv = proj(ATTN_WIDTH, 2 * KV_WIDTH)
    k, v = kv[:, :KV_WIDTH], kv[:, KV_WIDTH:]
    q = q * lax.rsqrt(_head_mean_sq(q, seg_ref[...]) + EPS) * (qg_ref[...] * HEAD_DIM ** -0.5)
    k = k * lax.rsqrt(_head_mean_sq(k, seg_ref[...]) + EPS) * kg_ref[...]
    if ctx:
        q_s[...] = q.astype(BF16)
        k_s[...] = k.astype(BF16)
        for b in range(SEQ_PER_TILE):
            vt = v[b * SEQ:(b + 1) * SEQ, :].T
            kt_ref[b] = k[b * SEQ:(b + 1) * SEQ, :].T
            vt_ref[b] = vt
            vt_s[:, b * SEQ:(b + 1) * SEQ] = vt.astype(BF16)
    else:
        q_s[...] = _rope(q, cos_ref[...], sin_ref[...]).astype(BF16)
        k_s[...] = _rope(k, cos_ref[:, :KV_WIDTH], sin_ref[:, :KV_WIDTH]).astype(BF16)
        vt_s[...] = v.T.astype(BF16)

    base = ATTN_WIDTH + 2 * KV_WIDTH
    z = proj(base + CONV_WIDTH, CONV_WIDTH) * proj(base + 2 * CONV_WIDTH, CONV_WIDTH)
    pos = lax.broadcasted_iota(jnp.int32, (MIX_TILE, 1), 0) & (seq_len - 1)
    z_prev = jnp.where(pos == 0, 0.0, pltpu.roll(z, 1, 0))
    z_next = jnp.where(pos == seq_len - 1, 0.0, pltpu.roll(z, MIX_TILE - 1, 0))
    conv = z_prev * cw_ref[0:1, :] + z * cw_ref[1:2, :] + z_next * cw_ref[2:3, :]
    mix_s[:, ATTN_WIDTH:] = (proj(base, CONV_WIDTH) * conv).astype(BF16)

    def head(kvh):
        return slice(kvh * HEAD_DIM, (kvh + 1) * HEAD_DIM)

    if ctx:
        for b in range(SEQ_PER_TILE):
            rows = slice(b * SEQ, (b + 1) * SEQ)
            _attend_rows(q_s, rows, SEQ,
                         lambda kvh: (k_s[rows, head(kvh)], _with_ones(vt_s[head(kvh), rows])), mix_s)
    else:
        kv = [(jnp.concatenate([ck_ref[:, head(kvh)].astype(BF16), k_s[:, head(kvh)]], axis=0),
               _with_ones(jnp.concatenate([cvt_ref[head(kvh), :].astype(BF16), vt_s[head(kvh), :]],
                                          axis=1)))
              for kvh in range(N_KV_HEADS)]
        for t in range(MIX_TILE // Q_ROWS):
            _attend_rows(q_s, slice(t * Q_ROWS, (t + 1) * Q_ROWS), Q_ROWS, lambda kvh: kv[kvh], mix_s)

    out = (jnp.dot(mix_s[:, :ATTN_WIDTH], wo_ref[:ATTN_WIDTH, :], preferred_element_type=F32)
           + jnp.dot(mix_s[:, ATTN_WIDTH:], wo_ref[ATTN_WIDTH:, :], preferred_element_type=F32))
    o_ref[...] = x + _mod_vec(mod_ref, r, 5) * out


def _mixer_kernel(*refs):
    i = pl.program_id(0)
    r = _mod_row(i, MIX_TILE)
    is_ctx = i < _n_ctx_tiles(MIX_TILE)

    @pl.when(is_ctx)
    def _():
        _mixer_tile(True, r, *refs)

    @pl.when(jnp.logical_not(is_ctx))
    def _():
        _mixer_tile(False, r, *refs)


def _mixer_call(x, mod, norm_g, w_in, qg, kg, conv_w, seg, cos, sin, cache_k, cache_vt, w_out):
    first_smp = _n_ctx_tiles(MIX_TILE)
    smp_batch = lambda i: (jnp.maximum(i - first_smp, 0), 0, 0)
    kvt_spec = pl.BlockSpec((SEQ_PER_TILE, KV_WIDTH, SEQ), lambda i: (jnp.minimum(i, first_smp - 1), 0, 0))
    kvt_shape = jax.ShapeDtypeStruct((BATCH, KV_WIDTH, SEQ), F32)
    return pl.pallas_call(
        _mixer_kernel,
        grid=(N_ALL // MIX_TILE,),
        in_specs=[
            _rows_spec(MIX_TILE, D_MODEL), _mod_spec(mod), _gain_spec(0, 1),
            _resident_spec((D_MODEL, IN_WIDTH)),
            _resident_spec((1, ATTN_WIDTH)), _resident_spec((1, KV_WIDTH)),
            _resident_spec((3, CONV_WIDTH), 0),
            _resident_spec((SEG_WIDTH, SEG_WIDTH)),
            _resident_spec((DEC_SEQ, ATTN_WIDTH)), _resident_spec((DEC_SEQ, ATTN_WIDTH)),
            pl.BlockSpec((None, PAST_LEN, KV_WIDTH), smp_batch),
            pl.BlockSpec((None, KV_WIDTH, PAST_LEN), smp_batch),
            _resident_spec((MIX_WIDTH, D_MODEL)),
        ],
        out_specs=[_rows_spec(MIX_TILE, D_MODEL), kvt_spec, kvt_spec],
        out_shape=[jax.ShapeDtypeStruct((N_ALL, D_MODEL), F32), kvt_shape, kvt_shape],
        scratch_shapes=[pltpu.VMEM((MIX_TILE, ATTN_WIDTH), BF16),
                        pltpu.VMEM((MIX_TILE, KV_WIDTH), BF16),
                        pltpu.VMEM((KV_WIDTH, MIX_TILE), BF16),
                        pltpu.VMEM((MIX_TILE, MIX_WIDTH), BF16)],
        compiler_params=_params(),
        name="mixer",
    )(x, mod, norm_g, w_in, qg, kg, conv_w, seg, cos, sin, cache_k, cache_vt, w_out)


def _window_sums(h, band_ref, gi, seq_len):
    hi = h.astype(BF16)
    lo = (h - hi.astype(F32)).astype(BF16)
    per_seq = seq_len // POOL_BLOCK
    sums = []
    for b in range(h.shape[0] // POOL_BLOCK):
        def part(kind, src):
            rows = slice(src * POOL_BLOCK, (src + 1) * POOL_BLOCK)
            band = band_ref[gi, kind]
            return (jnp.dot(band, hi[rows], preferred_element_type=F32)
                    + jnp.dot(band, lo[rows], preferred_element_type=F32))
        total = part(0, b)
        if (b + 1) % per_seq:
            total = total + part(1, b + 1)
        if b % per_seq:
            total = total + part(2, b - 1)
        sums.append(total)
    return jnp.concatenate(sums, axis=0)


def _pool_rows(x, pos, seq_len, r, mod_ref, gm_ref, pw_ref, ps_ref, band_ref):
    gate = _mod_vec(mod_ref, r, 5)
    h_all = _modulate(x, gm_ref[...], mod_ref, r, 3)
    outs = []
    for gi, w in enumerate(POOL_WINDOWS):
        lanes = slice(gi * POOL_GROUP, (gi + 1) * POOL_GROUP)
        h = h_all[:, lanes]
        left = w // 2
        right = w - 1 - left
        count = jnp.minimum(pos + right + 1, seq_len) - jnp.maximum(pos - left, 0)
        diff = (_window_sums(h, band_ref, gi, seq_len) / count.astype(F32) - h).astype(BF16)
        out = jnp.dot(diff, pw_ref[lanes, :], preferred_element_type=F32)
        outs.append(x[:, lanes] + (gate[:, lanes] * ps_ref[:, lanes]) * out)
    return jnp.concatenate(outs, axis=1)


def _pool_tile(ctx, r, x_ref, mod_ref, gm_ref, gf_ref, gn_ref, pw_ref, ps_ref, band_ref, w1_ref, w2_ref,
               y_ref, x2_ref):
    seq_len = SEQ if ctx else DEC_SEQ
    block = max(seq_len, FFN_TILE)
    pos = lax.broadcasted_iota(jnp.int32, (block, 1), 0) & (seq_len - 1)
    for b in range(MIX_TILE // block):
        rows = slice(b * block, (b + 1) * block)
        x2_ref[rows, :] = _pool_rows(x_ref[rows, :], pos, seq_len, r, mod_ref, gm_ref, pw_ref, ps_ref,
                                     band_ref)

    def ffn_step(s, carry):
        rows = pl.ds(pl.multiple_of(s * FFN_TILE, FFN_TILE), FFN_TILE)
        y = _ffn_rows(x2_ref[rows, :], gf_ref[...], mod_ref, r, 6, w1_ref, w2_ref)
        y_ref[rows, :] = _rms(y, gn_ref[...])
        return carry

    lax.fori_loop(0, MIX_TILE // FFN_TILE, ffn_step, 0)


def _pool_kernel(*refs):
    ins, (yp_ref, ys_ref, x2_ref) = refs[:-3], refs[-3:]
    i = pl.program_id(0)
    r = _mod_row(i, MIX_TILE)
    is_ctx = i < _n_ctx_tiles(MIX_TILE)

    @pl.when(is_ctx)
    def _():
        _pool_tile(True, r, *ins, yp_ref, x2_ref)

    @pl.when(jnp.logical_not(is_ctx))
    def _():
        _pool_tile(False, r, *ins, ys_ref, x2_ref)


def _pool_call(x, mod, norm_g, final_g, pool_w, pool_scale, bands, w1, w2):
    return pl.pallas_call(
        _pool_kernel,
        grid=(N_ALL // MIX_TILE,),
        in_specs=[
            _rows_spec(MIX_TILE, D_MODEL), _mod_spec(mod), _gain_spec(1, 1), _gain_spec(1, 2),
            _resident_spec((1, D_MODEL)),
            _resident_spec((D_MODEL, POOL_GROUP)),
            _resident_spec((1, D_MODEL), 0),
            _resident_spec(bands.shape),
            _resident_spec((D_MODEL, 2 * D_FF)), _resident_spec((D_FF, D_MODEL)),
        ],
        out_specs=[_ctx_rows_spec(MIX_TILE, D_MODEL), _smp_rows_spec(MIX_TILE, D_MODEL)],
        out_shape=[jax.ShapeDtypeStruct((N_CTX, D_MODEL), F32),
                   jax.ShapeDtypeStruct((N_SMP, D_MODEL), F32)],
        scratch_shapes=[pltpu.VMEM((MIX_TILE, D_MODEL), F32)],
        compiler_params=_params(),
        name="pool_ffn_norm",
    )(x, mod, norm_g, norm_g, final_g, pool_w, pool_scale, bands, w1, w2)


def _rope_tables():
    t = np.arange(DEC_SEQ)
    half = HEAD_DIM // 2
    inv = ROPE_THETA ** (-np.arange(0, half, 2, dtype=np.float64) / half)
    ang_row = (t // GRID_W)[:, None] * inv[None, :]
    ang_col = (t % GRID_W)[:, None] * inv[None, :]
    cos = np.concatenate([np.cos(ang_row), np.cos(ang_row), np.cos(ang_col), np.cos(ang_col)], axis=1)
    sin = np.concatenate([-np.sin(ang_row), np.sin(ang_row), -np.sin(ang_col), np.sin(ang_col)], axis=1)
    return (jnp.asarray(np.tile(cos, (1, N_HEADS)), F32), jnp.asarray(np.tile(sin, (1, N_HEADS)), F32))


def _head_segments():
    head = np.arange(SEG_WIDTH) // HEAD_DIM
    return jnp.asarray((head[:, None] == head[None, :]) / HEAD_DIM, BF16)


def _pool_bands():
    t = np.arange(POOL_BLOCK)[:, None]
    bands = []
    for w in POOL_WINDOWS:
        left = w // 2
        right = w - 1 - left
        src = [np.arange(POOL_BLOCK)[None, :] + shift for shift in (0, POOL_BLOCK, -POOL_BLOCK)]
        bands.append(np.stack([(s >= t - left) & (s <= t + right) for s in src]))
    return jnp.asarray(np.stack(bands), BF16)


def _cache_layout(t):
    return jnp.transpose(t.reshape(BATCH, 1, N_KV_HEADS, HEAD_DIM, SEQ), (0, 1, 4, 2, 3))


def kernel(x_prompt, x_sample, c, cache_k, cache_v, c_ctx, ada_w, ada_b, norm_g, ffn_w1, ffn_w2,
           mix_w_in, mix_w_out, q_norm, k_norm, conv_w, pool_w, pool_scale, final_g):
    cvec = jnp.concatenate(
        [c_ctx[None, :], c, jnp.zeros((MOD_ROWS - 1 - DEC_BATCH, D_MODEL), F32)], axis=0)
    gains = norm_g.reshape(norm_g.shape[0], 3, 1, D_MODEL)
    cos, sin = _rope_tables()

    mod_first, _ = _mod_call(_ModJob(cvec, ada_w, ada_b, 0, MOD_STEPS, 3 * D_MODEL), [])
    x, (w_in, w_out, mod0) = _ffn_call(
        [x_prompt.reshape(N_CTX, D_MODEL), x_sample.reshape(N_SMP, D_MODEL)],
        mod_first, gains, ffn_w1, ffn_w2, 0, 0,
        [_Cast(mix_w_in, (0,), FFN_CAST_BLOCKS), _Cast(mix_w_out, (0,), FFN_CAST_BLOCKS)],
        _ModJob(cvec, ada_w, ada_b, 0, MOD_SIDE_BLOCKS))
    x, kt, vt = _mixer_call(
        x, mod0, gains, w_in, jnp.tile(q_norm[0], N_HEADS)[None, :],
        jnp.tile(k_norm[0], N_KV_HEADS)[None, :], conv_w, _head_segments(), cos, sin,
        cache_k[:, 0].reshape(DEC_BATCH, PAST_LEN, KV_WIDTH),
        jnp.transpose(cache_v[:, 0], (0, 2, 3, 1)).reshape(DEC_BATCH, KV_WIDTH, PAST_LEN), w_out)
    x, (pw, mod1) = _ffn_call(
        [x], mod0, gains, ffn_w1, ffn_w2, 0, 1,
        [_Cast(pool_w.reshape(pool_w.shape[0], D_MODEL, POOL_GROUP), (0,), FFN_CAST_BLOCKS)],
        _ModJob(cvec, ada_w, ada_b, 1, MOD_SIDE_BLOCKS))
    x, (w1_last, w2_last) = _ffn_call(
        [x], mod1, gains, ffn_w1, ffn_w2, 1, 0,
        [_Cast(ffn_w1, (1, 1), FFN_CAST_BLOCKS), _Cast(ffn_w2, (1, 1), FFN_CAST_BLOCKS)])
    yp, ys = _pool_call(x, mod1, gains, final_g[None, :], pw,
                        pool_scale.reshape(pool_scale.shape[0], 1, D_MODEL), _pool_bands(), w1_last, w2_last)

    return (yp.reshape(BATCH, SEQ, D_MODEL), ys.reshape(DEC_BATCH, DEC_SEQ, D_MODEL),
            _cache_layout(kt), _cache_layout(vt))
```

```python
import functools

import numpy as np
import jax
import jax.numpy as jnp
from jax import lax
from jax.experimental import pallas as pl
from jax.experimental.pallas import tpu as pltpu

F32 = jnp.float32
BF16 = jnp.bfloat16

D_MODEL = 1024
BATCH = 32
SEQ = 256
DEC_BATCH = 2
DEC_SEQ = 1024
PAST_LEN = 512
GRID_W = 64
N_HEADS = 8
N_KV_HEADS = 2
HEAD_DIM = 64
HEADS_PER_KV = N_HEADS // N_KV_HEADS
ATTN_WIDTH = N_HEADS * HEAD_DIM
KV_WIDTH = N_KV_HEADS * HEAD_DIM
CONV_WIDTH = D_MODEL // 2
MIX_WIDTH = ATTN_WIDTH + CONV_WIDTH
IN_WIDTH = ATTN_WIDTH + 2 * KV_WIDTH + 3 * CONV_WIDTH
D_FF = 2816
POOL_WINDOWS = (2, 4, 8, 16)
POOL_GROUP = D_MODEL // len(POOL_WINDOWS)
N_MOD = 9
ROPE_THETA = 10000.0
EPS = 1e-6

N_CTX = BATCH * SEQ
N_SMP = DEC_BATCH * DEC_SEQ
N_ALL = N_CTX + N_SMP
MIX_TILE = 1024
FFN_TILE = 512
FF_CHUNK = 256
N_FF_CHUNKS = D_FF // FF_CHUNK
STAGE_SLOTS = 2
Q_ROWS = 256
SEG_WIDTH = 256
POOL_BLOCK = 256
SEQ_PER_TILE = MIX_TILE // SEQ
MOD_ROWS = 8
MOD_WIDTH = N_MOD * D_MODEL
MOD_STEPS = 2
MOD_SIDE_BLOCKS = 18
FFN_CAST_BLOCKS = 16
VMEM_LIMIT = 56 * 1024 * 1024


def _n_ctx_tiles(tile):
    return N_CTX // tile


def _mod_row(i, tile):
    first = _n_ctx_tiles(tile)
    return jnp.where(i < first, 0, 1 + (i - first) * tile // DEC_SEQ)


def _rms(x, g):
    return x * lax.rsqrt(jnp.mean(x * x, axis=-1, keepdims=True) + EPS) * g


def _mod_vec(mod_ref, r, j):
    return mod_ref[pl.ds(r, 1), j * D_MODEL:(j + 1) * D_MODEL]


def _modulate(x, g, mod_ref, r, j):
    gain = g * (1 + _mod_vec(mod_ref, r, j + 1))
    return x * lax.rsqrt(jnp.mean(x * x, axis=-1, keepdims=True) + EPS) * gain + _mod_vec(mod_ref, r, j)


def _ffn_rows(x, g, mod_ref, r, j, w1_ref, w2_ref, before_chunk=None):
    h = _modulate(x, g, mod_ref, r, j).astype(BF16)
    acc = None
    for c in range(N_FF_CHUNKS):
        lo = c * FF_CHUNK
        if before_chunk is not None:
            before_chunk(c)
        gate = jnp.dot(h, w1_ref[:, lo:lo + FF_CHUNK], preferred_element_type=F32)
        up = jnp.dot(h, w1_ref[:, D_FF + lo:D_FF + lo + FF_CHUNK], preferred_element_type=F32)
        act = (gate / (1 + jnp.exp(-gate)) * up).astype(BF16)
        y = jnp.dot(act, w2_ref[lo:lo + FF_CHUNK, :], preferred_element_type=F32)
        acc = y if acc is None else acc + y
    return x + (0.5 * _mod_vec(mod_ref, r, j + 2)) * acc


class _Cast:
    def __init__(self, src, lead, n_blocks):
        self.src, self.lead, self.n_blocks = src, tuple(lead), n_blocks
        self.rows, self.cols = src.shape[len(lead):]
        self.block_rows = self.rows // n_blocks

    def in_spec(self, step):
        lead, last = self.lead, self.n_blocks - 1
        return pl.BlockSpec((None,) * len(lead) + (self.block_rows, self.cols),
                            lambda *g: lead + (jnp.minimum(step(*g), last), 0))

    def out_spec(self, step):
        last = self.n_blocks - 1
        return pl.BlockSpec((self.block_rows, self.cols), lambda *g: (jnp.minimum(step(*g), last), 0))

    def out_shape(self):
        return jax.ShapeDtypeStruct((self.rows, self.cols), BF16)


def _cast_blocks(step, n_blocks, srcs, dsts):
    @pl.when(step < n_blocks)
    def _():
        for s, d in zip(srcs, dsts):
            d[...] = s[...].astype(BF16)


class _WeightStream:
    def __init__(self, w1_hbm, w2_hbm, layer, which, w1_s, w2_s, gate_st, up_st, down_st, sems):
        self.w1_hbm, self.w2_hbm, self.lead = w1_hbm, w2_hbm, (layer, which)
        self.w1_s, self.w2_s = w1_s, w2_s
        self.stages, self.sems = (gate_st, up_st, down_st), sems

    def _copies(self, c):
        slot, lo = c % STAGE_SLOTS, c * FF_CHUNK
        l, w = self.lead
        srcs = (self.w1_hbm.at[l, w, :, pl.ds(lo, FF_CHUNK)],
                self.w1_hbm.at[l, w, :, pl.ds(D_FF + lo, FF_CHUNK)],
                self.w2_hbm.at[l, w, pl.ds(lo, FF_CHUNK), :])
        return [pltpu.make_async_copy(src, st.at[slot], self.sems.at[k, slot])
                for k, (src, st) in enumerate(zip(srcs, self.stages))]

    def start(self, c):
        for cp in self._copies(c):
            cp.start()

    def prime(self):
        for c in range(STAGE_SLOTS):
            self.start(c)

    def land(self, c):
        for cp in self._copies(c):
            cp.wait()
        slot, lo = c % STAGE_SLOTS, c * FF_CHUNK
        gate_st, up_st, down_st = self.stages
        self.w1_s[:, lo:lo + FF_CHUNK] = gate_st[slot].astype(BF16)
        self.w1_s[:, D_FF + lo:D_FF + lo + FF_CHUNK] = up_st[slot].astype(BF16)
        self.w2_s[lo:lo + FF_CHUNK, :] = down_st[slot].astype(BF16)
        if c + STAGE_SLOTS < N_FF_CHUNKS:
            self.start(c + STAGE_SLOTS)


def _weight_stream_specs():
    return [pl.BlockSpec(memory_space=pl.ANY), pl.BlockSpec(memory_space=pl.ANY)]


N_STREAM_SCRATCH = 6


def _weight_stream_scratch():
    return [pltpu.VMEM((D_MODEL, 2 * D_FF), BF16), pltpu.VMEM((D_FF, D_MODEL), BF16),
            pltpu.VMEM((STAGE_SLOTS, D_MODEL, FF_CHUNK), F32),
            pltpu.VMEM((STAGE_SLOTS, D_MODEL, FF_CHUNK), F32),
            pltpu.VMEM((STAGE_SLOTS, FF_CHUNK, D_MODEL), F32),
            pltpu.SemaphoreType.DMA((3, STAGE_SLOTS))]


def _mod_block(c_ref, w_ref, b_ref):
    c = c_ref[...]
    s = (c / (1 + jnp.exp(-c))).astype(BF16)
    return jnp.dot(s, w_ref[...].astype(BF16), preferred_element_type=F32) + b_ref[...]


class _ModJob:
    def __init__(self, cvec, ada_w, ada_b, layer, n_blocks, width=MOD_WIDTH):
        self.args = (cvec, ada_w, ada_b.reshape(ada_b.shape[0], 1, MOD_WIDTH))
        self.layer, self.n_blocks, self.width, self.cols = layer, n_blocks, width, width // n_blocks

    def in_specs(self):
        layer, last, cols = self.layer, self.n_blocks - 1, self.cols
        block = lambda i: (layer, 0, jnp.minimum(i, last))
        return [pl.BlockSpec((MOD_ROWS, D_MODEL), lambda i: (0, 0)),
                pl.BlockSpec((None, D_MODEL, cols), block),
                pl.BlockSpec((None, 1, cols), block)]

    def out_spec(self):
        last = self.n_blocks - 1
        return pl.BlockSpec((MOD_ROWS, self.cols), lambda i: (0, jnp.minimum(i, last)))

    def out_shape(self):
        return jax.ShapeDtypeStruct((MOD_ROWS, self.width), F32)


def _mod_kernel(c_ref, w_ref, b_ref, *refs, n_cast):
    srcs, o_ref, dsts = refs[:n_cast], refs[n_cast], refs[n_cast + 1:]
    o_ref[...] = _mod_block(c_ref, w_ref, b_ref)
    _cast_blocks(pl.program_id(0), pl.num_programs(0), srcs, dsts)


def _mod_call(job, casts):
    step = lambda i: i
    out = pl.pallas_call(
        functools.partial(_mod_kernel, n_cast=len(casts)),
        grid=(job.n_blocks,),
        in_specs=job.in_specs() + [c.in_spec(step) for c in casts],
        out_specs=[job.out_spec()] + [c.out_spec(step) for c in casts],
        out_shape=[job.out_shape()] + [c.out_shape() for c in casts],
        compiler_params=_params(),
        name="adaln_mod",
    )(*job.args, *[c.src for c in casts])
    return out[0], out[1:]


def _rows_spec(tile, width):
    return pl.BlockSpec((tile, width), lambda i: (i, 0))


def _ctx_rows_spec(tile, width):
    last = _n_ctx_tiles(tile) - 1
    return pl.BlockSpec((tile, width), lambda i: (jnp.minimum(i, last), 0))


def _smp_rows_spec(tile, width):
    first = _n_ctx_tiles(tile)
    return pl.BlockSpec((tile, width), lambda i: (jnp.maximum(i - first, 0), 0))


def _mod_spec(mod):
    return pl.BlockSpec(mod.shape, lambda i: (0, 0))


def _gain_spec(layer, j):
    return pl.BlockSpec((None, None, 1, D_MODEL), lambda i: (layer, j, 0, 0))


def _resident_spec(shape, *lead):
    index = tuple(lead) + (0,) * len(shape)
    return pl.BlockSpec((None,) * len(lead) + tuple(shape), lambda i: index,
                        pipeline_mode=pl.Buffered(1))


def _params():
    return pltpu.CompilerParams(dimension_semantics=("arbitrary",), vmem_limit_bytes=VMEM_LIMIT)


def _ffn_kernel(*refs, j, n_in, n_cast, mod_blocks, layer, which):
    n_side = n_cast + (3 if mod_blocks else 0)
    x_refs = refs[:n_in]
    mod_ref, g_ref, w1_hbm, w2_hbm = refs[n_in:n_in + 4]
    side_in = refs[n_in + 4:n_in + 4 + n_side]
    o_ref = refs[n_in + 4 + n_side]
    side_out = refs[n_in + 5 + n_side:len(refs) - N_STREAM_SCRATCH]
    stream = _WeightStream(w1_hbm, w2_hbm, layer, which, *refs[len(refs) - N_STREAM_SCRATCH:])
    i = pl.program_id(0)

    def run(before_chunk):
        if n_in == 2:
            x = jnp.where(i < _n_ctx_tiles(FFN_TILE), x_refs[0][...], x_refs[1][...])
        else:
            x = x_refs[0][...]
        o_ref[...] = _ffn_rows(x, g_ref[...], mod_ref, _mod_row(i, FFN_TILE), j,
                               stream.w1_s, stream.w2_s, before_chunk)

    @pl.when(i == 0)
    def _():
        stream.prime()
        run(stream.land)

    @pl.when(i > 0)
    def _():
        run(None)

    _cast_blocks(i, FFN_CAST_BLOCKS, side_in[:n_cast], side_out[:n_cast])
    if mod_blocks:
        @pl.when(i < mod_blocks)
        def _():
            side_out[n_cast][...] = _mod_block(*side_in[n_cast:])


def _ffn_call(xs, mod, norm_g, w1, w2, layer, which, casts=(), mod_job=None):
    step = lambda i: i
    if len(xs) == 2:
        x_specs = [_ctx_rows_spec(FFN_TILE, D_MODEL), _smp_rows_spec(FFN_TILE, D_MODEL)]
    else:
        x_specs = [_rows_spec(FFN_TILE, D_MODEL)]
    jobs = [mod_job] if mod_job else []
    out = pl.pallas_call(
        functools.partial(_ffn_kernel, j=6 * which, n_in=len(xs), n_cast=len(casts),
                          mod_blocks=mod_job.n_blocks if mod_job else 0, layer=layer, which=which),
        grid=(N_ALL // FFN_TILE,),
        in_specs=x_specs + [_mod_spec(mod), _gain_spec(layer, 2 * which)] + _weight_stream_specs()
        + [c.in_spec(step) for c in casts] + [s for m in jobs for s in m.in_specs()],
        out_specs=[_rows_spec(FFN_TILE, D_MODEL)] + [c.out_spec(step) for c in casts]
        + [m.out_spec() for m in jobs],
        out_shape=[jax.ShapeDtypeStruct((N_ALL, D_MODEL), F32)] + [c.out_shape() for c in casts]
        + [m.out_shape() for m in jobs],
        scratch_shapes=_weight_stream_scratch(),
        compiler_params=_params(),
        name="ffn",
    )(*xs, mod, norm_g, w1, w2, *[c.src for c in casts], *[a for m in jobs for a in m.args])
    return out[0], out[1:]


def _head_mean_sq(x, seg):
    sq = x * x
    hi = sq.astype(BF16)
    lo = (sq - hi.astype(F32)).astype(BF16)
    width = x.shape[-1]
    parts = []
    for c in range(0, width, SEG_WIDTH):
        n = min(SEG_WIDTH, width - c)
        parts.append(jnp.dot(hi[:, c:c + n], seg[:n, :n], preferred_element_type=F32)
                     + jnp.dot(lo[:, c:c + n], seg[:n, :n], preferred_element_type=F32))
    return parts[0] if len(parts) == 1 else jnp.concatenate(parts, axis=1)


def _rope(x, cos, sin_signed):
    width = x.shape[-1]
    lane = lax.broadcasted_iota(jnp.int32, (1, width), 1)
    partner = jnp.where((lane & 31) < 16,
                        pltpu.roll(x, width - 16, 1), pltpu.roll(x, 16, 1))
    return x * cos + partner * sin_signed


def _stack_heads_t(qt_ref, rows, kvh):
    first = kvh * HEADS_PER_KV
    return jnp.concatenate(
        [qt_ref[(first + g) * HEAD_DIM:(first + g + 1) * HEAD_DIM, rows] for g in range(HEADS_PER_KV)],
        axis=1)


ONES_ROWS = 16


def _attend_group(qt, k, v1t):
    st = jnp.dot(k, qt, preferred_element_type=F32)
    e = jnp.exp(st - jnp.max(st, axis=0, keepdims=True)).astype(BF16)
    ot = jnp.dot(v1t, e, preferred_element_type=F32)
    return ot[:HEAD_DIM] / ot[HEAD_DIM:HEAD_DIM + 1]


def _attend_rows(qt_s, rows, n, kv, mix_s):
    heads = []
    for kvh in range(N_KV_HEADS):
        k, v1t = kv(kvh)
        ot = _attend_group(_stack_heads_t(qt_s, rows, kvh), k, v1t)
        heads += [ot[:, g * n:(g + 1) * n] for g in range(HEADS_PER_KV)]
    mix_s[rows, :ATTN_WIDTH] = jnp.concatenate(heads, axis=0).T.astype(BF16)


def _with_ones(vt):
    return jnp.concatenate([vt, jnp.ones((ONES_ROWS, vt.shape[1]), BF16)], axis=0)


def _mixer_tile(ctx, r, x_ref, mod_ref, g_ref, w_ref, qg_ref, kg_ref, cw_ref, seg_ref, cos_ref, sin_ref,
                ck_ref, cvt_ref, wo_ref, o_ref, kt_ref, vt_ref, qt_s, k_s, vt_s, mix_s):
    seq_len = SEQ if ctx else DEC_SEQ
    x = x_ref[...]
    h = _modulate(x, g_ref[...], mod_ref, r, 3).astype(BF16)

    def proj(lo, width):
        return jnp.dot(h, w_ref[:, lo:lo + width], preferred_element_type=F32)

    q = proj(0, ATTN_WIDTH)
    kv = proj(ATTN_WIDTH, 2 * KV_WIDTH)
    k, v = kv[:, :KV_WIDTH], kv[:, KV_WIDTH:]
    q = q * lax.rsqrt(_head_mean_sq(q, seg_ref[...]) + EPS) * (qg_ref[...] * HEAD_DIM ** -0.5)
    k = k * lax.rsqrt(_head_mean_sq(k, seg_ref[...]) + EPS) * kg_ref[...]
    if ctx:
        qt_s[...] = q.T.astype(BF16)
        k_s[...] = k.astype(BF16)
        for b in range(SEQ_PER_TILE):
            vt = v[b * SEQ:(b + 1) * SEQ, :].T
            kt_ref[b] = k[b * SEQ:(b + 1) * SEQ, :].T
            vt_ref[b] = vt
            vt_s[:, b * SEQ:(b + 1) * SEQ] = vt.astype(BF16)
    else:
        qt_s[...] = _rope(q, cos_ref[...], sin_ref[...]).T.astype(BF16)
        k_s[...] = _rope(k, cos_ref[:, :KV_WIDTH], sin_ref[:, :KV_WIDTH]).astype(BF16)
        vt_s[...] = v.T.astype(BF16)

    base = ATTN_WIDTH + 2 * KV_WIDTH
    z = proj(base + CONV_WIDTH, CONV_WIDTH) * proj(base + 2 * CONV_WIDTH, CONV_WIDTH)
    pos = lax.broadcasted_iota(jnp.int32, (MIX_TILE, 1), 0) & (seq_len - 1)
    z_prev = jnp.where(pos == 0, 0.0, pltpu.roll(z, 1, 0))
    z_next = jnp.where(pos == seq_len - 1, 0.0, pltpu.roll(z, MIX_TILE - 1, 0))
    conv = z_prev * cw_ref[0:1, :] + z * cw_ref[1:2, :] + z_next * cw_ref[2:3, :]
    mix_s[:, ATTN_WIDTH:] = (proj(base, CONV_WIDTH) * conv).astype(BF16)

    def head(kvh):
        return slice(kvh * HEAD_DIM, (kvh + 1) * HEAD_DIM)

    if ctx:
        for b in range(SEQ_PER_TILE):
            rows = slice(b * SEQ, (b + 1) * SEQ)
            _attend_rows(qt_s, rows, SEQ,
                         lambda kvh: (k_s[rows, head(kvh)], _with_ones(vt_s[head(kvh), rows])), mix_s)
    else:
        kv = [(jnp.concatenate([ck_ref[:, head(kvh)].astype(BF16), k_s[:, head(kvh)]], axis=0),
               _with_ones(jnp.concatenate([cvt_ref[head(kvh), :].astype(BF16), vt_s[head(kvh), :]],
                                          axis=1)))
              for kvh in range(N_KV_HEADS)]
        for t in range(MIX_TILE // Q_ROWS):
            _attend_rows(qt_s, slice(t * Q_ROWS, (t + 1) * Q_ROWS), Q_ROWS, lambda kvh: kv[kvh], mix_s)

    out = (jnp.dot(mix_s[:, :ATTN_WIDTH], wo_ref[:ATTN_WIDTH, :], preferred_element_type=F32)
           + jnp.dot(mix_s[:, ATTN_WIDTH:], wo_ref[ATTN_WIDTH:, :], preferred_element_type=F32))
    o_ref[...] = x + _mod_vec(mod_ref, r, 5) * out


def _mixer_kernel(*refs):
    i = pl.program_id(0)
    r = _mod_row(i, MIX_TILE)
    is_ctx = i < _n_ctx_tiles(MIX_TILE)

    @pl.when(is_ctx)
    def _():
        _mixer_tile(True, r, *refs)

    @pl.when(jnp.logical_not(is_ctx))
    def _():
        _mixer_tile(False, r, *refs)


def _mixer_call(x, mod, norm_g, w_in, qg, kg, conv_w, seg, cos, sin, cache_k, cache_vt, w_out):
    first_smp = _n_ctx_tiles(MIX_TILE)
    smp_batch = lambda i: (jnp.maximum(i - first_smp, 0), 0, 0)
    kvt_spec = pl.BlockSpec((SEQ_PER_TILE, KV_WIDTH, SEQ), lambda i: (jnp.minimum(i, first_smp - 1), 0, 0))
    kvt_shape = jax.ShapeDtypeStruct((BATCH, KV_WIDTH, SEQ), F32)
    return pl.pallas_call(
        _mixer_kernel,
        grid=(N_ALL // MIX_TILE,),
        in_specs=[
            _rows_spec(MIX_TILE, D_MODEL), _mod_spec(mod), _gain_spec(0, 1),
            _resident_spec((D_MODEL, IN_WIDTH)),
            _resident_spec((1, ATTN_WIDTH)), _resident_spec((1, KV_WIDTH)),
            _resident_spec((3, CONV_WIDTH), 0),
            _resident_spec((SEG_WIDTH, SEG_WIDTH)),
            _resident_spec((DEC_SEQ, ATTN_WIDTH)), _resident_spec((DEC_SEQ, ATTN_WIDTH)),
            pl.BlockSpec((None, PAST_LEN, KV_WIDTH), smp_batch),
            pl.BlockSpec((None, KV_WIDTH, PAST_LEN), smp_batch),
            _resident_spec((MIX_WIDTH, D_MODEL)),
        ],
        out_specs=[_rows_spec(MIX_TILE, D_MODEL), kvt_spec, kvt_spec],
        out_shape=[jax.ShapeDtypeStruct((N_ALL, D_MODEL), F32), kvt_shape, kvt_shape],
        scratch_shapes=[pltpu.VMEM((ATTN_WIDTH, MIX_TILE), BF16),
                        pltpu.VMEM((MIX_TILE, KV_WIDTH), BF16),
                        pltpu.VMEM((KV_WIDTH, MIX_TILE), BF16),
                        pltpu.VMEM((MIX_TILE, MIX_WIDTH), BF16)],
        compiler_params=_params(),
        name="mixer",
    )(x, mod, norm_g, w_in, qg, kg, conv_w, seg, cos, sin, cache_k, cache_vt, w_out)


def _window_sums(h, band_ref, gi, seq_len):
    hi = h.astype(BF16)
    lo = (h - hi.astype(F32)).astype(BF16)
    per_seq = seq_len // POOL_BLOCK
    sums = []
    for b in range(h.shape[0] // POOL_BLOCK):
        def part(kind, src):
            rows = slice(src * POOL_BLOCK, (src + 1) * POOL_BLOCK)
            band = band_ref[gi, kind]
            return (jnp.dot(band, hi[rows], preferred_element_type=F32)
                    + jnp.dot(band, lo[rows], preferred_element_type=F32))
        total = part(0, b)
        if (b + 1) % per_seq:
            total = total + part(1, b + 1)
        if b % per_seq:
            total = total + part(2, b - 1)
        sums.append(total)
    return jnp.concatenate(sums, axis=0)


def _pool_rows(x, pos, seq_len, r, mod_ref, gm_ref, pw_ref, ps_ref, band_ref):
    gate = _mod_vec(mod_ref, r, 5)
    h_all = _modulate(x, gm_ref[...], mod_ref, r, 3)
    outs = []
    for gi, w in enumerate(POOL_WINDOWS):
        lanes = slice(gi * POOL_GROUP, (gi + 1) * POOL_GROUP)
        h = h_all[:, lanes]
        left = w // 2
        right = w - 1 - left
        count = jnp.minimum(pos + right + 1, seq_len) - jnp.maximum(pos - left, 0)
        diff = (_window_sums(h, band_ref, gi, seq_len) / count.astype(F32) - h).astype(BF16)
        out = jnp.dot(diff, pw_ref[lanes, :], preferred_element_type=F32)
        outs.append(x[:, lanes] + (gate[:, lanes] * ps_ref[:, lanes]) * out)
    return jnp.concatenate(outs, axis=1)


def _pool_tile(ctx, r, x_ref, mod_ref, gm_ref, gf_ref, gn_ref, pw_ref, ps_ref, band_ref, w1_ref, w2_ref,
               y_ref, x2_ref):
    seq_len = SEQ if ctx else DEC_SEQ
    block = max(seq_len, FFN_TILE)
    pos = lax.broadcasted_iota(jnp.int32, (block, 1), 0) & (seq_len - 1)
    for b in range(MIX_TILE // block):
        rows = slice(b * block, (b + 1) * block)
        x2_ref[rows, :] = _pool_rows(x_ref[rows, :], pos, seq_len, r, mod_ref, gm_ref, pw_ref, ps_ref,
                                     band_ref)

    def ffn_step(s, carry):
        rows = pl.ds(pl.multiple_of(s * FFN_TILE, FFN_TILE), FFN_TILE)
        y = _ffn_rows(x2_ref[rows, :], gf_ref[...], mod_ref, r, 6, w1_ref, w2_ref)
        y_ref[rows, :] = _rms(y, gn_ref[...])
        return carry

    lax.fori_loop(0, MIX_TILE // FFN_TILE, ffn_step, 0)


def _pool_kernel(*refs):
    ins, (yp_ref, ys_ref, x2_ref) = refs[:-3], refs[-3:]
    i = pl.program_id(0)
    r = _mod_row(i, MIX_TILE)
    is_ctx = i < _n_ctx_tiles(MIX_TILE)

    @pl.when(is_ctx)
    def _():
        _pool_tile(True, r, *ins, yp_ref, x2_ref)

    @pl.when(jnp.logical_not(is_ctx))
    def _():
        _pool_tile(False, r, *ins, ys_ref, x2_ref)


def _pool_call(x, mod, norm_g, final_g, pool_w, pool_scale, bands, w1, w2):
    return pl.pallas_call(
        _pool_kernel,
        grid=(N_ALL // MIX_TILE,),
        in_specs=[
            _rows_spec(MIX_TILE, D_MODEL), _mod_spec(mod), _gain_spec(1, 1), _gain_spec(1, 2),
            _resident_spec((1, D_MODEL)),
            _resident_spec((D_MODEL, POOL_GROUP)),
            _resident_spec((1, D_MODEL), 0),
            _resident_spec(bands.shape),
            _resident_spec((D_MODEL, 2 * D_FF)), _resident_spec((D_FF, D_MODEL)),
        ],
        out_specs=[_ctx_rows_spec(MIX_TILE, D_MODEL), _smp_rows_spec(MIX_TILE, D_MODEL)],
        out_shape=[jax.ShapeDtypeStruct((N_CTX, D_MODEL), F32),
                   jax.ShapeDtypeStruct((N_SMP, D_MODEL), F32)],
        scratch_shapes=[pltpu.VMEM((MIX_TILE, D_MODEL), F32)],
        compiler_params=_params(),
        name="pool_ffn_norm",
    )(x, mod, norm_g, norm_g, final_g, pool_w, pool_scale, bands, w1, w2)


def _rope_tables():
    t = np.arange(DEC_SEQ)
    half = HEAD_DIM // 2
    inv = ROPE_THETA ** (-np.arange(0, half, 2, dtype=np.float64) / half)
    ang_row = (t // GRID_W)[:, None] * inv[None, :]
    ang_col = (t % GRID_W)[:, None] * inv[None, :]
    cos = np.concatenate([np.cos(ang_row), np.cos(ang_row), np.cos(ang_col), np.cos(ang_col)], axis=1)
    sin = np.concatenate([-np.sin(ang_row), np.sin(ang_row), -np.sin(ang_col), np.sin(ang_col)], axis=1)
    return (jnp.asarray(np.tile(cos, (1, N_HEADS)), F32), jnp.asarray(np.tile(sin, (1, N_HEADS)), F32))


def _head_segments():
    head = np.arange(SEG_WIDTH) // HEAD_DIM
    return jnp.asarray((head[:, None] == head[None, :]) / HEAD_DIM, BF16)


def _pool_bands():
    t = np.arange(POOL_BLOCK)[:, None]
    bands = []
    for w in POOL_WINDOWS:
        left = w // 2
        right = w - 1 - left
        src = [np.arange(POOL_BLOCK)[None, :] + shift for shift in (0, POOL_BLOCK, -POOL_BLOCK)]
        bands.append(np.stack([(s >= t - left) & (s <= t + right) for s in src]))
    return jnp.asarray(np.stack(bands), BF16)


def _cache_layout(t):
    return jnp.transpose(t.reshape(BATCH, 1, N_KV_HEADS, HEAD_DIM, SEQ), (0, 1, 4, 2, 3))


def kernel(x_prompt, x_sample, c, cache_k, cache_v, c_ctx, ada_w, ada_b, norm_g, ffn_w1, ffn_w2,
           mix_w_in, mix_w_out, q_norm, k_norm, conv_w, pool_w, pool_scale, final_g):
    cvec = jnp.concatenate(
        [c_ctx[None, :], c, jnp.zeros((MOD_ROWS - 1 - DEC_BATCH, D_MODEL), F32)], axis=0)
    gains = norm_g.reshape(norm_g.shape[0], 3, 1, D_MODEL)
    cos, sin = _rope_tables()

    mod_first, _ = _mod_call(_ModJob(cvec, ada_w, ada_b, 0, MOD_STEPS, 3 * D_MODEL), [])
    x, (w_in, w_out, mod0) = _ffn_call(
        [x_prompt.reshape(N_CTX, D_MODEL), x_sample.reshape(N_SMP, D_MODEL)],
        mod_first, gains, ffn_w1, ffn_w2, 0, 0,
        [_Cast(mix_w_in, (0,), FFN_CAST_BLOCKS), _Cast(mix_w_out, (0,), FFN_CAST_BLOCKS)],
        _ModJob(cvec, ada_w, ada_b, 0, MOD_SIDE_BLOCKS))
    x, kt, vt = _mixer_call(
        x, mod0, gains, w_in, jnp.tile(q_norm[0], N_HEADS)[None, :],
        jnp.tile(k_norm[0], N_KV_HEADS)[None, :], conv_w, _head_segments(), cos, sin,
        cache_k[:, 0].reshape(DEC_BATCH, PAST_LEN, KV_WIDTH),
        jnp.transpose(cache_v[:, 0], (0, 2, 3, 1)).reshape(DEC_BATCH, KV_WIDTH, PAST_LEN), w_out)
    x, (pw, mod1) = _ffn_call(
        [x], mod0, gains, ffn_w1, ffn_w2, 0, 1,
        [_Cast(pool_w.reshape(pool_w.shape[0], D_MODEL, POOL_GROUP), (0,), FFN_CAST_BLOCKS)],
        _ModJob(cvec, ada_w, ada_b, 1, MOD_SIDE_BLOCKS))
    x, (w1_last, w2_last) = _ffn_call(
        [x], mod1, gains, ffn_w1, ffn_w2, 1, 0,
        [_Cast(ffn_w1, (1, 1), FFN_CAST_BLOCKS), _Cast(ffn_w2, (1, 1), FFN_CAST_BLOCKS)])
    yp, ys = _pool_call(x, mod1, gains, final_g[None, :], pw,
                        pool_scale.reshape(pool_scale.shape[0], 1, D_MODEL), _pool_bands(), w1_last, w2_last)

    return (yp.reshape(BATCH, SEQ, D_MODEL), ys.reshape(DEC_BATCH, DEC_SEQ, D_MODEL),
            _cache_layout(kt), _cache_layout(vt))
```

```python
import functools

import numpy as np
import jax
import jax.numpy as jnp
from jax import lax
from jax.experimental import pallas as pl
from jax.experimental.pallas import tpu as pltpu

F32 = jnp.float32
BF16 = jnp.bfloat16

D_MODEL = 1024
BATCH = 32
SEQ = 256
DEC_BATCH = 2
DEC_SEQ = 1024
PAST_LEN = 512
GRID_W = 64
N_HEADS = 8
N_KV_HEADS = 2
HEAD_DIM = 64
HEADS_PER_KV = N_HEADS // N_KV_HEADS
ATTN_WIDTH = N_HEADS * HEAD_DIM
KV_WIDTH = N_KV_HEADS * HEAD_DIM
CONV_WIDTH = D_MODEL // 2
MIX_WIDTH = ATTN_WIDTH + CONV_WIDTH
IN_WIDTH = ATTN_WIDTH + 2 * KV_WIDTH + 3 * CONV_WIDTH
D_FF = 2816
POOL_WINDOWS = (2, 4, 8, 16)
POOL_GROUP = D_MODEL // len(POOL_WINDOWS)
N_MOD = 9
ROPE_THETA = 10000.0
EPS = 1e-6

N_CTX = BATCH * SEQ
N_SMP = DEC_BATCH * DEC_SEQ
N_ALL = N_CTX + N_SMP
MIX_TILE = 1024
FFN_TILE = 512
FF_CHUNK = 256
N_FF_CHUNKS = D_FF // FF_CHUNK
STAGE_SLOTS = 2
Q_ROWS = 256
SEG_WIDTH = 256
POOL_BLOCK = 256
SEQ_PER_TILE = MIX_TILE // SEQ
MOD_ROWS = 8
MOD_WIDTH = N_MOD * D_MODEL
MOD_STEPS = 2
MOD_SIDE_BLOCKS = 18
FFN_CAST_BLOCKS = 16
VMEM_LIMIT = 56 * 1024 * 1024


def _n_ctx_tiles(tile):
    return N_CTX // tile


def _mod_row(i, tile):
    first = _n_ctx_tiles(tile)
    return jnp.where(i < first, 0, 1 + (i - first) * tile // DEC_SEQ)


def _rms(x, g):
    return x * lax.rsqrt(jnp.mean(x * x, axis=-1, keepdims=True) + EPS) * g


def _mod_vec(mod_ref, r, j):
    return mod_ref[pl.ds(r, 1), j * D_MODEL:(j + 1) * D_MODEL]


def _modulate(x, g, mod_ref, r, j):
    gain = g * (1 + _mod_vec(mod_ref, r, j + 1))
    return x * lax.rsqrt(jnp.mean(x * x, axis=-1, keepdims=True) + EPS) * gain + _mod_vec(mod_ref, r, j)


def _ffn_rows(x, g, mod_ref, r, j, w1_ref, w2_ref, before_chunk=None):
    h = _modulate(x, g, mod_ref, r, j).astype(BF16)
    acc = None
    for c in range(N_FF_CHUNKS):
        lo = c * FF_CHUNK
        if before_chunk is not None:
            before_chunk(c)
        gate = jnp.dot(h, w1_ref[:, lo:lo + FF_CHUNK], preferred_element_type=F32)
        up = jnp.dot(h, w1_ref[:, D_FF + lo:D_FF + lo + FF_CHUNK], preferred_element_type=F32)
        act = (gate / (1 + jnp.exp(-gate)) * up).astype(BF16)
        y = jnp.dot(act, w2_ref[lo:lo + FF_CHUNK, :], preferred_element_type=F32)
        acc = y if acc is None else acc + y
    return x + (0.5 * _mod_vec(mod_ref, r, j + 2)) * acc


class _Cast:
    def __init__(self, src, lead, n_blocks):
        self.src, self.lead, self.n_blocks = src, tuple(lead), n_blocks
        self.rows, self.cols = src.shape[len(lead):]
        self.block_rows = self.rows // n_blocks

    def in_spec(self, step):
        lead, last = self.lead, self.n_blocks - 1
        return pl.BlockSpec((None,) * len(lead) + (self.block_rows, self.cols),
                            lambda *g: lead + (jnp.minimum(step(*g), last), 0))

    def out_spec(self, step):
        last = self.n_blocks - 1
        return pl.BlockSpec((self.block_rows, self.cols), lambda *g: (jnp.minimum(step(*g), last), 0))

    def out_shape(self):
        return jax.ShapeDtypeStruct((self.rows, self.cols), BF16)


def _cast_blocks(step, n_blocks, srcs, dsts):
    @pl.when(step < n_blocks)
    def _():
        for s, d in zip(srcs, dsts):
            d[...] = s[...].astype(BF16)


class _WeightStream:
    def __init__(self, w1_hbm, w2_hbm, layer, which, w1_s, w2_s, gate_st, up_st, down_st, sems):
        self.w1_hbm, self.w2_hbm, self.lead = w1_hbm, w2_hbm, (layer, which)
        self.w1_s, self.w2_s = w1_s, w2_s
        self.stages, self.sems = (gate_st, up_st, down_st), sems

    def _copies(self, c):
        slot, lo = c % STAGE_SLOTS, c * FF_CHUNK
        l, w = self.lead
        srcs = (self.w1_hbm.at[l, w, :, pl.ds(lo, FF_CHUNK)],
                self.w1_hbm.at[l, w, :, pl.ds(D_FF + lo, FF_CHUNK)],
                self.w2_hbm.at[l, w, pl.ds(lo, FF_CHUNK), :])
        return [pltpu.make_async_copy(src, st.at[slot], self.sems.at[k, slot])
                for k, (src, st) in enumerate(zip(srcs, self.stages))]

    def start(self, c):
        for cp in self._copies(c):
            cp.start()

    def prime(self):
        for c in range(STAGE_SLOTS):
            self.start(c)

    def land(self, c):
        for cp in self._copies(c):
            cp.wait()
        slot, lo = c % STAGE_SLOTS, c * FF_CHUNK
        gate_st, up_st, down_st = self.stages
        self.w1_s[:, lo:lo + FF_CHUNK] = gate_st[slot].astype(BF16)
        self.w1_s[:, D_FF + lo:D_FF + lo + FF_CHUNK] = up_st[slot].astype(BF16)
        self.w2_s[lo:lo + FF_CHUNK, :] = down_st[slot].astype(BF16)
        if c + STAGE_SLOTS < N_FF_CHUNKS:
            self.start(c + STAGE_SLOTS)


def _weight_stream_specs():
    return [pl.BlockSpec(memory_space=pl.ANY), pl.BlockSpec(memory_space=pl.ANY)]


N_STREAM_SCRATCH = 6


def _weight_stream_scratch():
    return [pltpu.VMEM((D_MODEL, 2 * D_FF), BF16), pltpu.VMEM((D_FF, D_MODEL), BF16),
            pltpu.VMEM((STAGE_SLOTS, D_MODEL, FF_CHUNK), F32),
            pltpu.VMEM((STAGE_SLOTS, D_MODEL, FF_CHUNK), F32),
            pltpu.VMEM((STAGE_SLOTS, FF_CHUNK, D_MODEL), F32),
            pltpu.SemaphoreType.DMA((3, STAGE_SLOTS))]


def _mod_block(c_ref, w_ref, b_ref):
    c = c_ref[...]
    s = (c / (1 + jnp.exp(-c))).astype(BF16)
    return jnp.dot(s, w_ref[...].astype(BF16), preferred_element_type=F32) + b_ref[...]


class _ModJob:
    def __init__(self, cvec, ada_w, ada_b, layer, n_blocks, width=MOD_WIDTH):
        self.args = (cvec, ada_w, ada_b.reshape(ada_b.shape[0], 1, MOD_WIDTH))
        self.layer, self.n_blocks, self.width, self.cols = layer, n_blocks, width, width // n_blocks

    def in_specs(self):
        layer, last, cols = self.layer, self.n_blocks - 1, self.cols
        block = lambda i: (layer, 0, jnp.minimum(i, last))
        return [pl.BlockSpec((MOD_ROWS, D_MODEL), lambda i: (0, 0)),
                pl.BlockSpec((None, D_MODEL, cols), block),
                pl.BlockSpec((None, 1, cols), block)]

    def out_spec(self):
        last = self.n_blocks - 1
        return pl.BlockSpec((MOD_ROWS, self.cols), lambda i: (0, jnp.minimum(i, last)))

    def out_shape(self):
        return jax.ShapeDtypeStruct((MOD_ROWS, self.width), F32)


def _mod_kernel(c_ref, w_ref, b_ref, *refs, n_cast):
    srcs, o_ref, dsts = refs[:n_cast], refs[n_cast], refs[n_cast + 1:]
    o_ref[...] = _mod_block(c_ref, w_ref, b_ref)
    _cast_blocks(pl.program_id(0), pl.num_programs(0), srcs, dsts)


def _mod_call(job, casts):
    step = lambda i: i
    out = pl.pallas_call(
        functools.partial(_mod_kernel, n_cast=len(casts)),
        grid=(job.n_blocks,),
        in_specs=job.in_specs() + [c.in_spec(step) for c in casts],
        out_specs=[job.out_spec()] + [c.out_spec(step) for c in casts],
        out_shape=[job.out_shape()] + [c.out_shape() for c in casts],
        compiler_params=_params(),
        name="adaln_mod",
    )(*job.args, *[c.src for c in casts])
    return out[0], out[1:]


def _rows_spec(tile, width):
    return pl.BlockSpec((tile, width), lambda i: (i, 0))


def _ctx_rows_spec(tile, width):
    last = _n_ctx_tiles(tile) - 1
    return pl.BlockSpec((tile, width), lambda i: (jnp.minimum(i, last), 0))


def _smp_rows_spec(tile, width):
    first = _n_ctx_tiles(tile)
    return pl.BlockSpec((tile, width), lambda i: (jnp.maximum(i - first, 0), 0))


def _mod_spec(mod):
    return pl.BlockSpec(mod.shape, lambda i: (0, 0))


def _gain_spec(layer, j):
    return pl.BlockSpec((None, None, 1, D_MODEL), lambda i: (layer, j, 0, 0))


def _resident_spec(shape, *lead):
    index = tuple(lead) + (0,) * len(shape)
    return pl.BlockSpec((None,) * len(lead) + tuple(shape), lambda i: index,
                        pipeline_mode=pl.Buffered(1))


def _params():
    return pltpu.CompilerParams(dimension_semantics=("arbitrary",), vmem_limit_bytes=VMEM_LIMIT)


def _ffn_kernel(*refs, j, n_in, n_cast, mod_blocks, layer, which):
    n_side = n_cast + (3 if mod_blocks else 0)
    x_refs = refs[:n_in]
    mod_ref, g_ref, w1_hbm, w2_hbm = refs[n_in:n_in + 4]
    side_in = refs[n_in + 4:n_in + 4 + n_side]
    o_ref = refs[n_in + 4 + n_side]
    side_out = refs[n_in + 5 + n_side:len(refs) - N_STREAM_SCRATCH]
    stream = _WeightStream(w1_hbm, w2_hbm, layer, which, *refs[len(refs) - N_STREAM_SCRATCH:])
    i = pl.program_id(0)

    def run(before_chunk):
        if n_in == 2:
            x = jnp.where(i < _n_ctx_tiles(FFN_TILE), x_refs[0][...], x_refs[1][...])
        else:
            x = x_refs[0][...]
        o_ref[...] = _ffn_rows(x, g_ref[...], mod_ref, _mod_row(i, FFN_TILE), j,
                               stream.w1_s, stream.w2_s, before_chunk)

    @pl.when(i == 0)
    def _():
        stream.prime()
        run(stream.land)

    @pl.when(i > 0)
    def _():
        run(None)

    _cast_blocks(i, FFN_CAST_BLOCKS, side_in[:n_cast], side_out[:n_cast])
    if mod_blocks:
        @pl.when(i < mod_blocks)
        def _():
            side_out[n_cast][...] = _mod_block(*side_in[n_cast:])


def _ffn_call(xs, mod, norm_g, w1, w2, layer, which, casts=(), mod_job=None):
    step = lambda i: i
    if len(xs) == 2:
        x_specs = [_ctx_rows_spec(FFN_TILE, D_MODEL), _smp_rows_spec(FFN_TILE, D_MODEL)]
    else:
        x_specs = [_rows_spec(FFN_TILE, D_MODEL)]
    jobs = [mod_job] if mod_job else []
    out = pl.pallas_call(
        functools.partial(_ffn_kernel, j=6 * which, n_in=len(xs), n_cast=len(casts),
                          mod_blocks=mod_job.n_blocks if mod_job else 0, layer=layer, which=which),
        grid=(N_ALL // FFN_TILE,),
        in_specs=x_specs + [_mod_spec(mod), _gain_spec(layer, 2 * which)] + _weight_stream_specs()
        + [c.in_spec(step) for c in casts] + [s for m in jobs for s in m.in_specs()],
        out_specs=[_rows_spec(FFN_TILE, D_MODEL)] + [c.out_spec(step) for c in casts]
        + [m.out_spec() for m in jobs],
        out_shape=[jax.ShapeDtypeStruct((N_ALL, D_MODEL), F32)] + [c.out_shape() for c in casts]
        + [m.out_shape() for m in jobs],
        scratch_shapes=_weight_stream_scratch(),
        compiler_params=_params(),
        name="ffn",
    )(*xs, mod, norm_g, w1, w2, *[c.src for c in casts], *[a for m in jobs for a in m.args])
    return out[0], out[1:]


def _head_mean_sq(x, seg):
    sq = x * x
    hi = sq.astype(BF16)
    lo = (sq - hi.astype(F32)).astype(BF16)
    width = x.shape[-1]
    parts = []
    for c in range(0, width, SEG_WIDTH):
        n = min(SEG_WIDTH, width - c)
        parts.append(jnp.dot(hi[:, c:c + n], seg[:n, :n], preferred_element_type=F32)
                     + jnp.dot(lo[:, c:c + n], seg[:n, :n], preferred_element_type=F32))
    return parts[0] if len(parts) == 1 else jnp.concatenate(parts, axis=1)


def _rope(x, cos, sin_signed):
    width = x.shape[-1]
    lane = lax.broadcasted_iota(jnp.int32, (1, width), 1)
    partner = jnp.where((lane & 31) < 16,
                        pltpu.roll(x, width - 16, 1), pltpu.roll(x, 16, 1))
    return x * cos + partner * sin_signed


def _stack_heads_t(qt_ref, rows, kvh):
    first = kvh * HEADS_PER_KV
    return jnp.concatenate(
        [qt_ref[(first + g) * HEAD_DIM:(first + g + 1) * HEAD_DIM, rows] for g in range(HEADS_PER_KV)],
        axis=1)


ONES_ROWS = 16


def _attend_group(qt, k, v1t):
    st = jnp.dot(k, qt, preferred_element_type=F32)
    e = jnp.exp(st - jnp.max(st, axis=0, keepdims=True)).astype(BF16)
    ot = jnp.dot(v1t, e, preferred_element_type=F32)
    return ot[:HEAD_DIM] / ot[HEAD_DIM:HEAD_DIM + 1]


def _attend_rows(qt_s, rows, n, kv, mix_s):
    heads = []
    for kvh in range(N_KV_HEADS):
        k, v1t = kv(kvh)
        ot = _attend_group(_stack_heads_t(qt_s, rows, kvh), k, v1t)
        heads += [ot[:, g * n:(g + 1) * n] for g in range(HEADS_PER_KV)]
    mix_s[rows, :ATTN_WIDTH] = jnp.concatenate(heads, axis=0).T.astype(BF16)


def _with_ones(vt):
    return jnp.concatenate([vt, jnp.ones((ONES_ROWS, vt.shape[1]), BF16)], axis=0)


def _mixer_tile(ctx, r, x_ref, mod_ref, g_ref, w_ref, qg_ref, kg_ref, cw_ref, seg_ref, cos_ref, sin_ref,
                ck_ref, cvt_ref, wo_ref, o_ref, kt_ref, vt_ref, qt_s, k_s, vt_s, mix_s):
    seq_len = SEQ if ctx else DEC_SEQ
    x = x_ref[...]
    h = _modulate(x, g_ref[...], mod_ref, r, 3).astype(BF16)

    def proj(lo, width):
        return jnp.dot(h, w_ref[:, lo:lo + width], preferred_element_type=F32)

    q = proj(0, ATTN_WIDTH)
    kv = proj(ATTN_WIDTH, 2 * KV_WIDTH)
    k, v = kv[:, :KV_WIDTH], kv[:, KV_WIDTH:]
    q = q * lax.rsqrt(_head_mean_sq(q, seg_ref[...]) + EPS) * (qg_ref[...] * HEAD_DIM ** -0.5)
    k = k * lax.rsqrt(_head_mean_sq(k, seg_ref[...]) + EPS) * kg_ref[...]
    if ctx:
        qt_s[...] = q.T.astype(BF16)
        k_s[...] = k.astype(BF16)
        for b in range(SEQ_PER_TILE):
            vt = v[b * SEQ:(b + 1) * SEQ, :].T
            kt_ref[b] = k[b * SEQ:(b + 1) * SEQ, :].T
            vt_ref[b] = vt
            vt_s[:, b * SEQ:(b + 1) * SEQ] = vt.astype(BF16)
    else:
        qt_s[...] = _rope(q, cos_ref[...], sin_ref[...]).T.astype(BF16)
        k_s[...] = _rope(k, cos_ref[:, :KV_WIDTH], sin_ref[:, :KV_WIDTH]).astype(BF16)
        vt_s[...] = v.T.astype(BF16)

    base = ATTN_WIDTH + 2 * KV_WIDTH
    z = proj(base + CONV_WIDTH, CONV_WIDTH) * proj(base + 2 * CONV_WIDTH, CONV_WIDTH)
    pos = lax.broadcasted_iota(jnp.int32, (MIX_TILE, 1), 0) & (seq_len - 1)
    z_prev = jnp.where(pos == 0, 0.0, pltpu.roll(z, 1, 0))
    z_next = jnp.where(pos == seq_len - 1, 0.0, pltpu.roll(z, MIX_TILE - 1, 0))
    conv = z_prev * cw_ref[0:1, :] + z * cw_ref[1:2, :] + z_next * cw_ref[2:3, :]
    mix_s[:, ATTN_WIDTH:] = (proj(base, CONV_WIDTH) * conv).astype(BF16)

    def head(kvh):
        return slice(kvh * HEAD_DIM, (kvh + 1) * HEAD_DIM)

    if ctx:
        for b in range(SEQ_PER_TILE):
            rows = slice(b * SEQ, (b + 1) * SEQ)
            _attend_rows(qt_s, rows, SEQ,
                         lambda kvh: (k_s[rows, head(kvh)], _with_ones(vt_s[head(kvh), rows])), mix_s)
    else:
        kv = [(jnp.concatenate([ck_ref[:, head(kvh)].astype(BF16), k_s[:, head(kvh)]], axis=0),
               _with_ones(jnp.concatenate([cvt_ref[head(kvh), :].astype(BF16), vt_s[head(kvh), :]],
                                          axis=1)))
              for kvh in range(N_KV_HEADS)]
        for t in range(MIX_TILE // Q_ROWS):
            _attend_rows(qt_s, slice(t * Q_ROWS, (t + 1) * Q_ROWS), Q_ROWS, lambda kvh: kv[kvh], mix_s)

    out = (jnp.dot(mix_s[:, :ATTN_WIDTH], wo_ref[:ATTN_WIDTH, :], preferred_element_type=F32)
           + jnp.dot(mix_s[:, ATTN_WIDTH:], wo_ref[ATTN_WIDTH:, :], preferred_element_type=F32))
    o_ref[...] = x + _mod_vec(mod_ref, r, 5) * out


def _mixer_kernel(*refs):
    i = pl.program_id(0)
    r = _mod_row(i, MIX_TILE)
    is_ctx = i < _n_ctx_tiles(MIX_TILE)

    @pl.when(is_ctx)
    def _():
        _mixer_tile(True, r, *refs)

    @pl.when(jnp.logical_not(is_ctx))
    def _():
        _mixer_tile(False, r, *refs)


def _mixer_call(x, mod, norm_g, w_in, qg, kg, conv_w, seg, cos, sin, cache_k, cache_vt, w_out):
    first_smp = _n_ctx_tiles(MIX_TILE)
    smp_batch = lambda i: (jnp.maximum(i - first_smp, 0), 0, 0)
    kvt_spec = pl.BlockSpec((SEQ_PER_TILE, KV_WIDTH, SEQ), lambda i: (jnp.minimum(i, first_smp - 1), 0, 0))
    kvt_shape = jax.ShapeDtypeStruct((BATCH, KV_WIDTH, SEQ), F32)
    return pl.pallas_call(
        _mixer_kernel,
        grid=(N_ALL // MIX_TILE,),
        in_specs=[
            _rows_spec(MIX_TILE, D_MODEL), _mod_spec(mod), _gain_spec(0, 1),
            _resident_spec((D_MODEL, IN_WIDTH)),
            _resident_spec((1, ATTN_WIDTH)), _resident_spec((1, KV_WIDTH)),
            _resident_spec((3, CONV_WIDTH), 0),
            _resident_spec((SEG_WIDTH, SEG_WIDTH)),
            _resident_spec((DEC_SEQ, ATTN_WIDTH)), _resident_spec((DEC_SEQ, ATTN_WIDTH)),
            pl.BlockSpec((None, PAST_LEN, KV_WIDTH), smp_batch),
            pl.BlockSpec((None, KV_WIDTH, PAST_LEN), smp_batch),
            _resident_spec((MIX_WIDTH, D_MODEL)),
        ],
        out_specs=[_rows_spec(MIX_TILE, D_MODEL), kvt_spec, kvt_spec],
        out_shape=[jax.ShapeDtypeStruct((N_ALL, D_MODEL), F32), kvt_shape, kvt_shape],
        scratch_shapes=[pltpu.VMEM((ATTN_WIDTH, MIX_TILE), BF16),
                        pltpu.VMEM((MIX_TILE, KV_WIDTH), BF16),
                        pltpu.VMEM((KV_WIDTH, MIX_TILE), BF16),
                        pltpu.VMEM((MIX_TILE, MIX_WIDTH), BF16)],
        compiler_params=_params(),
        name="mixer",
    )(x, mod, norm_g, w_in, qg, kg, conv_w, seg, cos, sin, cache_k, cache_vt, w_out)


def _window_sums(h, band_ref, gi, seq_len):
    hi = h.astype(BF16)
    lo = (h - hi.astype(F32)).astype(BF16)
    per_seq = seq_len // POOL_BLOCK
    sums = []
    for b in range(h.shape[0] // POOL_BLOCK):
        def part(kind, src):
            rows = slice(src * POOL_BLOCK, (src + 1) * POOL_BLOCK)
            band = band_ref[gi, kind]
            return (jnp.dot(band, hi[rows], preferred_element_type=F32)
                    + jnp.dot(band, lo[rows], preferred_element_type=F32))
        total = part(0, b)
        if (b + 1) % per_seq:
            total = total + part(1, b + 1)
        if b % per_seq:
            total = total + part(2, b - 1)
        sums.append(total)
    return jnp.concatenate(sums, axis=0)


def _pool_rows(x, pos, seq_len, r, mod_ref, gm_ref, pw_ref, ps_ref, band_ref):
    gate = _mod_vec(mod_ref, r, 5)
    inv = lax.rsqrt(jnp.mean(x * x, axis=-1, keepdims=True) + EPS)
    gain = gm_ref[...] * (1 + _mod_vec(mod_ref, r, 4))
    shift = _mod_vec(mod_ref, r, 3)
    outs = []
    for gi, w in enumerate(POOL_WINDOWS):
        lanes = slice(gi * POOL_GROUP, (gi + 1) * POOL_GROUP)
        h = x[:, lanes] * inv * gain[:, lanes] + shift[:, lanes]
        left = w // 2
        right = w - 1 - left
        count = jnp.minimum(pos + right + 1, seq_len) - jnp.maximum(pos - left, 0)
        diff = (_window_sums(h, band_ref, gi, seq_len) / count.astype(F32) - h).astype(BF16)
        out = jnp.dot(diff, pw_ref[lanes, :], preferred_element_type=F32)
        outs.append(x[:, lanes] + (gate[:, lanes] * ps_ref[:, lanes]) * out)
    return jnp.concatenate(outs, axis=1)


def _pool_tile(ctx, r, x_ref, mod_ref, gm_ref, gf_ref, gn_ref, pw_ref, ps_ref, band_ref, w1_ref, w2_ref,
               y_ref, x2_ref):
    seq_len = SEQ if ctx else DEC_SEQ
    block = max(seq_len, FFN_TILE)
    pos = lax.broadcasted_iota(jnp.int32, (block, 1), 0) & (seq_len - 1)
    for b in range(MIX_TILE // block):
        rows = slice(b * block, (b + 1) * block)
        x2_ref[rows, :] = _pool_rows(x_ref[rows, :], pos, seq_len, r, mod_ref, gm_ref, pw_ref, ps_ref,
                                     band_ref)

    def ffn_step(s, carry):
        rows = pl.ds(pl.multiple_of(s * FFN_TILE, FFN_TILE), FFN_TILE)
        y = _ffn_rows(x2_ref[rows, :], gf_ref[...], mod_ref, r, 6, w1_ref, w2_ref)
        y_ref[rows, :] = _rms(y, gn_ref[...])
        return carry

    lax.fori_loop(0, MIX_TILE // FFN_TILE, ffn_step, 0)


def _pool_kernel(*refs):
    ins, (yp_ref, ys_ref, x2_ref) = refs[:-3], refs[-3:]
    i = pl.program_id(0)
    r = _mod_row(i, MIX_TILE)
    is_ctx = i < _n_ctx_tiles(MIX_TILE)

    @pl.when(is_ctx)
    def _():
        _pool_tile(True, r, *ins, yp_ref, x2_ref)

    @pl.when(jnp.logical_not(is_ctx))
    def _():
        _pool_tile(False, r, *ins, ys_ref, x2_ref)


def _pool_call(x, mod, norm_g, final_g, pool_w, pool_scale, bands, w1, w2):
    return pl.pallas_call(
        _pool_kernel,
        grid=(N_ALL // MIX_TILE,),
        in_specs=[
            _rows_spec(MIX_TILE, D_MODEL), _mod_spec(mod), _gain_spec(1, 1), _gain_spec(1, 2),
            _resident_spec((1, D_MODEL)),
            _resident_spec((D_MODEL, POOL_GROUP)),
            _resident_spec((1, D_MODEL), 0),
            _resident_spec(bands.shape),
            _resident_spec((D_MODEL, 2 * D_FF)), _resident_spec((D_FF, D_MODEL)),
        ],
        out_specs=[_ctx_rows_spec(MIX_TILE, D_MODEL), _smp_rows_spec(MIX_TILE, D_MODEL)],
        out_shape=[jax.ShapeDtypeStruct((N_CTX, D_MODEL), F32),
                   jax.ShapeDtypeStruct((N_SMP, D_MODEL), F32)],
        scratch_shapes=[pltpu.VMEM((MIX_TILE, D_MODEL), F32)],
        compiler_params=_params(),
        name="pool_ffn_norm",
    )(x, mod, norm_g, norm_g, final_g, pool_w, pool_scale, bands, w1, w2)


def _rope_tables():
    t = np.arange(DEC_SEQ)
    half = HEAD_DIM // 2
    inv = ROPE_THETA ** (-np.arange(0, half, 2, dtype=np.float64) / half)
    ang_row = (t // GRID_W)[:, None] * inv[None, :]
    ang_col = (t % GRID_W)[:, None] * inv[None, :]
    cos = np.concatenate([np.cos(ang_row), np.cos(ang_row), np.cos(ang_col), np.cos(ang_col)], axis=1)
    sin = np.concatenate([-np.sin(ang_row), np.sin(ang_row), -np.sin(ang_col), np.sin(ang_col)], axis=1)
    return (jnp.asarray(np.tile(cos, (1, N_HEADS)), F32), jnp.asarray(np.tile(sin, (1, N_HEADS)), F32))


def _head_segments():
    head = np.arange(SEG_WIDTH) // HEAD_DIM
    return jnp.asarray((head[:, None] == head[None, :]) / HEAD_DIM, BF16)


def _pool_bands():
    t = np.arange(POOL_BLOCK)[:, None]
    bands = []
    for w in POOL_WINDOWS:
        left = w // 2
        right = w - 1 - left
        src = [np.arange(POOL_BLOCK)[None, :] + shift for shift in (0, POOL_BLOCK, -POOL_BLOCK)]
        bands.append(np.stack([(s >= t - left) & (s <= t + right) for s in src]))
    return jnp.asarray(np.stack(bands), BF16)


def _cache_layout(t):
    return jnp.transpose(t.reshape(BATCH, 1, N_KV_HEADS, HEAD_DIM, SEQ), (0, 1, 4, 2, 3))


def kernel(x_prompt, x_sample, c, cache_k, cache_v, c_ctx, ada_w, ada_b, norm_g, ffn_w1, ffn_w2,
           mix_w_in, mix_w_out, q_norm, k_norm, conv_w, pool_w, pool_scale, final_g):
    cvec = jnp.concatenate(
        [c_ctx[None, :], c, jnp.zeros((MOD_ROWS - 1 - DEC_BATCH, D_MODEL), F32)], axis=0)
    gains = norm_g.reshape(norm_g.shape[0], 3, 1, D_MODEL)
    cos, sin = _rope_tables()

    mod_first, _ = _mod_call(_ModJob(cvec, ada_w, ada_b, 0, MOD_STEPS, 3 * D_MODEL), [])
    x, (w_in, w_out, mod0) = _ffn_call(
        [x_prompt.reshape(N_CTX, D_MODEL), x_sample.reshape(N_SMP, D_MODEL)],
        mod_first, gains, ffn_w1, ffn_w2, 0, 0,
        [_Cast(mix_w_in, (0,), FFN_CAST_BLOCKS), _Cast(mix_w_out, (0,), FFN_CAST_BLOCKS)],
        _ModJob(cvec, ada_w, ada_b, 0, MOD_SIDE_BLOCKS))
    x, kt, vt = _mixer_call(
        x, mod0, gains, w_in, jnp.tile(q_norm[0], N_HEADS)[None, :],
        jnp.tile(k_norm[0], N_KV_HEADS)[None, :], conv_w, _head_segments(), cos, sin,
        cache_k[:, 0].reshape(DEC_BATCH, PAST_LEN, KV_WIDTH),
        jnp.transpose(cache_v[:, 0], (0, 2, 3, 1)).reshape(DEC_BATCH, KV_WIDTH, PAST_LEN), w_out)
    x, (pw, mod1) = _ffn_call(
        [x], mod0, gains, ffn_w1, ffn_w2, 0, 1,
        [_Cast(pool_w.reshape(pool_w.shape[0], D_MODEL, POOL_GROUP), (0,), FFN_CAST_BLOCKS)],
        _ModJob(cvec, ada_w, ada_b, 1, MOD_SIDE_BLOCKS))
    x, (w1_last, w2_last) = _ffn_call(
        [x], mod1, gains, ffn_w1, ffn_w2, 1, 0,
        [_Cast(ffn_w1, (1, 1), FFN_CAST_BLOCKS), _Cast(ffn_w2, (1, 1), FFN_CAST_BLOCKS)])
    yp, ys = _pool_call(x, mod1, gains, final_g[None, :], pw,
                        pool_scale.reshape(pool_scale.shape[0], 1, D_MODEL), _pool_bands(), w1_last, w2_last)

    return (yp.reshape(BATCH, SEQ, D_MODEL), ys.reshape(DEC_BATCH, DEC_SEQ, D_MODEL),
            _cache_layout(kt), _cache_layout(vt))
```

```python
import functools

import numpy as np
import jax
import jax.numpy as jnp
from jax import lax
from jax.experimental import pallas as pl
from jax.experimental.pallas import tpu as pltpu

F32 = jnp.float32
BF16 = jnp.bfloat16

D_MODEL = 1024
BATCH = 32
SEQ = 256
DEC_BATCH = 2
DEC_SEQ = 1024
PAST_LEN = 512
GRID_W = 64
N_HEADS = 8
N_KV_HEADS = 2
HEAD_DIM = 64
HEADS_PER_KV = N_HEADS // N_KV_HEADS
ATTN_WIDTH = N_HEADS * HEAD_DIM
KV_WIDTH = N_KV_HEADS * HEAD_DIM
CONV_WIDTH = D_MODEL // 2
MIX_WIDTH = ATTN_WIDTH + CONV_WIDTH
IN_WIDTH = ATTN_WIDTH + 2 * KV_WIDTH + 3 * CONV_WIDTH
D_FF = 2816
POOL_WINDOWS = (2, 4, 8, 16)
POOL_GROUP = D_MODEL // len(POOL_WINDOWS)
N_MOD = 9
ROPE_THETA = 10000.0
EPS = 1e-6

N_CTX = BATCH * SEQ
N_SMP = DEC_BATCH * DEC_SEQ
N_ALL = N_CTX + N_SMP
MIX_TILE = 1024
FFN_TILE = 512
FF_CHUNK = 256
N_FF_CHUNKS = D_FF // FF_CHUNK
STAGE_SLOTS = 2
Q_ROWS = 256
CTX_HEADS_PER_GROUP = 2
SEG_WIDTH = 256
POOL_BLOCK = 256
SEQ_PER_TILE = MIX_TILE // SEQ
MOD_ROWS = 8
MOD_WIDTH = N_MOD * D_MODEL
MOD_STEPS = 2
MOD_SIDE_BLOCKS = 18
FFN_CAST_BLOCKS = 16
VMEM_LIMIT = 56 * 1024 * 1024


def _n_ctx_tiles(tile):
    return N_CTX // tile


def _mod_row(i, tile):
    first = _n_ctx_tiles(tile)
    return jnp.where(i < first, 0, 1 + (i - first) * tile // DEC_SEQ)


def _rms(x, g):
    return x * lax.rsqrt(jnp.mean(x * x, axis=-1, keepdims=True) + EPS) * g


def _mod_vec(mod_ref, r, j):
    return mod_ref[pl.ds(r, 1), j * D_MODEL:(j + 1) * D_MODEL]


def _modulate(x, g, mod_ref, r, j):
    gain = g * (1 + _mod_vec(mod_ref, r, j + 1))
    return x * lax.rsqrt(jnp.mean(x * x, axis=-1, keepdims=True) + EPS) * gain + _mod_vec(mod_ref, r, j)


def _ffn_rows(x, g, mod_ref, r, j, w1_ref, w2_ref, before_chunk=None):
    h = _modulate(x, g, mod_ref, r, j).astype(BF16)
    acc = None
    for c in range(N_FF_CHUNKS):
        lo = c * FF_CHUNK
        if before_chunk is not None:
            before_chunk(c)
        gate = jnp.dot(h, w1_ref[:, lo:lo + FF_CHUNK], preferred_element_type=F32)
        up = jnp.dot(h, w1_ref[:, D_FF + lo:D_FF + lo + FF_CHUNK], preferred_element_type=F32)
        act = (gate / (1 + jnp.exp(-gate)) * up).astype(BF16)
        y = jnp.dot(act, w2_ref[lo:lo + FF_CHUNK, :], preferred_element_type=F32)
        acc = y if acc is None else acc + y
    return x + (0.5 * _mod_vec(mod_ref, r, j + 2)) * acc


class _Cast:
    def __init__(self, src, lead, n_blocks):
        self.src, self.lead, self.n_blocks = src, tuple(lead), n_blocks
        self.rows, self.cols = src.shape[len(lead):]
        self.block_rows = self.rows // n_blocks

    def in_spec(self, step):
        lead, last = self.lead, self.n_blocks - 1
        return pl.BlockSpec((None,) * len(lead) + (self.block_rows, self.cols),
                            lambda *g: lead + (jnp.minimum(step(*g), last), 0))

    def out_spec(self, step):
        last = self.n_blocks - 1
        return pl.BlockSpec((self.block_rows, self.cols), lambda *g: (jnp.minimum(step(*g), last), 0))

    def out_shape(self):
        return jax.ShapeDtypeStruct((self.rows, self.cols), BF16)


def _cast_blocks(step, n_blocks, srcs, dsts):
    @pl.when(step < n_blocks)
    def _():
        for s, d in zip(srcs, dsts):
            d[...] = s[...].astype(BF16)


class _WeightStream:
    def __init__(self, w1_hbm, w2_hbm, layer, which, w1_s, w2_s, gate_st, up_st, down_st, sems):
        self.w1_hbm, self.w2_hbm, self.lead = w1_hbm, w2_hbm, (layer, which)
        self.w1_s, self.w2_s = w1_s, w2_s
        self.stages, self.sems = (gate_st, up_st, down_st), sems

    def _copies(self, c):
        slot, lo = c % STAGE_SLOTS, c * FF_CHUNK
        l, w = self.lead
        srcs = (self.w1_hbm.at[l, w, :, pl.ds(lo, FF_CHUNK)],
                self.w1_hbm.at[l, w, :, pl.ds(D_FF + lo, FF_CHUNK)],
                self.w2_hbm.at[l, w, pl.ds(lo, FF_CHUNK), :])
        return [pltpu.make_async_copy(src, st.at[slot], self.sems.at[k, slot])
                for k, (src, st) in enumerate(zip(srcs, self.stages))]

    def start(self, c):
        for cp in self._copies(c):
            cp.start()

    def prime(self):
        for c in range(STAGE_SLOTS):
            self.start(c)

    def land(self, c):
        for cp in self._copies(c):
            cp.wait()
        slot, lo = c % STAGE_SLOTS, c * FF_CHUNK
        gate_st, up_st, down_st = self.stages
        self.w1_s[:, lo:lo + FF_CHUNK] = gate_st[slot].astype(BF16)
        self.w1_s[:, D_FF + lo:D_FF + lo + FF_CHUNK] = up_st[slot].astype(BF16)
        self.w2_s[lo:lo + FF_CHUNK, :] = down_st[slot].astype(BF16)
        if c + STAGE_SLOTS < N_FF_CHUNKS:
            self.start(c + STAGE_SLOTS)


def _weight_stream_specs():
    return [pl.BlockSpec(memory_space=pl.ANY), pl.BlockSpec(memory_space=pl.ANY)]


N_STREAM_SCRATCH = 6


def _weight_stream_scratch():
    return [pltpu.VMEM((D_MODEL, 2 * D_FF), BF16), pltpu.VMEM((D_FF, D_MODEL), BF16),
            pltpu.VMEM((STAGE_SLOTS, D_MODEL, FF_CHUNK), F32),
            pltpu.VMEM((STAGE_SLOTS, D_MODEL, FF_CHUNK), F32),
            pltpu.VMEM((STAGE_SLOTS, FF_CHUNK, D_MODEL), F32),
            pltpu.SemaphoreType.DMA((3, STAGE_SLOTS))]


def _mod_block(c_ref, w_ref, b_ref):
    c = c_ref[...]
    s = (c / (1 + jnp.exp(-c))).astype(BF16)
    return jnp.dot(s, w_ref[...].astype(BF16), preferred_element_type=F32) + b_ref[...]


class _ModJob:
    def __init__(self, cvec, ada_w, ada_b, layer, n_blocks, width=MOD_WIDTH):
        self.args = (cvec, ada_w, ada_b.reshape(ada_b.shape[0], 1, MOD_WIDTH))
        self.layer, self.n_blocks, self.width, self.cols = layer, n_blocks, width, width // n_blocks

    def in_specs(self):
        layer, last, cols = self.layer, self.n_blocks - 1, self.cols
        block = lambda i: (layer, 0, jnp.minimum(i, last))
        return [pl.BlockSpec((MOD_ROWS, D_MODEL), lambda i: (0, 0)),
                pl.BlockSpec((None, D_MODEL, cols), block),
                pl.BlockSpec((None, 1, cols), block)]

    def out_spec(self):
        last = self.n_blocks - 1
        return pl.BlockSpec((MOD_ROWS, self.cols), lambda i: (0, jnp.minimum(i, last)))

    def out_shape(self):
        return jax.ShapeDtypeStruct((MOD_ROWS, self.width), F32)


def _mod_kernel(c_ref, w_ref, b_ref, *refs, n_cast):
    srcs, o_ref, dsts = refs[:n_cast], refs[n_cast], refs[n_cast + 1:]
    o_ref[...] = _mod_block(c_ref, w_ref, b_ref)
    _cast_blocks(pl.program_id(0), pl.num_programs(0), srcs, dsts)


def _mod_call(job, casts):
    step = lambda i: i
    out = pl.pallas_call(
        functools.partial(_mod_kernel, n_cast=len(casts)),
        grid=(job.n_blocks,),
        in_specs=job.in_specs() + [c.in_spec(step) for c in casts],
        out_specs=[job.out_spec()] + [c.out_spec(step) for c in casts],
        out_shape=[job.out_shape()] + [c.out_shape() for c in casts],
        compiler_params=_params(),
        name="adaln_mod",
    )(*job.args, *[c.src for c in casts])
    return out[0], out[1:]


def _rows_spec(tile, width):
    return pl.BlockSpec((tile, width), lambda i: (i, 0))


def _ctx_rows_spec(tile, width):
    last = _n_ctx_tiles(tile) - 1
    return pl.BlockSpec((tile, width), lambda i: (jnp.minimum(i, last), 0))


def _smp_rows_spec(tile, width):
    first = _n_ctx_tiles(tile)
    return pl.BlockSpec((tile, width), lambda i: (jnp.maximum(i - first, 0), 0))


def _mod_spec(mod):
    return pl.BlockSpec(mod.shape, lambda i: (0, 0))


def _gain_spec(layer, j):
    return pl.BlockSpec((None, None, 1, D_MODEL), lambda i: (layer, j, 0, 0))


def _resident_spec(shape, *lead):
    index = tuple(lead) + (0,) * len(shape)
    return pl.BlockSpec((None,) * len(lead) + tuple(shape), lambda i: index,
                        pipeline_mode=pl.Buffered(1))


def _params():
    return pltpu.CompilerParams(dimension_semantics=("arbitrary",), vmem_limit_bytes=VMEM_LIMIT)


def _ffn_kernel(*refs, j, n_in, n_cast, mod_blocks, layer, which):
    n_side = n_cast + (3 if mod_blocks else 0)
    x_refs = refs[:n_in]
    mod_ref, g_ref, w1_hbm, w2_hbm = refs[n_in:n_in + 4]
    side_in = refs[n_in + 4:n_in + 4 + n_side]
    o_ref = refs[n_in + 4 + n_side]
    side_out = refs[n_in + 5 + n_side:len(refs) - N_STREAM_SCRATCH]
    stream = _WeightStream(w1_hbm, w2_hbm, layer, which, *refs[len(refs) - N_STREAM_SCRATCH:])
    i = pl.program_id(0)

    def run(before_chunk):
        if n_in == 2:
            x = jnp.where(i < _n_ctx_tiles(FFN_TILE), x_refs[0][...], x_refs[1][...])
        else:
            x = x_refs[0][...]
        o_ref[...] = _ffn_rows(x, g_ref[...], mod_ref, _mod_row(i, FFN_TILE), j,
                               stream.w1_s, stream.w2_s, before_chunk)

    @pl.when(i == 0)
    def _():
        stream.prime()
        run(stream.land)

    @pl.when(i > 0)
    def _():
        run(None)

    _cast_blocks(i, FFN_CAST_BLOCKS, side_in[:n_cast], side_out[:n_cast])
    if mod_blocks:
        @pl.when(i < mod_blocks)
        def _():
            side_out[n_cast][...] = _mod_block(*side_in[n_cast:])


def _ffn_call(xs, mod, norm_g, w1, w2, layer, which, casts=(), mod_job=None):
    step = lambda i: i
    if len(xs) == 2:
        x_specs = [_ctx_rows_spec(FFN_TILE, D_MODEL), _smp_rows_spec(FFN_TILE, D_MODEL)]
    else:
        x_specs = [_rows_spec(FFN_TILE, D_MODEL)]
    jobs = [mod_job] if mod_job else []
    out = pl.pallas_call(
        functools.partial(_ffn_kernel, j=6 * which, n_in=len(xs), n_cast=len(casts),
                          mod_blocks=mod_job.n_blocks if mod_job else 0, layer=layer, which=which),
        grid=(N_ALL // FFN_TILE,),
        in_specs=x_specs + [_mod_spec(mod), _gain_spec(layer, 2 * which)] + _weight_stream_specs()
        + [c.in_spec(step) for c in casts] + [s for m in jobs for s in m.in_specs()],
        out_specs=[_rows_spec(FFN_TILE, D_MODEL)] + [c.out_spec(step) for c in casts]
        + [m.out_spec() for m in jobs],
        out_shape=[jax.ShapeDtypeStruct((N_ALL, D_MODEL), F32)] + [c.out_shape() for c in casts]
        + [m.out_shape() for m in jobs],
        scratch_shapes=_weight_stream_scratch(),
        compiler_params=_params(),
        name="ffn",
    )(*xs, mod, norm_g, w1, w2, *[c.src for c in casts], *[a for m in jobs for a in m.args])
    return out[0], out[1:]


def _head_mean_sq(x, seg):
    sq = x * x
    hi = sq.astype(BF16)
    lo = (sq - hi.astype(F32)).astype(BF16)
    width = x.shape[-1]
    parts = []
    for c in range(0, width, SEG_WIDTH):
        n = min(SEG_WIDTH, width - c)
        parts.append(jnp.dot(hi[:, c:c + n], seg[:n, :n], preferred_element_type=F32)
                     + jnp.dot(lo[:, c:c + n], seg[:n, :n], preferred_element_type=F32))
    return parts[0] if len(parts) == 1 else jnp.concatenate(parts, axis=1)


def _rope(x, cos, sin_signed):
    width = x.shape[-1]
    lane = lax.broadcasted_iota(jnp.int32, (1, width), 1)
    partner = jnp.where((lane & 31) < 16,
                        pltpu.roll(x, width - 16, 1), pltpu.roll(x, 16, 1))
    return x * cos + partner * sin_signed


def _stack_heads_t(qt_ref, rows, first, count):
    return jnp.concatenate(
        [qt_ref[(first + g) * HEAD_DIM:(first + g + 1) * HEAD_DIM, rows] for g in range(count)],
        axis=1)


ONES_ROWS = 16


def _attend_group(qt, k, v1t):
    st = jnp.dot(k, qt, preferred_element_type=F32)
    e = jnp.exp(st - jnp.max(st, axis=0, keepdims=True)).astype(BF16)
    ot = jnp.dot(v1t, e, preferred_element_type=F32)
    return ot[:HEAD_DIM] / ot[HEAD_DIM:HEAD_DIM + 1]


def _attend_rows(qt_s, rows, n, kv, mix_s, heads_per_group=HEADS_PER_KV):
    heads = []
    for kvh in range(N_KV_HEADS):
        k, v1t = kv(kvh)
        for first in range(kvh * HEADS_PER_KV, (kvh + 1) * HEADS_PER_KV, heads_per_group):
            ot = _attend_group(_stack_heads_t(qt_s, rows, first, heads_per_group), k, v1t)
            heads += [ot[:, g * n:(g + 1) * n] for g in range(heads_per_group)]
    mix_s[rows, :ATTN_WIDTH] = jnp.concatenate(heads, axis=0).T.astype(BF16)


def _with_ones(vt):
    return jnp.concatenate([vt, jnp.ones((ONES_ROWS, vt.shape[1]), BF16)], axis=0)


def _mixer_tile(ctx, r, x_ref, mod_ref, g_ref, w_ref, qg_ref, kg_ref, cw_ref, seg_ref, cos_ref, sin_ref,
                ck_ref, cvt_ref, wo_ref, o_ref, kt_ref, vt_ref, qt_s, k_s, vt_s, mix_s):
    seq_len = SEQ if ctx else DEC_SEQ
    x = x_ref[...]
    h = _modulate(x, g_ref[...], mod_ref, r, 3).astype(BF16)

    def proj(lo, width):
        return jnp.dot(h, w_ref[:, lo:lo + width], preferred_element_type=F32)

    q = proj(0, ATTN_WIDTH)
    kv = proj(ATTN_WIDTH, 2 * KV_WIDTH)
    k, v = kv[:, :KV_WIDTH], kv[:, KV_WIDTH:]
    q = q * lax.rsqrt(_head_mean_sq(q, seg_ref[...]) + EPS) * (qg_ref[...] * HEAD_DIM ** -0.5)
    k = k * lax.rsqrt(_head_mean_sq(k, seg_ref[...]) + EPS) * kg_ref[...]
    if ctx:
        qt_s[...] = q.T.astype(BF16)
        k_s[...] = k.astype(BF16)
        for b in range(SEQ_PER_TILE):
            vt = v[b * SEQ:(b + 1) * SEQ, :].T
            kt_ref[b] = k[b * SEQ:(b + 1) * SEQ, :].T
            vt_ref[b] = vt
            vt_s[:, b * SEQ:(b + 1) * SEQ] = vt.astype(BF16)
    else:
        qt_s[...] = _rope(q, cos_ref[...], sin_ref[...]).T.astype(BF16)
        k_s[...] = _rope(k, cos_ref[:, :KV_WIDTH], sin_ref[:, :KV_WIDTH]).astype(BF16)
        vt_s[...] = v.T.astype(BF16)

    base = ATTN_WIDTH + 2 * KV_WIDTH
    z = proj(base + CONV_WIDTH, CONV_WIDTH) * proj(base + 2 * CONV_WIDTH, CONV_WIDTH)
    pos = lax.broadcasted_iota(jnp.int32, (MIX_TILE, 1), 0) & (seq_len - 1)
    z_prev = jnp.where(pos == 0, 0.0, pltpu.roll(z, 1, 0))
    z_next = jnp.where(pos == seq_len - 1, 0.0, pltpu.roll(z, MIX_TILE - 1, 0))
    conv = z_prev * cw_ref[0:1, :] + z * cw_ref[1:2, :] + z_next * cw_ref[2:3, :]
    mix_s[:, ATTN_WIDTH:] = (proj(base, CONV_WIDTH) * conv).astype(BF16)

    def head(kvh):
        return slice(kvh * HEAD_DIM, (kvh + 1) * HEAD_DIM)

    if ctx:
        for b in range(SEQ_PER_TILE):
            rows = slice(b * SEQ, (b + 1) * SEQ)
            _attend_rows(qt_s, rows, SEQ,
                         lambda kvh: (k_s[rows, head(kvh)], _with_ones(vt_s[head(kvh), rows])), mix_s,
                         heads_per_group=CTX_HEADS_PER_GROUP)
    else:
        kv = [(jnp.concatenate([ck_ref[:, head(kvh)].astype(BF16), k_s[:, head(kvh)]], axis=0),
               _with_ones(jnp.concatenate([cvt_ref[head(kvh), :].astype(BF16), vt_s[head(kvh), :]],
                                          axis=1)))
              for kvh in range(N_KV_HEADS)]
        for t in range(MIX_TILE // Q_ROWS):
            _attend_rows(qt_s, slice(t * Q_ROWS, (t + 1) * Q_ROWS), Q_ROWS, lambda kvh: kv[kvh], mix_s)

    out = (jnp.dot(mix_s[:, :ATTN_WIDTH], wo_ref[:ATTN_WIDTH, :], preferred_element_type=F32)
           + jnp.dot(mix_s[:, ATTN_WIDTH:], wo_ref[ATTN_WIDTH:, :], preferred_element_type=F32))
    o_ref[...] = x + _mod_vec(mod_ref, r, 5) * out


def _mixer_kernel(*refs):
    i = pl.program_id(0)
    r = _mod_row(i, MIX_TILE)
    is_ctx = i < _n_ctx_tiles(MIX_TILE)

    @pl.when(is_ctx)
    def _():
        _mixer_tile(True, r, *refs)

    @pl.when(jnp.logical_not(is_ctx))
    def _():
        _mixer_tile(False, r, *refs)


def _mixer_call(x, mod, norm_g, w_in, qg, kg, conv_w, seg, cos, sin, cache_k, cache_vt, w_out):
    first_smp = _n_ctx_tiles(MIX_TILE)
    smp_batch = lambda i: (jnp.maximum(i - first_smp, 0), 0, 0)
    kvt_spec = pl.BlockSpec((SEQ_PER_TILE, KV_WIDTH, SEQ), lambda i: (jnp.minimum(i, first_smp - 1), 0, 0))
    kvt_shape = jax.ShapeDtypeStruct((BATCH, KV_WIDTH, SEQ), F32)
    return pl.pallas_call(
        _mixer_kernel,
        grid=(N_ALL // MIX_TILE,),
        in_specs=[
            _rows_spec(MIX_TILE, D_MODEL), _mod_spec(mod), _gain_spec(0, 1),
            _resident_spec((D_MODEL, IN_WIDTH)),
            _resident_spec((1, ATTN_WIDTH)), _resident_spec((1, KV_WIDTH)),
            _resident_spec((3, CONV_WIDTH), 0),
            _resident_spec((SEG_WIDTH, SEG_WIDTH)),
            _resident_spec((DEC_SEQ, ATTN_WIDTH)), _resident_spec((DEC_SEQ, ATTN_WIDTH)),
            pl.BlockSpec((None, PAST_LEN, KV_WIDTH), smp_batch),
            pl.BlockSpec((None, KV_WIDTH, PAST_LEN), smp_batch),
            _resident_spec((MIX_WIDTH, D_MODEL)),
        ],
        out_specs=[_rows_spec(MIX_TILE, D_MODEL), kvt_spec, kvt_spec],
        out_shape=[jax.ShapeDtypeStruct((N_ALL, D_MODEL), F32), kvt_shape, kvt_shape],
        scratch_shapes=[pltpu.VMEM((ATTN_WIDTH, MIX_TILE), BF16),
                        pltpu.VMEM((MIX_TILE, KV_WIDTH), BF16),
                        pltpu.VMEM((KV_WIDTH, MIX_TILE), BF16),
                        pltpu.VMEM((MIX_TILE, MIX_WIDTH), BF16)],
        compiler_params=_params(),
        name="mixer",
    )(x, mod, norm_g, w_in, qg, kg, conv_w, seg, cos, sin, cache_k, cache_vt, w_out)


def _window_sums(h, band_ref, gi, seq_len):
    hi = h.astype(BF16)
    lo = (h - hi.astype(F32)).astype(BF16)
    per_seq = seq_len // POOL_BLOCK
    sums = []
    for b in range(h.shape[0] // POOL_BLOCK):
        def part(kind, src):
            rows = slice(src * POOL_BLOCK, (src + 1) * POOL_BLOCK)
            band = band_ref[gi, kind]
            return (jnp.dot(band, hi[rows], preferred_element_type=F32)
                    + jnp.dot(band, lo[rows], preferred_element_type=F32))
        total = part(0, b)
        if (b + 1) % per_seq:
            total = total + part(1, b + 1)
        if b % per_seq:
            total = total + part(2, b - 1)
        sums.append(total)
    return jnp.concatenate(sums, axis=0)


def _pool_rows(x, pos, seq_len, r, mod_ref, gm_ref, pw_ref, ps_ref, band_ref):
    gate = _mod_vec(mod_ref, r, 5)
    h_all = _modulate(x, gm_ref[...], mod_ref, r, 3)
    outs = []
    for gi, w in enumerate(POOL_WINDOWS):
        lanes = slice(gi * POOL_GROUP, (gi + 1) * POOL_GROUP)
        h = h_all[:, lanes]
        left = w // 2
        right = w - 1 - left
        count = jnp.minimum(pos + right + 1, seq_len) - jnp.maximum(pos - left, 0)
        diff = (_window_sums(h, band_ref, gi, seq_len) / count.astype(F32) - h).astype(BF16)
        out = jnp.dot(diff, pw_ref[lanes, :], preferred_element_type=F32)
        outs.append(x[:, lanes] + (gate[:, lanes] * ps_ref[:, lanes]) * out)
    return jnp.concatenate(outs, axis=1)


def _pool_tile(ctx, r, x_ref, mod_ref, gm_ref, gf_ref, gn_ref, pw_ref, ps_ref, band_ref, w1_ref, w2_ref,
               y_ref, x2_ref):
    seq_len = SEQ if ctx else DEC_SEQ
    block = max(seq_len, FFN_TILE)
    pos = lax.broadcasted_iota(jnp.int32, (block, 1), 0) & (seq_len - 1)
    for b in range(MIX_TILE // block):
        rows = slice(b * block, (b + 1) * block)
        x2_ref[rows, :] = _pool_rows(x_ref[rows, :], pos, seq_len, r, mod_ref, gm_ref, pw_ref, ps_ref,
                                     band_ref)

    def ffn_step(s, carry):
        rows = pl.ds(pl.multiple_of(s * FFN_TILE, FFN_TILE), FFN_TILE)
        y = _ffn_rows(x2_ref[rows, :], gf_ref[...], mod_ref, r, 6, w1_ref, w2_ref)
        y_ref[rows, :] = _rms(y, gn_ref[...])
        return carry

    lax.fori_loop(0, MIX_TILE // FFN_TILE, ffn_step, 0)


def _pool_kernel(*refs):
    ins, (yp_ref, ys_ref, x2_ref) = refs[:-3], refs[-3:]
    i = pl.program_id(0)
    r = _mod_row(i, MIX_TILE)
    is_ctx = i < _n_ctx_tiles(MIX_TILE)

    @pl.when(is_ctx)
    def _():
        _pool_tile(True, r, *ins, yp_ref, x2_ref)

    @pl.when(jnp.logical_not(is_ctx))
    def _():
        _pool_tile(False, r, *ins, ys_ref, x2_ref)


def _pool_call(x, mod, norm_g, final_g, pool_w, pool_scale, bands, w1, w2):
    return pl.pallas_call(
        _pool_kernel,
        grid=(N_ALL // MIX_TILE,),
        in_specs=[
            _rows_spec(MIX_TILE, D_MODEL), _mod_spec(mod), _gain_spec(1, 1), _gain_spec(1, 2),
            _resident_spec((1, D_MODEL)),
            _resident_spec((D_MODEL, POOL_GROUP)),
            _resident_spec((1, D_MODEL), 0),
            _resident_spec(bands.shape),
            _resident_spec((D_MODEL, 2 * D_FF)), _resident_spec((D_FF, D_MODEL)),
        ],
        out_specs=[_ctx_rows_spec(MIX_TILE, D_MODEL), _smp_rows_spec(MIX_TILE, D_MODEL)],
        out_shape=[jax.ShapeDtypeStruct((N_CTX, D_MODEL), F32),
                   jax.ShapeDtypeStruct((N_SMP, D_MODEL), F32)],
        scratch_shapes=[pltpu.VMEM((MIX_TILE, D_MODEL), F32)],
        compiler_params=_params(),
        name="pool_ffn_norm",
    )(x, mod, norm_g, norm_g, final_g, pool_w, pool_scale, bands, w1, w2)


def _rope_tables():
    t = np.arange(DEC_SEQ)
    half = HEAD_DIM // 2
    inv = ROPE_THETA ** (-np.arange(0, half, 2, dtype=np.float64) / half)
    ang_row = (t // GRID_W)[:, None] * inv[None, :]
    ang_col = (t % GRID_W)[:, None] * inv[None, :]
    cos = np.concatenate([np.cos(ang_row), np.cos(ang_row), np.cos(ang_col), np.cos(ang_col)], axis=1)
    sin = np.concatenate([-np.sin(ang_row), np.sin(ang_row), -np.sin(ang_col), np.sin(ang_col)], axis=1)
    return (jnp.asarray(np.tile(cos, (1, N_HEADS)), F32), jnp.asarray(np.tile(sin, (1, N_HEADS)), F32))


def _head_segments():
    head = np.arange(SEG_WIDTH) // HEAD_DIM
    return jnp.asarray((head[:, None] == head[None, :]) / HEAD_DIM, BF16)


def _pool_bands():
    t = np.arange(POOL_BLOCK)[:, None]
    bands = []
    for w in POOL_WINDOWS:
        left = w // 2
        right = w - 1 - left
        src = [np.arange(POOL_BLOCK)[None, :] + shift for shift in (0, POOL_BLOCK, -POOL_BLOCK)]
        bands.append(np.stack([(s >= t - left) & (s <= t + right) for s in src]))
    return jnp.asarray(np.stack(bands), BF16)


def _cache_layout(t):
    return jnp.transpose(t.reshape(BATCH, 1, N_KV_HEADS, HEAD_DIM, SEQ), (0, 1, 4, 2, 3))


def kernel(x_prompt, x_sample, c, cache_k, cache_v, c_ctx, ada_w, ada_b, norm_g, ffn_w1, ffn_w2,
           mix_w_in, mix_w_out, q_norm, k_norm, conv_w, pool_w, pool_scale, final_g):
    cvec = jnp.concatenate(
        [c_ctx[None, :], c, jnp.zeros((MOD_ROWS - 1 - DEC_BATCH, D_MODEL), F32)], axis=0)
    gains = norm_g.reshape(norm_g.shape[0], 3, 1, D_MODEL)
    cos, sin = _rope_tables()

    mod_first, _ = _mod_call(_ModJob(cvec, ada_w, ada_b, 0, MOD_STEPS, 3 * D_MODEL), [])
    x, (w_in, w_out, mod0) = _ffn_call(
        [x_prompt.reshape(N_CTX, D_MODEL), x_sample.reshape(N_SMP, D_MODEL)],
        mod_first, gains, ffn_w1, ffn_w2, 0, 0,
        [_Cast(mix_w_in, (0,), FFN_CAST_BLOCKS), _Cast(mix_w_out, (0,), FFN_CAST_BLOCKS)],
        _ModJob(cvec, ada_w, ada_b, 0, MOD_SIDE_BLOCKS))
    x, kt, vt = _mixer_call(
        x, mod0, gains, w_in, jnp.tile(q_norm[0], N_HEADS)[None, :],
        jnp.tile(k_norm[0], N_KV_HEADS)[None, :], conv_w, _head_segments(), cos, sin,
        cache_k[:, 0].reshape(DEC_BATCH, PAST_LEN, KV_WIDTH),
        jnp.transpose(cache_v[:, 0], (0, 2, 3, 1)).reshape(DEC_BATCH, KV_WIDTH, PAST_LEN), w_out)
    x, (pw, mod1) = _ffn_call(
        [x], mod0, gains, ffn_w1, ffn_w2, 0, 1,
        [_Cast(pool_w.reshape(pool_w.shape[0], D_MODEL, POOL_GROUP), (0,), FFN_CAST_BLOCKS)],
        _ModJob(cvec, ada_w, ada_b, 1, MOD_SIDE_BLOCKS))
    x, (w1_last, w2_last) = _ffn_call(
        [x], mod1, gains, ffn_w1, ffn_w2, 1, 0,
        [_Cast(ffn_w1, (1, 1), FFN_CAST_BLOCKS), _Cast(ffn_w2, (1, 1), FFN_CAST_BLOCKS)])
    yp, ys = _pool_call(x, mod1, gains, final_g[None, :], pw,
                        pool_scale.reshape(pool_scale.shape[0], 1, D_MODEL), _pool_bands(), w1_last, w2_last)

    return (yp.reshape(BATCH, SEQ, D_MODEL), ys.reshape(DEC_BATCH, DEC_SEQ, D_MODEL),
            _cache_layout(kt), _cache_layout(vt))
```
